```python
import jax, jax.numpy as jnp
from jax import lax
import numpy as np

D_MODEL = 1024
BATCH = 4
SEQ = 4096
DEPTH = 1

HEAD_DIM = 64
N_HEADS = D_MODEL // HEAD_DIM
N_MOBA_HEADS = N_HEADS // 2
N_FOX_HEADS = N_HEADS - N_MOBA_HEADS
MOBA_WIDTH = N_MOBA_HEADS * HEAD_DIM
FOX_WIDTH = N_FOX_HEADS * HEAD_DIM
IN_WIDTH = 3 * MOBA_WIDTH + 3 * FOX_WIDTH + N_FOX_HEADS
MOBA_BLOCK = 256
MOBA_TOPK = 3
MOBA_Q_CHUNK = 32
FOX_Q_BLOCK = 128
ROPE_THETA = 500000.0
ROPE_DIM = HEAD_DIM // 4
D_FF = 4 * D_MODEL
EPS = 1e-6

kernel_name = "hymba_moba_fox_sqrelu_adaln"


def _rms(x):
    xf = x.astype(jnp.float32)
    return xf * lax.rsqrt(jnp.mean(xf * xf, axis=-1, keepdims=True) + EPS)


def _head_norm(x, gain):
    return (_rms(x) * gain.astype(jnp.float32)).astype(x.dtype)


def _partial_rope(x):
    S = x.shape[2]
    half = ROPE_DIM // 2
    inv_freq = ROPE_THETA ** (-jnp.arange(0, ROPE_DIM, 2, dtype=jnp.float32) / ROPE_DIM)
    ang = jnp.arange(S, dtype=jnp.float32)[:, None] * inv_freq[None, :]
    cos, sin = jnp.cos(ang), jnp.sin(ang)
    xr = x[..., :ROPE_DIM].astype(jnp.float32)
    x1, x2 = xr[..., :half], xr[..., half:]
    rot = jnp.concatenate([x1 * cos - x2 * sin, x2 * cos + x1 * sin], axis=-1)
    return jnp.concatenate([rot.astype(x.dtype), x[..., ROPE_DIM:]], axis=-1)


def _split_heads(t, n_heads):
    B, S, _ = t.shape
    return t.reshape(B, S, n_heads, HEAD_DIM).transpose(0, 2, 1, 3)


def _merge_heads(t):
    B, H, S, d = t.shape
    return t.transpose(0, 2, 1, 3).reshape(B, S, H * d)


def moba_attention(q, k, v):
    B, H, S, d = q.shape
    nb = -(-S // MOBA_BLOCK)
    s_pad = nb * MOBA_BLOCK
    pad = ((0, 0), (0, 0), (0, s_pad - S), (0, 0))
    q, k, v = jnp.pad(q, pad), jnp.pad(k, pad), jnp.pad(v, pad)
    kb = k.reshape(B, H, nb, MOBA_BLOCK, d)
    vb = v.reshape(B, H, nb, MOBA_BLOCK, d)
    k_mean = jnp.mean(kb.astype(jnp.float32), axis=3)
    t_blk = jnp.arange(s_pad) // MOBA_BLOCK
    gate = jnp.einsum('bhtd,bhnd->bhtn', q.astype(jnp.float32), k_mean)
    fully_past = jnp.arange(nb)[None, :] < t_blk[:, None]
    gate = jnp.where(fully_past, gate, -jnp.inf)
    top_k = min(MOBA_TOPK, nb)
    _, sel = lax.top_k(gate, top_k)

    C = MOBA_Q_CHUNK
    n_chunks = s_pad // C
    q_c = q.reshape(B, H, n_chunks, C, d).transpose(2, 0, 1, 3, 4)
    sel_c = sel.reshape(B, H, n_chunks, C, top_k).transpose(2, 0, 1, 3, 4)
    b_idx = jnp.arange(B)[:, None, None, None]
    h_idx = jnp.arange(H)[None, :, None, None]
    local = jnp.arange(MOBA_BLOCK)
    scale = d ** -0.5
    n_sel = top_k * MOBA_BLOCK

    def chunk(args):
        qc, selc, ci = args
        t0 = ci * C
        blk = t0 // MOBA_BLOCK
        k_sel = kb[b_idx, h_idx, selc]
        v_sel = vb[b_idx, h_idx, selc]
        s_sel = jnp.einsum('bhtd,bhtkld->bhtkl', qc, k_sel,
                           preferred_element_type=jnp.float32) * scale
        valid = jnp.arange(top_k) < blk
        s_sel = jnp.where(valid[:, None], s_sel, -jnp.inf).reshape(B, H, C, n_sel)
        k_own = lax.dynamic_index_in_dim(kb, blk, axis=2, keepdims=False)
        v_own = lax.dynamic_index_in_dim(vb, blk, axis=2, keepdims=False)
        s_own = jnp.einsum('bhtd,bhld->bhtl', qc, k_own,
                           preferred_element_type=jnp.float32) * scale
        t_loc = t0 - blk * MOBA_BLOCK + jnp.arange(C)
        s_own = jnp.where(local[None, :] <= t_loc[:, None], s_own, -jnp.inf)
        p = jax.nn.softmax(jnp.concatenate([s_sel, s_own], axis=-1), axis=-1)
        p_sel = p[..., :n_sel].reshape(B, H, C, top_k, MOBA_BLOCK).astype(v.dtype)
        p_own = p[..., n_sel:].astype(v.dtype)
        return (jnp.einsum('bhtkl,bhtkld->bhtd', p_sel, v_sel)
                + jnp.einsum('bhtl,bhld->bhtd', p_own, v_own))

    out = lax.map(chunk, (q_c, sel_c, jnp.arange(n_chunks)))
    return out.transpose(1, 2, 0, 3, 4).reshape(B, H, s_pad, d)[:, :, :S]


def forgetting_attention(q, k, v, log_f):
    B, H, S, d = q.shape
    cum = jnp.cumsum(log_f, axis=-1)
    nq = S // FOX_Q_BLOCK
    q_b = q.reshape(B, H, nq, FOX_Q_BLOCK, d).transpose(2, 0, 1, 3, 4)
    c_b = cum.reshape(B, H, nq, FOX_Q_BLOCK).transpose(2, 0, 1, 3)
    key_pos = jnp.arange(S)
    scale = d ** -0.5

    def block(args):
        qb, cb, i = args
        s = jnp.einsum('bhtd,bhsd->bhts', qb, k, preferred_element_type=jnp.float32) * scale
        s = s + cb[..., :, None] - cum[:, :, None, :]
        q_pos = i * FOX_Q_BLOCK + jnp.arange(FOX_Q_BLOCK)
        s = jnp.where(key_pos[None, :] <= q_pos[:, None], s, -jnp.inf)
        p = jax.nn.softmax(s, axis=-1).astype(v.dtype)
        return jnp.einsum('bhts,bhsd->bhtd', p, v)

    out = lax.map(block, (q_b, c_b, jnp.arange(nq)))
    return out.transpose(1, 2, 0, 3, 4).reshape(B, H, S, d)


def setup_inputs(seed: int = 0) -> dict:
    key = jax.random.key(seed)
    ks = jax.random.split(key, 16)
    f32 = jnp.float32
    x = jax.random.normal(ks[0], (BATCH, SEQ, D_MODEL), f32)
    c = jax.random.normal(ks[1], (BATCH, D_MODEL), f32)
    w_ada = jax.random.normal(ks[2], (DEPTH, D_MODEL, 6 * D_MODEL), f32) * (0.5 * D_MODEL ** -0.5)
    b_ada = 0.01 * jax.random.normal(ks[3], (DEPTH, 6 * D_MODEL), f32)
    w_in = jax.random.normal(ks[4], (DEPTH, D_MODEL, IN_WIDTH), f32) * D_MODEL ** -0.5
    b_forget = jax.random.uniform(ks[5], (DEPTH, N_FOX_HEADS), f32, 1.0, 4.0)
    g_qn_moba = 1.0 + 0.02 * jax.random.normal(ks[6], (DEPTH, HEAD_DIM), f32)
    g_kn_moba = 1.0 + 0.02 * jax.random.normal(ks[7], (DEPTH, HEAD_DIM), f32)
    g_qn_fox = 1.0 + 0.02 * jax.random.normal(ks[8], (DEPTH, HEAD_DIM), f32)
    g_kn_fox = 1.0 + 0.02 * jax.random.normal(ks[9], (DEPTH, HEAD_DIM), f32)
    g_out_moba = 1.0 + 0.02 * jax.random.normal(ks[10], (DEPTH, MOBA_WIDTH), f32)
    g_out_fox = 1.0 + 0.02 * jax.random.normal(ks[11], (DEPTH, FOX_WIDTH), f32)
    w_out = jax.random.normal(ks[12], (DEPTH, D_MODEL, D_MODEL), f32) * D_MODEL ** -0.5
    w_ff1 = jax.random.normal(ks[13], (DEPTH, D_MODEL, D_FF), f32) * D_MODEL ** -0.5
    w_ff2 = jax.random.normal(ks[14], (DEPTH, D_FF, D_MODEL), f32) * D_FF ** -0.5
    return {"x": x, "c": c, "w_ada": w_ada, "b_ada": b_ada, "w_in": w_in, "b_forget": b_forget,
            "g_qn_moba": g_qn_moba, "g_kn_moba": g_kn_moba, "g_qn_fox": g_qn_fox, "g_kn_fox": g_kn_fox,
            "g_out_moba": g_out_moba, "g_out_fox": g_out_fox, "w_out": w_out,
            "w_ff1": w_ff1, "w_ff2": w_ff2}


def reference(x, c, w_ada, b_ada, w_in, b_forget, g_qn_moba, g_kn_moba, g_qn_fox, g_kn_fox,
              g_out_moba, g_out_fox, w_out, w_ff1, w_ff2):
    M = MOBA_WIDTH
    F = FOX_WIDTH
    for l in range(DEPTH):
        mod = jax.nn.silu(c) @ w_ada[l] + b_ada[l]
        sh_a, sc_a, g_a, sh_m, sc_m, g_m = [m[:, None, :] for m in jnp.split(mod, 6, axis=-1)]

        h = (_rms(x) * (1.0 + sc_a) + sh_a).astype(x.dtype)
        proj = h @ w_in[l]
        q_m = proj[..., 0:M]
        k_m = proj[..., M:2 * M]
        v_m = proj[..., 2 * M:3 * M]
        o0 = 3 * M
        q_f = proj[..., o0:o0 + F]
        k_f = proj[..., o0 + F:o0 + 2 * F]
        v_f = proj[..., o0 + 2 * F:o0 + 3 * F]
        f_logit = proj[..., o0 + 3 * F:]

        q_m = _partial_rope(_head_norm(_split_heads(q_m, N_MOBA_HEADS), g_qn_moba[l]))
        k_m = _partial_rope(_head_norm(_split_heads(k_m, N_MOBA_HEADS), g_kn_moba[l]))
        v_m = _split_heads(v_m, N_MOBA_HEADS)
        o_m = _merge_heads(moba_attention(q_m, k_m, v_m))

        q_f = _head_norm(_split_heads(q_f, N_FOX_HEADS), g_qn_fox[l])
        k_f = _head_norm(_split_heads(k_f, N_FOX_HEADS), g_kn_fox[l])
        v_f = _split_heads(v_f, N_FOX_HEADS)
        log_f = jax.nn.log_sigmoid(f_logit.astype(jnp.float32) + b_forget[l].astype(jnp.float32))
        o_f = _merge_heads(forgetting_attention(q_f, k_f, v_f, log_f.transpose(0, 2, 1)))

        mixed = jnp.concatenate([(_rms(o_m) * g_out_moba[l]).astype(x.dtype),
                                 (_rms(o_f) * g_out_fox[l]).astype(x.dtype)], axis=-1)
        x = x + g_a * (mixed @ w_out[l])

        h = (_rms(x) * (1.0 + sc_m) + sh_m).astype(x.dtype)
        x = x + g_m * (jnp.square(jax.nn.relu(h @ w_ff1[l])) @ w_ff2[l])
    return x
```

```python
import functools
import math

import numpy as np
import jax
import jax.numpy as jnp
from jax import lax
from jax.experimental import pallas as pl
from jax.experimental.pallas import tpu as pltpu

F32 = jnp.float32
BF16 = jnp.bfloat16
HIGHEST = lax.Precision.HIGHEST

HEAD_DIM = 64
GROUP_HEADS = 8
GROUP_WIDTH = GROUP_HEADS * HEAD_DIM
PAIR_WIDTH = 2 * HEAD_DIM
N_PAIRS = GROUP_HEADS // 2
MOBA_BLOCK = 256
MOBA_TOPK = 3
ROPE_THETA = 500000.0
ROPE_DIM = HEAD_DIM // 4
ROPE_HALF = ROPE_DIM // 2
EPS = 1e-6
LOG2E = math.log2(math.e)
NEG_BIG = -1e30
ATT_TILE = 256
LANES = 128
VMEM_LIMIT = 48 * 1024 * 1024


def _dot(a, b, precision=None):
    return jnp.dot(a, b, preferred_element_type=F32, precision=precision)


def _dot_nt(a, b, precision=None):
    return lax.dot_general(a, b, (((1,), (1,)), ((), ())),
                           preferred_element_type=F32, precision=precision)


def _adaln_kernel(c_ref, w_ref, b_ref, o_ref):
    c = c_ref[...]
    s = c / (1.0 + jnp.exp(-c))
    o_ref[...] = _dot(s, w_ref[...], precision=HIGHEST) + b_ref[...]


def _adaln(c, w_ada, b_ada):
    bsz, d = c.shape
    n = w_ada.shape[1]
    return pl.pallas_call(
        _adaln_kernel,
        grid=(n // d,),
        in_specs=[pl.BlockSpec((bsz, d), lambda j: (0, 0)),
                  pl.BlockSpec((d, d), lambda j: (0, j)),
                  pl.BlockSpec((1, d), lambda j: (0, j))],
        out_specs=pl.BlockSpec((bsz, d), lambda j: (0, j)),
        out_shape=jax.ShapeDtypeStruct((bsz, n), F32),
        compiler_params=pltpu.CompilerParams(dimension_semantics=("arbitrary",)),
        name="adaln",
    )(c, w_ada, b_ada.reshape(1, n))


def _inproj_kernel(x_ref, sc_ref, sh_ref,
                   wqm_ref, wkm_ref, wvm_ref, wqf_ref, wkf_ref, wvf_ref, wf_ref, bf_ref,
                   gqm_ref, gkm_ref, gqf_ref, gkf_ref,
                   cos_t_ref, sin_t_ref, rc_ref, rs1_ref, rs2_ref,
                   j_ref, place_ref,
                   qm_o, km_o, vm_o, bias_o, qf_o, kf_o, vf_o, ka_o,
                   kmean_sc, carry_sc, *, tm):
    i = pl.program_id(1)
    nb = tm // MOBA_BLOCK

    @pl.when(i == 0)
    def _():
        kmean_sc[...] = jnp.zeros_like(kmean_sc)
        carry_sc[...] = jnp.zeros_like(carry_sc)

    x = x_ref[0]
    ms = jnp.mean(x * x, axis=-1, keepdims=True)
    h = x * lax.rsqrt(ms + EPS) * (1.0 + sc_ref[...]) + sh_ref[...]
    hb = h.astype(BF16)

    def q_feature_major(w_ref, g_ref, rope):
        q_t = _dot_nt(w_ref[...], hb)
        gain = g_ref[...]
        heads = []
        for hd in range(GROUP_HEADS):
            xh = q_t[hd * HEAD_DIM:(hd + 1) * HEAD_DIM, :]
            ss = jnp.sum(xh * xh, axis=0, keepdims=True)
            xh = xh * lax.rsqrt(ss * (1.0 / HEAD_DIM) + EPS) * gain
            if rope:
                x1 = xh[0:ROPE_HALF, :]
                x2 = xh[ROPE_HALF:ROPE_DIM, :]
                cs = cos_t_ref[...]
                sn = sin_t_ref[...]
                xh = jnp.concatenate(
                    [x1 * cs - x2 * sn, x2 * cs + x1 * sn, xh[ROPE_DIM:, :]], axis=0)
            heads.append(xh)
        return jnp.concatenate(heads, axis=0)

    def k_token_major(w_ref, g_ref, rope):
        k = _dot(hb, w_ref[...])
        chunks = []
        for c in range(GROUP_WIDTH // 256):
            seg = k[:, c * 256:(c + 1) * 256]
            msq = _dot((seg * seg).astype(BF16), j_ref[...])
            chunks.append(seg * lax.rsqrt(msq + EPS) * g_ref[:, c * 256:(c + 1) * 256])
        k = jnp.concatenate(chunks, axis=1)
        if rope:
            outs = []
            for c in range(GROUP_WIDTH // LANES):
                seg = k[:, c * LANES:(c + 1) * LANES]
                outs.append(seg * rc_ref[...]
                            + pltpu.roll(seg, ROPE_HALF, 1) * rs1_ref[...]
                            + pltpu.roll(seg, LANES - ROPE_HALF, 1) * rs2_ref[...])
            k = jnp.concatenate(outs, axis=1)
        return k

    def store_tiles(o_ref, val_t):
        for t in range(tm // ATT_TILE):
            o_ref[0, t] = val_t[:, t * ATT_TILE:(t + 1) * ATT_TILE].astype(o_ref.dtype)

    qm_t = q_feature_major(wqm_ref, gqm_ref, True)
    store_tiles(qm_o, qm_t)
    km = k_token_major(wkm_ref, gkm_ref, True)
    km_o[0] = km.astype(BF16)
    store_tiles(vm_o, _dot_nt(wvm_ref[...], hb))

    blk0 = i * nb
    lane_head = lax.broadcasted_iota(jnp.int32, (1, GROUP_WIDTH), 1) // HEAD_DIM
    for sb in range(nb):
        kmean = jnp.mean(km[sb * MOBA_BLOCK:(sb + 1) * MOBA_BLOCK, :], axis=0, keepdims=True)
        for hd in range(GROUP_HEADS):
            row = hd * 16 + blk0 + sb
            kmean_sc[pl.ds(row, 1), :] = jnp.where(lane_head == hd, kmean, 0.0)

    gate_t = _dot(kmean_sc[...], qm_t, precision=HIGHEST)
    tok_blk = blk0 + lax.broadcasted_iota(jnp.int32, (16, tm), 1) // MOBA_BLOCK
    n_idx = lax.broadcasted_iota(jnp.int32, (16, tm), 0)
    valid = n_idx < tok_blk
    bias_heads = []
    for hd in range(GROUP_HEADS):
        g = jnp.where(valid, gate_t[hd * 16:(hd + 1) * 16, :], -jnp.inf)
        cnt = jnp.zeros((16, tm), jnp.int32)
        for m in range(16):
            gm = g[m:m + 1, :]
            cnt = cnt + jnp.where(gm > g, 1, jnp.where((gm == g) & (n_idx > m), 1, 0))
        sel = (valid & (cnt < MOBA_TOPK)) | (n_idx == tok_blk)
        bias_heads.append(jnp.where(sel, 0.0, NEG_BIG))
    zeros48 = jnp.zeros((HEAD_DIM - 16, tm), F32)
    pair_rows = []
    for hd in range(GROUP_HEADS):
        pair_rows += [bias_heads[hd], zeros48]
    store_tiles(bias_o, jnp.concatenate(pair_rows, axis=0))

    store_tiles(qf_o, q_feature_major(wqf_ref, gqf_ref, False))
    kf_o[0] = k_token_major(wkf_ref, gkf_ref, False).astype(BF16)
    store_tiles(vf_o, _dot_nt(wvf_ref[...], hb))

    f = _dot(hb, wf_ref[...]) + bf_ref[...]
    logf = jnp.minimum(f, 0.0) - jnp.log1p(jnp.exp(-jnp.abs(f)))
    r_i = lax.broadcasted_iota(jnp.int32, (tm, tm), 0)
    c_i = lax.broadcasted_iota(jnp.int32, (tm, tm), 1)
    tri = jnp.where(r_i >= c_i, 1.0, 0.0).astype(F32)
    cum = _dot(tri, logf, precision=HIGHEST) + carry_sc[...]
    carry_sc[...] = cum[tm - 1:tm, :]
    nc = cum * (-LOG2E)
    hi = nc.astype(BF16)
    rem = nc - hi.astype(F32)
    mid = rem.astype(BF16)
    lo = (rem - mid.astype(F32)).astype(BF16)
    ka = _dot(hi, place_ref[0]) + _dot(mid, place_ref[1]) + _dot(lo, place_ref[2])
    ka_o[0] = ka.astype(BF16)


def _inproj(x, sc, sh, wts, consts, tm):
    bsz, s, d = x.shape
    nt = s // ATT_TILE
    grid = (bsz, s // tm)
    wqm, wkm, wvm, wqf, wkf, wvf, wf, bfor, gqm, gkm, gqf, gkf = wts
    cos_t, sin_t, rc, rs1, rs2, jmat, place = consts

    def const(a):
        nd = a.ndim
        return pl.BlockSpec(a.shape, lambda b, i, _n=nd: (0,) * _n,
                            pipeline_mode=pl.Buffered(1))

    mod_spec = pl.BlockSpec((None, 1, d), lambda b, i: (b, 0, 0))
    in_specs = [pl.BlockSpec((1, tm, d), lambda b, i: (b, i, 0)), mod_spec, mod_spec]
    in_specs += [const(a) for a in (wqm, wkm, wvm, wqf, wkf, wvf, wf, bfor,
                                    gqm, gkm, gqf, gkf)]
    in_specs += [pl.BlockSpec((ROPE_HALF, tm), lambda b, i: (0, i)),
                 pl.BlockSpec((ROPE_HALF, tm), lambda b, i: (0, i)),
                 pl.BlockSpec((tm, LANES), lambda b, i: (i, 0)),
                 pl.BlockSpec((tm, LANES), lambda b, i: (i, 0)),
                 pl.BlockSpec((tm, LANES), lambda b, i: (i, 0)),
                 const(jmat), const(place)]

    tiles = tm // ATT_TILE
    feat_spec = pl.BlockSpec((1, tiles, GROUP_WIDTH, ATT_TILE), lambda b, i: (b, i, 0, 0))
    tok_spec = pl.BlockSpec((1, tm, GROUP_WIDTH), lambda b, i: (b, i, 0))
    feat_shape = jax.ShapeDtypeStruct((bsz, nt, GROUP_WIDTH, ATT_TILE), BF16)
    tok_shape = jax.ShapeDtypeStruct((bsz, s, GROUP_WIDTH), BF16)
    out_specs = [feat_spec, tok_spec, feat_spec, feat_spec,
                 feat_spec, tok_spec, feat_spec, tok_spec]
    out_shape = [feat_shape, tok_shape, feat_shape, feat_shape,
                 feat_shape, tok_shape, feat_shape, tok_shape]
    return pl.pallas_call(
        functools.partial(_inproj_kernel, tm=tm),
        grid=grid,
        in_specs=in_specs,
        out_specs=out_specs,
        out_shape=out_shape,
        scratch_shapes=[pltpu.VMEM((GROUP_HEADS * 16, GROUP_WIDTH), F32),
                        pltpu.VMEM((1, LANES), F32)],
        compiler_params=pltpu.CompilerParams(
            dimension_semantics=("arbitrary", "arbitrary"),
            vmem_limit_bytes=VMEM_LIMIT),
        name="inproj",
    )(x, sc, sh, wqm, wkm, wvm, wqf, wkf, wvf, wf, bfor, gqm, gkm, gqf, gkf,
      cos_t, sin_t, rc, rs1, rs2, jmat, place)


def _attn_kernel(q_ref, qa_ref, k_ref, ka_ref, v_ref, o_ref, *, t):
    i = pl.program_id(2)
    q = q_ref[0, 0]
    qa = qa_ref[0, 0]
    row = lax.broadcasted_iota(jnp.int32, (PAIR_WIDTH, t), 0)
    rhs = []
    for e in range(2):
        keep = (row >= e * HEAD_DIM) & (row < (e + 1) * HEAD_DIM)
        zero = jnp.zeros_like(q)
        rhs.append(jnp.concatenate([jnp.where(keep, q, zero), jnp.where(keep, qa, zero)],
                                   axis=0))

    def tile_scores(j):
        start = pl.multiple_of(j * t, t)
        lhs = jnp.concatenate([k_ref[0, pl.ds(start, t), :], ka_ref[0, pl.ds(start, t), :]],
                              axis=1)
        return [_dot(lhs, rhs[e]) for e in range(2)]

    def values(j, e):
        return v_ref[0, j, e * HEAD_DIM:(e + 1) * HEAD_DIM, :]

    key_i = lax.broadcasted_iota(jnp.int32, (t, t), 0)
    qry_i = lax.broadcasted_iota(jnp.int32, (t, t), 1)
    causal = key_i <= qry_i
    state = []
    for e, s_t in enumerate(tile_scores(i)):
        s_t = jnp.where(causal, s_t, NEG_BIG)
        m = jnp.max(s_t, axis=0, keepdims=True)
        p = jnp.exp2(s_t - m)
        l = jnp.sum(p, axis=0, keepdims=True)
        acc = _dot(values(i, e), p.astype(BF16))
        state += [m, l, acc]

    def body(j, carry):
        out = []
        for e, s_t in enumerate(tile_scores(j)):
            m, l, acc = carry[3 * e:3 * e + 3]
            m_new = jnp.maximum(m, jnp.max(s_t, axis=0, keepdims=True))
            alpha = jnp.exp2(m - m_new)
            p = jnp.exp2(s_t - m_new)
            l = alpha * l + jnp.sum(p, axis=0, keepdims=True)
            acc = alpha * acc + _dot(values(j, e), p.astype(BF16))
            out += [m_new, l, acc]
        return tuple(out)

    state = lax.fori_loop(0, i, body, tuple(state))
    o_t = jnp.concatenate([state[2] / state[1], state[5] / state[4]], axis=0)
    o_ref[0] = o_t.T


def _attention(q_t, qa, k, ka, v_t, *, qa_const, ka_const):
    bsz, nt, _, t = q_t.shape
    s = k.shape[1]
    grid = (bsz, N_PAIRS, nt)
    if qa_const:
        qa_spec = pl.BlockSpec((1, 1, PAIR_WIDTH, t), lambda b, p, i: (0, 0, 0, 0))
    else:
        qa_spec = pl.BlockSpec((1, 1, PAIR_WIDTH, t), lambda b, p, i: (b, i, p, 0))
    if ka_const:
        ka_spec = pl.BlockSpec((1, s, PAIR_WIDTH), lambda b, p, i: (0, 0, 0))
    else:
        ka_spec = pl.BlockSpec((1, s, PAIR_WIDTH), lambda b, p, i: (b, 0, p))
    return pl.pallas_call(
        functools.partial(_attn_kernel, t=t),
        grid=grid,
        in_specs=[pl.BlockSpec((1, 1, PAIR_WIDTH, t), lambda b, p, i: (b, i, p, 0)),
                  qa_spec,
                  pl.BlockSpec((1, s, PAIR_WIDTH), lambda b, p, i: (b, 0, p)),
                  ka_spec,
                  pl.BlockSpec((1, nt, PAIR_WIDTH, t), lambda b, p, i: (b, 0, p, 0))],
        out_specs=pl.BlockSpec((1, t, PAIR_WIDTH), lambda b, p, i: (b, i, p)),
        out_shape=jax.ShapeDtypeStruct((bsz, s, GROUP_WIDTH), F32),
        compiler_params=pltpu.CompilerParams(
            dimension_semantics=("arbitrary", "arbitrary", "arbitrary"),
            vmem_limit_bytes=VMEM_LIMIT),
        name="attn_fox" if qa_const else "attn_moba",
    )(q_t, qa, k, ka, v_t)


def _post_kernel(x_ref, om_ref, of_ref, ga_ref, scm_ref, shm_ref, gm_ref,
                 gom_ref, gof_ref, wout_ref, w1_ref, w2_ref, o_ref, *, ff_chunk):
    def rms(v):
        return v * lax.rsqrt(jnp.mean(v * v, axis=-1, keepdims=True) + EPS)

    mixed = jnp.concatenate([(rms(om_ref[0]) * gom_ref[...]).astype(BF16),
                             (rms(of_ref[0]) * gof_ref[...]).astype(BF16)], axis=1)
    x1 = x_ref[0] + ga_ref[...] * _dot(mixed, wout_ref[...])
    h = (rms(x1) * (1.0 + scm_ref[...]) + shm_ref[...]).astype(BF16)
    d_ff = w1_ref.shape[1]
    y = None
    for c in range(d_ff // ff_chunk):
        hid = _dot(h, w1_ref[:, c * ff_chunk:(c + 1) * ff_chunk])
        hid = jnp.square(jnp.maximum(hid, 0.0)).astype(BF16)
        part = _dot(hid, w2_ref[c * ff_chunk:(c + 1) * ff_chunk, :])
        y = part if y is None else y + part
    o_ref[0] = x1 + gm_ref[...] * y


def _post(x, o_m, o_f, mod4, g_out_m, g_out_f, w_out, w_ff1, w_ff2, tm):
    bsz, s, d = x.shape
    d_ff = w_ff1.shape[1]

    def const(a):
        nd = a.ndim
        return pl.BlockSpec(a.shape, lambda b, i, _n=nd: (0,) * _n,
                            pipeline_mode=pl.Buffered(1))

    def mod_spec(k):
        return pl.BlockSpec((None, None, 1, d), lambda b, i, _k=k: (b, _k, 0, 0))

    tok = lambda w: pl.BlockSpec((1, tm, w), lambda b, i: (b, i, 0))
    return pl.pallas_call(
        functools.partial(_post_kernel, ff_chunk=1024),
        grid=(bsz, s // tm),
        in_specs=[tok(d), tok(GROUP_WIDTH), tok(GROUP_WIDTH),
                  mod_spec(2), mod_spec(4), mod_spec(3), mod_spec(5),
                  const(g_out_m), const(g_out_f), const(w_out), const(w_ff1), const(w_ff2)],
        out_specs=tok(d),
        out_shape=jax.ShapeDtypeStruct((bsz, s, d), F32),
        compiler_params=pltpu.CompilerParams(
            dimension_semantics=("arbitrary", "arbitrary"),
            vmem_limit_bytes=VMEM_LIMIT),
        name="post",
    )(x, o_m, o_f, mod4, mod4, mod4, mod4, g_out_m, g_out_f, w_out, w_ff1, w_ff2)


def _rope_tables(s):
    inv_freq = ROPE_THETA ** (-np.arange(0, ROPE_DIM, 2, dtype=np.float64) / ROPE_DIM)
    ang = np.arange(s, dtype=np.float64)[:, None] * inv_freq[None, :]
    cos, sin = np.cos(ang), np.sin(ang)
    rc = np.ones((s, LANES)); rs1 = np.zeros((s, LANES)); rs2 = np.zeros((s, LANES))
    for base in (0, HEAD_DIM):
        rc[:, base:base + ROPE_HALF] = cos
        rc[:, base + ROPE_HALF:base + ROPE_DIM] = cos
        rs2[:, base:base + ROPE_HALF] = -sin
        rs1[:, base + ROPE_HALF:base + ROPE_DIM] = sin
    f = lambda a: jnp.asarray(a, dtype=F32)
    return f(cos.T), f(sin.T), f(rc), f(rs1), f(rs2)


def _static_mats(s, t):
    idx = np.arange(256)
    jmat = (idx[:, None] // HEAD_DIM == idx[None, :] // HEAD_DIM) / HEAD_DIM
    place = np.zeros((3, LANES, GROUP_WIDTH))
    for part in range(3):
        for hd in range(GROUP_HEADS):
            place[part, hd, (hd // 2) * PAIR_WIDTH + (hd % 2) * HEAD_DIM + part] = 1.0
    qa_fox = np.zeros((1, 1, PAIR_WIDTH, t))
    for e in range(2):
        qa_fox[0, 0, e * HEAD_DIM:e * HEAD_DIM + 3, :] = 1.0
    ka_moba = np.zeros((1, s, PAIR_WIDTH))
    blk = np.arange(s) // MOBA_BLOCK
    for e in range(2):
        ka_moba[0, np.arange(s), e * HEAD_DIM + blk] = 1.0
    b = lambda a: jnp.asarray(a, dtype=BF16)
    return b(jmat), b(place), b(qa_fox), b(ka_moba)


def kernel(x, c, w_ada, b_ada, w_in, b_forget, g_qn_moba, g_kn_moba, g_qn_fox, g_kn_fox,
           g_out_moba, g_out_fox, w_out, w_ff1, w_ff2):
    bsz, s, d = x.shape
    depth = w_ada.shape[0]
    w = GROUP_WIDTH
    assert s % MOBA_BLOCK == 0 and s // MOBA_BLOCK <= 16
    tm_in = 256
    tm_post = 512
    cos_t, sin_t, rc, rs1, rs2 = _rope_tables(s)
    jmat, place, qa_fox, ka_moba = _static_mats(s, ATT_TILE)
    q_scale = HEAD_DIM ** -0.5 * LOG2E

    for l in range(depth):
        mod = _adaln(c, w_ada[l], b_ada[l])
        mod4 = mod.reshape(bsz, 6, 1, d)
        mod3 = mod.reshape(bsz * 6, 1, d)

        wl = w_in[l]
        col = lambda k: wl[:, k * w:(k + 1) * w]
        wf = jnp.zeros((d, LANES), F32).at[:, :GROUP_HEADS].set(wl[:, 6 * w:]).astype(BF16)
        bfor = jnp.zeros((1, LANES), F32).at[0, :GROUP_HEADS].set(b_forget[l])
        q_gain = lambda g: jnp.broadcast_to((g * q_scale)[:, None], (HEAD_DIM, tm_in))
        k_gain = lambda g: jnp.tile(g, GROUP_HEADS).reshape(1, w)
        wts = (col(0).T.astype(BF16), col(1).astype(BF16), col(2).T.astype(BF16),
               col(3).T.astype(BF16), col(4).astype(BF16), col(5).T.astype(BF16),
               wf, bfor,
               q_gain(g_qn_moba[l]), k_gain(g_kn_moba[l]),
               q_gain(g_qn_fox[l]), k_gain(g_kn_fox[l]))
        consts = (cos_t, sin_t, rc, rs1, rs2, jmat, place)

        sc_a = mod3.reshape(bsz, 6, 1, d)[:, 1]
        sh_a = mod3.reshape(bsz, 6, 1, d)[:, 0]
        qm_t, km, vm_t, bias_t, qf_t, kf, vf_t, ka_f = _inproj(
            x, sc_a, sh_a, wts, consts, tm_in)

        o_m = _attention(qm_t, bias_t, km, ka_moba, vm_t, qa_const=False, ka_const=True)
        o_f = _attention(qf_t, qa_fox, kf, ka_f, vf_t, qa_const=True, ka_const=False)

        x = _post(x, o_m, o_f, mod4,
                  g_out_moba[l].reshape(1, w), g_out_fox[l].reshape(1, w),
                  w_out[l].astype(BF16), w_ff1[l].astype(BF16), w_ff2[l].astype(BF16),
                  tm_post)
    return x
```

```python
import functools
import math

import numpy as np
import jax
import jax.numpy as jnp
from jax import lax
from jax.experimental import pallas as pl
from jax.experimental.pallas import tpu as pltpu

F32 = jnp.float32
BF16 = jnp.bfloat16
HIGHEST = lax.Precision.HIGHEST

HEAD_DIM = 64
GROUP_HEADS = 8
GROUP_WIDTH = GROUP_HEADS * HEAD_DIM
PAIR_WIDTH = 2 * HEAD_DIM
N_PAIRS = GROUP_HEADS // 2
MOBA_BLOCK = 256
MOBA_TOPK = 3
ROPE_THETA = 500000.0
ROPE_DIM = HEAD_DIM // 4
ROPE_HALF = ROPE_DIM // 2
EPS = 1e-6
LOG2E = math.log2(math.e)
NEG_BIG = -1e30
ATT_TILE = 512
LANES = 128
VMEM_LIMIT = 48 * 1024 * 1024


def _dot(a, b, precision=None):
    return jnp.dot(a, b, preferred_element_type=F32, precision=precision)


def _dot_nt(a, b, precision=None):
    return lax.dot_general(a, b, (((1,), (1,)), ((), ())),
                           preferred_element_type=F32, precision=precision)


def _adaln_kernel(c_ref, w_ref, b_ref, o_ref):
    c = c_ref[...]
    s = c / (1.0 + jnp.exp(-c))
    o_ref[...] = _dot(s, w_ref[...], precision=HIGHEST) + b_ref[...]


def _adaln(c, w_ada, b_ada):
    bsz, d = c.shape
    n = w_ada.shape[1]
    return pl.pallas_call(
        _adaln_kernel,
        grid=(n // d,),
        in_specs=[pl.BlockSpec((bsz, d), lambda j: (0, 0)),
                  pl.BlockSpec((d, d), lambda j: (0, j)),
                  pl.BlockSpec((1, d), lambda j: (0, j))],
        out_specs=pl.BlockSpec((bsz, d), lambda j: (0, j)),
        out_shape=jax.ShapeDtypeStruct((bsz, n), F32),
        compiler_params=pltpu.CompilerParams(dimension_semantics=("arbitrary",)),
        name="adaln",
    )(c, w_ada, b_ada.reshape(1, n))


def _inproj_kernel(x_ref, sc_ref, sh_ref,
                   wqm_ref, wkm_ref, wvm_ref, wqf_ref, wkf_ref, wvf_ref, wf_ref, bf_ref,
                   gqm_ref, gkm_ref, gqf_ref, gkf_ref,
                   cos_t_ref, sin_t_ref, rc_ref, rs1_ref, rs2_ref,
                   j_ref, place_ref,
                   qm_o, km_o, vm_o, bias_o, qf_o, kf_o, vf_o, ka_o,
                   kmean_sc, carry_sc, *, tm):
    i = pl.program_id(1)
    nb = tm // MOBA_BLOCK

    @pl.when(i == 0)
    def _():
        kmean_sc[...] = jnp.zeros_like(kmean_sc)
        carry_sc[...] = jnp.zeros_like(carry_sc)

    x = x_ref[0]
    ms = jnp.mean(x * x, axis=-1, keepdims=True)
    h = x * lax.rsqrt(ms + EPS) * (1.0 + sc_ref[...]) + sh_ref[...]
    hb = h.astype(BF16)

    def q_feature_major(w_ref, g_ref, rope):
        q_t = _dot_nt(w_ref[...], hb)
        gain = g_ref[...]
        heads = []
        for hd in range(GROUP_HEADS):
            xh = q_t[hd * HEAD_DIM:(hd + 1) * HEAD_DIM, :]
            ss = jnp.sum(xh * xh, axis=0, keepdims=True)
            xh = xh * lax.rsqrt(ss * (1.0 / HEAD_DIM) + EPS) * gain
            if rope:
                x1 = xh[0:ROPE_HALF, :]
                x2 = xh[ROPE_HALF:ROPE_DIM, :]
                cs = cos_t_ref[...]
                sn = sin_t_ref[...]
                xh = jnp.concatenate(
                    [x1 * cs - x2 * sn, x2 * cs + x1 * sn, xh[ROPE_DIM:, :]], axis=0)
            heads.append(xh)
        return jnp.concatenate(heads, axis=0)

    def k_token_major(w_ref, g_ref, rope):
        k = _dot(hb, w_ref[...])
        chunks = []
        for c in range(GROUP_WIDTH // 256):
            seg = k[:, c * 256:(c + 1) * 256]
            msq = _dot((seg * seg).astype(BF16), j_ref[...])
            chunks.append(seg * lax.rsqrt(msq + EPS) * g_ref[:, c * 256:(c + 1) * 256])
        k = jnp.concatenate(chunks, axis=1)
        if rope:
            outs = []
            for c in range(GROUP_WIDTH // LANES):
                seg = k[:, c * LANES:(c + 1) * LANES]
                outs.append(seg * rc_ref[...]
                            + pltpu.roll(seg, ROPE_HALF, 1) * rs1_ref[...]
                            + pltpu.roll(seg, LANES - ROPE_HALF, 1) * rs2_ref[...])
            k = jnp.concatenate(outs, axis=1)
        return k

    def store_tiles(o_ref, val_t):
        for t in range(tm // ATT_TILE):
            o_ref[0, t] = val_t[:, t * ATT_TILE:(t + 1) * ATT_TILE].astype(o_ref.dtype)

    qm_t = q_feature_major(wqm_ref, gqm_ref, True)
    store_tiles(qm_o, qm_t)
    km = k_token_major(wkm_ref, gkm_ref, True)
    km_o[0] = km.astype(BF16)
    store_tiles(vm_o, _dot_nt(wvm_ref[...], hb))

    blk0 = i * nb
    lane_head = lax.broadcasted_iota(jnp.int32, (1, GROUP_WIDTH), 1) // HEAD_DIM
    for sb in range(nb):
        kmean = jnp.mean(km[sb * MOBA_BLOCK:(sb + 1) * MOBA_BLOCK, :], axis=0, keepdims=True)
        for hd in range(GROUP_HEADS):
            row = hd * 16 + blk0 + sb
            kmean_sc[pl.ds(row, 1), :] = jnp.where(lane_head == hd, kmean, 0.0)

    gate_t = _dot(kmean_sc[...], qm_t, precision=HIGHEST)
    tok_blk = blk0 + lax.broadcasted_iota(jnp.int32, (16, tm), 1) // MOBA_BLOCK
    n_idx = lax.broadcasted_iota(jnp.int32, (16, tm), 0)
    valid = n_idx < tok_blk
    bias_heads = []
    for hd in range(GROUP_HEADS):
        g = jnp.where(valid, gate_t[hd * 16:(hd + 1) * 16, :], -jnp.inf)
        cnt = jnp.zeros((16, tm), jnp.int32)
        for m in range(16):
            gm = g[m:m + 1, :]
            cnt = cnt + jnp.where(gm > g, 1, jnp.where((gm == g) & (n_idx > m), 1, 0))
        sel = (valid & (cnt < MOBA_TOPK)) | (n_idx == tok_blk)
        bias_heads.append(jnp.where(sel, 0.0, NEG_BIG))
    zeros48 = jnp.zeros((HEAD_DIM - 16, tm), F32)
    pair_rows = []
    for hd in range(GROUP_HEADS):
        pair_rows += [bias_heads[hd], zeros48]
    store_tiles(bias_o, jnp.concatenate(pair_rows, axis=0))

    store_tiles(qf_o, q_feature_major(wqf_ref, gqf_ref, False))
    kf_o[0] = k_token_major(wkf_ref, gkf_ref, False).astype(BF16)
    store_tiles(vf_o, _dot_nt(wvf_ref[...], hb))

    f = _dot(hb, wf_ref[...]) + bf_ref[...]
    logf = jnp.minimum(f, 0.0) - jnp.log1p(jnp.exp(-jnp.abs(f)))
    r_i = lax.broadcasted_iota(jnp.int32, (tm, tm), 0)
    c_i = lax.broadcasted_iota(jnp.int32, (tm, tm), 1)
    tri = jnp.where(r_i >= c_i, 1.0, 0.0).astype(F32)
    cum = _dot(tri, logf, precision=HIGHEST) + carry_sc[...]
    carry_sc[...] = cum[tm - 1:tm, :]
    nc = cum * (-LOG2E)
    hi = nc.astype(BF16)
    rem = nc - hi.astype(F32)
    mid = rem.astype(BF16)
    lo = (rem - mid.astype(F32)).astype(BF16)
    ka = _dot(hi, place_ref[0]) + _dot(mid, place_ref[1]) + _dot(lo, place_ref[2])
    ka_o[0] = ka.astype(BF16)


def _inproj(x, sc, sh, wts, consts, tm):
    bsz, s, d = x.shape
    nt = s // ATT_TILE
    grid = (bsz, s // tm)
    wqm, wkm, wvm, wqf, wkf, wvf, wf, bfor, gqm, gkm, gqf, gkf = wts
    cos_t, sin_t, rc, rs1, rs2, jmat, place = consts

    def const(a):
        nd = a.ndim
        return pl.BlockSpec(a.shape, lambda b, i, _n=nd: (0,) * _n,
                            pipeline_mode=pl.Buffered(1))

    mod_spec = pl.BlockSpec((None, 1, d), lambda b, i: (b, 0, 0))
    in_specs = [pl.BlockSpec((1, tm, d), lambda b, i: (b, i, 0)), mod_spec, mod_spec]
    in_specs += [const(a) for a in (wqm, wkm, wvm, wqf, wkf, wvf, wf, bfor,
                                    gqm, gkm, gqf, gkf)]
    in_specs += [pl.BlockSpec((ROPE_HALF, tm), lambda b, i: (0, i)),
                 pl.BlockSpec((ROPE_HALF, tm), lambda b, i: (0, i)),
                 pl.BlockSpec((tm, LANES), lambda b, i: (i, 0)),
                 pl.BlockSpec((tm, LANES), lambda b, i: (i, 0)),
                 pl.BlockSpec((tm, LANES), lambda b, i: (i, 0)),
                 const(jmat), const(place)]

    tiles = tm // ATT_TILE
    feat_spec = pl.BlockSpec((1, tiles, GROUP_WIDTH, ATT_TILE), lambda b, i: (b, i, 0, 0))
    tok_spec = pl.BlockSpec((1, tm, GROUP_WIDTH), lambda b, i: (b, i, 0))
    feat_shape = jax.ShapeDtypeStruct((bsz, nt, GROUP_WIDTH, ATT_TILE), BF16)
    tok_shape = jax.ShapeDtypeStruct((bsz, s, GROUP_WIDTH), BF16)
    out_specs = [feat_spec, tok_spec, feat_spec, feat_spec,
                 feat_spec, tok_spec, feat_spec, tok_spec]
    out_shape = [feat_shape, tok_shape, feat_shape, feat_shape,
                 feat_shape, tok_shape, feat_shape, tok_shape]
    return pl.pallas_call(
        functools.partial(_inproj_kernel, tm=tm),
        grid=grid,
        in_specs=in_specs,
        out_specs=out_specs,
        out_shape=out_shape,
        scratch_shapes=[pltpu.VMEM((GROUP_HEADS * 16, GROUP_WIDTH), F32),
                        pltpu.VMEM((1, LANES), F32)],
        compiler_params=pltpu.CompilerParams(
            dimension_semantics=("arbitrary", "arbitrary"),
            vmem_limit_bytes=VMEM_LIMIT),
        name="inproj",
    )(x, sc, sh, wqm, wkm, wvm, wqf, wkf, wvf, wf, bfor, gqm, gkm, gqf, gkf,
      cos_t, sin_t, rc, rs1, rs2, jmat, place)


def _attn_kernel(q_ref, qa_ref, k_ref, ka_ref, v_ref, o_ref, *, t):
    i = pl.program_id(2)
    q = q_ref[0, 0]
    qa = qa_ref[0, 0]
    row = lax.broadcasted_iota(jnp.int32, (PAIR_WIDTH, t), 0)
    rhs = []
    for e in range(2):
        keep = (row >= e * HEAD_DIM) & (row < (e + 1) * HEAD_DIM)
        zero = jnp.zeros_like(q)
        rhs.append(jnp.concatenate([jnp.where(keep, q, zero), jnp.where(keep, qa, zero)],
                                   axis=0))

    def tile_scores(j):
        start = pl.multiple_of(j * t, t)
        lhs = jnp.concatenate([k_ref[0, pl.ds(start, t), :], ka_ref[0, pl.ds(start, t), :]],
                              axis=1)
        return [_dot(lhs, rhs[e]) for e in range(2)]

    def values(j, e):
        return v_ref[0, j, e * HEAD_DIM:(e + 1) * HEAD_DIM, :]

    key_i = lax.broadcasted_iota(jnp.int32, (t, t), 0)
    qry_i = lax.broadcasted_iota(jnp.int32, (t, t), 1)
    causal = key_i <= qry_i
    state = []
    for e, s_t in enumerate(tile_scores(i)):
        s_t = jnp.where(causal, s_t, NEG_BIG)
        m = jnp.max(s_t, axis=0, keepdims=True)
        p = jnp.exp2(s_t - m)
        l = jnp.sum(p, axis=0, keepdims=True)
        acc = _dot(values(i, e), p.astype(BF16))
        state += [m, l, acc]

    def body(j, carry):
        out = []
        for e, s_t in enumerate(tile_scores(j)):
            m, l, acc = carry[3 * e:3 * e + 3]
            m_new = jnp.maximum(m, jnp.max(s_t, axis=0, keepdims=True))
            alpha = jnp.exp2(m - m_new)
            p = jnp.exp2(s_t - m_new)
            l = alpha * l + jnp.sum(p, axis=0, keepdims=True)
            acc = alpha * acc + _dot(values(j, e), p.astype(BF16))
            out += [m_new, l, acc]
        return tuple(out)

    state = lax.fori_loop(0, i, body, tuple(state))
    o_t = jnp.concatenate([state[2] / state[1], state[5] / state[4]], axis=0)
    o_ref[0] = o_t.T


def _attention(q_t, qa, k, ka, v_t, *, qa_const, ka_const):
    bsz, nt, _, t = q_t.shape
    s = k.shape[1]
    grid = (bsz, N_PAIRS, nt)
    if qa_const:
        qa_spec = pl.BlockSpec((1, 1, PAIR_WIDTH, t), lambda b, p, i: (0, 0, 0, 0))
    else:
        qa_spec = pl.BlockSpec((1, 1, PAIR_WIDTH, t), lambda b, p, i: (b, i, p, 0))
    if ka_const:
        ka_spec = pl.BlockSpec((1, s, PAIR_WIDTH), lambda b, p, i: (0, 0, 0))
    else:
        ka_spec = pl.BlockSpec((1, s, PAIR_WIDTH), lambda b, p, i: (b, 0, p))
    return pl.pallas_call(
        functools.partial(_attn_kernel, t=t),
        grid=grid,
        in_specs=[pl.BlockSpec((1, 1, PAIR_WIDTH, t), lambda b, p, i: (b, i, p, 0)),
                  qa_spec,
                  pl.BlockSpec((1, s, PAIR_WIDTH), lambda b, p, i: (b, 0, p)),
                  ka_spec,
                  pl.BlockSpec((1, nt, PAIR_WIDTH, t), lambda b, p, i: (b, 0, p, 0))],
        out_specs=pl.BlockSpec((1, t, PAIR_WIDTH), lambda b, p, i: (b, i, p)),
        out_shape=jax.ShapeDtypeStruct((bsz, s, GROUP_WIDTH), F32),
        compiler_params=pltpu.CompilerParams(
            dimension_semantics=("arbitrary", "arbitrary", "arbitrary"),
            vmem_limit_bytes=VMEM_LIMIT),
        name="attn_fox" if qa_const else "attn_moba",
    )(q_t, qa, k, ka, v_t)


def _post_kernel(x_ref, om_ref, of_ref, ga_ref, scm_ref, shm_ref, gm_ref,
                 gom_ref, gof_ref, wout_ref, w1_ref, w2_ref, o_ref, *, ff_chunk):
    def rms(v):
        return v * lax.rsqrt(jnp.mean(v * v, axis=-1, keepdims=True) + EPS)

    mixed = jnp.concatenate([(rms(om_ref[0]) * gom_ref[...]).astype(BF16),
                             (rms(of_ref[0]) * gof_ref[...]).astype(BF16)], axis=1)
    x1 = x_ref[0] + ga_ref[...] * _dot(mixed, wout_ref[...])
    h = (rms(x1) * (1.0 + scm_ref[...]) + shm_ref[...]).astype(BF16)
    d_ff = w1_ref.shape[1]
    y = None
    for c in range(d_ff // ff_chunk):
        hid = _dot(h, w1_ref[:, c * ff_chunk:(c + 1) * ff_chunk])
        hid = jnp.square(jnp.maximum(hid, 0.0)).astype(BF16)
        part = _dot(hid, w2_ref[c * ff_chunk:(c + 1) * ff_chunk, :])
        y = part if y is None else y + part
    o_ref[0] = x1 + gm_ref[...] * y


def _post(x, o_m, o_f, mod4, g_out_m, g_out_f, w_out, w_ff1, w_ff2, tm):
    bsz, s, d = x.shape
    d_ff = w_ff1.shape[1]

    def const(a):
        nd = a.ndim
        return pl.BlockSpec(a.shape, lambda b, i, _n=nd: (0,) * _n,
                            pipeline_mode=pl.Buffered(1))

    def mod_spec(k):
        return pl.BlockSpec((None, None, 1, d), lambda b, i, _k=k: (b, _k, 0, 0))

    tok = lambda w: pl.BlockSpec((1, tm, w), lambda b, i: (b, i, 0))
    return pl.pallas_call(
        functools.partial(_post_kernel, ff_chunk=1024),
        grid=(bsz, s // tm),
        in_specs=[tok(d), tok(GROUP_WIDTH), tok(GROUP_WIDTH),
                  mod_spec(2), mod_spec(4), mod_spec(3), mod_spec(5),
                  const(g_out_m), const(g_out_f), const(w_out), const(w_ff1), const(w_ff2)],
        out_specs=tok(d),
        out_shape=jax.ShapeDtypeStruct((bsz, s, d), F32),
        compiler_params=pltpu.CompilerParams(
            dimension_semantics=("arbitrary", "arbitrary"),
            vmem_limit_bytes=VMEM_LIMIT),
        name="post",
    )(x, o_m, o_f, mod4, mod4, mod4, mod4, g_out_m, g_out_f, w_out, w_ff1, w_ff2)


def _rope_tables(s):
    inv_freq = ROPE_THETA ** (-np.arange(0, ROPE_DIM, 2, dtype=np.float64) / ROPE_DIM)
    ang = np.arange(s, dtype=np.float64)[:, None] * inv_freq[None, :]
    cos, sin = np.cos(ang), np.sin(ang)
    rc = np.ones((s, LANES)); rs1 = np.zeros((s, LANES)); rs2 = np.zeros((s, LANES))
    for base in (0, HEAD_DIM):
        rc[:, base:base + ROPE_HALF] = cos
        rc[:, base + ROPE_HALF:base + ROPE_DIM] = cos
        rs2[:, base:base + ROPE_HALF] = -sin
        rs1[:, base + ROPE_HALF:base + ROPE_DIM] = sin
    f = lambda a: jnp.asarray(a, dtype=F32)
    return f(cos.T), f(sin.T), f(rc), f(rs1), f(rs2)


def _static_mats(s, t):
    idx = np.arange(256)
    jmat = (idx[:, None] // HEAD_DIM == idx[None, :] // HEAD_DIM) / HEAD_DIM
    place = np.zeros((3, LANES, GROUP_WIDTH))
    for part in range(3):
        for hd in range(GROUP_HEADS):
            place[part, hd, (hd // 2) * PAIR_WIDTH + (hd % 2) * HEAD_DIM + part] = 1.0
    qa_fox = np.zeros((1, 1, PAIR_WIDTH, t))
    for e in range(2):
        qa_fox[0, 0, e * HEAD_DIM:e * HEAD_DIM + 3, :] = 1.0
    ka_moba = np.zeros((1, s, PAIR_WIDTH))
    blk = np.arange(s) // MOBA_BLOCK
    for e in range(2):
        ka_moba[0, np.arange(s), e * HEAD_DIM + blk] = 1.0
    b = lambda a: jnp.asarray(a, dtype=BF16)
    return b(jmat), b(place), b(qa_fox), b(ka_moba)


def kernel(x, c, w_ada, b_ada, w_in, b_forget, g_qn_moba, g_kn_moba, g_qn_fox, g_kn_fox,
           g_out_moba, g_out_fox, w_out, w_ff1, w_ff2):
    bsz, s, d = x.shape
    depth = w_ada.shape[0]
    w = GROUP_WIDTH
    assert s % MOBA_BLOCK == 0 and s // MOBA_BLOCK <= 16
    tm_in = 512
    tm_post = 512
    cos_t, sin_t, rc, rs1, rs2 = _rope_tables(s)
    jmat, place, qa_fox, ka_moba = _static_mats(s, ATT_TILE)
    q_scale = HEAD_DIM ** -0.5 * LOG2E

    for l in range(depth):
        mod = _adaln(c, w_ada[l], b_ada[l])
        mod4 = mod.reshape(bsz, 6, 1, d)
        mod3 = mod.reshape(bsz * 6, 1, d)

        wl = w_in[l]
        col = lambda k: wl[:, k * w:(k + 1) * w]
        wf = jnp.zeros((d, LANES), F32).at[:, :GROUP_HEADS].set(wl[:, 6 * w:]).astype(BF16)
        bfor = jnp.zeros((1, LANES), F32).at[0, :GROUP_HEADS].set(b_forget[l])
        q_gain = lambda g: jnp.broadcast_to((g * q_scale)[:, None], (HEAD_DIM, tm_in))
        k_gain = lambda g: jnp.tile(g, GROUP_HEADS).reshape(1, w)
        wts = (col(0).T.astype(BF16), col(1).astype(BF16), col(2).T.astype(BF16),
               col(3).T.astype(BF16), col(4).astype(BF16), col(5).T.astype(BF16),
               wf, bfor,
               q_gain(g_qn_moba[l]), k_gain(g_kn_moba[l]),
               q_gain(g_qn_fox[l]), k_gain(g_kn_fox[l]))
        consts = (cos_t, sin_t, rc, rs1, rs2, jmat, place)

        sc_a = mod3.reshape(bsz, 6, 1, d)[:, 1]
        sh_a = mod3.reshape(bsz, 6, 1, d)[:, 0]
        qm_t, km, vm_t, bias_t, qf_t, kf, vf_t, ka_f = _inproj(
            x, sc_a, sh_a, wts, consts, tm_in)

        o_m = _attention(qm_t, bias_t, km, ka_moba, vm_t, qa_const=False, ka_const=True)
        o_f = _attention(qf_t, qa_fox, kf, ka_f, vf_t, qa_const=True, ka_const=False)

        x = _post(x, o_m, o_f, mod4,
                  g_out_moba[l].reshape(1, w), g_out_fox[l].reshape(1, w),
                  w_out[l].astype(BF16), w_ff1[l].astype(BF16), w_ff2[l].astype(BF16),
                  tm_post)
    return x
```

```python
import functools
import math

import numpy as np
import jax
import jax.numpy as jnp
from jax import lax
from jax.experimental import pallas as pl
from jax.experimental.pallas import tpu as pltpu

F32 = jnp.float32
BF16 = jnp.bfloat16
HIGHEST = lax.Precision.HIGHEST

HEAD_DIM = 64
GROUP_HEADS = 8
GROUP_WIDTH = GROUP_HEADS * HEAD_DIM
PAIR_WIDTH = 2 * HEAD_DIM
N_PAIRS = GROUP_HEADS // 2
MOBA_BLOCK = 256
MOBA_TOPK = 3
ROPE_THETA = 500000.0
ROPE_DIM = HEAD_DIM // 4
ROPE_HALF = ROPE_DIM // 2
EPS = 1e-6
LOG2E = math.log2(math.e)
NEG_BIG = -1e30
MAX_SAFE_EXPONENT = 60.0
ATT_TILE = 512
EXP_ROWS = 64
LANES = 128
VMEM_LIMIT = 48 * 1024 * 1024


def _dot(a, b, precision=None):
    return jnp.dot(a, b, preferred_element_type=F32, precision=precision)


def _dot_nt(a, b, precision=None):
    return lax.dot_general(a, b, (((1,), (1,)), ((), ())),
                           preferred_element_type=F32, precision=precision)


def _adaln_kernel(c_ref, w_ref, b_ref, o_ref):
    c = c_ref[...]
    s = c / (1.0 + jnp.exp(-c))
    o_ref[...] = _dot(s, w_ref[...], precision=HIGHEST) + b_ref[...]


def _adaln(c, w_ada, b_ada):
    bsz, d = c.shape
    n = w_ada.shape[1]
    return pl.pallas_call(
        _adaln_kernel,
        grid=(n // d,),
        in_specs=[pl.BlockSpec((bsz, d), lambda j: (0, 0)),
                  pl.BlockSpec((d, d), lambda j: (0, j)),
                  pl.BlockSpec((1, d), lambda j: (0, j))],
        out_specs=pl.BlockSpec((bsz, d), lambda j: (0, j)),
        out_shape=jax.ShapeDtypeStruct((bsz, n), F32),
        compiler_params=pltpu.CompilerParams(dimension_semantics=("arbitrary",)),
        name="adaln",
    )(c, w_ada, b_ada.reshape(1, n))


def _inproj_kernel(x_ref, sc_ref, sh_ref,
                   wqm_ref, wkm_ref, wvm_ref, wqf_ref, wkf_ref, wvf_ref, wf_ref, bf_ref,
                   gqm_ref, gkm_ref, gqf_ref, gkf_ref,
                   cos_t_ref, sin_t_ref, rc_ref, rs1_ref, rs2_ref,
                   j_ref, place_ref, kones_ref,
                   qm_o, km_o, vm_o, bias_o, qf_o, kf_o, vf_o, ka_o, qa_o,
                   kmean_sc, carry_sc, *, tm):
    i = pl.program_id(1)
    nb = tm // MOBA_BLOCK

    @pl.when(i == 0)
    def _():
        kmean_sc[...] = jnp.zeros_like(kmean_sc)
        carry_sc[...] = jnp.zeros_like(carry_sc)

    x = x_ref[0]
    ms = jnp.mean(x * x, axis=-1, keepdims=True)
    h = x * lax.rsqrt(ms + EPS) * (1.0 + sc_ref[...]) + sh_ref[...]
    hb = h.astype(BF16)

    def q_feature_major(w_ref, g_ref, rope):
        q_t = _dot_nt(w_ref[...], hb)
        gain = g_ref[...]
        heads = []
        for hd in range(GROUP_HEADS):
            xh = q_t[hd * HEAD_DIM:(hd + 1) * HEAD_DIM, :]
            ss = jnp.sum(xh * xh, axis=0, keepdims=True)
            xh = xh * lax.rsqrt(ss * (1.0 / HEAD_DIM) + EPS) * gain
            if rope:
                x1 = xh[0:ROPE_HALF, :]
                x2 = xh[ROPE_HALF:ROPE_DIM, :]
                cs = cos_t_ref[...]
                sn = sin_t_ref[...]
                xh = jnp.concatenate(
                    [x1 * cs - x2 * sn, x2 * cs + x1 * sn, xh[ROPE_DIM:, :]], axis=0)
            heads.append(xh)
        return jnp.concatenate(heads, axis=0)

    def k_token_major(w_ref, g_ref, rope):
        k = _dot(hb, w_ref[...])
        chunks = []
        for c in range(GROUP_WIDTH // 256):
            seg = k[:, c * 256:(c + 1) * 256]
            msq = _dot((seg * seg).astype(BF16), j_ref[...])
            chunks.append(seg * lax.rsqrt(msq + EPS) * g_ref[:, c * 256:(c + 1) * 256])
        k = jnp.concatenate(chunks, axis=1)
        if rope:
            outs = []
            for c in range(GROUP_WIDTH // LANES):
                seg = k[:, c * LANES:(c + 1) * LANES]
                outs.append(seg * rc_ref[...]
                            + pltpu.roll(seg, ROPE_HALF, 1) * rs1_ref[...]
                            + pltpu.roll(seg, LANES - ROPE_HALF, 1) * rs2_ref[...])
            k = jnp.concatenate(outs, axis=1)
        return k

    def store_tiles(o_ref, val_t):
        for t in range(tm // ATT_TILE):
            o_ref[0, t] = val_t[:, t * ATT_TILE:(t + 1) * ATT_TILE].astype(o_ref.dtype)

    qm_t = q_feature_major(wqm_ref, gqm_ref, True)
    store_tiles(qm_o, qm_t)
    km = k_token_major(wkm_ref, gkm_ref, True)
    km_o[0] = km.astype(BF16)
    store_tiles(vm_o, _dot_nt(wvm_ref[...], hb))

    blk0 = i * nb
    lane_head = lax.broadcasted_iota(jnp.int32, (1, GROUP_WIDTH), 1) // HEAD_DIM
    for sb in range(nb):
        kmean = jnp.mean(km[sb * MOBA_BLOCK:(sb + 1) * MOBA_BLOCK, :], axis=0, keepdims=True)
        for hd in range(GROUP_HEADS):
            row = hd * 16 + blk0 + sb
            kmean_sc[pl.ds(row, 1), :] = jnp.where(lane_head == hd, kmean, 0.0)

    gate_t = _dot(kmean_sc[...], qm_t, precision=HIGHEST)
    tok_blk = blk0 + lax.broadcasted_iota(jnp.int32, (16, tm), 1) // MOBA_BLOCK
    n_idx = lax.broadcasted_iota(jnp.int32, (16, tm), 0)
    valid = n_idx < tok_blk
    bias_heads = []
    for hd in range(GROUP_HEADS):
        g = jnp.where(valid, gate_t[hd * 16:(hd + 1) * 16, :], -jnp.inf)
        cnt = jnp.zeros((16, tm), jnp.int32)
        for m in range(16):
            gm = g[m:m + 1, :]
            cnt = cnt + jnp.where(gm > g, 1, jnp.where((gm == g) & (n_idx > m), 1, 0))
        sel = (valid & (cnt < MOBA_TOPK)) | (n_idx == tok_blk)
        bias_heads.append(jnp.where(sel, 0.0, NEG_BIG))
    zeros48 = jnp.zeros((HEAD_DIM - 16, tm), F32)
    pair_rows = []
    for hd in range(GROUP_HEADS):
        pair_rows += [bias_heads[hd], zeros48]
    store_tiles(bias_o, jnp.concatenate(pair_rows, axis=0))

    store_tiles(qf_o, q_feature_major(wqf_ref, gqf_ref, False))
    kf_o[0] = k_token_major(wkf_ref, gkf_ref, False).astype(BF16)
    store_tiles(vf_o, _dot_nt(wvf_ref[...], hb))

    f = _dot(hb, wf_ref[...]) + bf_ref[...]
    logf = jnp.minimum(f, 0.0) - jnp.log1p(jnp.exp(-jnp.abs(f)))
    r_i = lax.broadcasted_iota(jnp.int32, (tm, tm), 0)
    c_i = lax.broadcasted_iota(jnp.int32, (tm, tm), 1)
    tri = jnp.where(r_i >= c_i, 1.0, 0.0).astype(F32)
    cum = _dot(tri, logf, precision=HIGHEST) + carry_sc[...]
    carry_sc[...] = cum[tm - 1:tm, :]
    nc = cum * (-LOG2E)
    hi = nc.astype(BF16)
    rem = nc - hi.astype(F32)
    mid = rem.astype(BF16)
    lo = (rem - mid.astype(F32)).astype(BF16)
    ka = (_dot(hi, place_ref[0]) + _dot(mid, place_ref[1]) + _dot(lo, place_ref[2])
          + kones_ref[...])
    ka_o[0] = ka.astype(BF16)

    nc_t = nc.T[0:GROUP_HEADS, :]
    hi_t = nc_t.astype(BF16).astype(F32)
    rem_t = nc_t - hi_t
    mid_t = rem_t.astype(BF16).astype(F32)
    lo_t = (rem_t - mid_t).astype(BF16).astype(F32)
    r8 = lax.broadcasted_iota(jnp.int32, (8, tm), 0)
    zeros56 = jnp.zeros((HEAD_DIM - 8, tm), F32)
    qa_rows = []
    for hd in range(GROUP_HEADS):
        blk = jnp.where(r8 < 3, 1.0,
                        jnp.where(r8 == 3, -hi_t[hd:hd + 1, :],
                                  jnp.where(r8 == 4, -mid_t[hd:hd + 1, :],
                                            jnp.where(r8 == 5, -lo_t[hd:hd + 1, :], 0.0))))
        qa_rows += [blk, zeros56]
    store_tiles(qa_o, jnp.concatenate(qa_rows, axis=0))


def _inproj(x, sc, sh, wts, consts, tm):
    bsz, s, d = x.shape
    nt = s // ATT_TILE
    grid = (bsz, s // tm)
    wqm, wkm, wvm, wqf, wkf, wvf, wf, bfor, gqm, gkm, gqf, gkf = wts
    cos_t, sin_t, rc, rs1, rs2, jmat, place, kones = consts

    def const(a):
        nd = a.ndim
        return pl.BlockSpec(a.shape, lambda b, i, _n=nd: (0,) * _n,
                            pipeline_mode=pl.Buffered(1))

    mod_spec = pl.BlockSpec((None, 1, d), lambda b, i: (b, 0, 0))
    in_specs = [pl.BlockSpec((1, tm, d), lambda b, i: (b, i, 0)), mod_spec, mod_spec]
    in_specs += [const(a) for a in (wqm, wkm, wvm, wqf, wkf, wvf, wf, bfor,
                                    gqm, gkm, gqf, gkf)]
    in_specs += [pl.BlockSpec((ROPE_HALF, tm), lambda b, i: (0, i)),
                 pl.BlockSpec((ROPE_HALF, tm), lambda b, i: (0, i)),
                 pl.BlockSpec((tm, LANES), lambda b, i: (i, 0)),
                 pl.BlockSpec((tm, LANES), lambda b, i: (i, 0)),
                 pl.BlockSpec((tm, LANES), lambda b, i: (i, 0)),
                 const(jmat), const(place), const(kones)]

    tiles = tm // ATT_TILE
    feat_spec = pl.BlockSpec((1, tiles, GROUP_WIDTH, ATT_TILE), lambda b, i: (b, i, 0, 0))
    tok_spec = pl.BlockSpec((1, tm, GROUP_WIDTH), lambda b, i: (b, i, 0))
    feat_shape = jax.ShapeDtypeStruct((bsz, nt, GROUP_WIDTH, ATT_TILE), BF16)
    tok_shape = jax.ShapeDtypeStruct((bsz, s, GROUP_WIDTH), BF16)
    out_specs = [feat_spec, tok_spec, feat_spec, feat_spec,
                 feat_spec, tok_spec, feat_spec, tok_spec, feat_spec]
    out_shape = [feat_shape, tok_shape, feat_shape, feat_shape,
                 feat_shape, tok_shape, feat_shape, tok_shape, feat_shape]
    return pl.pallas_call(
        functools.partial(_inproj_kernel, tm=tm),
        grid=grid,
        in_specs=in_specs,
        out_specs=out_specs,
        out_shape=out_shape,
        scratch_shapes=[pltpu.VMEM((GROUP_HEADS * 16, GROUP_WIDTH), F32),
                        pltpu.VMEM((1, LANES), F32)],
        compiler_params=pltpu.CompilerParams(
            dimension_semantics=("arbitrary", "arbitrary"),
            vmem_limit_bytes=VMEM_LIMIT),
        name="inproj",
    )(x, sc, sh, wqm, wkm, wvm, wqf, wkf, wvf, wf, bfor, gqm, gkm, gqf, gkf,
      cos_t, sin_t, rc, rs1, rs2, jmat, place, kones)


def _attn_kernel(q_ref, qa_ref, k_ref, ka_ref, v_ref, o_ref, *, t):
    i = pl.program_id(2)
    q = q_ref[0, 0]
    qa = qa_ref[0, 0]
    row = lax.broadcasted_iota(jnp.int32, (PAIR_WIDTH, t), 0)
    rhs = []
    for e in range(2):
        keep = (row >= e * HEAD_DIM) & (row < (e + 1) * HEAD_DIM)
        zero = jnp.zeros_like(q)
        rhs.append(jnp.concatenate([jnp.where(keep, q, zero), jnp.where(keep, qa, zero)],
                                   axis=0))

    def tile_scores(j):
        start = pl.multiple_of(j * t, t)
        lhs = jnp.concatenate([k_ref[0, pl.ds(start, t), :], ka_ref[0, pl.ds(start, t), :]],
                              axis=1)
        return [_dot(lhs, rhs[e]) for e in range(2)]

    def values(j, e):
        return v_ref[0, j, e * HEAD_DIM:(e + 1) * HEAD_DIM, :]

    key_i = lax.broadcasted_iota(jnp.int32, (t, t), 0)
    qry_i = lax.broadcasted_iota(jnp.int32, (t, t), 1)
    causal = key_i <= qry_i
    state = []
    for e, s_t in enumerate(tile_scores(i)):
        s_t = jnp.where(causal, s_t, NEG_BIG)
        m = jnp.max(s_t, axis=0, keepdims=True)
        p = jnp.exp2(s_t - m)
        l = jnp.sum(p, axis=0, keepdims=True)
        acc = _dot(values(i, e), p.astype(BF16))
        state += [m, l, acc]

    def body(j, carry):
        out = []
        for e, s_t in enumerate(tile_scores(j)):
            m, l, acc = carry[3 * e:3 * e + 3]
            m_new = jnp.maximum(m, jnp.max(s_t, axis=0, keepdims=True))
            alpha = jnp.exp2(m - m_new)
            p = jnp.exp2(s_t - m_new)
            l = alpha * l + jnp.sum(p, axis=0, keepdims=True)
            acc = alpha * acc + _dot(values(j, e), p.astype(BF16))
            out += [m_new, l, acc]
        return tuple(out)

    state = lax.fori_loop(0, i, body, tuple(state))
    o_t = jnp.concatenate([state[2] / state[1], state[5] / state[4]], axis=0)
    o_ref[0] = o_t.T


def _attn_bounded_kernel(q_ref, qa_ref, k_ref, ka_ref, v_ref, o_ref, s_sc, p_sc, *, t):
    i = pl.program_id(2)
    q = q_ref[0, 0]
    qa = qa_ref[0, 0]
    row = lax.broadcasted_iota(jnp.int32, (PAIR_WIDTH, t), 0)
    rhs = []
    for e in range(2):
        keep = (row >= e * HEAD_DIM) & (row < (e + 1) * HEAD_DIM)
        zero = jnp.zeros_like(q)
        rhs.append(jnp.concatenate([jnp.where(keep, q, zero), jnp.where(keep, qa, zero)],
                                   axis=0))
    ones = jnp.ones((16, t), BF16)

    def scores(j, e):
        start = pl.multiple_of(j * t, t)
        lhs = jnp.concatenate([k_ref[0, pl.ds(start, t), :], ka_ref[0, pl.ds(start, t), :]],
                              axis=1)
        return _dot(lhs, rhs[e])

    def exp_stage():
        for e in range(2):
            for r in range(0, t, EXP_ROWS):
                p_sc[e, r:r + EXP_ROWS, :] = jnp.exp2(s_sc[e, r:r + EXP_ROWS, :]).astype(BF16)

    def value_stage(j, acc):
        out = []
        for e in range(2):
            v_e = jnp.concatenate([v_ref[0, j, e * HEAD_DIM:(e + 1) * HEAD_DIM, :], ones],
                                  axis=0)
            out.append(acc[e] + _dot(v_e, p_sc[e]))
        return out

    key_i = lax.broadcasted_iota(jnp.int32, (t, t), 0)
    qry_i = lax.broadcasted_iota(jnp.int32, (t, t), 1)
    causal = key_i <= qry_i
    for e in range(2):
        s_sc[e] = jnp.where(causal, scores(i, e), NEG_BIG)
        p_sc[e] = jnp.zeros((t, t), BF16)
    acc = [jnp.zeros((HEAD_DIM + 16, t), F32)] * 2

    def body(j, carry):
        acc0, acc1, j_cur, j_prev = carry
        acc = value_stage(j_prev, [acc0, acc1])
        exp_stage()
        for e in range(2):
            s_sc[e] = scores(j, e)
        return acc[0], acc[1], j, j_cur

    acc0, acc1, j_cur, j_prev = lax.fori_loop(0, i, body, (acc[0], acc[1], i, i))
    acc = value_stage(j_prev, [acc0, acc1])
    exp_stage()
    acc = value_stage(j_cur, acc)
    o_t = jnp.concatenate([a[0:HEAD_DIM, :] / a[HEAD_DIM:HEAD_DIM + 1, :] for a in acc],
                          axis=0)
    o_ref[0] = o_t.T


def _attention(q_t, qa, k, ka, v_t, *, ka_const, bounded, name):
    bsz, nt, _, t = q_t.shape
    s = k.shape[1]
    grid = (bsz, N_PAIRS, nt)
    if ka_const:
        ka_spec = pl.BlockSpec((1, s, PAIR_WIDTH), lambda b, p, i: (0, 0, 0))
    else:
        ka_spec = pl.BlockSpec((1, s, PAIR_WIDTH), lambda b, p, i: (b, 0, p))
    body = _attn_bounded_kernel if bounded else _attn_kernel
    return pl.pallas_call(
        functools.partial(body, t=t),
        grid=grid,
        in_specs=[pl.BlockSpec((1, 1, PAIR_WIDTH, t), lambda b, p, i: (b, i, p, 0)),
                  pl.BlockSpec((1, 1, PAIR_WIDTH, t), lambda b, p, i: (b, i, p, 0)),
                  pl.BlockSpec((1, s, PAIR_WIDTH), lambda b, p, i: (b, 0, p)),
                  ka_spec,
                  pl.BlockSpec((1, nt, PAIR_WIDTH, t), lambda b, p, i: (b, 0, p, 0))],
        out_specs=pl.BlockSpec((1, t, PAIR_WIDTH), lambda b, p, i: (b, i, p)),
        out_shape=jax.ShapeDtypeStruct((bsz, s, GROUP_WIDTH), F32),
        scratch_shapes=([pltpu.VMEM((2, t, t), F32), pltpu.VMEM((2, t, t), BF16)]
                        if bounded else []),
        compiler_params=pltpu.CompilerParams(
            dimension_semantics=("arbitrary", "arbitrary", "arbitrary"),
            vmem_limit_bytes=VMEM_LIMIT),
        name=name,
    )(q_t, qa, k, ka, v_t)


def _attention_dispatch(g_q, g_k, q_t, qa, k, ka, v_t, *, ka_const, name):
    bound = HEAD_DIM * jnp.max(jnp.abs(g_q)) * jnp.max(jnp.abs(g_k))
    run = lambda bounded, suffix: functools.partial(
        _attention, ka_const=ka_const, bounded=bounded, name=name + suffix)
    return lax.cond(bound <= MAX_SAFE_EXPONENT,
                    run(True, "_bounded"), run(False, "_online"),
                    q_t, qa, k, ka, v_t)


def _post_kernel(x_ref, om_ref, of_ref, ga_ref, scm_ref, shm_ref, gm_ref,
                 gom_ref, gof_ref, wout_ref, w1_ref, w2_ref, o_ref, *, ff_chunk):
    def rms(v):
        return v * lax.rsqrt(jnp.mean(v * v, axis=-1, keepdims=True) + EPS)

    mixed = jnp.concatenate([(rms(om_ref[0]) * gom_ref[...]).astype(BF16),
                             (rms(of_ref[0]) * gof_ref[...]).astype(BF16)], axis=1)
    x1 = x_ref[0] + ga_ref[...] * _dot(mixed, wout_ref[...])
    h = (rms(x1) * (1.0 + scm_ref[...]) + shm_ref[...]).astype(BF16)
    d_ff = w1_ref.shape[1]
    y = None
    for c in range(d_ff // ff_chunk):
        hid = _dot(h, w1_ref[:, c * ff_chunk:(c + 1) * ff_chunk])
        hid = jnp.square(jnp.maximum(hid, 0.0)).astype(BF16)
        part = _dot(hid, w2_ref[c * ff_chunk:(c + 1) * ff_chunk, :])
        y = part if y is None else y + part
    o_ref[0] = x1 + gm_ref[...] * y


def _post(x, o_m, o_f, mod4, g_out_m, g_out_f, w_out, w_ff1, w_ff2, tm):
    bsz, s, d = x.shape
    d_ff = w_ff1.shape[1]

    def const(a):
        nd = a.ndim
        return pl.BlockSpec(a.shape, lambda b, i, _n=nd: (0,) * _n,
                            pipeline_mode=pl.Buffered(1))

    def mod_spec(k):
        return pl.BlockSpec((None, None, 1, d), lambda b, i, _k=k: (b, _k, 0, 0))

    tok = lambda w: pl.BlockSpec((1, tm, w), lambda b, i: (b, i, 0))
    return pl.pallas_call(
        functools.partial(_post_kernel, ff_chunk=1024),
        grid=(bsz, s // tm),
        in_specs=[tok(d), tok(GROUP_WIDTH), tok(GROUP_WIDTH),
                  mod_spec(2), mod_spec(4), mod_spec(3), mod_spec(5),
                  const(g_out_m), const(g_out_f), const(w_out), const(w_ff1), const(w_ff2)],
        out_specs=tok(d),
        out_shape=jax.ShapeDtypeStruct((bsz, s, d), F32),
        compiler_params=pltpu.CompilerParams(
            dimension_semantics=("arbitrary", "arbitrary"),
            vmem_limit_bytes=VMEM_LIMIT),
        name="post",
    )(x, o_m, o_f, mod4, mod4, mod4, mod4, g_out_m, g_out_f, w_out, w_ff1, w_ff2)


def _rope_tables(s):
    inv_freq = ROPE_THETA ** (-np.arange(0, ROPE_DIM, 2, dtype=np.float64) / ROPE_DIM)
    ang = np.arange(s, dtype=np.float64)[:, None] * inv_freq[None, :]
    cos, sin = np.cos(ang), np.sin(ang)
    rc = np.ones((s, LANES)); rs1 = np.zeros((s, LANES)); rs2 = np.zeros((s, LANES))
    for base in (0, HEAD_DIM):
        rc[:, base:base + ROPE_HALF] = cos
        rc[:, base + ROPE_HALF:base + ROPE_DIM] = cos
        rs2[:, base:base + ROPE_HALF] = -sin
        rs1[:, base + ROPE_HALF:base + ROPE_DIM] = sin
    f = lambda a: jnp.asarray(a, dtype=F32)
    return f(cos.T), f(sin.T), f(rc), f(rs1), f(rs2)


def _static_mats(s):
    idx = np.arange(256)
    jmat = (idx[:, None] // HEAD_DIM == idx[None, :] // HEAD_DIM) / HEAD_DIM
    place = np.zeros((3, LANES, GROUP_WIDTH))
    for part in range(3):
        for hd in range(GROUP_HEADS):
            place[part, hd, (hd // 2) * PAIR_WIDTH + (hd % 2) * HEAD_DIM + part] = 1.0
    kones = np.zeros((1, GROUP_WIDTH))
    for hd in range(GROUP_HEADS):
        kones[0, hd * HEAD_DIM + 3:hd * HEAD_DIM + 6] = 1.0
    ka_moba = np.zeros((1, s, PAIR_WIDTH))
    blk = np.arange(s) // MOBA_BLOCK
    for e in range(2):
        ka_moba[0, np.arange(s), e * HEAD_DIM + blk] = 1.0
    b = lambda a: jnp.asarray(a, dtype=BF16)
    return b(jmat), b(place), jnp.asarray(kones, dtype=F32), b(ka_moba)


def kernel(x, c, w_ada, b_ada, w_in, b_forget, g_qn_moba, g_kn_moba, g_qn_fox, g_kn_fox,
           g_out_moba, g_out_fox, w_out, w_ff1, w_ff2):
    bsz, s, d = x.shape
    depth = w_ada.shape[0]
    w = GROUP_WIDTH
    assert s % MOBA_BLOCK == 0 and s // MOBA_BLOCK <= 16
    tm_in = 512
    tm_post = 512
    cos_t, sin_t, rc, rs1, rs2 = _rope_tables(s)
    jmat, place, kones, ka_moba = _static_mats(s)
    q_scale = HEAD_DIM ** -0.5 * LOG2E

    for l in range(depth):
        mod = _adaln(c, w_ada[l], b_ada[l])
        mod4 = mod.reshape(bsz, 6, 1, d)

        wl = w_in[l]
        col = lambda k: wl[:, k * w:(k + 1) * w]
        wf = jnp.zeros((d, LANES), F32).at[:, :GROUP_HEADS].set(wl[:, 6 * w:]).astype(BF16)
        bfor = jnp.zeros((1, LANES), F32).at[0, :GROUP_HEADS].set(b_forget[l])
        q_gain = lambda g: jnp.broadcast_to((g * q_scale)[:, None], (HEAD_DIM, tm_in))
        k_gain = lambda g: jnp.tile(g, GROUP_HEADS).reshape(1, w)
        wts = (col(0).T.astype(BF16), col(1).astype(BF16), col(2).T.astype(BF16),
               col(3).T.astype(BF16), col(4).astype(BF16), col(5).T.astype(BF16),
               wf, bfor,
               q_gain(g_qn_moba[l]), k_gain(g_kn_moba[l]),
               q_gain(g_qn_fox[l]), k_gain(g_kn_fox[l]))
        consts = (cos_t, sin_t, rc, rs1, rs2, jmat, place, kones)

        qm_t, km, vm_t, bias_t, qf_t, kf, vf_t, ka_f, qa_f = _inproj(
            x, mod4[:, 1], mod4[:, 0], wts, consts, tm_in)

        o_m = _attention_dispatch(g_qn_moba[l] * q_scale, g_kn_moba[l],
                                  qm_t, bias_t, km, ka_moba, vm_t, ka_const=True,
                                  name="attn_moba")
        o_f = _attention_dispatch(g_qn_fox[l] * q_scale, g_kn_fox[l],
                                  qf_t, qa_f, kf, ka_f, vf_t, ka_const=False,
                                  name="attn_fox")

        x = _post(x, o_m, o_f, mod4,
                  g_out_moba[l].reshape(1, w), g_out_fox[l].reshape(1, w),
                  w_out[l].astype(BF16), w_ff1[l].astype(BF16), w_ff2[l].astype(BF16),
                  tm_post)
    return x
```

```python
import functools
import math

import numpy as np
import jax
import jax.numpy as jnp
from jax import lax
from jax.experimental import pallas as pl
from jax.experimental.pallas import tpu as pltpu

F32 = jnp.float32
BF16 = jnp.bfloat16
HIGHEST = lax.Precision.HIGHEST

HEAD_DIM = 64
GROUP_HEADS = 8
GROUP_WIDTH = GROUP_HEADS * HEAD_DIM
PAIR_WIDTH = 2 * HEAD_DIM
N_PAIRS = GROUP_HEADS // 2
MOBA_BLOCK = 256
MOBA_TOPK = 3
ROPE_THETA = 500000.0
ROPE_DIM = HEAD_DIM // 4
ROPE_HALF = ROPE_DIM // 2
EPS = 1e-6
LOG2E = math.log2(math.e)
NEG_BIG = -1e30
MAX_SAFE_EXPONENT = 60.0
ATT_TILE = 512
EXP_ROWS = 64
LANES = 128
VMEM_LIMIT = 48 * 1024 * 1024


def _dot(a, b, precision=None):
    return jnp.dot(a, b, preferred_element_type=F32, precision=precision)


def _dot_nt(a, b, precision=None):
    return lax.dot_general(a, b, (((1,), (1,)), ((), ())),
                           preferred_element_type=F32, precision=precision)


def _adaln_kernel(c_ref, w_ref, b_ref, o_ref):
    c = c_ref[...]
    s = c / (1.0 + jnp.exp(-c))
    o_ref[...] = _dot(s, w_ref[...], precision=HIGHEST) + b_ref[...]


def _adaln(c, w_ada, b_ada):
    bsz, d = c.shape
    n = w_ada.shape[1]
    return pl.pallas_call(
        _adaln_kernel,
        grid=(n // d,),
        in_specs=[pl.BlockSpec((bsz, d), lambda j: (0, 0)),
                  pl.BlockSpec((d, d), lambda j: (0, j)),
                  pl.BlockSpec((1, d), lambda j: (0, j))],
        out_specs=pl.BlockSpec((bsz, d), lambda j: (0, j)),
        out_shape=jax.ShapeDtypeStruct((bsz, n), F32),
        compiler_params=pltpu.CompilerParams(dimension_semantics=("arbitrary",)),
        name="adaln",
    )(c, w_ada, b_ada.reshape(1, n))


def _inproj_kernel(x_ref, sc_ref, sh_ref,
                   wqm_ref, wkm_ref, wvm_ref, wqf_ref, wkf_ref, wvf_ref, wf_ref, bf_ref,
                   gqm_ref, gkm_ref, gqf_ref, gkf_ref,
                   cos_t_ref, sin_t_ref, rc_ref, rs1_ref, rs2_ref,
                   j_ref, place_ref, kones_ref,
                   qm_o, km_o, vm_o, bias_o, qf_o, kf_o, vf_o, ka_o, qa_o,
                   kmean_sc, carry_sc, *, tm):
    i = pl.program_id(1)
    nb = tm // MOBA_BLOCK

    @pl.when(i == 0)
    def _():
        kmean_sc[...] = jnp.zeros_like(kmean_sc)
        carry_sc[...] = jnp.zeros_like(carry_sc)

    x = x_ref[0]
    ms = jnp.mean(x * x, axis=-1, keepdims=True)
    h = x * lax.rsqrt(ms + EPS) * (1.0 + sc_ref[...]) + sh_ref[...]
    hb = h.astype(BF16)

    def q_feature_major(w_ref, g_ref, rope):
        q_t = _dot_nt(w_ref[...], hb)
        gain = g_ref[...]
        heads = []
        for hd in range(GROUP_HEADS):
            xh = q_t[hd * HEAD_DIM:(hd + 1) * HEAD_DIM, :]
            ss = jnp.sum(xh * xh, axis=0, keepdims=True)
            xh = xh * lax.rsqrt(ss * (1.0 / HEAD_DIM) + EPS) * gain
            if rope:
                x1 = xh[0:ROPE_HALF, :]
                x2 = xh[ROPE_HALF:ROPE_DIM, :]
                cs = cos_t_ref[...]
                sn = sin_t_ref[...]
                xh = jnp.concatenate(
                    [x1 * cs - x2 * sn, x2 * cs + x1 * sn, xh[ROPE_DIM:, :]], axis=0)
            heads.append(xh)
        return jnp.concatenate(heads, axis=0)

    def k_token_major(w_ref, g_ref, rope):
        k = _dot(hb, w_ref[...])
        chunks = []
        for c in range(GROUP_WIDTH // 256):
            seg = k[:, c * 256:(c + 1) * 256]
            msq = _dot((seg * seg).astype(BF16), j_ref[...])
            chunks.append(seg * lax.rsqrt(msq + EPS) * g_ref[:, c * 256:(c + 1) * 256])
        k = jnp.concatenate(chunks, axis=1)
        if rope:
            outs = []
            for c in range(GROUP_WIDTH // LANES):
                seg = k[:, c * LANES:(c + 1) * LANES]
                outs.append(seg * rc_ref[...]
                            + pltpu.roll(seg, ROPE_HALF, 1) * rs1_ref[...]
                            + pltpu.roll(seg, LANES - ROPE_HALF, 1) * rs2_ref[...])
            k = jnp.concatenate(outs, axis=1)
        return k

    def store_tiles(o_ref, val_t):
        for t in range(tm // ATT_TILE):
            o_ref[0, t] = val_t[:, t * ATT_TILE:(t + 1) * ATT_TILE].astype(o_ref.dtype)

    qm_t = q_feature_major(wqm_ref, gqm_ref, True)
    store_tiles(qm_o, qm_t)
    km = k_token_major(wkm_ref, gkm_ref, True)
    km_o[0] = km.astype(BF16)
    store_tiles(vm_o, _dot_nt(wvm_ref[...], hb))

    blk0 = i * nb
    lane_head = lax.broadcasted_iota(jnp.int32, (1, GROUP_WIDTH), 1) // HEAD_DIM
    for sb in range(nb):
        kmean = jnp.mean(km[sb * MOBA_BLOCK:(sb + 1) * MOBA_BLOCK, :], axis=0, keepdims=True)
        for hd in range(GROUP_HEADS):
            row = hd * 16 + blk0 + sb
            kmean_sc[pl.ds(row, 1), :] = jnp.where(lane_head == hd, kmean, 0.0)

    gate_t = _dot(kmean_sc[...], qm_t, precision=HIGHEST)
    tok_blk = blk0 + lax.broadcasted_iota(jnp.int32, (16, tm), 1) // MOBA_BLOCK
    n_idx = lax.broadcasted_iota(jnp.int32, (16, tm), 0)
    valid = n_idx < tok_blk
    bias_heads = []
    for hd in range(GROUP_HEADS):
        g = jnp.where(valid, gate_t[hd * 16:(hd + 1) * 16, :], -jnp.inf)
        cnt = jnp.zeros((16, tm), jnp.int32)
        for m in range(16):
            gm = g[m:m + 1, :]
            cnt = cnt + jnp.where(gm > g, 1, jnp.where((gm == g) & (n_idx > m), 1, 0))
        sel = (valid & (cnt < MOBA_TOPK)) | (n_idx == tok_blk)
        bias_heads.append(jnp.where(sel, 0.0, NEG_BIG))
    zeros48 = jnp.zeros((HEAD_DIM - 16, tm), F32)
    pair_rows = []
    for hd in range(GROUP_HEADS):
        pair_rows += [bias_heads[hd], zeros48]
    store_tiles(bias_o, jnp.concatenate(pair_rows, axis=0))

    store_tiles(qf_o, q_feature_major(wqf_ref, gqf_ref, False))
    kf_o[0] = k_token_major(wkf_ref, gkf_ref, False).astype(BF16)
    store_tiles(vf_o, _dot_nt(wvf_ref[...], hb))

    f = _dot(hb, wf_ref[...]) + bf_ref[...]
    logf = jnp.minimum(f, 0.0) - jnp.log1p(jnp.exp(-jnp.abs(f)))
    r_i = lax.broadcasted_iota(jnp.int32, (tm, tm), 0)
    c_i = lax.broadcasted_iota(jnp.int32, (tm, tm), 1)
    tri = jnp.where(r_i >= c_i, 1.0, 0.0).astype(F32)
    cum = _dot(tri, logf, precision=HIGHEST) + carry_sc[...]
    carry_sc[...] = cum[tm - 1:tm, :]
    nc = cum * (-LOG2E)
    hi = nc.astype(BF16)
    rem = nc - hi.astype(F32)
    mid = rem.astype(BF16)
    lo = (rem - mid.astype(F32)).astype(BF16)
    ka = (_dot(hi, place_ref[0]) + _dot(mid, place_ref[1]) + _dot(lo, place_ref[2])
          + kones_ref[...])
    ka_o[0] = ka.astype(BF16)

    nc_t = nc.T[0:GROUP_HEADS, :]
    hi_t = nc_t.astype(BF16).astype(F32)
    rem_t = nc_t - hi_t
    mid_t = rem_t.astype(BF16).astype(F32)
    lo_t = (rem_t - mid_t).astype(BF16).astype(F32)
    r8 = lax.broadcasted_iota(jnp.int32, (8, tm), 0)
    zeros56 = jnp.zeros((HEAD_DIM - 8, tm), F32)
    qa_rows = []
    for hd in range(GROUP_HEADS):
        blk = jnp.where(r8 < 3, 1.0,
                        jnp.where(r8 == 3, -hi_t[hd:hd + 1, :],
                                  jnp.where(r8 == 4, -mid_t[hd:hd + 1, :],
                                            jnp.where(r8 == 5, -lo_t[hd:hd + 1, :], 0.0))))
        qa_rows += [blk, zeros56]
    store_tiles(qa_o, jnp.concatenate(qa_rows, axis=0))


def _inproj(x, sc, sh, wts, consts, tm):
    bsz, s, d = x.shape
    nt = s // ATT_TILE
    grid = (bsz, s // tm)
    wqm, wkm, wvm, wqf, wkf, wvf, wf, bfor, gqm, gkm, gqf, gkf = wts
    cos_t, sin_t, rc, rs1, rs2, jmat, place, kones = consts

    def const(a):
        nd = a.ndim
        return pl.BlockSpec(a.shape, lambda b, i, _n=nd: (0,) * _n,
                            pipeline_mode=pl.Buffered(1))

    mod_spec = pl.BlockSpec((None, 1, d), lambda b, i: (b, 0, 0))
    in_specs = [pl.BlockSpec((1, tm, d), lambda b, i: (b, i, 0)), mod_spec, mod_spec]
    in_specs += [const(a) for a in (wqm, wkm, wvm, wqf, wkf, wvf, wf, bfor,
                                    gqm, gkm, gqf, gkf)]
    in_specs += [pl.BlockSpec((ROPE_HALF, tm), lambda b, i: (0, i)),
                 pl.BlockSpec((ROPE_HALF, tm), lambda b, i: (0, i)),
                 pl.BlockSpec((tm, LANES), lambda b, i: (i, 0)),
                 pl.BlockSpec((tm, LANES), lambda b, i: (i, 0)),
                 pl.BlockSpec((tm, LANES), lambda b, i: (i, 0)),
                 const(jmat), const(place), const(kones)]

    tiles = tm // ATT_TILE
    feat_spec = pl.BlockSpec((1, tiles, GROUP_WIDTH, ATT_TILE), lambda b, i: (b, i, 0, 0))
    tok_spec = pl.BlockSpec((1, tm, GROUP_WIDTH), lambda b, i: (b, i, 0))
    feat_shape = jax.ShapeDtypeStruct((bsz, nt, GROUP_WIDTH, ATT_TILE), BF16)
    tok_shape = jax.ShapeDtypeStruct((bsz, s, GROUP_WIDTH), BF16)
    out_specs = [feat_spec, tok_spec, feat_spec, feat_spec,
                 feat_spec, tok_spec, feat_spec, tok_spec, feat_spec]
    out_shape = [feat_shape, tok_shape, feat_shape, feat_shape,
                 feat_shape, tok_shape, feat_shape, tok_shape, feat_shape]
    return pl.pallas_call(
        functools.partial(_inproj_kernel, tm=tm),
        grid=grid,
        in_specs=in_specs,
        out_specs=out_specs,
        out_shape=out_shape,
        scratch_shapes=[pltpu.VMEM((GROUP_HEADS * 16, GROUP_WIDTH), F32),
                        pltpu.VMEM((1, LANES), F32)],
        compiler_params=pltpu.CompilerParams(
            dimension_semantics=("arbitrary", "arbitrary"),
            vmem_limit_bytes=VMEM_LIMIT),
        name="inproj",
    )(x, sc, sh, wqm, wkm, wvm, wqf, wkf, wvf, wf, bfor, gqm, gkm, gqf, gkf,
      cos_t, sin_t, rc, rs1, rs2, jmat, place, kones)


def _attn_kernel(q_ref, qa_ref, k_ref, ka_ref, v_ref, o_ref, *, t):
    i = pl.program_id(2)
    q = q_ref[0, 0]
    qa = qa_ref[0, 0]
    row = lax.broadcasted_iota(jnp.int32, (PAIR_WIDTH, t), 0)
    rhs = []
    for e in range(2):
        keep = (row >= e * HEAD_DIM) & (row < (e + 1) * HEAD_DIM)
        zero = jnp.zeros_like(q)
        rhs.append(jnp.concatenate([jnp.where(keep, q, zero), jnp.where(keep, qa, zero)],
                                   axis=0))

    def tile_scores(j):
        start = pl.multiple_of(j * t, t)
        lhs = jnp.concatenate([k_ref[0, pl.ds(start, t), :], ka_ref[0, pl.ds(start, t), :]],
                              axis=1)
        return [_dot(lhs, rhs[e]) for e in range(2)]

    def values(j, e):
        return v_ref[0, j, e * HEAD_DIM:(e + 1) * HEAD_DIM, :]

    key_i = lax.broadcasted_iota(jnp.int32, (t, t), 0)
    qry_i = lax.broadcasted_iota(jnp.int32, (t, t), 1)
    causal = key_i <= qry_i
    state = []
    for e, s_t in enumerate(tile_scores(i)):
        s_t = jnp.where(causal, s_t, NEG_BIG)
        m = jnp.max(s_t, axis=0, keepdims=True)
        p = jnp.exp2(s_t - m)
        l = jnp.sum(p, axis=0, keepdims=True)
        acc = _dot(values(i, e), p.astype(BF16))
        state += [m, l, acc]

    def body(j, carry):
        out = []
        for e, s_t in enumerate(tile_scores(j)):
            m, l, acc = carry[3 * e:3 * e + 3]
            m_new = jnp.maximum(m, jnp.max(s_t, axis=0, keepdims=True))
            alpha = jnp.exp2(m - m_new)
            p = jnp.exp2(s_t - m_new)
            l = alpha * l + jnp.sum(p, axis=0, keepdims=True)
            acc = alpha * acc + _dot(values(j, e), p.astype(BF16))
            out += [m_new, l, acc]
        return tuple(out)

    state = lax.fori_loop(0, i, body, tuple(state))
    o_t = jnp.concatenate([state[2] / state[1], state[5] / state[4]], axis=0)
    o_ref[0] = o_t.T


def _attn_bounded_kernel(q_ref, qa_ref, k_ref, ka_ref, v_ref, o_ref, s_sc, p_sc, *, t,
                         ka_const):
    i = pl.program_id(1)
    row = lax.broadcasted_iota(jnp.int32, (PAIR_WIDTH, t), 0)
    ones = jnp.ones((16, t), BF16)
    key_i = lax.broadcasted_iota(jnp.int32, (t, t), 0)
    qry_i = lax.broadcasted_iota(jnp.int32, (t, t), 1)
    causal = key_i <= qry_i
    have_past = i > 0

    for pr in range(N_PAIRS):
        st = pr % 2
        feat = slice(pr * PAIR_WIDTH, (pr + 1) * PAIR_WIDTH)
        q = q_ref[0, 0, feat, :]
        qa = qa_ref[0, 0, feat, :]
        rhs = []
        for e in range(2):
            keep = (row >= e * HEAD_DIM) & (row < (e + 1) * HEAD_DIM)
            zero = jnp.zeros_like(q)
            rhs.append(jnp.concatenate([jnp.where(keep, q, zero), jnp.where(keep, qa, zero)],
                                       axis=0))

        def scores(j, e, feat=feat, rhs=rhs):
            start = pl.multiple_of(j * t, t)
            k_aug = (ka_ref[0, pl.ds(start, t), :] if ka_const
                     else ka_ref[0, pl.ds(start, t), feat])
            lhs = jnp.concatenate([k_ref[0, pl.ds(start, t), feat], k_aug], axis=1)
            return _dot(lhs, rhs[e])

        def exp_stage(st=st):
            for e in range(2):
                for r in range(0, t, EXP_ROWS):
                    p_sc[st, e, r:r + EXP_ROWS, :] = jnp.exp2(
                        s_sc[st, e, r:r + EXP_ROWS, :]).astype(BF16)

        def value_stage(j, acc, st=st, pr=pr):
            out = []
            for e in range(2):
                r0 = pr * PAIR_WIDTH + e * HEAD_DIM
                v_e = jnp.concatenate([v_ref[0, j, r0:r0 + HEAD_DIM, :], ones], axis=0)
                out.append(acc[e] + _dot(v_e, p_sc[st, e]))
            return out

        for e in range(2):
            s_sc[st, e] = jnp.where(causal, scores(i, e), NEG_BIG)
        exp_stage()
        for e in range(2):
            s_sc[st, e] = jnp.where(have_past, scores(0, e), NEG_BIG)
        acc = [jnp.zeros((HEAD_DIM + 16, t), F32)] * 2

        def body(j, carry, scores=scores, exp_stage=exp_stage, value_stage=value_stage, st=st):
            acc0, acc1, j_cur, j_prev = carry
            acc = value_stage(j_prev, [acc0, acc1])
            exp_stage()
            for e in range(2):
                s_sc[st, e] = scores(j, e)
            return acc[0], acc[1], j, j_cur

        acc0, acc1, j_cur, j_prev = lax.fori_loop(1, i, body, (acc[0], acc[1], 0, i))
        acc = value_stage(j_prev, [acc0, acc1])
        exp_stage()
        acc = value_stage(j_cur, acc)
        o_t = jnp.concatenate([a[0:HEAD_DIM, :] / a[HEAD_DIM:HEAD_DIM + 1, :] for a in acc],
                              axis=0)
        o_ref[0, :, feat] = o_t.T


def _attention_online(q_t, qa, k, ka, v_t, *, ka_const, name):
    bsz, nt, _, t = q_t.shape
    s = k.shape[1]
    grid = (bsz, N_PAIRS, nt)
    if ka_const:
        ka_spec = pl.BlockSpec((1, s, PAIR_WIDTH), lambda b, p, i: (0, 0, 0))
    else:
        ka_spec = pl.BlockSpec((1, s, PAIR_WIDTH), lambda b, p, i: (b, 0, p))
    return pl.pallas_call(
        functools.partial(_attn_kernel, t=t),
        grid=grid,
        in_specs=[pl.BlockSpec((1, 1, PAIR_WIDTH, t), lambda b, p, i: (b, i, p, 0)),
                  pl.BlockSpec((1, 1, PAIR_WIDTH, t), lambda b, p, i: (b, i, p, 0)),
                  pl.BlockSpec((1, s, PAIR_WIDTH), lambda b, p, i: (b, 0, p)),
                  ka_spec,
                  pl.BlockSpec((1, nt, PAIR_WIDTH, t), lambda b, p, i: (b, 0, p, 0))],
        out_specs=pl.BlockSpec((1, t, PAIR_WIDTH), lambda b, p, i: (b, i, p)),
        out_shape=jax.ShapeDtypeStruct((bsz, s, GROUP_WIDTH), F32),
        compiler_params=pltpu.CompilerParams(
            dimension_semantics=("arbitrary", "arbitrary", "arbitrary"),
            vmem_limit_bytes=VMEM_LIMIT),
        name=name,
    )(q_t, qa, k, ka, v_t)


def _attention_bounded(q_t, qa, k, ka, v_t, *, ka_const, name):
    bsz, nt, _, t = q_t.shape
    s = k.shape[1]
    feat_spec = pl.BlockSpec((1, 1, GROUP_WIDTH, t), lambda b, i: (b, i, 0, 0))
    tok_spec = pl.BlockSpec((1, s, GROUP_WIDTH), lambda b, i: (b, 0, 0))
    ka_spec = pl.BlockSpec((1, s, PAIR_WIDTH), lambda b, i: (0, 0, 0)) if ka_const else tok_spec
    return pl.pallas_call(
        functools.partial(_attn_bounded_kernel, t=t, ka_const=ka_const),
        grid=(bsz, nt),
        in_specs=[feat_spec, feat_spec, tok_spec, ka_spec,
                  pl.BlockSpec((1, nt, GROUP_WIDTH, t), lambda b, i: (b, 0, 0, 0))],
        out_specs=pl.BlockSpec((1, t, GROUP_WIDTH), lambda b, i: (b, i, 0)),
        out_shape=jax.ShapeDtypeStruct((bsz, s, GROUP_WIDTH), F32),
        scratch_shapes=[pltpu.VMEM((2, 2, t, t), F32), pltpu.VMEM((2, 2, t, t), BF16)],
        compiler_params=pltpu.CompilerParams(
            dimension_semantics=("arbitrary", "arbitrary"),
            vmem_limit_bytes=VMEM_LIMIT),
        name=name,
    )(q_t, qa, k, ka, v_t)


def _attention_dispatch(g_q, g_k, q_t, qa, k, ka, v_t, *, ka_const, name):
    bound = HEAD_DIM * jnp.max(jnp.abs(g_q)) * jnp.max(jnp.abs(g_k))
    return lax.cond(
        bound <= MAX_SAFE_EXPONENT,
        functools.partial(_attention_bounded, ka_const=ka_const, name=name + "_bounded"),
        functools.partial(_attention_online, ka_const=ka_const, name=name + "_online"),
        q_t, qa, k, ka, v_t)


def _post_kernel(x_ref, om_ref, of_ref, ga_ref, scm_ref, shm_ref, gm_ref,
                 gom_ref, gof_ref, wout_ref, w1_ref, w2_ref, o_ref, *, ff_chunk):
    def rms(v):
        return v * lax.rsqrt(jnp.mean(v * v, axis=-1, keepdims=True) + EPS)

    mixed = jnp.concatenate([(rms(om_ref[0]) * gom_ref[...]).astype(BF16),
                             (rms(of_ref[0]) * gof_ref[...]).astype(BF16)], axis=1)
    x1 = x_ref[0] + ga_ref[...] * _dot(mixed, wout_ref[...])
    h = (rms(x1) * (1.0 + scm_ref[...]) + shm_ref[...]).astype(BF16)
    d_ff = w1_ref.shape[1]
    y = None
    for c in range(d_ff // ff_chunk):
        hid = _dot(h, w1_ref[:, c * ff_chunk:(c + 1) * ff_chunk])
        hid = jnp.square(jnp.maximum(hid, 0.0)).astype(BF16)
        part = _dot(hid, w2_ref[c * ff_chunk:(c + 1) * ff_chunk, :])
        y = part if y is None else y + part
    o_ref[0] = x1 + gm_ref[...] * y


def _post(x, o_m, o_f, mod4, g_out_m, g_out_f, w_out, w_ff1, w_ff2, tm):
    bsz, s, d = x.shape
    d_ff = w_ff1.shape[1]

    def const(a):
        nd = a.ndim
        return pl.BlockSpec(a.shape, lambda b, i, _n=nd: (0,) * _n,
                            pipeline_mode=pl.Buffered(1))

    def mod_spec(k):
        return pl.BlockSpec((None, None, 1, d), lambda b, i, _k=k: (b, _k, 0, 0))

    tok = lambda w: pl.BlockSpec((1, tm, w), lambda b, i: (b, i, 0))
    return pl.pallas_call(
        functools.partial(_post_kernel, ff_chunk=1024),
        grid=(bsz, s // tm),
        in_specs=[tok(d), tok(GROUP_WIDTH), tok(GROUP_WIDTH),
                  mod_spec(2), mod_spec(4), mod_spec(3), mod_spec(5),
                  const(g_out_m), const(g_out_f), const(w_out), const(w_ff1), const(w_ff2)],
        out_specs=tok(d),
        out_shape=jax.ShapeDtypeStruct((bsz, s, d), F32),
        compiler_params=pltpu.CompilerParams(
            dimension_semantics=("arbitrary", "arbitrary"),
            vmem_limit_bytes=VMEM_LIMIT),
        name="post",
    )(x, o_m, o_f, mod4, mod4, mod4, mod4, g_out_m, g_out_f, w_out, w_ff1, w_ff2)


def _rope_tables(s):
    inv_freq = ROPE_THETA ** (-np.arange(0, ROPE_DIM, 2, dtype=np.float64) / ROPE_DIM)
    ang = np.arange(s, dtype=np.float64)[:, None] * inv_freq[None, :]
    cos, sin = np.cos(ang), np.sin(ang)
    rc = np.ones((s, LANES)); rs1 = np.zeros((s, LANES)); rs2 = np.zeros((s, LANES))
    for base in (0, HEAD_DIM):
        rc[:, base:base + ROPE_HALF] = cos
        rc[:, base + ROPE_HALF:base + ROPE_DIM] = cos
        rs2[:, base:base + ROPE_HALF] = -sin
        rs1[:, base + ROPE_HALF:base + ROPE_DIM] = sin
    f = lambda a: jnp.asarray(a, dtype=F32)
    return f(cos.T), f(sin.T), f(rc), f(rs1), f(rs2)


def _static_mats(s):
    idx = np.arange(256)
    jmat = (idx[:, None] // HEAD_DIM == idx[None, :] // HEAD_DIM) / HEAD_DIM
    place = np.zeros((3, LANES, GROUP_WIDTH))
    for part in range(3):
        for hd in range(GROUP_HEADS):
            place[part, hd, (hd // 2) * PAIR_WIDTH + (hd % 2) * HEAD_DIM + part] = 1.0
    kones = np.zeros((1, GROUP_WIDTH))
    for hd in range(GROUP_HEADS):
        kones[0, hd * HEAD_DIM + 3:hd * HEAD_DIM + 6] = 1.0
    ka_moba = np.zeros((1, s, PAIR_WIDTH))
    blk = np.arange(s) // MOBA_BLOCK
    for e in range(2):
        ka_moba[0, np.arange(s), e * HEAD_DIM + blk] = 1.0
    b = lambda a: jnp.asarray(a, dtype=BF16)
    return b(jmat), b(place), jnp.asarray(kones, dtype=F32), b(ka_moba)


def kernel(x, c, w_ada, b_ada, w_in, b_forget, g_qn_moba, g_kn_moba, g_qn_fox, g_kn_fox,
           g_out_moba, g_out_fox, w_out, w_ff1, w_ff2):
    bsz, s, d = x.shape
    depth = w_ada.shape[0]
    w = GROUP_WIDTH
    assert s % MOBA_BLOCK == 0 and s // MOBA_BLOCK <= 16
    tm_in = 512
    tm_post = 512
    cos_t, sin_t, rc, rs1, rs2 = _rope_tables(s)
    jmat, place, kones, ka_moba = _static_mats(s)
    q_scale = HEAD_DIM ** -0.5 * LOG2E

    for l in range(depth):
        mod = _adaln(c, w_ada[l], b_ada[l])
        mod4 = mod.reshape(bsz, 6, 1, d)

        wl = w_in[l]
        col = lambda k: wl[:, k * w:(k + 1) * w]
        wf = jnp.zeros((d, LANES), F32).at[:, :GROUP_HEADS].set(wl[:, 6 * w:]).astype(BF16)
        bfor = jnp.zeros((1, LANES), F32).at[0, :GROUP_HEADS].set(b_forget[l])
        q_gain = lambda g: jnp.broadcast_to((g * q_scale)[:, None], (HEAD_DIM, tm_in))
        k_gain = lambda g: jnp.tile(g, GROUP_HEADS).reshape(1, w)
        wts = (col(0).T.astype(BF16), col(1).astype(BF16), col(2).T.astype(BF16),
               col(3).T.astype(BF16), col(4).astype(BF16), col(5).T.astype(BF16),
               wf, bfor,
               q_gain(g_qn_moba[l]), k_gain(g_kn_moba[l]),
               q_gain(g_qn_fox[l]), k_gain(g_kn_fox[l]))
        consts = (cos_t, sin_t, rc, rs1, rs2, jmat, place, kones)

        qm_t, km, vm_t, bias_t, qf_t, kf, vf_t, ka_f, qa_f = _inproj(
            x, mod4[:, 1], mod4[:, 0], wts, consts, tm_in)

        o_m = _attention_dispatch(g_qn_moba[l] * q_scale, g_kn_moba[l],
                                  qm_t, bias_t, km, ka_moba, vm_t, ka_const=True,
                                  name="attn_moba")
        o_f = _attention_dispatch(g_qn_fox[l] * q_scale, g_kn_fox[l],
                                  qf_t, qa_f, kf, ka_f, vf_t, ka_const=False,
                                  name="attn_fox")

        x = _post(x, o_m, o_f, mod4,
                  g_out_moba[l].reshape(1, w), g_out_fox[l].reshape(1, w),
                  w_out[l].astype(BF16), w_ff1[l].astype(BF16), w_ff2[l].astype(BF16),
                  tm_post)
    return x
```

```python
import functools
import math

import numpy as np
import jax
import jax.numpy as jnp
from jax import lax
from jax.experimental import pallas as pl
from jax.experimental.pallas import tpu as pltpu

F32 = jnp.float32
BF16 = jnp.bfloat16
HIGHEST = lax.Precision.HIGHEST

HEAD_DIM = 64
GROUP_HEADS = 8
GROUP_WIDTH = GROUP_HEADS * HEAD_DIM
PAIR_WIDTH = 2 * HEAD_DIM
N_PAIRS = GROUP_HEADS // 2
MOBA_BLOCK = 256
MOBA_TOPK = 3
ROPE_THETA = 500000.0
ROPE_DIM = HEAD_DIM // 4
ROPE_HALF = ROPE_DIM // 2
EPS = 1e-6
LOG2E = math.log2(math.e)
NEG_BIG = -1e30
MAX_SAFE_EXPONENT = 60.0
ATT_TILE = 512
LANES = 128
VMEM_LIMIT = 48 * 1024 * 1024


def _dot(a, b, precision=None):
    return jnp.dot(a, b, preferred_element_type=F32, precision=precision)


def _dot_nt(a, b, precision=None):
    return lax.dot_general(a, b, (((1,), (1,)), ((), ())),
                           preferred_element_type=F32, precision=precision)


def _adaln_kernel(c_ref, w_ref, b_ref, o_ref):
    c = c_ref[...]
    s = c / (1.0 + jnp.exp(-c))
    o_ref[...] = _dot(s, w_ref[...], precision=HIGHEST) + b_ref[...]


def _adaln(c, w_ada, b_ada):
    bsz, d = c.shape
    n = w_ada.shape[1]
    return pl.pallas_call(
        _adaln_kernel,
        grid=(n // d,),
        in_specs=[pl.BlockSpec((bsz, d), lambda j: (0, 0)),
                  pl.BlockSpec((d, d), lambda j: (0, j)),
                  pl.BlockSpec((1, d), lambda j: (0, j))],
        out_specs=pl.BlockSpec((bsz, d), lambda j: (0, j)),
        out_shape=jax.ShapeDtypeStruct((bsz, n), F32),
        compiler_params=pltpu.CompilerParams(dimension_semantics=("arbitrary",)),
        name="adaln",
    )(c, w_ada, b_ada.reshape(1, n))


def _inproj_kernel(x_ref, sc_ref, sh_ref,
                   wqm_ref, wkm_ref, wvm_ref, wqf_ref, wkf_ref, wvf_ref, wf_ref, bf_ref,
                   gqm_ref, gkm_ref, gqf_ref, gkf_ref,
                   cos_t_ref, sin_t_ref, rc_ref, rs1_ref, rs2_ref,
                   j_ref, place_ref, kones_ref,
                   qm_o, km_o, vm_o, bias_o, qf_o, kf_o, vf_o, ka_o, qa_o,
                   kmean_sc, carry_sc, *, tm):
    i = pl.program_id(1)
    nb = tm // MOBA_BLOCK

    @pl.when(i == 0)
    def _():
        kmean_sc[...] = jnp.zeros_like(kmean_sc)
        carry_sc[...] = jnp.zeros_like(carry_sc)

    x = x_ref[0]
    ms = jnp.mean(x * x, axis=-1, keepdims=True)
    h = x * lax.rsqrt(ms + EPS) * (1.0 + sc_ref[...]) + sh_ref[...]
    hb = h.astype(BF16)

    def q_feature_major(w_ref, g_ref, rope):
        q_t = _dot_nt(w_ref[...], hb)
        gain = g_ref[...]
        heads = []
        for hd in range(GROUP_HEADS):
            xh = q_t[hd * HEAD_DIM:(hd + 1) * HEAD_DIM, :]
            ss = jnp.sum(xh * xh, axis=0, keepdims=True)
            xh = xh * lax.rsqrt(ss * (1.0 / HEAD_DIM) + EPS) * gain
            if rope:
                x1 = xh[0:ROPE_HALF, :]
                x2 = xh[ROPE_HALF:ROPE_DIM, :]
                cs = cos_t_ref[...]
                sn = sin_t_ref[...]
                xh = jnp.concatenate(
                    [x1 * cs - x2 * sn, x2 * cs + x1 * sn, xh[ROPE_DIM:, :]], axis=0)
            heads.append(xh)
        return jnp.concatenate(heads, axis=0)

    def k_token_major(w_ref, g_ref, rope):
        k = _dot(hb, w_ref[...])
        chunks = []
        for c in range(GROUP_WIDTH // 256):
            seg = k[:, c * 256:(c + 1) * 256]
            msq = _dot((seg * seg).astype(BF16), j_ref[...])
            chunks.append(seg * lax.rsqrt(msq + EPS) * g_ref[:, c * 256:(c + 1) * 256])
        k = jnp.concatenate(chunks, axis=1)
        if rope:
            outs = []
            for c in range(GROUP_WIDTH // LANES):
                seg = k[:, c * LANES:(c + 1) * LANES]
                outs.append(seg * rc_ref[...]
                            + pltpu.roll(seg, ROPE_HALF, 1) * rs1_ref[...]
                            + pltpu.roll(seg, LANES - ROPE_HALF, 1) * rs2_ref[...])
            k = jnp.concatenate(outs, axis=1)
        return k

    def store_tiles(o_ref, val_t):
        for t in range(tm // ATT_TILE):
            o_ref[0, t] = val_t[:, t * ATT_TILE:(t + 1) * ATT_TILE].astype(o_ref.dtype)

    qm_t = q_feature_major(wqm_ref, gqm_ref, True)
    store_tiles(qm_o, qm_t)
    km = k_token_major(wkm_ref, gkm_ref, True)
    km_o[0] = km.astype(BF16)
    store_tiles(vm_o, _dot_nt(wvm_ref[...], hb))

    blk0 = i * nb
    lane_head = lax.broadcasted_iota(jnp.int32, (1, GROUP_WIDTH), 1) // HEAD_DIM
    for sb in range(nb):
        kmean = jnp.mean(km[sb * MOBA_BLOCK:(sb + 1) * MOBA_BLOCK, :], axis=0, keepdims=True)
        for hd in range(GROUP_HEADS):
            row = hd * 16 + blk0 + sb
            kmean_sc[pl.ds(row, 1), :] = jnp.where(lane_head == hd, kmean, 0.0)

    gate_t = _dot(kmean_sc[...], qm_t, precision=HIGHEST)
    tok_blk = blk0 + lax.broadcasted_iota(jnp.int32, (16, tm), 1) // MOBA_BLOCK
    n_idx = lax.broadcasted_iota(jnp.int32, (16, tm), 0)
    valid = n_idx < tok_blk
    bias_heads = []
    for hd in range(GROUP_HEADS):
        g = jnp.where(valid, gate_t[hd * 16:(hd + 1) * 16, :], -jnp.inf)
        cnt = jnp.zeros((16, tm), jnp.int32)
        for m in range(16):
            gm = g[m:m + 1, :]
            cnt = cnt + jnp.where(gm > g, 1, jnp.where((gm == g) & (n_idx > m), 1, 0))
        sel = (valid & (cnt < MOBA_TOPK)) | (n_idx == tok_blk)
        bias_heads.append(jnp.where(sel, 0.0, NEG_BIG))
    zeros48 = jnp.zeros((HEAD_DIM - 16, tm), F32)
    pair_rows = []
    for hd in range(GROUP_HEADS):
        pair_rows += [bias_heads[hd], zeros48]
    store_tiles(bias_o, jnp.concatenate(pair_rows, axis=0))

    store_tiles(qf_o, q_feature_major(wqf_ref, gqf_ref, False))
    kf_o[0] = k_token_major(wkf_ref, gkf_ref, False).astype(BF16)
    store_tiles(vf_o, _dot_nt(wvf_ref[...], hb))

    f = _dot(hb, wf_ref[...]) + bf_ref[...]
    logf = jnp.minimum(f, 0.0) - jnp.log1p(jnp.exp(-jnp.abs(f)))
    r_i = lax.broadcasted_iota(jnp.int32, (tm, tm), 0)
    c_i = lax.broadcasted_iota(jnp.int32, (tm, tm), 1)
    tri = jnp.where(r_i >= c_i, 1.0, 0.0).astype(F32)
    cum = _dot(tri, logf, precision=HIGHEST) + carry_sc[...]
    carry_sc[...] = cum[tm - 1:tm, :]
    nc = cum * (-LOG2E)
    hi = nc.astype(BF16)
    rem = nc - hi.astype(F32)
    mid = rem.astype(BF16)
    lo = (rem - mid.astype(F32)).astype(BF16)
    ka = (_dot(hi, place_ref[0]) + _dot(mid, place_ref[1]) + _dot(lo, place_ref[2])
          + kones_ref[...])
    ka_o[0] = ka.astype(BF16)

    nc_t = nc.T[0:GROUP_HEADS, :]
    hi_t = nc_t.astype(BF16).astype(F32)
    rem_t = nc_t - hi_t
    mid_t = rem_t.astype(BF16).astype(F32)
    lo_t = (rem_t - mid_t).astype(BF16).astype(F32)
    r8 = lax.broadcasted_iota(jnp.int32, (8, tm), 0)
    zeros56 = jnp.zeros((HEAD_DIM - 8, tm), F32)
    qa_rows = []
    for hd in range(GROUP_HEADS):
        blk = jnp.where(r8 < 3, 1.0,
                        jnp.where(r8 == 3, -hi_t[hd:hd + 1, :],
                                  jnp.where(r8 == 4, -mid_t[hd:hd + 1, :],
                                            jnp.where(r8 == 5, -lo_t[hd:hd + 1, :], 0.0))))
        qa_rows += [blk, zeros56]
    store_tiles(qa_o, jnp.concatenate(qa_rows, axis=0))


def _inproj(x, sc, sh, wts, consts, tm):
    bsz, s, d = x.shape
    nt = s // ATT_TILE
    grid = (bsz, s // tm)
    wqm, wkm, wvm, wqf, wkf, wvf, wf, bfor, gqm, gkm, gqf, gkf = wts
    cos_t, sin_t, rc, rs1, rs2, jmat, place, kones = consts

    def const(a):
        nd = a.ndim
        return pl.BlockSpec(a.shape, lambda b, i, _n=nd: (0,) * _n,
                            pipeline_mode=pl.Buffered(1))

    mod_spec = pl.BlockSpec((None, 1, d), lambda b, i: (b, 0, 0))
    in_specs = [pl.BlockSpec((1, tm, d), lambda b, i: (b, i, 0)), mod_spec, mod_spec]
    in_specs += [const(a) for a in (wqm, wkm, wvm, wqf, wkf, wvf, wf, bfor,
                                    gqm, gkm, gqf, gkf)]
    in_specs += [pl.BlockSpec((ROPE_HALF, tm), lambda b, i: (0, i)),
                 pl.BlockSpec((ROPE_HALF, tm), lambda b, i: (0, i)),
                 pl.BlockSpec((tm, LANES), lambda b, i: (i, 0)),
                 pl.BlockSpec((tm, LANES), lambda b, i: (i, 0)),
                 pl.BlockSpec((tm, LANES), lambda b, i: (i, 0)),
                 const(jmat), const(place), const(kones)]

    tiles = tm // ATT_TILE
    feat_spec = pl.BlockSpec((1, tiles, GROUP_WIDTH, ATT_TILE), lambda b, i: (b, i, 0, 0))
    tok_spec = pl.BlockSpec((1, tm, GROUP_WIDTH), lambda b, i: (b, i, 0))
    feat_shape = jax.ShapeDtypeStruct((bsz, nt, GROUP_WIDTH, ATT_TILE), BF16)
    tok_shape = jax.ShapeDtypeStruct((bsz, s, GROUP_WIDTH), BF16)
    out_specs = [feat_spec, tok_spec, feat_spec, feat_spec,
                 feat_spec, tok_spec, feat_spec, tok_spec, feat_spec]
    out_shape = [feat_shape, tok_shape, feat_shape, feat_shape,
                 feat_shape, tok_shape, feat_shape, tok_shape, feat_shape]
    return pl.pallas_call(
        functools.partial(_inproj_kernel, tm=tm),
        grid=grid,
        in_specs=in_specs,
        out_specs=out_specs,
        out_shape=out_shape,
        scratch_shapes=[pltpu.VMEM((GROUP_HEADS * 16, GROUP_WIDTH), F32),
                        pltpu.VMEM((1, LANES), F32)],
        compiler_params=pltpu.CompilerParams(
            dimension_semantics=("arbitrary", "arbitrary"),
            vmem_limit_bytes=VMEM_LIMIT),
        name="inproj",
    )(x, sc, sh, wqm, wkm, wvm, wqf, wkf, wvf, wf, bfor, gqm, gkm, gqf, gkf,
      cos_t, sin_t, rc, rs1, rs2, jmat, place, kones)


def _attn_kernel(q_ref, qa_ref, k_ref, ka_ref, v_ref, o_ref, *, t):
    i = pl.program_id(2)
    q = q_ref[0, 0]
    qa = qa_ref[0, 0]
    row = lax.broadcasted_iota(jnp.int32, (PAIR_WIDTH, t), 0)
    rhs = []
    for e in range(2):
        keep = (row >= e * HEAD_DIM) & (row < (e + 1) * HEAD_DIM)
        zero = jnp.zeros_like(q)
        rhs.append(jnp.concatenate([jnp.where(keep, q, zero), jnp.where(keep, qa, zero)],
                                   axis=0))

    def tile_scores(j):
        start = pl.multiple_of(j * t, t)
        lhs = jnp.concatenate([k_ref[0, pl.ds(start, t), :], ka_ref[0, pl.ds(start, t), :]],
                              axis=1)
        return [_dot(lhs, rhs[e]) for e in range(2)]

    def values(j, e):
        return v_ref[0, j, e * HEAD_DIM:(e + 1) * HEAD_DIM, :]

    key_i = lax.broadcasted_iota(jnp.int32, (t, t), 0)
    qry_i = lax.broadcasted_iota(jnp.int32, (t, t), 1)
    causal = key_i <= qry_i
    state = []
    for e, s_t in enumerate(tile_scores(i)):
        s_t = jnp.where(causal, s_t, NEG_BIG)
        m = jnp.max(s_t, axis=0, keepdims=True)
        p = jnp.exp2(s_t - m)
        l = jnp.sum(p, axis=0, keepdims=True)
        acc = _dot(values(i, e), p.astype(BF16))
        state += [m, l, acc]

    def body(j, carry):
        out = []
        for e, s_t in enumerate(tile_scores(j)):
            m, l, acc = carry[3 * e:3 * e + 3]
            m_new = jnp.maximum(m, jnp.max(s_t, axis=0, keepdims=True))
            alpha = jnp.exp2(m - m_new)
            p = jnp.exp2(s_t - m_new)
            l = alpha * l + jnp.sum(p, axis=0, keepdims=True)
            acc = alpha * acc + _dot(values(j, e), p.astype(BF16))
            out += [m_new, l, acc]
        return tuple(out)

    state = lax.fori_loop(0, i, body, tuple(state))
    o_t = jnp.concatenate([state[2] / state[1], state[5] / state[4]], axis=0)
    o_ref[0] = o_t.T


def _attn_bounded_kernel(q_ref, qa_ref, k_ref, ka_ref, v_ref, o_ref, p_sc, *, t,
                         ka_const):
    i = pl.program_id(1)
    row = lax.broadcasted_iota(jnp.int32, (PAIR_WIDTH, t), 0)
    ones = jnp.ones((16, t), BF16)
    key_i = lax.broadcasted_iota(jnp.int32, (t, t), 0)
    qry_i = lax.broadcasted_iota(jnp.int32, (t, t), 1)
    causal = key_i <= qry_i

    for pr in range(N_PAIRS):
        st = pr % 2
        feat = slice(pr * PAIR_WIDTH, (pr + 1) * PAIR_WIDTH)
        q = q_ref[0, 0, feat, :]
        qa = qa_ref[0, 0, feat, :]
        rhs = []
        for e in range(2):
            keep = (row >= e * HEAD_DIM) & (row < (e + 1) * HEAD_DIM)
            zero = jnp.zeros_like(q)
            rhs.append(jnp.concatenate([jnp.where(keep, q, zero), jnp.where(keep, qa, zero)],
                                       axis=0))

        def scores(j, e, feat=feat, rhs=rhs):
            start = pl.multiple_of(j * t, t)
            k_aug = (ka_ref[0, pl.ds(start, t), :] if ka_const
                     else ka_ref[0, pl.ds(start, t), feat])
            lhs = jnp.concatenate([k_ref[0, pl.ds(start, t), feat], k_aug], axis=1)
            return _dot(lhs, rhs[e])

        def value_stage(j, acc, st=st, pr=pr):
            out = []
            for e in range(2):
                r0 = pr * PAIR_WIDTH + e * HEAD_DIM
                v_e = jnp.concatenate([v_ref[0, j, r0:r0 + HEAD_DIM, :], ones], axis=0)
                out.append(acc[e] + _dot(v_e, p_sc[st, e]))
            return out

        for e in range(2):
            p_sc[st, e] = jnp.exp2(jnp.where(causal, scores(i, e), NEG_BIG)).astype(BF16)
        acc = [jnp.zeros((HEAD_DIM + 16, t), F32)] * 2

        def body(j, carry, scores=scores, value_stage=value_stage, st=st):
            acc0, acc1, j_prev = carry
            acc = value_stage(j_prev, [acc0, acc1])
            for e in range(2):
                p_sc[st, e] = jnp.exp2(scores(j, e)).astype(BF16)
            return acc[0], acc[1], j

        acc0, acc1, j_prev = lax.fori_loop(0, i, body, (acc[0], acc[1], i))
        acc = value_stage(j_prev, [acc0, acc1])
        o_t = jnp.concatenate([a[0:HEAD_DIM, :] / a[HEAD_DIM:HEAD_DIM + 1, :] for a in acc],
                              axis=0)
        o_ref[0, :, feat] = o_t.T


def _attention_online(q_t, qa, k, ka, v_t, *, ka_const, name):
    bsz, nt, _, t = q_t.shape
    s = k.shape[1]
    grid = (bsz, N_PAIRS, nt)
    if ka_const:
        ka_spec = pl.BlockSpec((1, s, PAIR_WIDTH), lambda b, p, i: (0, 0, 0))
    else:
        ka_spec = pl.BlockSpec((1, s, PAIR_WIDTH), lambda b, p, i: (b, 0, p))
    return pl.pallas_call(
        functools.partial(_attn_kernel, t=t),
        grid=grid,
        in_specs=[pl.BlockSpec((1, 1, PAIR_WIDTH, t), lambda b, p, i: (b, i, p, 0)),
                  pl.BlockSpec((1, 1, PAIR_WIDTH, t), lambda b, p, i: (b, i, p, 0)),
                  pl.BlockSpec((1, s, PAIR_WIDTH), lambda b, p, i: (b, 0, p)),
                  ka_spec,
                  pl.BlockSpec((1, nt, PAIR_WIDTH, t), lambda b, p, i: (b, 0, p, 0))],
        out_specs=pl.BlockSpec((1, t, PAIR_WIDTH), lambda b, p, i: (b, i, p)),
        out_shape=jax.ShapeDtypeStruct((bsz, s, GROUP_WIDTH), F32),
        compiler_params=pltpu.CompilerParams(
            dimension_semantics=("arbitrary", "arbitrary", "arbitrary"),
            vmem_limit_bytes=VMEM_LIMIT),
        name=name,
    )(q_t, qa, k, ka, v_t)


def _attention_bounded(q_t, qa, k, ka, v_t, *, ka_const, name):
    bsz, nt, _, t = q_t.shape
    s = k.shape[1]
    feat_spec = pl.BlockSpec((1, 1, GROUP_WIDTH, t), lambda b, i: (b, i, 0, 0))
    tok_spec = pl.BlockSpec((1, s, GROUP_WIDTH), lambda b, i: (b, 0, 0))
    ka_spec = pl.BlockSpec((1, s, PAIR_WIDTH), lambda b, i: (0, 0, 0)) if ka_const else tok_spec
    return pl.pallas_call(
        functools.partial(_attn_bounded_kernel, t=t, ka_const=ka_const),
        grid=(bsz, nt),
        in_specs=[feat_spec, feat_spec, tok_spec, ka_spec,
                  pl.BlockSpec((1, nt, GROUP_WIDTH, t), lambda b, i: (b, 0, 0, 0))],
        out_specs=pl.BlockSpec((1, t, GROUP_WIDTH), lambda b, i: (b, i, 0)),
        out_shape=jax.ShapeDtypeStruct((bsz, s, GROUP_WIDTH), F32),
        scratch_shapes=[pltpu.VMEM((2, 2, t, t), BF16)],
        compiler_params=pltpu.CompilerParams(
            dimension_semantics=("arbitrary", "arbitrary"),
            vmem_limit_bytes=VMEM_LIMIT),
        name=name,
    )(q_t, qa, k, ka, v_t)


def _attention_dispatch(g_q, g_k, q_t, qa, k, ka, v_t, *, ka_const, name):
    bound = HEAD_DIM * jnp.max(jnp.abs(g_q)) * jnp.max(jnp.abs(g_k))
    return lax.cond(
        bound <= MAX_SAFE_EXPONENT,
        functools.partial(_attention_bounded, ka_const=ka_const, name=name + "_bounded"),
        functools.partial(_attention_online, ka_const=ka_const, name=name + "_online"),
        q_t, qa, k, ka, v_t)


def _post_kernel(x_ref, om_ref, of_ref, ga_ref, scm_ref, shm_ref, gm_ref,
                 gom_ref, gof_ref, wout_ref, w1_ref, w2_ref, o_ref, *, ff_chunk):
    def rms(v):
        return v * lax.rsqrt(jnp.mean(v * v, axis=-1, keepdims=True) + EPS)

    mixed = jnp.concatenate([(rms(om_ref[0]) * gom_ref[...]).astype(BF16),
                             (rms(of_ref[0]) * gof_ref[...]).astype(BF16)], axis=1)
    x1 = x_ref[0] + ga_ref[...] * _dot(mixed, wout_ref[...])
    h = (rms(x1) * (1.0 + scm_ref[...]) + shm_ref[...]).astype(BF16)
    d_ff = w1_ref.shape[1]
    y = None
    for c in range(d_ff // ff_chunk):
        hid = _dot(h, w1_ref[:, c * ff_chunk:(c + 1) * ff_chunk])
        hid = jnp.square(jnp.maximum(hid, 0.0)).astype(BF16)
        part = _dot(hid, w2_ref[c * ff_chunk:(c + 1) * ff_chunk, :])
        y = part if y is None else y + part
    o_ref[0] = x1 + gm_ref[...] * y


def _post(x, o_m, o_f, mod4, g_out_m, g_out_f, w_out, w_ff1, w_ff2, tm):
    bsz, s, d = x.shape
    d_ff = w_ff1.shape[1]

    def const(a):
        nd = a.ndim
        return pl.BlockSpec(a.shape, lambda b, i, _n=nd: (0,) * _n,
                            pipeline_mode=pl.Buffered(1))

    def mod_spec(k):
        return pl.BlockSpec((None, None, 1, d), lambda b, i, _k=k: (b, _k, 0, 0))

    tok = lambda w: pl.BlockSpec((1, tm, w), lambda b, i: (b, i, 0))
    return pl.pallas_call(
        functools.partial(_post_kernel, ff_chunk=1024),
        grid=(bsz, s // tm),
        in_specs=[tok(d), tok(GROUP_WIDTH), tok(GROUP_WIDTH),
                  mod_spec(2), mod_spec(4), mod_spec(3), mod_spec(5),
                  const(g_out_m), const(g_out_f), const(w_out), const(w_ff1), const(w_ff2)],
        out_specs=tok(d),
        out_shape=jax.ShapeDtypeStruct((bsz, s, d), F32),
        compiler_params=pltpu.CompilerParams(
            dimension_semantics=("arbitrary", "arbitrary"),
            vmem_limit_bytes=VMEM_LIMIT),
        name="post",
    )(x, o_m, o_f, mod4, mod4, mod4, mod4, g_out_m, g_out_f, w_out, w_ff1, w_ff2)


def _rope_tables(s):
    inv_freq = ROPE_THETA ** (-np.arange(0, ROPE_DIM, 2, dtype=np.float64) / ROPE_DIM)
    ang = np.arange(s, dtype=np.float64)[:, None] * inv_freq[None, :]
    cos, sin = np.cos(ang), np.sin(ang)
    rc = np.ones((s, LANES)); rs1 = np.zeros((s, LANES)); rs2 = np.zeros((s, LANES))
    for base in (0, HEAD_DIM):
        rc[:, base:base + ROPE_HALF] = cos
        rc[:, base + ROPE_HALF:base + ROPE_DIM] = cos
        rs2[:, base:base + ROPE_HALF] = -sin
        rs1[:, base + ROPE_HALF:base + ROPE_DIM] = sin
    f = lambda a: jnp.asarray(a, dtype=F32)
    return f(cos.T), f(sin.T), f(rc), f(rs1), f(rs2)


def _static_mats(s):
    idx = np.arange(256)
    jmat = (idx[:, None] // HEAD_DIM == idx[None, :] // HEAD_DIM) / HEAD_DIM
    place = np.zeros((3, LANES, GROUP_WIDTH))
    for part in range(3):
        for hd in range(GROUP_HEADS):
            place[part, hd, (hd // 2) * PAIR_WIDTH + (hd % 2) * HEAD_DIM + part] = 1.0
    kones = np.zeros((1, GROUP_WIDTH))
    for hd in range(GROUP_HEADS):
        kones[0, hd * HEAD_DIM + 3:hd * HEAD_DIM + 6] = 1.0
    ka_moba = np.zeros((1, s, PAIR_WIDTH))
    blk = np.arange(s) // MOBA_BLOCK
    for e in range(2):
        ka_moba[0, np.arange(s), e * HEAD_DIM + blk] = 1.0
    b = lambda a: jnp.asarray(a, dtype=BF16)
    return b(jmat), b(place), jnp.asarray(kones, dtype=F32), b(ka_moba)


def kernel(x, c, w_ada, b_ada, w_in, b_forget, g_qn_moba, g_kn_moba, g_qn_fox, g_kn_fox,
           g_out_moba, g_out_fox, w_out, w_ff1, w_ff2):
    bsz, s, d = x.shape
    depth = w_ada.shape[0]
    w = GROUP_WIDTH
    assert s % MOBA_BLOCK == 0 and s // MOBA_BLOCK <= 16
    tm_in = 512
    tm_post = 512
    cos_t, sin_t, rc, rs1, rs2 = _rope_tables(s)
    jmat, place, kones, ka_moba = _static_mats(s)
    q_scale = HEAD_DIM ** -0.5 * LOG2E

    for l in range(depth):
        mod = _adaln(c, w_ada[l], b_ada[l])
        mod4 = mod.reshape(bsz, 6, 1, d)

        wl = w_in[l]
        col = lambda k: wl[:, k * w:(k + 1) * w]
        wf = jnp.zeros((d, LANES), F32).at[:, :GROUP_HEADS].set(wl[:, 6 * w:]).astype(BF16)
        bfor = jnp.zeros((1, LANES), F32).at[0, :GROUP_HEADS].set(b_forget[l])
        q_gain = lambda g: jnp.broadcast_to((g * q_scale)[:, None], (HEAD_DIM, tm_in))
        k_gain = lambda g: jnp.tile(g, GROUP_HEADS).reshape(1, w)
        wts = (col(0).T.astype(BF16), col(1).astype(BF16), col(2).T.astype(BF16),
               col(3).T.astype(BF16), col(4).astype(BF16), col(5).T.astype(BF16),
               wf, bfor,
               q_gain(g_qn_moba[l]), k_gain(g_kn_moba[l]),
               q_gain(g_qn_fox[l]), k_gain(g_kn_fox[l]))
        consts = (cos_t, sin_t, rc, rs1, rs2, jmat, place, kones)

        qm_t, km, vm_t, bias_t, qf_t, kf, vf_t, ka_f, qa_f = _inproj(
            x, mod4[:, 1], mod4[:, 0], wts, consts, tm_in)

        o_m = _attention_dispatch(g_qn_moba[l] * q_scale, g_kn_moba[l],
                                  qm_t, bias_t, km, ka_moba, vm_t, ka_const=True,
                                  name="attn_moba")
        o_f = _attention_dispatch(g_qn_fox[l] * q_scale, g_kn_fox[l],
                                  qf_t, qa_f, kf, ka_f, vf_t, ka_const=False,
                                  name="attn_fox")

        x = _post(x, o_m, o_f, mod4,
                  g_out_moba[l].reshape(1, w), g_out_fox[l].reshape(1, w),
                  w_out[l].astype(BF16), w_ff1[l].astype(BF16), w_ff2[l].astype(BF16),
                  tm_post)
    return x
```

```python
import functools
import math

import numpy as np
import jax
import jax.numpy as jnp
from jax import lax
from jax.experimental import pallas as pl
from jax.experimental.pallas import tpu as pltpu

F32 = jnp.float32
BF16 = jnp.bfloat16
HIGHEST = lax.Precision.HIGHEST

HEAD_DIM = 64
GROUP_HEADS = 8
GROUP_WIDTH = GROUP_HEADS * HEAD_DIM
PAIR_WIDTH = 2 * HEAD_DIM
N_PAIRS = GROUP_HEADS // 2
MOBA_BLOCK = 256
MOBA_TOPK = 3
ROPE_THETA = 500000.0
ROPE_DIM = HEAD_DIM // 4
ROPE_HALF = ROPE_DIM // 2
EPS = 1e-6
LOG2E = math.log2(math.e)
NEG_BIG = -1e30
MAX_SAFE_EXPONENT = 60.0
ATT_TILE = 512
KEY_TILE = 256
CUM_BLOCK = 128
LANES = 128
VMEM_LIMIT = 48 * 1024 * 1024


def _dot(a, b, precision=None):
    return jnp.dot(a, b, preferred_element_type=F32, precision=precision)


def _split3(x):
    hi = x.astype(BF16)
    rem = x - hi.astype(F32)
    mid = rem.astype(BF16)
    lo = (rem - mid.astype(F32)).astype(BF16)
    return hi, mid, lo


def _dot_nt(a, b, precision=None):
    return lax.dot_general(a, b, (((1,), (1,)), ((), ())),
                           preferred_element_type=F32, precision=precision)


def _adaln_kernel(c_ref, w_ref, b_ref, o_ref):
    c = c_ref[...]
    s = c / (1.0 + jnp.exp(-c))
    o_ref[...] = _dot(s, w_ref[...], precision=HIGHEST) + b_ref[...]


def _adaln(c, w_ada, b_ada):
    bsz, d = c.shape
    n = w_ada.shape[1]
    return pl.pallas_call(
        _adaln_kernel,
        grid=(n // d,),
        in_specs=[pl.BlockSpec((bsz, d), lambda j: (0, 0)),
                  pl.BlockSpec((d, d), lambda j: (0, j)),
                  pl.BlockSpec((1, d), lambda j: (0, j))],
        out_specs=pl.BlockSpec((bsz, d), lambda j: (0, j)),
        out_shape=jax.ShapeDtypeStruct((bsz, n), F32),
        compiler_params=pltpu.CompilerParams(dimension_semantics=("arbitrary",)),
        name="adaln",
    )(c, w_ada, b_ada.reshape(1, n))


def _inproj_kernel(x_ref, sc_ref, sh_ref,
                   wqm_ref, wkm_ref, wvm_ref, wqf_ref, wkf_ref, wvf_ref, wf_ref, bf_ref,
                   gqm_ref, gkm_ref, gqf_ref, gkf_ref,
                   cos_t_ref, sin_t_ref, rc_ref, rs1_ref, rs2_ref,
                   j_ref, place_ref, kones_ref,
                   qm_o, km_o, vm_o, bias_o, qf_o, kf_o, vf_o, ka_o, qa_o,
                   kmean_sc, carry_sc, *, tm):
    i = pl.program_id(1)
    nb = tm // MOBA_BLOCK

    @pl.when(i == 0)
    def _():
        kmean_sc[...] = jnp.zeros_like(kmean_sc)
        carry_sc[...] = jnp.zeros_like(carry_sc)

    x = x_ref[0]
    ms = jnp.mean(x * x, axis=-1, keepdims=True)
    h = x * lax.rsqrt(ms + EPS) * (1.0 + sc_ref[...]) + sh_ref[...]
    hb = h.astype(BF16)

    def q_feature_major(w_ref, g_ref, rope):
        q_t = _dot_nt(w_ref[...], hb)
        gain = g_ref[...]
        heads = []
        for hd in range(GROUP_HEADS):
            xh = q_t[hd * HEAD_DIM:(hd + 1) * HEAD_DIM, :]
            ss = jnp.sum(xh * xh, axis=0, keepdims=True)
            xh = xh * lax.rsqrt(ss * (1.0 / HEAD_DIM) + EPS) * gain
            if rope:
                x1 = xh[0:ROPE_HALF, :]
                x2 = xh[ROPE_HALF:ROPE_DIM, :]
                cs = cos_t_ref[...]
                sn = sin_t_ref[...]
                xh = jnp.concatenate(
                    [x1 * cs - x2 * sn, x2 * cs + x1 * sn, xh[ROPE_DIM:, :]], axis=0)
            heads.append(xh)
        return jnp.concatenate(heads, axis=0)

    def k_token_major(w_ref, g_ref, rope):
        k = _dot(hb, w_ref[...])
        chunks = []
        for c in range(GROUP_WIDTH // 256):
            seg = k[:, c * 256:(c + 1) * 256]
            msq = _dot((seg * seg).astype(BF16), j_ref[...])
            chunks.append(seg * lax.rsqrt(msq + EPS) * g_ref[:, c * 256:(c + 1) * 256])
        k = jnp.concatenate(chunks, axis=1)
        if rope:
            outs = []
            for c in range(GROUP_WIDTH // LANES):
                seg = k[:, c * LANES:(c + 1) * LANES]
                outs.append(seg * rc_ref[...]
                            + pltpu.roll(seg, ROPE_HALF, 1) * rs1_ref[...]
                            + pltpu.roll(seg, LANES - ROPE_HALF, 1) * rs2_ref[...])
            k = jnp.concatenate(outs, axis=1)
        return k

    def store_tiles(o_ref, val_t, width=ATT_TILE):
        for t in range(tm // width):
            o_ref[0, t] = val_t[:, t * width:(t + 1) * width].astype(o_ref.dtype)

    qm_t = q_feature_major(wqm_ref, gqm_ref, True)
    store_tiles(qm_o, qm_t)
    km = k_token_major(wkm_ref, gkm_ref, True)
    km_o[0] = km.astype(BF16)
    store_tiles(vm_o, _dot_nt(wvm_ref[...], hb), KEY_TILE)

    blk0 = i * nb
    lane_head = lax.broadcasted_iota(jnp.int32, (1, GROUP_WIDTH), 1) // HEAD_DIM
    for sb in range(nb):
        kmean = jnp.mean(km[sb * MOBA_BLOCK:(sb + 1) * MOBA_BLOCK, :], axis=0, keepdims=True)
        for hd in range(GROUP_HEADS):
            row = hd * 16 + blk0 + sb
            kmean_sc[pl.ds(row, 1), :] = jnp.where(lane_head == hd, kmean, 0.0)

    km_hi, km_lo, _ = _split3(kmean_sc[...])
    q_hi, q_lo, _ = _split3(qm_t)
    gate_t = _dot(km_hi, q_hi) + _dot(km_hi, q_lo) + _dot(km_lo, q_hi)
    tok_blk = blk0 + lax.broadcasted_iota(jnp.int32, (16, tm), 1) // MOBA_BLOCK
    n_idx = lax.broadcasted_iota(jnp.int32, (16, tm), 0)
    valid = n_idx < tok_blk
    bias_heads = []
    for hd in range(GROUP_HEADS):
        g = jnp.where(valid, gate_t[hd * 16:(hd + 1) * 16, :], -jnp.inf)
        cnt = jnp.zeros((16, tm), jnp.int32)
        for m in range(16):
            gm = g[m:m + 1, :]
            cnt = cnt + jnp.where(gm > g, 1, jnp.where((gm == g) & (n_idx > m), 1, 0))
        sel = (valid & (cnt < MOBA_TOPK)) | (n_idx == tok_blk)
        bias_heads.append(jnp.where(sel, 0.0, NEG_BIG))
    zeros48 = jnp.zeros((HEAD_DIM - 16, tm), F32)
    pair_rows = []
    for hd in range(GROUP_HEADS):
        pair_rows += [bias_heads[hd], zeros48]
    store_tiles(bias_o, jnp.concatenate(pair_rows, axis=0))

    store_tiles(qf_o, q_feature_major(wqf_ref, gqf_ref, False))
    kf_o[0] = k_token_major(wkf_ref, gkf_ref, False).astype(BF16)
    store_tiles(vf_o, _dot_nt(wvf_ref[...], hb), KEY_TILE)

    f = _dot(hb, wf_ref[...]) + bf_ref[...]
    logf = jnp.minimum(f, 0.0) - jnp.log1p(jnp.exp(-jnp.abs(f)))
    live = lax.broadcasted_iota(jnp.int32, (1, LANES), 1) < GROUP_HEADS
    logf = jnp.where(live, logf, 0.0)
    r_i = lax.broadcasted_iota(jnp.int32, (CUM_BLOCK, CUM_BLOCK), 0)
    c_i = lax.broadcasted_iota(jnp.int32, (CUM_BLOCK, CUM_BLOCK), 1)
    tri = jnp.where(r_i >= c_i, 1.0, 0.0).astype(BF16)
    carry = carry_sc[...]
    cums = []
    for blk in range(tm // CUM_BLOCK):
        parts = _split3(logf[blk * CUM_BLOCK:(blk + 1) * CUM_BLOCK, :])
        c_blk = _dot(tri, parts[0]) + _dot(tri, parts[1]) + _dot(tri, parts[2]) + carry
        carry = c_blk[CUM_BLOCK - 1:CUM_BLOCK, :]
        cums.append(c_blk)
    carry_sc[...] = carry
    nc = jnp.concatenate(cums, axis=0) * (-LOG2E)
    hi, mid, lo = [part.astype(F32) for part in _split3(nc)]
    packed = hi + pltpu.roll(mid, GROUP_HEADS, 1) + pltpu.roll(lo, 2 * GROUP_HEADS, 1)
    ka_o[0] = (_dot(packed.astype(BF16), place_ref[...]) + kones_ref[...]).astype(BF16)

    nc_t = nc.T[0:GROUP_HEADS, :]
    hi_t = nc_t.astype(BF16).astype(F32)
    rem_t = nc_t - hi_t
    mid_t = rem_t.astype(BF16).astype(F32)
    lo_t = (rem_t - mid_t).astype(BF16).astype(F32)
    r8 = lax.broadcasted_iota(jnp.int32, (8, tm), 0)
    zeros56 = jnp.zeros((HEAD_DIM - 8, tm), F32)
    qa_rows = []
    for hd in range(GROUP_HEADS):
        blk = jnp.where(r8 < 3, 1.0,
                        jnp.where(r8 == 3, -hi_t[hd:hd + 1, :],
                                  jnp.where(r8 == 4, -mid_t[hd:hd + 1, :],
                                            jnp.where(r8 == 5, -lo_t[hd:hd + 1, :], 0.0))))
        qa_rows += [blk, zeros56]
    store_tiles(qa_o, jnp.concatenate(qa_rows, axis=0))


def _inproj(x, sc, sh, wts, consts, tm):
    bsz, s, d = x.shape
    nt = s // ATT_TILE
    grid = (bsz, s // tm)
    wqm, wkm, wvm, wqf, wkf, wvf, wf, bfor, gqm, gkm, gqf, gkf = wts
    cos_t, sin_t, rc, rs1, rs2, jmat, place, kones = consts

    def const(a):
        nd = a.ndim
        return pl.BlockSpec(a.shape, lambda b, i, _n=nd: (0,) * _n,
                            pipeline_mode=pl.Buffered(1))

    mod_spec = pl.BlockSpec((None, 1, d), lambda b, i: (b, 0, 0))
    in_specs = [pl.BlockSpec((1, tm, d), lambda b, i: (b, i, 0)), mod_spec, mod_spec]
    in_specs += [const(a) for a in (wqm, wkm, wvm, wqf, wkf, wvf, wf, bfor,
                                    gqm, gkm, gqf, gkf)]
    in_specs += [pl.BlockSpec((ROPE_HALF, tm), lambda b, i: (0, i)),
                 pl.BlockSpec((ROPE_HALF, tm), lambda b, i: (0, i)),
                 pl.BlockSpec((tm, LANES), lambda b, i: (i, 0)),
                 pl.BlockSpec((tm, LANES), lambda b, i: (i, 0)),
                 pl.BlockSpec((tm, LANES), lambda b, i: (i, 0)),
                 const(jmat), const(place), const(kones)]

    tiles = tm // ATT_TILE
    feat_spec = pl.BlockSpec((1, tiles, GROUP_WIDTH, ATT_TILE), lambda b, i: (b, i, 0, 0))
    tok_spec = pl.BlockSpec((1, tm, GROUP_WIDTH), lambda b, i: (b, i, 0))
    feat_shape = jax.ShapeDtypeStruct((bsz, nt, GROUP_WIDTH, ATT_TILE), BF16)
    tok_shape = jax.ShapeDtypeStruct((bsz, s, GROUP_WIDTH), BF16)
    val_spec = pl.BlockSpec((1, tm // KEY_TILE, GROUP_WIDTH, KEY_TILE),
                            lambda b, i: (b, i, 0, 0))
    val_shape = jax.ShapeDtypeStruct((bsz, s // KEY_TILE, GROUP_WIDTH, KEY_TILE), BF16)
    out_specs = [feat_spec, tok_spec, val_spec, feat_spec,
                 feat_spec, tok_spec, val_spec, tok_spec, feat_spec]
    out_shape = [feat_shape, tok_shape, val_shape, feat_shape,
                 feat_shape, tok_shape, val_shape, tok_shape, feat_shape]
    return pl.pallas_call(
        functools.partial(_inproj_kernel, tm=tm),
        grid=grid,
        in_specs=in_specs,
        out_specs=out_specs,
        out_shape=out_shape,
        scratch_shapes=[pltpu.VMEM((GROUP_HEADS * 16, GROUP_WIDTH), F32),
                        pltpu.VMEM((1, LANES), F32)],
        compiler_params=pltpu.CompilerParams(
            dimension_semantics=("arbitrary", "arbitrary"),
            vmem_limit_bytes=VMEM_LIMIT),
        name="inproj",
    )(x, sc, sh, wqm, wkm, wvm, wqf, wkf, wvf, wf, bfor, gqm, gkm, gqf, gkf,
      cos_t, sin_t, rc, rs1, rs2, jmat, place, kones)


def _attn_kernel(q_ref, qa_ref, k_ref, ka_ref, v_ref, o_ref, *, t):
    i = pl.program_id(2)
    q = q_ref[0, 0]
    qa = qa_ref[0, 0]
    row = lax.broadcasted_iota(jnp.int32, (PAIR_WIDTH, t), 0)
    rhs = []
    for e in range(2):
        keep = (row >= e * HEAD_DIM) & (row < (e + 1) * HEAD_DIM)
        zero = jnp.zeros_like(q)
        rhs.append(jnp.concatenate([jnp.where(keep, q, zero), jnp.where(keep, qa, zero)],
                                   axis=0))

    def tile_scores(j):
        start = pl.multiple_of(j * t, t)
        lhs = jnp.concatenate([k_ref[0, pl.ds(start, t), :], ka_ref[0, pl.ds(start, t), :]],
                              axis=1)
        return [_dot(lhs, rhs[e]) for e in range(2)]

    def values(j, e):
        rows = slice(e * HEAD_DIM, (e + 1) * HEAD_DIM)
        slabs = t // KEY_TILE
        return jnp.concatenate([v_ref[0, slabs * j + h, rows, :] for h in range(slabs)],
                               axis=1)

    key_i = lax.broadcasted_iota(jnp.int32, (t, t), 0)
    qry_i = lax.broadcasted_iota(jnp.int32, (t, t), 1)
    causal = key_i <= qry_i
    state = []
    for e, s_t in enumerate(tile_scores(i)):
        s_t = jnp.where(causal, s_t, NEG_BIG)
        m = jnp.max(s_t, axis=0, keepdims=True)
        p = jnp.exp2(s_t - m)
        l = jnp.sum(p, axis=0, keepdims=True)
        acc = _dot(values(i, e), p.astype(BF16))
        state += [m, l, acc]

    def body(j, carry):
        out = []
        for e, s_t in enumerate(tile_scores(j)):
            m, l, acc = carry[3 * e:3 * e + 3]
            m_new = jnp.maximum(m, jnp.max(s_t, axis=0, keepdims=True))
            alpha = jnp.exp2(m - m_new)
            p = jnp.exp2(s_t - m_new)
            l = alpha * l + jnp.sum(p, axis=0, keepdims=True)
            acc = alpha * acc + _dot(values(j, e), p.astype(BF16))
            out += [m_new, l, acc]
        return tuple(out)

    state = lax.fori_loop(0, i, body, tuple(state))
    o_t = jnp.concatenate([state[2] / state[1], state[5] / state[4]], axis=0)
    o_ref[0] = o_t.T


def _attn_bounded_kernel(q_ref, qa_ref, k_ref, ka_ref, v_ref, o_ref, p_sc, *, t,
                         ka_const):
    i = pl.program_id(1)
    kt = KEY_TILE
    row = lax.broadcasted_iota(jnp.int32, (PAIR_WIDTH, t), 0)
    ones = jnp.ones((16, kt), BF16)
    key_i = lax.broadcasted_iota(jnp.int32, (kt, t), 0)
    qry_i = lax.broadcasted_iota(jnp.int32, (kt, t), 1)
    assert t == 2 * kt

    for pr in range(N_PAIRS):
        st = pr % 2
        feat = slice(pr * PAIR_WIDTH, (pr + 1) * PAIR_WIDTH)
        q = q_ref[0, 0, feat, :]
        qa = qa_ref[0, 0, feat, :]
        rhs = []
        for e in range(2):
            keep = (row >= e * HEAD_DIM) & (row < (e + 1) * HEAD_DIM)
            zero = jnp.zeros_like(q)
            rhs.append(jnp.concatenate([jnp.where(keep, q, zero), jnp.where(keep, qa, zero)],
                                       axis=0))

        def probs_to(slot, jk, mask=None, feat=feat, rhs=rhs, st=st):
            start = pl.multiple_of(jk * kt, kt)
            k_aug = (ka_ref[0, pl.ds(start, kt), :] if ka_const
                     else ka_ref[0, pl.ds(start, kt), feat])
            lhs = jnp.concatenate([k_ref[0, pl.ds(start, kt), feat], k_aug], axis=1)
            for e in range(2):
                s_t = _dot(lhs, rhs[e])
                if mask is not None:
                    s_t = jnp.where(mask, s_t, NEG_BIG)
                p_sc[st, slot, e] = jnp.exp2(s_t).astype(BF16)

        def values_from(slot, jk, acc, st=st, pr=pr):
            out = []
            for e in range(2):
                r0 = pr * PAIR_WIDTH + e * HEAD_DIM
                v_e = jnp.concatenate([v_ref[0, jk, r0:r0 + HEAD_DIM, :], ones], axis=0)
                out.append(acc[e] + _dot(v_e, p_sc[st, slot, e]))
            return out

        d0 = 2 * i
        probs_to(0, d0, key_i <= qry_i)
        probs_to(1, d0 + 1, key_i + kt <= qry_i)
        acc = values_from(0, d0, [jnp.zeros((HEAD_DIM + 16, t), F32)] * 2)

        def body(m, carry, probs_to=probs_to, values_from=values_from):
            acc0, acc1, j_prev = carry
            probs_to(0, 2 * m)
            acc = values_from(1, j_prev, [acc0, acc1])
            probs_to(1, 2 * m + 1)
            acc = values_from(0, 2 * m, acc)
            return acc[0], acc[1], 2 * m + 1

        acc0, acc1, j_prev = lax.fori_loop(0, i, body, (acc[0], acc[1], d0 + 1))
        acc = values_from(1, j_prev, [acc0, acc1])
        o_t = jnp.concatenate([a[0:HEAD_DIM, :] / a[HEAD_DIM:HEAD_DIM + 1, :] for a in acc],
                              axis=0)
        o_ref[0, :, feat] = o_t.T


def _attention_online(q_t, qa, k, ka, v_t, *, ka_const, name):
    bsz, nt, _, t = q_t.shape
    s = k.shape[1]
    grid = (bsz, N_PAIRS, nt)
    if ka_const:
        ka_spec = pl.BlockSpec((1, s, PAIR_WIDTH), lambda b, p, i: (0, 0, 0))
    else:
        ka_spec = pl.BlockSpec((1, s, PAIR_WIDTH), lambda b, p, i: (b, 0, p))
    return pl.pallas_call(
        functools.partial(_attn_kernel, t=t),
        grid=grid,
        in_specs=[pl.BlockSpec((1, 1, PAIR_WIDTH, t), lambda b, p, i: (b, i, p, 0)),
                  pl.BlockSpec((1, 1, PAIR_WIDTH, t), lambda b, p, i: (b, i, p, 0)),
                  pl.BlockSpec((1, s, PAIR_WIDTH), lambda b, p, i: (b, 0, p)),
                  ka_spec,
                  pl.BlockSpec((1, s // KEY_TILE, PAIR_WIDTH, KEY_TILE),
                               lambda b, p, i: (b, 0, p, 0))],
        out_specs=pl.BlockSpec((1, t, PAIR_WIDTH), lambda b, p, i: (b, i, p)),
        out_shape=jax.ShapeDtypeStruct((bsz, s, GROUP_WIDTH), F32),
        compiler_params=pltpu.CompilerParams(
            dimension_semantics=("arbitrary", "arbitrary", "arbitrary"),
            vmem_limit_bytes=VMEM_LIMIT),
        name=name,
    )(q_t, qa, k, ka, v_t)


def _attention_bounded(q_t, qa, k, ka, v_t, *, ka_const, name):
    bsz, nt, _, t = q_t.shape
    s = k.shape[1]
    feat_spec = pl.BlockSpec((1, 1, GROUP_WIDTH, t), lambda b, i: (b, i, 0, 0))
    tok_spec = pl.BlockSpec((1, s, GROUP_WIDTH), lambda b, i: (b, 0, 0))
    ka_spec = pl.BlockSpec((1, s, PAIR_WIDTH), lambda b, i: (0, 0, 0)) if ka_const else tok_spec
    return pl.pallas_call(
        functools.partial(_attn_bounded_kernel, t=t, ka_const=ka_const),
        grid=(bsz, nt),
        in_specs=[feat_spec, feat_spec, tok_spec, ka_spec,
                  pl.BlockSpec((1, s // KEY_TILE, GROUP_WIDTH, KEY_TILE),
                               lambda b, i: (b, 0, 0, 0))],
        out_specs=pl.BlockSpec((1, t, GROUP_WIDTH), lambda b, i: (b, i, 0)),
        out_shape=jax.ShapeDtypeStruct((bsz, s, GROUP_WIDTH), F32),
        scratch_shapes=[pltpu.VMEM((2, 2, 2, KEY_TILE, t), BF16)],
        compiler_params=pltpu.CompilerParams(
            dimension_semantics=("arbitrary", "arbitrary"),
            vmem_limit_bytes=VMEM_LIMIT),
        name=name,
    )(q_t, qa, k, ka, v_t)


def _attention_dispatch(g_q, g_k, q_t, qa, k, ka, v_t, *, ka_const, name):
    bound = HEAD_DIM * jnp.max(jnp.abs(g_q)) * jnp.max(jnp.abs(g_k))
    return lax.cond(
        bound <= MAX_SAFE_EXPONENT,
        functools.partial(_attention_bounded, ka_const=ka_const, name=name + "_bounded"),
        functools.partial(_attention_online, ka_const=ka_const, name=name + "_online"),
        q_t, qa, k, ka, v_t)


def _post_kernel(x_ref, om_ref, of_ref, ga_ref, scm_ref, shm_ref, gm_ref,
                 gom_ref, gof_ref, wout_ref, w1_ref, w2_ref, o_ref, *, ff_chunk):
    def rms(v):
        return v * lax.rsqrt(jnp.mean(v * v, axis=-1, keepdims=True) + EPS)

    mixed = jnp.concatenate([(rms(om_ref[0]) * gom_ref[...]).astype(BF16),
                             (rms(of_ref[0]) * gof_ref[...]).astype(BF16)], axis=1)
    x1 = x_ref[0] + ga_ref[...] * _dot(mixed, wout_ref[...])
    h = (rms(x1) * (1.0 + scm_ref[...]) + shm_ref[...]).astype(BF16)
    d_ff = w1_ref.shape[1]
    y = None
    for c in range(d_ff // ff_chunk):
        hid = _dot(h, w1_ref[:, c * ff_chunk:(c + 1) * ff_chunk])
        hid = jnp.square(jnp.maximum(hid, 0.0)).astype(BF16)
        part = _dot(hid, w2_ref[c * ff_chunk:(c + 1) * ff_chunk, :])
        y = part if y is None else y + part
    o_ref[0] = x1 + gm_ref[...] * y


def _post(x, o_m, o_f, mod4, g_out_m, g_out_f, w_out, w_ff1, w_ff2, tm):
    bsz, s, d = x.shape
    d_ff = w_ff1.shape[1]

    def const(a):
        nd = a.ndim
        return pl.BlockSpec(a.shape, lambda b, i, _n=nd: (0,) * _n,
                            pipeline_mode=pl.Buffered(1))

    def mod_spec(k):
        return pl.BlockSpec((None, None, 1, d), lambda b, i, _k=k: (b, _k, 0, 0))

    tok = lambda w: pl.BlockSpec((1, tm, w), lambda b, i: (b, i, 0))
    return pl.pallas_call(
        functools.partial(_post_kernel, ff_chunk=1024),
        grid=(bsz, s // tm),
        in_specs=[tok(d), tok(GROUP_WIDTH), tok(GROUP_WIDTH),
                  mod_spec(2), mod_spec(4), mod_spec(3), mod_spec(5),
                  const(g_out_m), const(g_out_f), const(w_out), const(w_ff1), const(w_ff2)],
        out_specs=tok(d),
        out_shape=jax.ShapeDtypeStruct((bsz, s, d), F32),
        compiler_params=pltpu.CompilerParams(
            dimension_semantics=("arbitrary", "arbitrary"),
            vmem_limit_bytes=VMEM_LIMIT),
        name="post",
    )(x, o_m, o_f, mod4, mod4, mod4, mod4, g_out_m, g_out_f, w_out, w_ff1, w_ff2)


def _rope_tables(s):
    inv_freq = ROPE_THETA ** (-np.arange(0, ROPE_DIM, 2, dtype=np.float64) / ROPE_DIM)
    ang = np.arange(s, dtype=np.float64)[:, None] * inv_freq[None, :]
    cos, sin = np.cos(ang), np.sin(ang)
    rc = np.ones((s, LANES)); rs1 = np.zeros((s, LANES)); rs2 = np.zeros((s, LANES))
    for base in (0, HEAD_DIM):
        rc[:, base:base + ROPE_HALF] = cos
        rc[:, base + ROPE_HALF:base + ROPE_DIM] = cos
        rs2[:, base:base + ROPE_HALF] = -sin
        rs1[:, base + ROPE_HALF:base + ROPE_DIM] = sin
    f = lambda a: jnp.asarray(a, dtype=F32)
    return f(cos.T), f(sin.T), f(rc), f(rs1), f(rs2)


def _static_mats(s):
    idx = np.arange(256)
    jmat = (idx[:, None] // HEAD_DIM == idx[None, :] // HEAD_DIM) / HEAD_DIM
    place = np.zeros((LANES, GROUP_WIDTH))
    for part in range(3):
        for hd in range(GROUP_HEADS):
            place[part * GROUP_HEADS + hd,
                  (hd // 2) * PAIR_WIDTH + (hd % 2) * HEAD_DIM + part] = 1.0
    kones = np.zeros((1, GROUP_WIDTH))
    for hd in range(GROUP_HEADS):
        kones[0, hd * HEAD_DIM + 3:hd * HEAD_DIM + 6] = 1.0
    ka_moba = np.zeros((1, s, PAIR_WIDTH))
    blk = np.arange(s) // MOBA_BLOCK
    for e in range(2):
        ka_moba[0, np.arange(s), e * HEAD_DIM + blk] = 1.0
    b = lambda a: jnp.asarray(a, dtype=BF16)
    return b(jmat), b(place), jnp.asarray(kones, dtype=F32), b(ka_moba)


def kernel(x, c, w_ada, b_ada, w_in, b_forget, g_qn_moba, g_kn_moba, g_qn_fox, g_kn_fox,
           g_out_moba, g_out_fox, w_out, w_ff1, w_ff2):
    bsz, s, d = x.shape
    depth = w_ada.shape[0]
    w = GROUP_WIDTH
    assert s % MOBA_BLOCK == 0 and s // MOBA_BLOCK <= 16
    tm_in = 512
    tm_post = 512
    cos_t, sin_t, rc, rs1, rs2 = _rope_tables(s)
    jmat, place, kones, ka_moba = _static_mats(s)
    q_scale = HEAD_DIM ** -0.5 * LOG2E

    for l in range(depth):
        mod = _adaln(c, w_ada[l], b_ada[l])
        mod4 = mod.reshape(bsz, 6, 1, d)

        wl = w_in[l]
        col = lambda k: wl[:, k * w:(k + 1) * w]
        wf = jnp.zeros((d, LANES), F32).at[:, :GROUP_HEADS].set(wl[:, 6 * w:]).astype(BF16)
        bfor = jnp.zeros((1, LANES), F32).at[0, :GROUP_HEADS].set(b_forget[l])
        q_gain = lambda g: jnp.broadcast_to((g * q_scale)[:, None], (HEAD_DIM, tm_in))
        k_gain = lambda g: jnp.tile(g, GROUP_HEADS).reshape(1, w)
        wts = (col(0).T.astype(BF16), col(1).astype(BF16), col(2).T.astype(BF16),
               col(3).T.astype(BF16), col(4).astype(BF16), col(5).T.astype(BF16),
               wf, bfor,
               q_gain(g_qn_moba[l]), k_gain(g_kn_moba[l]),
               q_gain(g_qn_fox[l]), k_gain(g_kn_fox[l]))
        consts = (cos_t, sin_t, rc, rs1, rs2, jmat, place, kones)

        qm_t, km, vm_t, bias_t, qf_t, kf, vf_t, ka_f, qa_f = _inproj(
            x, mod4[:, 1], mod4[:, 0], wts, consts, tm_in)

        o_m = _attention_dispatch(g_qn_moba[l] * q_scale, g_kn_moba[l],
                                  qm_t, bias_t, km, ka_moba, vm_t, ka_const=True,
                                  name="attn_moba")
        o_f = _attention_dispatch(g_qn_fox[l] * q_scale, g_kn_fox[l],
                                  qf_t, qa_f, kf, ka_f, vf_t, ka_const=False,
                                  name="attn_fox")

        x = _post(x, o_m, o_f, mod4,
                  g_out_moba[l].reshape(1, w), g_out_fox[l].reshape(1, w),
                  w_out[l].astype(BF16), w_ff1[l].astype(BF16), w_ff2[l].astype(BF16),
                  tm_post)
    return x
```

```python
import functools
import math

import numpy as np
import jax
import jax.numpy as jnp
from jax import lax
from jax.experimental import pallas as pl
from jax.experimental.pallas import tpu as pltpu

F32 = jnp.float32
BF16 = jnp.bfloat16
HIGHEST = lax.Precision.HIGHEST

HEAD_DIM = 64
GROUP_HEADS = 8
GROUP_WIDTH = GROUP_HEADS * HEAD_DIM
PAIR_WIDTH = 2 * HEAD_DIM
N_PAIRS = GROUP_HEADS // 2
MOBA_BLOCK = 256
MOBA_TOPK = 3
ROPE_THETA = 500000.0
ROPE_DIM = HEAD_DIM // 4
ROPE_HALF = ROPE_DIM // 2
EPS = 1e-6
LOG2E = math.log2(math.e)
NEG_BIG = -1e30
MAX_SAFE_EXPONENT = 60.0
ATT_TILE = 512
KEY_TILE = 256
CUM_BLOCK = 128
PAIRS_PER_TRIP = 4
LANES = 128
VMEM_LIMIT = 48 * 1024 * 1024


def _dot(a, b, precision=None):
    return jnp.dot(a, b, preferred_element_type=F32, precision=precision)


def _split3(x):
    hi = x.astype(BF16)
    rem = x - hi.astype(F32)
    mid = rem.astype(BF16)
    lo = (rem - mid.astype(F32)).astype(BF16)
    return hi, mid, lo


def _dot_nt(a, b, precision=None):
    return lax.dot_general(a, b, (((1,), (1,)), ((), ())),
                           preferred_element_type=F32, precision=precision)


def _adaln_kernel(c_ref, w_ref, b_ref, o_ref):
    c = c_ref[...]
    s = c / (1.0 + jnp.exp(-c))
    o_ref[...] = _dot(s, w_ref[...], precision=HIGHEST) + b_ref[...]


def _adaln(c, w_ada, b_ada):
    bsz, d = c.shape
    n = w_ada.shape[1]
    return pl.pallas_call(
        _adaln_kernel,
        grid=(n // d,),
        in_specs=[pl.BlockSpec((bsz, d), lambda j: (0, 0)),
                  pl.BlockSpec((d, d), lambda j: (0, j)),
                  pl.BlockSpec((1, d), lambda j: (0, j))],
        out_specs=pl.BlockSpec((bsz, d), lambda j: (0, j)),
        out_shape=jax.ShapeDtypeStruct((bsz, n), F32),
        compiler_params=pltpu.CompilerParams(dimension_semantics=("arbitrary",)),
        name="adaln",
    )(c, w_ada, b_ada.reshape(1, n))


def _inproj_kernel(x_ref, sc_ref, sh_ref,
                   wqm_ref, wkm_ref, wvm_ref, wqf_ref, wkf_ref, wvf_ref, wf_ref, bf_ref,
                   gqm_ref, gkm_ref, gqf_ref, gkf_ref,
                   cos_t_ref, sin_t_ref, rc_ref, rs1_ref, rs2_ref,
                   j_ref, place_ref, kones_ref,
                   qm_o, km_o, vm_o, bias_o, qf_o, kf_o, vf_o, ka_o, qa_o,
                   kmean_sc, carry_sc, *, tm):
    i = pl.program_id(1)
    nb = tm // MOBA_BLOCK

    @pl.when(i == 0)
    def _():
        kmean_sc[...] = jnp.zeros_like(kmean_sc)
        carry_sc[...] = jnp.zeros_like(carry_sc)

    x = x_ref[0]
    ms = jnp.mean(x * x, axis=-1, keepdims=True)
    h = x * lax.rsqrt(ms + EPS) * (1.0 + sc_ref[...]) + sh_ref[...]
    hb = h.astype(BF16)

    def q_feature_major(w_ref, g_ref, rope):
        q_t = _dot_nt(w_ref[...], hb)
        gain = g_ref[...]
        heads = []
        for hd in range(GROUP_HEADS):
            xh = q_t[hd * HEAD_DIM:(hd + 1) * HEAD_DIM, :]
            ss = jnp.sum(xh * xh, axis=0, keepdims=True)
            xh = xh * lax.rsqrt(ss * (1.0 / HEAD_DIM) + EPS) * gain
            if rope:
                x1 = xh[0:ROPE_HALF, :]
                x2 = xh[ROPE_HALF:ROPE_DIM, :]
                cs = cos_t_ref[...]
                sn = sin_t_ref[...]
                xh = jnp.concatenate(
                    [x1 * cs - x2 * sn, x2 * cs + x1 * sn, xh[ROPE_DIM:, :]], axis=0)
            heads.append(xh)
        return jnp.concatenate(heads, axis=0)

    def k_token_major(w_ref, g_ref, rope):
        k = _dot(hb, w_ref[...])
        chunks = []
        for c in range(GROUP_WIDTH // 256):
            seg = k[:, c * 256:(c + 1) * 256]
            msq = _dot((seg * seg).astype(BF16), j_ref[...])
            chunks.append(seg * lax.rsqrt(msq + EPS) * g_ref[:, c * 256:(c + 1) * 256])
        k = jnp.concatenate(chunks, axis=1)
        if rope:
            outs = []
            for c in range(GROUP_WIDTH // LANES):
                seg = k[:, c * LANES:(c + 1) * LANES]
                outs.append(seg * rc_ref[...]
                            + pltpu.roll(seg, ROPE_HALF, 1) * rs1_ref[...]
                            + pltpu.roll(seg, LANES - ROPE_HALF, 1) * rs2_ref[...])
            k = jnp.concatenate(outs, axis=1)
        return k

    def store_tiles(o_ref, val_t, width=ATT_TILE):
        for t in range(tm // width):
            o_ref[0, t] = val_t[:, t * width:(t + 1) * width].astype(o_ref.dtype)

    qm_t = q_feature_major(wqm_ref, gqm_ref, True)
    store_tiles(qm_o, qm_t)
    km = k_token_major(wkm_ref, gkm_ref, True)
    km_o[0] = km.astype(BF16)
    store_tiles(vm_o, _dot_nt(wvm_ref[...], hb), KEY_TILE)

    blk0 = i * nb
    lane_head = lax.broadcasted_iota(jnp.int32, (1, GROUP_WIDTH), 1) // HEAD_DIM
    for sb in range(nb):
        kmean = jnp.mean(km[sb * MOBA_BLOCK:(sb + 1) * MOBA_BLOCK, :], axis=0, keepdims=True)
        for hd in range(GROUP_HEADS):
            row = hd * 16 + blk0 + sb
            kmean_sc[pl.ds(row, 1), :] = jnp.where(lane_head == hd, kmean, 0.0)

    km_hi, km_lo, _ = _split3(kmean_sc[...])
    q_hi, q_lo, _ = _split3(qm_t)
    gate_t = _dot(km_hi, q_hi) + _dot(km_hi, q_lo) + _dot(km_lo, q_hi)
    tok_blk = blk0 + lax.broadcasted_iota(jnp.int32, (16, tm), 1) // MOBA_BLOCK
    n_idx = lax.broadcasted_iota(jnp.int32, (16, tm), 0)
    valid = n_idx < tok_blk
    bias_heads = []
    for hd in range(GROUP_HEADS):
        g = jnp.where(valid, gate_t[hd * 16:(hd + 1) * 16, :], -jnp.inf)
        cnt = jnp.zeros((16, tm), jnp.int32)
        for m in range(16):
            gm = g[m:m + 1, :]
            cnt = cnt + jnp.where(gm > g, 1, jnp.where((gm == g) & (n_idx > m), 1, 0))
        sel = (valid & (cnt < MOBA_TOPK)) | (n_idx == tok_blk)
        bias_heads.append(jnp.where(sel, 0.0, NEG_BIG))
    zeros48 = jnp.zeros((HEAD_DIM - 16, tm), F32)
    pair_rows = []
    for hd in range(GROUP_HEADS):
        pair_rows += [bias_heads[hd], zeros48]
    store_tiles(bias_o, jnp.concatenate(pair_rows, axis=0))

    store_tiles(qf_o, q_feature_major(wqf_ref, gqf_ref, False))
    kf_o[0] = k_token_major(wkf_ref, gkf_ref, False).astype(BF16)
    store_tiles(vf_o, _dot_nt(wvf_ref[...], hb), KEY_TILE)

    f = _dot(hb, wf_ref[...]) + bf_ref[...]
    logf = jnp.minimum(f, 0.0) - jnp.log1p(jnp.exp(-jnp.abs(f)))
    live = lax.broadcasted_iota(jnp.int32, (1, LANES), 1) < GROUP_HEADS
    logf = jnp.where(live, logf, 0.0)
    r_i = lax.broadcasted_iota(jnp.int32, (CUM_BLOCK, CUM_BLOCK), 0)
    c_i = lax.broadcasted_iota(jnp.int32, (CUM_BLOCK, CUM_BLOCK), 1)
    tri = jnp.where(r_i >= c_i, 1.0, 0.0).astype(BF16)
    carry = carry_sc[...]
    cums = []
    for blk in range(tm // CUM_BLOCK):
        parts = _split3(logf[blk * CUM_BLOCK:(blk + 1) * CUM_BLOCK, :])
        c_blk = _dot(tri, parts[0]) + _dot(tri, parts[1]) + _dot(tri, parts[2]) + carry
        carry = c_blk[CUM_BLOCK - 1:CUM_BLOCK, :]
        cums.append(c_blk)
    carry_sc[...] = carry
    nc = jnp.concatenate(cums, axis=0) * (-LOG2E)
    hi, mid, lo = [part.astype(F32) for part in _split3(nc)]
    packed = hi + pltpu.roll(mid, GROUP_HEADS, 1) + pltpu.roll(lo, 2 * GROUP_HEADS, 1)
    ka_o[0] = (_dot(packed.astype(BF16), place_ref[...]) + kones_ref[...]).astype(BF16)

    nc_t = nc.T[0:GROUP_HEADS, :]
    hi_t = nc_t.astype(BF16).astype(F32)
    rem_t = nc_t - hi_t
    mid_t = rem_t.astype(BF16).astype(F32)
    lo_t = (rem_t - mid_t).astype(BF16).astype(F32)
    r8 = lax.broadcasted_iota(jnp.int32, (8, tm), 0)
    zeros56 = jnp.zeros((HEAD_DIM - 8, tm), F32)
    qa_rows = []
    for hd in range(GROUP_HEADS):
        blk = jnp.where(r8 < 3, 1.0,
                        jnp.where(r8 == 3, -hi_t[hd:hd + 1, :],
                                  jnp.where(r8 == 4, -mid_t[hd:hd + 1, :],
                                            jnp.where(r8 == 5, -lo_t[hd:hd + 1, :], 0.0))))
        qa_rows += [blk, zeros56]
    store_tiles(qa_o, jnp.concatenate(qa_rows, axis=0))


def _inproj(x, sc, sh, wts, consts, tm):
    bsz, s, d = x.shape
    nt = s // ATT_TILE
    grid = (bsz, s // tm)
    wqm, wkm, wvm, wqf, wkf, wvf, wf, bfor, gqm, gkm, gqf, gkf = wts
    cos_t, sin_t, rc, rs1, rs2, jmat, place, kones = consts

    def const(a):
        nd = a.ndim
        return pl.BlockSpec(a.shape, lambda b, i, _n=nd: (0,) * _n,
                            pipeline_mode=pl.Buffered(1))

    mod_spec = pl.BlockSpec((None, 1, d), lambda b, i: (b, 0, 0))
    in_specs = [pl.BlockSpec((1, tm, d), lambda b, i: (b, i, 0)), mod_spec, mod_spec]
    in_specs += [const(a) for a in (wqm, wkm, wvm, wqf, wkf, wvf, wf, bfor,
                                    gqm, gkm, gqf, gkf)]
    in_specs += [pl.BlockSpec((ROPE_HALF, tm), lambda b, i: (0, i)),
                 pl.BlockSpec((ROPE_HALF, tm), lambda b, i: (0, i)),
                 pl.BlockSpec((tm, LANES), lambda b, i: (i, 0)),
                 pl.BlockSpec((tm, LANES), lambda b, i: (i, 0)),
                 pl.BlockSpec((tm, LANES), lambda b, i: (i, 0)),
                 const(jmat), const(place), const(kones)]

    tiles = tm // ATT_TILE
    feat_spec = pl.BlockSpec((1, tiles, GROUP_WIDTH, ATT_TILE), lambda b, i: (b, i, 0, 0))
    tok_spec = pl.BlockSpec((1, tm, GROUP_WIDTH), lambda b, i: (b, i, 0))
    feat_shape = jax.ShapeDtypeStruct((bsz, nt, GROUP_WIDTH, ATT_TILE), BF16)
    tok_shape = jax.ShapeDtypeStruct((bsz, s, GROUP_WIDTH), BF16)
    val_spec = pl.BlockSpec((1, tm // KEY_TILE, GROUP_WIDTH, KEY_TILE),
                            lambda b, i: (b, i, 0, 0))
    val_shape = jax.ShapeDtypeStruct((bsz, s // KEY_TILE, GROUP_WIDTH, KEY_TILE), BF16)
    out_specs = [feat_spec, tok_spec, val_spec, feat_spec,
                 feat_spec, tok_spec, val_spec, tok_spec, feat_spec]
    out_shape = [feat_shape, tok_shape, val_shape, feat_shape,
                 feat_shape, tok_shape, val_shape, tok_shape, feat_shape]
    return pl.pallas_call(
        functools.partial(_inproj_kernel, tm=tm),
        grid=grid,
        in_specs=in_specs,
        out_specs=out_specs,
        out_shape=out_shape,
        scratch_shapes=[pltpu.VMEM((GROUP_HEADS * 16, GROUP_WIDTH), F32),
                        pltpu.VMEM((1, LANES), F32)],
        compiler_params=pltpu.CompilerParams(
            dimension_semantics=("arbitrary", "arbitrary"),
            vmem_limit_bytes=VMEM_LIMIT),
        name="inproj",
    )(x, sc, sh, wqm, wkm, wvm, wqf, wkf, wvf, wf, bfor, gqm, gkm, gqf, gkf,
      cos_t, sin_t, rc, rs1, rs2, jmat, place, kones)


def _attn_kernel(q_ref, qa_ref, k_ref, ka_ref, v_ref, o_ref, *, t):
    i = pl.program_id(2)
    q = q_ref[0, 0]
    qa = qa_ref[0, 0]
    row = lax.broadcasted_iota(jnp.int32, (PAIR_WIDTH, t), 0)
    rhs = []
    for e in range(2):
        keep = (row >= e * HEAD_DIM) & (row < (e + 1) * HEAD_DIM)
        zero = jnp.zeros_like(q)
        rhs.append(jnp.concatenate([jnp.where(keep, q, zero), jnp.where(keep, qa, zero)],
                                   axis=0))

    def tile_scores(j):
        start = pl.multiple_of(j * t, t)
        lhs = jnp.concatenate([k_ref[0, pl.ds(start, t), :], ka_ref[0, pl.ds(start, t), :]],
                              axis=1)
        return [_dot(lhs, rhs[e]) for e in range(2)]

    def values(j, e):
        rows = slice(e * HEAD_DIM, (e + 1) * HEAD_DIM)
        slabs = t // KEY_TILE
        return jnp.concatenate([v_ref[0, slabs * j + h, rows, :] for h in range(slabs)],
                               axis=1)

    key_i = lax.broadcasted_iota(jnp.int32, (t, t), 0)
    qry_i = lax.broadcasted_iota(jnp.int32, (t, t), 1)
    causal = key_i <= qry_i
    state = []
    for e, s_t in enumerate(tile_scores(i)):
        s_t = jnp.where(causal, s_t, NEG_BIG)
        m = jnp.max(s_t, axis=0, keepdims=True)
        p = jnp.exp2(s_t - m)
        l = jnp.sum(p, axis=0, keepdims=True)
        acc = _dot(values(i, e), p.astype(BF16))
        state += [m, l, acc]

    def body(j, carry):
        out = []
        for e, s_t in enumerate(tile_scores(j)):
            m, l, acc = carry[3 * e:3 * e + 3]
            m_new = jnp.maximum(m, jnp.max(s_t, axis=0, keepdims=True))
            alpha = jnp.exp2(m - m_new)
            p = jnp.exp2(s_t - m_new)
            l = alpha * l + jnp.sum(p, axis=0, keepdims=True)
            acc = alpha * acc + _dot(values(j, e), p.astype(BF16))
            out += [m_new, l, acc]
        return tuple(out)

    state = lax.fori_loop(0, i, body, tuple(state))
    o_t = jnp.concatenate([state[2] / state[1], state[5] / state[4]], axis=0)
    o_ref[0] = o_t.T


def _attn_bounded_kernel(q_ref, qa_ref, k_ref, ka_ref, v_ref, o_ref, p_sc, *, t,
                         ka_const):
    i = pl.program_id(1)
    kt = KEY_TILE
    row = lax.broadcasted_iota(jnp.int32, (PAIR_WIDTH, t), 0)
    ones = jnp.ones((16, kt), BF16)
    key_i = lax.broadcasted_iota(jnp.int32, (kt, t), 0)
    qry_i = lax.broadcasted_iota(jnp.int32, (kt, t), 1)
    assert t == 2 * kt


    def rhs_of(pr):
        feat = slice(pr * PAIR_WIDTH, (pr + 1) * PAIR_WIDTH)
        q = q_ref[0, 0, feat, :]
        qa = qa_ref[0, 0, feat, :]
        out = []
        for e in range(2):
            keep = (row >= e * HEAD_DIM) & (row < (e + 1) * HEAD_DIM)
            zero = jnp.zeros_like(q)
            out.append(jnp.concatenate([jnp.where(keep, q, zero), jnp.where(keep, qa, zero)],
                                       axis=0))
        return out

    for grp in range(N_PAIRS // PAIRS_PER_TRIP):
        prs = list(range(grp * PAIRS_PER_TRIP, (grp + 1) * PAIRS_PER_TRIP))
        rhs = {pr: rhs_of(pr) for pr in prs}
        n_acc = 2 * len(prs)

        def probs_to(slot, jk, mask=None, prs=prs, rhs=rhs):
            start = pl.multiple_of(jk * kt, kt)
            for pr in prs:
                feat = slice(pr * PAIR_WIDTH, (pr + 1) * PAIR_WIDTH)
                k_aug = (ka_ref[0, pl.ds(start, kt), :] if ka_const
                         else ka_ref[0, pl.ds(start, kt), feat])
                lhs = jnp.concatenate([k_ref[0, pl.ds(start, kt), feat], k_aug], axis=1)
                for e in range(2):
                    s_t = _dot(lhs, rhs[pr][e])
                    if mask is not None:
                        s_t = jnp.where(mask, s_t, NEG_BIG)
                    p_sc[pr, slot, e] = jnp.exp2(s_t).astype(BF16)

        def values_from(slot, jk, acc, prs=prs):
            out = []
            for n, pr in enumerate(prs):
                for e in range(2):
                    r0 = pr * PAIR_WIDTH + e * HEAD_DIM
                    v_e = jnp.concatenate([v_ref[0, jk, r0:r0 + HEAD_DIM, :], ones], axis=0)
                    out.append(acc[2 * n + e] + _dot(v_e, p_sc[pr, slot, e]))
            return out

        d0 = 2 * i
        probs_to(0, d0, key_i <= qry_i)
        probs_to(1, d0 + 1, key_i + kt <= qry_i)
        acc = values_from(0, d0, [jnp.zeros((HEAD_DIM + 16, t), F32)] * n_acc)

        def body(m, carry, probs_to=probs_to, values_from=values_from):
            acc, j_prev = list(carry[:-1]), carry[-1]
            probs_to(0, 2 * m)
            acc = values_from(1, j_prev, acc)
            probs_to(1, 2 * m + 1)
            acc = values_from(0, 2 * m, acc)
            return tuple(acc) + (2 * m + 1,)

        carry = lax.fori_loop(0, i, body, tuple(acc) + (d0 + 1,))
        acc = values_from(1, carry[-1], list(carry[:-1]))
        for n, pr in enumerate(prs):
            o_t = jnp.concatenate([a[0:HEAD_DIM, :] / a[HEAD_DIM:HEAD_DIM + 1, :]
                                   for a in acc[2 * n:2 * n + 2]], axis=0)
            o_ref[0, :, pr * PAIR_WIDTH:(pr + 1) * PAIR_WIDTH] = o_t.T


def _attention_online(q_t, qa, k, ka, v_t, *, ka_const, name):
    bsz, nt, _, t = q_t.shape
    s = k.shape[1]
    grid = (bsz, N_PAIRS, nt)
    if ka_const:
        ka_spec = pl.BlockSpec((1, s, PAIR_WIDTH), lambda b, p, i: (0, 0, 0))
    else:
        ka_spec = pl.BlockSpec((1, s, PAIR_WIDTH), lambda b, p, i: (b, 0, p))
    return pl.pallas_call(
        functools.partial(_attn_kernel, t=t),
        grid=grid,
        in_specs=[pl.BlockSpec((1, 1, PAIR_WIDTH, t), lambda b, p, i: (b, i, p, 0)),
                  pl.BlockSpec((1, 1, PAIR_WIDTH, t), lambda b, p, i: (b, i, p, 0)),
                  pl.BlockSpec((1, s, PAIR_WIDTH), lambda b, p, i: (b, 0, p)),
                  ka_spec,
                  pl.BlockSpec((1, s // KEY_TILE, PAIR_WIDTH, KEY_TILE),
                               lambda b, p, i: (b, 0, p, 0))],
        out_specs=pl.BlockSpec((1, t, PAIR_WIDTH), lambda b, p, i: (b, i, p)),
        out_shape=jax.ShapeDtypeStruct((bsz, s, GROUP_WIDTH), F32),
        compiler_params=pltpu.CompilerParams(
            dimension_semantics=("arbitrary", "arbitrary", "arbitrary"),
            vmem_limit_bytes=VMEM_LIMIT),
        name=name,
    )(q_t, qa, k, ka, v_t)


def _attention_bounded(q_t, qa, k, ka, v_t, *, ka_const, name):
    bsz, nt, _, t = q_t.shape
    s = k.shape[1]
    feat_spec = pl.BlockSpec((1, 1, GROUP_WIDTH, t), lambda b, i: (b, i, 0, 0))
    tok_spec = pl.BlockSpec((1, s, GROUP_WIDTH), lambda b, i: (b, 0, 0))
    ka_spec = pl.BlockSpec((1, s, PAIR_WIDTH), lambda b, i: (0, 0, 0)) if ka_const else tok_spec
    return pl.pallas_call(
        functools.partial(_attn_bounded_kernel, t=t, ka_const=ka_const),
        grid=(bsz, nt),
        in_specs=[feat_spec, feat_spec, tok_spec, ka_spec,
                  pl.BlockSpec((1, s // KEY_TILE, GROUP_WIDTH, KEY_TILE),
                               lambda b, i: (b, 0, 0, 0))],
        out_specs=pl.BlockSpec((1, t, GROUP_WIDTH), lambda b, i: (b, i, 0)),
        out_shape=jax.ShapeDtypeStruct((bsz, s, GROUP_WIDTH), F32),
        scratch_shapes=[pltpu.VMEM((N_PAIRS, 2, 2, KEY_TILE, t), BF16)],
        compiler_params=pltpu.CompilerParams(
            dimension_semantics=("arbitrary", "arbitrary"),
            vmem_limit_bytes=VMEM_LIMIT),
        name=name,
    )(q_t, qa, k, ka, v_t)


def _attention_dispatch(g_q, g_k, q_t, qa, k, ka, v_t, *, ka_const, name):
    bound = HEAD_DIM * jnp.max(jnp.abs(g_q)) * jnp.max(jnp.abs(g_k))
    return lax.cond(
        bound <= MAX_SAFE_EXPONENT,
        functools.partial(_attention_bounded, ka_const=ka_const, name=name + "_bounded"),
        functools.partial(_attention_online, ka_const=ka_const, name=name + "_online"),
        q_t, qa, k, ka, v_t)


def _post_kernel(x_ref, om_ref, of_ref, ga_ref, scm_ref, shm_ref, gm_ref,
                 gom_ref, gof_ref, wout_ref, w1_ref, w2_ref, o_ref, *, ff_chunk):
    def rms(v):
        return v * lax.rsqrt(jnp.mean(v * v, axis=-1, keepdims=True) + EPS)

    mixed = jnp.concatenate([(rms(om_ref[0]) * gom_ref[...]).astype(BF16),
                             (rms(of_ref[0]) * gof_ref[...]).astype(BF16)], axis=1)
    x1 = x_ref[0] + ga_ref[...] * _dot(mixed, wout_ref[...])
    h = (rms(x1) * (1.0 + scm_ref[...]) + shm_ref[...]).astype(BF16)
    d_ff = w1_ref.shape[1]
    y = None
    for c in range(d_ff // ff_chunk):
        hid = _dot(h, w1_ref[:, c * ff_chunk:(c + 1) * ff_chunk])
        hid = jnp.square(jnp.maximum(hid, 0.0)).astype(BF16)
        part = _dot(hid, w2_ref[c * ff_chunk:(c + 1) * ff_chunk, :])
        y = part if y is None else y + part
    o_ref[0] = x1 + gm_ref[...] * y


def _post(x, o_m, o_f, mod4, g_out_m, g_out_f, w_out, w_ff1, w_ff2, tm):
    bsz, s, d = x.shape
    d_ff = w_ff1.shape[1]

    def const(a):
        nd = a.ndim
        return pl.BlockSpec(a.shape, lambda b, i, _n=nd: (0,) * _n,
                            pipeline_mode=pl.Buffered(1))

    def mod_spec(k):
        return pl.BlockSpec((None, None, 1, d), lambda b, i, _k=k: (b, _k, 0, 0))

    tok = lambda w: pl.BlockSpec((1, tm, w), lambda b, i: (b, i, 0))
    return pl.pallas_call(
        functools.partial(_post_kernel, ff_chunk=1024),
        grid=(bsz, s // tm),
        in_specs=[tok(d), tok(GROUP_WIDTH), tok(GROUP_WIDTH),
                  mod_spec(2), mod_spec(4), mod_spec(3), mod_spec(5),
                  const(g_out_m), const(g_out_f), const(w_out), const(w_ff1), const(w_ff2)],
        out_specs=tok(d),
        out_shape=jax.ShapeDtypeStruct((bsz, s, d), F32),
        compiler_params=pltpu.CompilerParams(
            dimension_semantics=("arbitrary", "arbitrary"),
            vmem_limit_bytes=VMEM_LIMIT),
        name="post",
    )(x, o_m, o_f, mod4, mod4, mod4, mod4, g_out_m, g_out_f, w_out, w_ff1, w_ff2)


def _rope_tables(s):
    inv_freq = ROPE_THETA ** (-np.arange(0, ROPE_DIM, 2, dtype=np.float64) / ROPE_DIM)
    ang = np.arange(s, dtype=np.float64)[:, None] * inv_freq[None, :]
    cos, sin = np.cos(ang), np.sin(ang)
    rc = np.ones((s, LANES)); rs1 = np.zeros((s, LANES)); rs2 = np.zeros((s, LANES))
    for base in (0, HEAD_DIM):
        rc[:, base:base + ROPE_HALF] = cos
        rc[:, base + ROPE_HALF:base + ROPE_DIM] = cos
        rs2[:, base:base + ROPE_HALF] = -sin
        rs1[:, base + ROPE_HALF:base + ROPE_DIM] = sin
    f = lambda a: jnp.asarray(a, dtype=F32)
    return f(cos.T), f(sin.T), f(rc), f(rs1), f(rs2)


def _static_mats(s):
    idx = np.arange(256)
    jmat = (idx[:, None] // HEAD_DIM == idx[None, :] // HEAD_DIM) / HEAD_DIM
    place = np.zeros((LANES, GROUP_WIDTH))
    for part in range(3):
        for hd in range(GROUP_HEADS):
            place[part * GROUP_HEADS + hd,
                  (hd // 2) * PAIR_WIDTH + (hd % 2) * HEAD_DIM + part] = 1.0
    kones = np.zeros((1, GROUP_WIDTH))
    for hd in range(GROUP_HEADS):
        kones[0, hd * HEAD_DIM + 3:hd * HEAD_DIM + 6] = 1.0
    ka_moba = np.zeros((1, s, PAIR_WIDTH))
    blk = np.arange(s) // MOBA_BLOCK
    for e in range(2):
        ka_moba[0, np.arange(s), e * HEAD_DIM + blk] = 1.0
    b = lambda a: jnp.asarray(a, dtype=BF16)
    return b(jmat), b(place), jnp.asarray(kones, dtype=F32), b(ka_moba)


def kernel(x, c, w_ada, b_ada, w_in, b_forget, g_qn_moba, g_kn_moba, g_qn_fox, g_kn_fox,
           g_out_moba, g_out_fox, w_out, w_ff1, w_ff2):
    bsz, s, d = x.shape
    depth = w_ada.shape[0]
    w = GROUP_WIDTH
    assert s % MOBA_BLOCK == 0 and s // MOBA_BLOCK <= 16
    tm_in = 512
    tm_post = 512
    cos_t, sin_t, rc, rs1, rs2 = _rope_tables(s)
    jmat, place, kones, ka_moba = _static_mats(s)
    q_scale = HEAD_DIM ** -0.5 * LOG2E

    for l in range(depth):
        mod = _adaln(c, w_ada[l], b_ada[l])
        mod4 = mod.reshape(bsz, 6, 1, d)

        wl = w_in[l]
        col = lambda k: wl[:, k * w:(k + 1) * w]
        wf = jnp.zeros((d, LANES), F32).at[:, :GROUP_HEADS].set(wl[:, 6 * w:]).astype(BF16)
        bfor = jnp.zeros((1, LANES), F32).at[0, :GROUP_HEADS].set(b_forget[l])
        q_gain = lambda g: jnp.broadcast_to((g * q_scale)[:, None], (HEAD_DIM, tm_in))
        k_gain = lambda g: jnp.tile(g, GROUP_HEADS).reshape(1, w)
        wts = (col(0).T.astype(BF16), col(1).astype(BF16), col(2).T.astype(BF16),
               col(3).T.astype(BF16), col(4).astype(BF16), col(5).T.astype(BF16),
               wf, bfor,
               q_gain(g_qn_moba[l]), k_gain(g_kn_moba[l]),
               q_gain(g_qn_fox[l]), k_gain(g_kn_fox[l]))
        consts = (cos_t, sin_t, rc, rs1, rs2, jmat, place, kones)

        qm_t, km, vm_t, bias_t, qf_t, kf, vf_t, ka_f, qa_f = _inproj(
            x, mod4[:, 1], mod4[:, 0], wts, consts, tm_in)

        o_m = _attention_dispatch(g_qn_moba[l] * q_scale, g_kn_moba[l],
                                  qm_t, bias_t, km, ka_moba, vm_t, ka_const=True,
                                  name="attn_moba")
        o_f = _attention_dispatch(g_qn_fox[l] * q_scale, g_kn_fox[l],
                                  qf_t, qa_f, kf, ka_f, vf_t, ka_const=False,
                                  name="attn_fox")

        x = _post(x, o_m, o_f, mod4,
                  g_out_moba[l].reshape(1, w), g_out_fox[l].reshape(1, w),
                  w_out[l].astype(BF16), w_ff1[l].astype(BF16), w_ff2[l].astype(BF16),
                  tm_post)
    return x
```

```python
import functools
import math

import numpy as np
import jax
import jax.numpy as jnp
from jax import lax
from jax.experimental import pallas as pl
from jax.experimental.pallas import tpu as pltpu

F32 = jnp.float32
BF16 = jnp.bfloat16
HIGHEST = lax.Precision.HIGHEST

HEAD_DIM = 64
GROUP_HEADS = 8
GROUP_WIDTH = GROUP_HEADS * HEAD_DIM
PAIR_WIDTH = 2 * HEAD_DIM
N_PAIRS = GROUP_HEADS // 2
MOBA_BLOCK = 256
MOBA_TOPK = 3
ROPE_THETA = 500000.0
ROPE_DIM = HEAD_DIM // 4
ROPE_HALF = ROPE_DIM // 2
EPS = 1e-6
LOG2E = math.log2(math.e)
NEG_BIG = -1e30
MAX_SAFE_EXPONENT = 60.0
ATT_TILE = 512
KEY_TILE = 256
CUM_BLOCK = 128
PAIRS_PER_TRIP = 4
LANES = 128
VMEM_LIMIT = 48 * 1024 * 1024


def _dot(a, b, precision=None):
    return jnp.dot(a, b, preferred_element_type=F32, precision=precision)


def _split3(x):
    hi = x.astype(BF16)
    rem = x - hi.astype(F32)
    mid = rem.astype(BF16)
    lo = (rem - mid.astype(F32)).astype(BF16)
    return hi, mid, lo


def _dot_nt(a, b, precision=None):
    return lax.dot_general(a, b, (((1,), (1,)), ((), ())),
                           preferred_element_type=F32, precision=precision)


def _adaln_kernel(c_ref, w_ref, b_ref, o_ref):
    c = c_ref[...]
    s = c / (1.0 + jnp.exp(-c))
    o_ref[...] = _dot(s, w_ref[...], precision=HIGHEST) + b_ref[...]


def _adaln(c, w_ada, b_ada):
    bsz, d = c.shape
    n = w_ada.shape[1]
    return pl.pallas_call(
        _adaln_kernel,
        grid=(n // d,),
        in_specs=[pl.BlockSpec((bsz, d), lambda j: (0, 0)),
                  pl.BlockSpec((d, d), lambda j: (0, j)),
                  pl.BlockSpec((1, d), lambda j: (0, j))],
        out_specs=pl.BlockSpec((bsz, d), lambda j: (0, j)),
        out_shape=jax.ShapeDtypeStruct((bsz, n), F32),
        compiler_params=pltpu.CompilerParams(dimension_semantics=("arbitrary",)),
        name="adaln",
    )(c, w_ada, b_ada.reshape(1, n))


def _inproj_kernel(x_ref, sc_ref, sh_ref,
                   wqm_ref, wkm_ref, wvm_ref, wqf_ref, wkf_ref, wvf_ref, wf_ref, bf_ref,
                   gqm_ref, gkm_ref, gqf_ref, gkf_ref,
                   cos_t_ref, sin_t_ref, rc_ref, rs1_ref, rs2_ref,
                   j_ref, place_ref, kones_ref,
                   qm_o, km_o, vm_o, bias_o, qf_o, kf_o, vf_o, ka_o, qa_o,
                   kmean_sc, carry_sc, gate_sc, *, tm):
    i = pl.program_id(1)
    nb = tm // MOBA_BLOCK

    @pl.when(i == 0)
    def _():
        kmean_sc[...] = jnp.zeros_like(kmean_sc)
        carry_sc[...] = jnp.zeros_like(carry_sc)

    x = x_ref[0]
    ms = jnp.mean(x * x, axis=-1, keepdims=True)
    h = x * lax.rsqrt(ms + EPS) * (1.0 + sc_ref[...]) + sh_ref[...]
    hb = h.astype(BF16)

    def q_feature_major(w_ref, g_ref, rope):
        q_t = _dot_nt(w_ref[...], hb)
        gain = g_ref[...]
        heads = []
        for hd in range(GROUP_HEADS):
            xh = q_t[hd * HEAD_DIM:(hd + 1) * HEAD_DIM, :]
            ss = jnp.sum(xh * xh, axis=0, keepdims=True)
            xh = xh * lax.rsqrt(ss * (1.0 / HEAD_DIM) + EPS) * gain
            if rope:
                x1 = xh[0:ROPE_HALF, :]
                x2 = xh[ROPE_HALF:ROPE_DIM, :]
                cs = cos_t_ref[...]
                sn = sin_t_ref[...]
                xh = jnp.concatenate(
                    [x1 * cs - x2 * sn, x2 * cs + x1 * sn, xh[ROPE_DIM:, :]], axis=0)
            heads.append(xh)
        return jnp.concatenate(heads, axis=0)

    def k_token_major(w_ref, g_ref, rope):
        k = _dot(hb, w_ref[...])
        chunks = []
        for c in range(GROUP_WIDTH // 256):
            seg = k[:, c * 256:(c + 1) * 256]
            msq = _dot((seg * seg).astype(BF16), j_ref[...])
            chunks.append(seg * lax.rsqrt(msq + EPS) * g_ref[:, c * 256:(c + 1) * 256])
        k = jnp.concatenate(chunks, axis=1)
        if rope:
            outs = []
            for c in range(GROUP_WIDTH // LANES):
                seg = k[:, c * LANES:(c + 1) * LANES]
                outs.append(seg * rc_ref[...]
                            + pltpu.roll(seg, ROPE_HALF, 1) * rs1_ref[...]
                            + pltpu.roll(seg, LANES - ROPE_HALF, 1) * rs2_ref[...])
            k = jnp.concatenate(outs, axis=1)
        return k

    def store_tiles(o_ref, val_t, width=ATT_TILE):
        for t in range(tm // width):
            o_ref[0, t] = val_t[:, t * width:(t + 1) * width].astype(o_ref.dtype)

    qm_t = q_feature_major(wqm_ref, gqm_ref, True)
    store_tiles(qm_o, qm_t)
    km = k_token_major(wkm_ref, gkm_ref, True)
    km_o[0] = km.astype(BF16)
    store_tiles(vm_o, _dot_nt(wvm_ref[...], hb), KEY_TILE)

    blk0 = i * nb
    lane_head = lax.broadcasted_iota(jnp.int32, (1, GROUP_WIDTH), 1) // HEAD_DIM
    for sb in range(nb):
        kmean = jnp.mean(km[sb * MOBA_BLOCK:(sb + 1) * MOBA_BLOCK, :], axis=0, keepdims=True)
        for hd in range(GROUP_HEADS):
            row = hd * 16 + blk0 + sb
            kmean_sc[pl.ds(row, 1), :] = jnp.where(lane_head == hd, kmean, 0.0)

    km_hi, km_lo, _ = _split3(kmean_sc[...])
    q_hi, q_lo, _ = _split3(qm_t)
    gate_t = _dot(km_hi, q_hi) + _dot(km_hi, q_lo) + _dot(km_lo, q_hi)
    tok_blk = blk0 + lax.broadcasted_iota(jnp.int32, (16, tm), 1) // MOBA_BLOCK
    n_idx = lax.broadcasted_iota(jnp.int32, (16, tm), 0)
    valid = n_idx < tok_blk
    for hd in range(GROUP_HEADS):
        gate_sc[hd * 16:(hd + 1) * 16, :] = jnp.where(
            valid, gate_t[hd * 16:(hd + 1) * 16, :], -jnp.inf)

    def rank_step(m, cnts):
        tie = jnp.where(n_idx > m, 1, 0)
        out = []
        for hd in range(GROUP_HEADS):
            g = gate_sc[hd * 16:(hd + 1) * 16, :]
            gm = jnp.broadcast_to(gate_sc[pl.ds(hd * 16 + m, 1), :], (16, tm))
            out.append(cnts[hd] + jnp.where(gm > g, 1, jnp.where(gm == g, tie, 0)))
        return tuple(out)

    cnts = lax.fori_loop(0, blk0 + nb - 1, rank_step,
                         tuple(jnp.zeros((16, tm), jnp.int32) for _ in range(GROUP_HEADS)))
    zeros48 = jnp.zeros((HEAD_DIM - 16, tm), F32)
    pair_rows = []
    for hd in range(GROUP_HEADS):
        sel = (valid & (cnts[hd] < MOBA_TOPK)) | (n_idx == tok_blk)
        pair_rows += [jnp.where(sel, 0.0, NEG_BIG), zeros48]
    store_tiles(bias_o, jnp.concatenate(pair_rows, axis=0))

    store_tiles(qf_o, q_feature_major(wqf_ref, gqf_ref, False))
    kf_o[0] = k_token_major(wkf_ref, gkf_ref, False).astype(BF16)
    store_tiles(vf_o, _dot_nt(wvf_ref[...], hb), KEY_TILE)

    f = _dot(hb, wf_ref[...]) + bf_ref[...]
    logf = jnp.minimum(f, 0.0) - jnp.log1p(jnp.exp(-jnp.abs(f)))
    live = lax.broadcasted_iota(jnp.int32, (1, LANES), 1) < GROUP_HEADS
    logf = jnp.where(live, logf, 0.0)
    r_i = lax.broadcasted_iota(jnp.int32, (CUM_BLOCK, CUM_BLOCK), 0)
    c_i = lax.broadcasted_iota(jnp.int32, (CUM_BLOCK, CUM_BLOCK), 1)
    tri = jnp.where(r_i >= c_i, 1.0, 0.0).astype(BF16)
    carry = carry_sc[...]
    cums = []
    for blk in range(tm // CUM_BLOCK):
        parts = _split3(logf[blk * CUM_BLOCK:(blk + 1) * CUM_BLOCK, :])
        c_blk = _dot(tri, parts[0]) + _dot(tri, parts[1]) + _dot(tri, parts[2]) + carry
        carry = c_blk[CUM_BLOCK - 1:CUM_BLOCK, :]
        cums.append(c_blk)
    carry_sc[...] = carry
    nc = jnp.concatenate(cums, axis=0) * (-LOG2E)
    hi, mid, lo = [part.astype(F32) for part in _split3(nc)]
    packed = hi + pltpu.roll(mid, GROUP_HEADS, 1) + pltpu.roll(lo, 2 * GROUP_HEADS, 1)
    ka_o[0] = (_dot(packed.astype(BF16), place_ref[...]) + kones_ref[...]).astype(BF16)

    nc_t = nc.T[0:GROUP_HEADS, :]
    hi_t = nc_t.astype(BF16).astype(F32)
    rem_t = nc_t - hi_t
    mid_t = rem_t.astype(BF16).astype(F32)
    lo_t = (rem_t - mid_t).astype(BF16).astype(F32)
    r8 = lax.broadcasted_iota(jnp.int32, (8, tm), 0)
    zeros56 = jnp.zeros((HEAD_DIM - 8, tm), F32)
    qa_rows = []
    for hd in range(GROUP_HEADS):
        blk = jnp.where(r8 < 3, 1.0,
                        jnp.where(r8 == 3, -hi_t[hd:hd + 1, :],
                                  jnp.where(r8 == 4, -mid_t[hd:hd + 1, :],
                                            jnp.where(r8 == 5, -lo_t[hd:hd + 1, :], 0.0))))
        qa_rows += [blk, zeros56]
    store_tiles(qa_o, jnp.concatenate(qa_rows, axis=0))


def _inproj(x, sc, sh, wts, consts, tm):
    bsz, s, d = x.shape
    nt = s // ATT_TILE
    grid = (bsz, s // tm)
    wqm, wkm, wvm, wqf, wkf, wvf, wf, bfor, gqm, gkm, gqf, gkf = wts
    cos_t, sin_t, rc, rs1, rs2, jmat, place, kones = consts

    def const(a):
        nd = a.ndim
        return pl.BlockSpec(a.shape, lambda b, i, _n=nd: (0,) * _n,
                            pipeline_mode=pl.Buffered(1))

    mod_spec = pl.BlockSpec((None, 1, d), lambda b, i: (b, 0, 0))
    in_specs = [pl.BlockSpec((1, tm, d), lambda b, i: (b, i, 0)), mod_spec, mod_spec]
    in_specs += [const(a) for a in (wqm, wkm, wvm, wqf, wkf, wvf, wf, bfor,
                                    gqm, gkm, gqf, gkf)]
    in_specs += [pl.BlockSpec((ROPE_HALF, tm), lambda b, i: (0, i)),
                 pl.BlockSpec((ROPE_HALF, tm), lambda b, i: (0, i)),
                 pl.BlockSpec((tm, LANES), lambda b, i: (i, 0)),
                 pl.BlockSpec((tm, LANES), lambda b, i: (i, 0)),
                 pl.BlockSpec((tm, LANES), lambda b, i: (i, 0)),
                 const(jmat), const(place), const(kones)]

    tiles = tm // ATT_TILE
    feat_spec = pl.BlockSpec((1, tiles, GROUP_WIDTH, ATT_TILE), lambda b, i: (b, i, 0, 0))
    tok_spec = pl.BlockSpec((1, tm, GROUP_WIDTH), lambda b, i: (b, i, 0))
    feat_shape = jax.ShapeDtypeStruct((bsz, nt, GROUP_WIDTH, ATT_TILE), BF16)
    tok_shape = jax.ShapeDtypeStruct((bsz, s, GROUP_WIDTH), BF16)
    val_spec = pl.BlockSpec((1, tm // KEY_TILE, GROUP_WIDTH, KEY_TILE),
                            lambda b, i: (b, i, 0, 0))
    val_shape = jax.ShapeDtypeStruct((bsz, s // KEY_TILE, GROUP_WIDTH, KEY_TILE), BF16)
    out_specs = [feat_spec, tok_spec, val_spec, feat_spec,
                 feat_spec, tok_spec, val_spec, tok_spec, feat_spec]
    out_shape = [feat_shape, tok_shape, val_shape, feat_shape,
                 feat_shape, tok_shape, val_shape, tok_shape, feat_shape]
    return pl.pallas_call(
        functools.partial(_inproj_kernel, tm=tm),
        grid=grid,
        in_specs=in_specs,
        out_specs=out_specs,
        out_shape=out_shape,
        scratch_shapes=[pltpu.VMEM((GROUP_HEADS * 16, GROUP_WIDTH), F32),
                        pltpu.VMEM((1, LANES), F32),
                        pltpu.VMEM((GROUP_HEADS * 16, tm), F32)],
        compiler_params=pltpu.CompilerParams(
            dimension_semantics=("arbitrary", "arbitrary"),
            vmem_limit_bytes=VMEM_LIMIT),
        name="inproj",
    )(x, sc, sh, wqm, wkm, wvm, wqf, wkf, wvf, wf, bfor, gqm, gkm, gqf, gkf,
      cos_t, sin_t, rc, rs1, rs2, jmat, place, kones)


def _attn_kernel(q_ref, qa_ref, k_ref, ka_ref, v_ref, o_ref, *, t):
    i = pl.program_id(2)
    q = q_ref[0, 0]
    qa = qa_ref[0, 0]
    row = lax.broadcasted_iota(jnp.int32, (PAIR_WIDTH, t), 0)
    rhs = []
    for e in range(2):
        keep = (row >= e * HEAD_DIM) & (row < (e + 1) * HEAD_DIM)
        zero = jnp.zeros_like(q)
        rhs.append(jnp.concatenate([jnp.where(keep, q, zero), jnp.where(keep, qa, zero)],
                                   axis=0))

    def tile_scores(j):
        start = pl.multiple_of(j * t, t)
        lhs = jnp.concatenate([k_ref[0, pl.ds(start, t), :], ka_ref[0, pl.ds(start, t), :]],
                              axis=1)
        return [_dot(lhs, rhs[e]) for e in range(2)]

    def values(j, e):
        rows = slice(e * HEAD_DIM, (e + 1) * HEAD_DIM)
        slabs = t // KEY_TILE
        return jnp.concatenate([v_ref[0, slabs * j + h, rows, :] for h in range(slabs)],
                               axis=1)

    key_i = lax.broadcasted_iota(jnp.int32, (t, t), 0)
    qry_i = lax.broadcasted_iota(jnp.int32, (t, t), 1)
    causal = key_i <= qry_i
    state = []
    for e, s_t in enumerate(tile_scores(i)):
        s_t = jnp.where(causal, s_t, NEG_BIG)
        m = jnp.max(s_t, axis=0, keepdims=True)
        p = jnp.exp2(s_t - m)
        l = jnp.sum(p, axis=0, keepdims=True)
        acc = _dot(values(i, e), p.astype(BF16))
        state += [m, l, acc]

    def body(j, carry):
        out = []
        for e, s_t in enumerate(tile_scores(j)):
            m, l, acc = carry[3 * e:3 * e + 3]
            m_new = jnp.maximum(m, jnp.max(s_t, axis=0, keepdims=True))
            alpha = jnp.exp2(m - m_new)
            p = jnp.exp2(s_t - m_new)
            l = alpha * l + jnp.sum(p, axis=0, keepdims=True)
            acc = alpha * acc + _dot(values(j, e), p.astype(BF16))
            out += [m_new, l, acc]
        return tuple(out)

    state = lax.fori_loop(0, i, body, tuple(state))
    o_t = jnp.concatenate([state[2] / state[1], state[5] / state[4]], axis=0)
    o_ref[0] = o_t.T


def _attn_bounded_kernel(q_ref, qa_ref, k_ref, ka_ref, v_ref, o_ref, p_sc, *, t,
                         ka_const):
    i = pl.program_id(1)
    kt = KEY_TILE
    row = lax.broadcasted_iota(jnp.int32, (PAIR_WIDTH, t), 0)
    ones = jnp.ones((16, kt), BF16)
    key_i = lax.broadcasted_iota(jnp.int32, (kt, t), 0)
    qry_i = lax.broadcasted_iota(jnp.int32, (kt, t), 1)
    assert t == 2 * kt


    def rhs_of(pr):
        feat = slice(pr * PAIR_WIDTH, (pr + 1) * PAIR_WIDTH)
        q = q_ref[0, 0, feat, :]
        qa = qa_ref[0, 0, feat, :]
        out = []
        for e in range(2):
            keep = (row >= e * HEAD_DIM) & (row < (e + 1) * HEAD_DIM)
            zero = jnp.zeros_like(q)
            out.append(jnp.concatenate([jnp.where(keep, q, zero), jnp.where(keep, qa, zero)],
                                       axis=0))
        return out

    for grp in range(N_PAIRS // PAIRS_PER_TRIP):
        prs = list(range(grp * PAIRS_PER_TRIP, (grp + 1) * PAIRS_PER_TRIP))
        rhs = {pr: rhs_of(pr) for pr in prs}
        n_acc = 2 * len(prs)

        def probs_to(slot, jk, mask=None, prs=prs, rhs=rhs):
            start = pl.multiple_of(jk * kt, kt)
            for pr in prs:
                feat = slice(pr * PAIR_WIDTH, (pr + 1) * PAIR_WIDTH)
                k_aug = (ka_ref[0, pl.ds(start, kt), :] if ka_const
                         else ka_ref[0, pl.ds(start, kt), feat])
                lhs = jnp.concatenate([k_ref[0, pl.ds(start, kt), feat], k_aug], axis=1)
                for e in range(2):
                    s_t = _dot(lhs, rhs[pr][e])
                    if mask is not None:
                        s_t = jnp.where(mask, s_t, NEG_BIG)
                    p_sc[pr, slot, e] = jnp.exp2(s_t).astype(BF16)

        def values_from(slot, jk, acc, prs=prs):
            out = []
            for n, pr in enumerate(prs):
                for e in range(2):
                    r0 = pr * PAIR_WIDTH + e * HEAD_DIM
                    v_e = jnp.concatenate([v_ref[0, jk, r0:r0 + HEAD_DIM, :], ones], axis=0)
                    out.append(acc[2 * n + e] + _dot(v_e, p_sc[pr, slot, e]))
            return out

        d0 = 2 * i
        probs_to(0, d0, key_i <= qry_i)
        probs_to(1, d0 + 1, key_i + kt <= qry_i)
        acc = values_from(0, d0, [jnp.zeros((HEAD_DIM + 16, t), F32)] * n_acc)

        def body(m, carry, probs_to=probs_to, values_from=values_from):
            acc, j_prev = list(carry[:-1]), carry[-1]
            probs_to(0, 2 * m)
            acc = values_from(1, j_prev, acc)
            probs_to(1, 2 * m + 1)
            acc = values_from(0, 2 * m, acc)
            return tuple(acc) + (2 * m + 1,)

        carry = lax.fori_loop(0, i, body, tuple(acc) + (d0 + 1,))
        acc = values_from(1, carry[-1], list(carry[:-1]))
        for n, pr in enumerate(prs):
            o_t = jnp.concatenate([a[0:HEAD_DIM, :] / a[HEAD_DIM:HEAD_DIM + 1, :]
                                   for a in acc[2 * n:2 * n + 2]], axis=0)
            o_ref[0, :, pr * PAIR_WIDTH:(pr + 1) * PAIR_WIDTH] = o_t.T


def _attention_online(q_t, qa, k, ka, v_t, *, ka_const, name):
    bsz, nt, _, t = q_t.shape
    s = k.shape[1]
    grid = (bsz, N_PAIRS, nt)
    if ka_const:
        ka_spec = pl.BlockSpec((1, s, PAIR_WIDTH), lambda b, p, i: (0, 0, 0))
    else:
        ka_spec = pl.BlockSpec((1, s, PAIR_WIDTH), lambda b, p, i: (b, 0, p))
    return pl.pallas_call(
        functools.partial(_attn_kernel, t=t),
        grid=grid,
        in_specs=[pl.BlockSpec((1, 1, PAIR_WIDTH, t), lambda b, p, i: (b, i, p, 0)),
                  pl.BlockSpec((1, 1, PAIR_WIDTH, t), lambda b, p, i: (b, i, p, 0)),
                  pl.BlockSpec((1, s, PAIR_WIDTH), lambda b, p, i: (b, 0, p)),
                  ka_spec,
                  pl.BlockSpec((1, s // KEY_TILE, PAIR_WIDTH, KEY_TILE),
                               lambda b, p, i: (b, 0, p, 0))],
        out_specs=pl.BlockSpec((1, t, PAIR_WIDTH), lambda b, p, i: (b, i, p)),
        out_shape=jax.ShapeDtypeStruct((bsz, s, GROUP_WIDTH), F32),
        compiler_params=pltpu.CompilerParams(
            dimension_semantics=("arbitrary", "arbitrary", "arbitrary"),
            vmem_limit_bytes=VMEM_LIMIT),
        name=name,
    )(q_t, qa, k, ka, v_t)


def _attention_bounded(q_t, qa, k, ka, v_t, *, ka_const, name):
    bsz, nt, _, t = q_t.shape
    s = k.shape[1]
    feat_spec = pl.BlockSpec((1, 1, GROUP_WIDTH, t), lambda b, i: (b, i, 0, 0))
    tok_spec = pl.BlockSpec((1, s, GROUP_WIDTH), lambda b, i: (b, 0, 0))
    ka_spec = pl.BlockSpec((1, s, PAIR_WIDTH), lambda b, i: (0, 0, 0)) if ka_const else tok_spec
    return pl.pallas_call(
        functools.partial(_attn_bounded_kernel, t=t, ka_const=ka_const),
        grid=(bsz, nt),
        in_specs=[feat_spec, feat_spec, tok_spec, ka_spec,
                  pl.BlockSpec((1, s // KEY_TILE, GROUP_WIDTH, KEY_TILE),
                               lambda b, i: (b, 0, 0, 0))],
        out_specs=pl.BlockSpec((1, t, GROUP_WIDTH), lambda b, i: (b, i, 0)),
        out_shape=jax.ShapeDtypeStruct((bsz, s, GROUP_WIDTH), F32),
        scratch_shapes=[pltpu.VMEM((N_PAIRS, 2, 2, KEY_TILE, t), BF16)],
        compiler_params=pltpu.CompilerParams(
            dimension_semantics=("arbitrary", "arbitrary"),
            vmem_limit_bytes=VMEM_LIMIT),
        name=name,
    )(q_t, qa, k, ka, v_t)


def _attention_dispatch(g_q, g_k, q_t, qa, k, ka, v_t, *, ka_const, name):
    bound = HEAD_DIM * jnp.max(jnp.abs(g_q)) * jnp.max(jnp.abs(g_k))
    return lax.cond(
        bound <= MAX_SAFE_EXPONENT,
        functools.partial(_attention_bounded, ka_const=ka_const, name=name + "_bounded"),
        functools.partial(_attention_online, ka_const=ka_const, name=name + "_online"),
        q_t, qa, k, ka, v_t)


def _post_kernel(x_ref, om_ref, of_ref, ga_ref, scm_ref, shm_ref, gm_ref,
                 gom_ref, gof_ref, wout_ref, w1_ref, w2_ref, o_ref, *, ff_chunk):
    def rms(v):
        return v * lax.rsqrt(jnp.mean(v * v, axis=-1, keepdims=True) + EPS)

    mixed = jnp.concatenate([(rms(om_ref[0]) * gom_ref[...]).astype(BF16),
                             (rms(of_ref[0]) * gof_ref[...]).astype(BF16)], axis=1)
    x1 = x_ref[0] + ga_ref[...] * _dot(mixed, wout_ref[...])
    h = (rms(x1) * (1.0 + scm_ref[...]) + shm_ref[...]).astype(BF16)
    d_ff = w1_ref.shape[1]
    y = None
    for c in range(d_ff // ff_chunk):
        hid = _dot(h, w1_ref[:, c * ff_chunk:(c + 1) * ff_chunk])
        hid = jnp.square(jnp.maximum(hid, 0.0)).astype(BF16)
        part = _dot(hid, w2_ref[c * ff_chunk:(c + 1) * ff_chunk, :])
        y = part if y is None else y + part
    o_ref[0] = x1 + gm_ref[...] * y


def _post(x, o_m, o_f, mod4, g_out_m, g_out_f, w_out, w_ff1, w_ff2, tm):
    bsz, s, d = x.shape
    d_ff = w_ff1.shape[1]

    def const(a):
        nd = a.ndim
        return pl.BlockSpec(a.shape, lambda b, i, _n=nd: (0,) * _n,
                            pipeline_mode=pl.Buffered(1))

    def mod_spec(k):
        return pl.BlockSpec((None, None, 1, d), lambda b, i, _k=k: (b, _k, 0, 0))

    tok = lambda w: pl.BlockSpec((1, tm, w), lambda b, i: (b, i, 0))
    return pl.pallas_call(
        functools.partial(_post_kernel, ff_chunk=1024),
        grid=(bsz, s // tm),
        in_specs=[tok(d), tok(GROUP_WIDTH), tok(GROUP_WIDTH),
                  mod_spec(2), mod_spec(4), mod_spec(3), mod_spec(5),
                  const(g_out_m), const(g_out_f), const(w_out), const(w_ff1), const(w_ff2)],
        out_specs=tok(d),
        out_shape=jax.ShapeDtypeStruct((bsz, s, d), F32),
        compiler_params=pltpu.CompilerParams(
            dimension_semantics=("arbitrary", "arbitrary"),
            vmem_limit_bytes=VMEM_LIMIT),
        name="post",
    )(x, o_m, o_f, mod4, mod4, mod4, mod4, g_out_m, g_out_f, w_out, w_ff1, w_ff2)


def _rope_tables(s):
    inv_freq = ROPE_THETA ** (-np.arange(0, ROPE_DIM, 2, dtype=np.float64) / ROPE_DIM)
    ang = np.arange(s, dtype=np.float64)[:, None] * inv_freq[None, :]
    cos, sin = np.cos(ang), np.sin(ang)
    rc = np.ones((s, LANES)); rs1 = np.zeros((s, LANES)); rs2 = np.zeros((s, LANES))
    for base in (0, HEAD_DIM):
        rc[:, base:base + ROPE_HALF] = cos
        rc[:, base + ROPE_HALF:base + ROPE_DIM] = cos
        rs2[:, base:base + ROPE_HALF] = -sin
        rs1[:, base + ROPE_HALF:base + ROPE_DIM] = sin
    f = lambda a: jnp.asarray(a, dtype=F32)
    return f(cos.T), f(sin.T), f(rc), f(rs1), f(rs2)


def _static_mats(s):
    idx = np.arange(256)
    jmat = (idx[:, None] // HEAD_DIM == idx[None, :] // HEAD_DIM) / HEAD_DIM
    place = np.zeros((LANES, GROUP_WIDTH))
    for part in range(3):
        for hd in range(GROUP_HEADS):
            place[part * GROUP_HEADS + hd,
                  (hd // 2) * PAIR_WIDTH + (hd % 2) * HEAD_DIM + part] = 1.0
    kones = np.zeros((1, GROUP_WIDTH))
    for hd in range(GROUP_HEADS):
        kones[0, hd * HEAD_DIM + 3:hd * HEAD_DIM + 6] = 1.0
    ka_moba = np.zeros((1, s, PAIR_WIDTH))
    blk = np.arange(s) // MOBA_BLOCK
    for e in range(2):
        ka_moba[0, np.arange(s), e * HEAD_DIM + blk] = 1.0
    b = lambda a: jnp.asarray(a, dtype=BF16)
    return b(jmat), b(place), jnp.asarray(kones, dtype=F32), b(ka_moba)


def kernel(x, c, w_ada, b_ada, w_in, b_forget, g_qn_moba, g_kn_moba, g_qn_fox, g_kn_fox,
           g_out_moba, g_out_fox, w_out, w_ff1, w_ff2):
    bsz, s, d = x.shape
    depth = w_ada.shape[0]
    w = GROUP_WIDTH
    assert s % MOBA_BLOCK == 0 and s // MOBA_BLOCK <= 16
    tm_in = 1024
    tm_post = 512
    cos_t, sin_t, rc, rs1, rs2 = _rope_tables(s)
    jmat, place, kones, ka_moba = _static_mats(s)
    q_scale = HEAD_DIM ** -0.5 * LOG2E

    for l in range(depth):
        mod = _adaln(c, w_ada[l], b_ada[l])
        mod4 = mod.reshape(bsz, 6, 1, d)

        wl = w_in[l]
        col = lambda k: wl[:, k * w:(k + 1) * w]
        wf = jnp.zeros((d, LANES), F32).at[:, :GROUP_HEADS].set(wl[:, 6 * w:]).astype(BF16)
        bfor = jnp.zeros((1, LANES), F32).at[0, :GROUP_HEADS].set(b_forget[l])
        q_gain = lambda g: jnp.broadcast_to((g * q_scale)[:, None], (HEAD_DIM, tm_in))
        k_gain = lambda g: jnp.tile(g, GROUP_HEADS).reshape(1, w)
        wts = (col(0).T.astype(BF16), col(1).astype(BF16), col(2).T.astype(BF16),
               col(3).T.astype(BF16), col(4).astype(BF16), col(5).T.astype(BF16),
               wf, bfor,
               q_gain(g_qn_moba[l]), k_gain(g_kn_moba[l]),
               q_gain(g_qn_fox[l]), k_gain(g_kn_fox[l]))
        consts = (cos_t, sin_t, rc, rs1, rs2, jmat, place, kones)

        qm_t, km, vm_t, bias_t, qf_t, kf, vf_t, ka_f, qa_f = _inproj(
            x, mod4[:, 1], mod4[:, 0], wts, consts, tm_in)

        o_m = _attention_dispatch(g_qn_moba[l] * q_scale, g_kn_moba[l],
                                  qm_t, bias_t, km, ka_moba, vm_t, ka_const=True,
                                  name="attn_moba")
        o_f = _attention_dispatch(g_qn_fox[l] * q_scale, g_kn_fox[l],
                                  qf_t, qa_f, kf, ka_f, vf_t, ka_const=False,
                                  name="attn_fox")

        x = _post(x, o_m, o_f, mod4,
                  g_out_moba[l].reshape(1, w), g_out_fox[l].reshape(1, w),
                  w_out[l].astype(BF16), w_ff1[l].astype(BF16), w_ff2[l].astype(BF16),
                  tm_post)
    return x
```

```python
import functools
import math

import numpy as np
import jax
import jax.numpy as jnp
from jax import lax
from jax.experimental import pallas as pl
from jax.experimental.pallas import tpu as pltpu

F32 = jnp.float32
BF16 = jnp.bfloat16
HIGHEST = lax.Precision.HIGHEST

HEAD_DIM = 64
GROUP_HEADS = 8
GROUP_WIDTH = GROUP_HEADS * HEAD_DIM
PAIR_WIDTH = 2 * HEAD_DIM
N_PAIRS = GROUP_HEADS // 2
MOBA_BLOCK = 256
MOBA_TOPK = 3
ROPE_THETA = 500000.0
ROPE_DIM = HEAD_DIM // 4
ROPE_HALF = ROPE_DIM // 2
EPS = 1e-6
LOG2E = math.log2(math.e)
NEG_BIG = -1e30
MAX_SAFE_EXPONENT = 60.0
ATT_TILE = 512
KEY_TILE = 256
CUM_BLOCK = 128
PAIRS_PER_TRIP = 4
LANES = 128
VMEM_LIMIT = 48 * 1024 * 1024


def _dot(a, b, precision=None):
    return jnp.dot(a, b, preferred_element_type=F32, precision=precision)


def _split3(x):
    hi = x.astype(BF16)
    rem = x - hi.astype(F32)
    mid = rem.astype(BF16)
    lo = (rem - mid.astype(F32)).astype(BF16)
    return hi, mid, lo


def _dot_nt(a, b, precision=None):
    return lax.dot_general(a, b, (((1,), (1,)), ((), ())),
                           preferred_element_type=F32, precision=precision)


def _adaln_kernel(ct_ref, w_ref, b_ref, o_ref):
    c_t = ct_ref[...]
    s_t = c_t / (1.0 + jnp.exp(-c_t))
    w = w_ref[...]
    rows = [jnp.sum(s_t[:, b:b + 1] * w, axis=0, keepdims=True) for b in range(c_t.shape[1])]
    o_ref[...] = jnp.concatenate(rows, axis=0) + b_ref[...]


def _adaln(c, w_ada, b_ada):
    bsz, d = c.shape
    n = w_ada.shape[1]
    return pl.pallas_call(
        _adaln_kernel,
        grid=(n // d,),
        in_specs=[pl.BlockSpec((d, bsz), lambda j: (0, 0)),
                  pl.BlockSpec((d, d), lambda j: (0, j)),
                  pl.BlockSpec((1, d), lambda j: (0, j))],
        out_specs=pl.BlockSpec((bsz, d), lambda j: (0, j)),
        out_shape=jax.ShapeDtypeStruct((bsz, n), F32),
        compiler_params=pltpu.CompilerParams(dimension_semantics=("arbitrary",)),
        name="adaln",
    )(c.T, w_ada, b_ada.reshape(1, n))


def _inproj_kernel(x_ref, sc_ref, sh_ref,
                   wqm_ref, wkm_ref, wvm_ref, wqf_ref, wkf_ref, wvf_ref, wf_ref, bf_ref,
                   gqm_ref, gkm_ref, gqf_ref, gkf_ref,
                   cos_t_ref, sin_t_ref, rc_ref, rs1_ref, rs2_ref,
                   j_ref, place_ref, kones_ref,
                   qm_o, km_o, vm_o, bias_o, qf_o, kf_o, vf_o, ka_o, qa_o,
                   kmean_sc, carry_sc, gate_sc, *, tm):
    i = pl.program_id(1)
    nb = tm // MOBA_BLOCK

    @pl.when(i == 0)
    def _():
        kmean_sc[...] = jnp.zeros_like(kmean_sc)
        carry_sc[...] = jnp.zeros_like(carry_sc)

    x = x_ref[0]
    ms = jnp.mean(x * x, axis=-1, keepdims=True)
    h = x * lax.rsqrt(ms + EPS) * (1.0 + sc_ref[...]) + sh_ref[...]
    hb = h.astype(BF16)

    def q_feature_major(w_ref, g_ref, rope):
        q_t = _dot_nt(w_ref[...], hb)
        gain = g_ref[...]
        heads = []
        for hd in range(GROUP_HEADS):
            xh = q_t[hd * HEAD_DIM:(hd + 1) * HEAD_DIM, :]
            ss = jnp.sum(xh * xh, axis=0, keepdims=True)
            xh = xh * lax.rsqrt(ss * (1.0 / HEAD_DIM) + EPS) * gain
            if rope:
                x1 = xh[0:ROPE_HALF, :]
                x2 = xh[ROPE_HALF:ROPE_DIM, :]
                cs = cos_t_ref[...]
                sn = sin_t_ref[...]
                xh = jnp.concatenate(
                    [x1 * cs - x2 * sn, x2 * cs + x1 * sn, xh[ROPE_DIM:, :]], axis=0)
            heads.append(xh)
        return jnp.concatenate(heads, axis=0)

    def k_token_major(w_ref, g_ref, rope):
        k = _dot(hb, w_ref[...])
        chunks = []
        for c in range(GROUP_WIDTH // 256):
            seg = k[:, c * 256:(c + 1) * 256]
            msq = _dot((seg * seg).astype(BF16), j_ref[...])
            chunks.append(seg * lax.rsqrt(msq + EPS) * g_ref[:, c * 256:(c + 1) * 256])
        k = jnp.concatenate(chunks, axis=1)
        if rope:
            outs = []
            for c in range(GROUP_WIDTH // LANES):
                seg = k[:, c * LANES:(c + 1) * LANES]
                outs.append(seg * rc_ref[...]
                            + pltpu.roll(seg, ROPE_HALF, 1) * rs1_ref[...]
                            + pltpu.roll(seg, LANES - ROPE_HALF, 1) * rs2_ref[...])
            k = jnp.concatenate(outs, axis=1)
        return k

    def store_tiles(o_ref, val_t, width=ATT_TILE):
        for t in range(tm // width):
            o_ref[0, t] = val_t[:, t * width:(t + 1) * width].astype(o_ref.dtype)

    qm_t = q_feature_major(wqm_ref, gqm_ref, True)
    store_tiles(qm_o, qm_t)
    km = k_token_major(wkm_ref, gkm_ref, True)
    km_o[0] = km.astype(BF16)
    store_tiles(vm_o, _dot_nt(wvm_ref[...], hb), KEY_TILE)

    blk0 = i * nb
    lane_head = lax.broadcasted_iota(jnp.int32, (1, GROUP_WIDTH), 1) // HEAD_DIM
    for sb in range(nb):
        kmean = jnp.mean(km[sb * MOBA_BLOCK:(sb + 1) * MOBA_BLOCK, :], axis=0, keepdims=True)
        for hd in range(GROUP_HEADS):
            row = hd * 16 + blk0 + sb
            kmean_sc[pl.ds(row, 1), :] = jnp.where(lane_head == hd, kmean, 0.0)

    km_hi, km_lo, _ = _split3(kmean_sc[...])
    q_hi, q_lo, _ = _split3(qm_t)
    gate_t = _dot(km_hi, q_hi) + _dot(km_hi, q_lo) + _dot(km_lo, q_hi)
    tok_blk = blk0 + lax.broadcasted_iota(jnp.int32, (16, tm), 1) // MOBA_BLOCK
    n_idx = lax.broadcasted_iota(jnp.int32, (16, tm), 0)
    valid = n_idx < tok_blk
    for hd in range(GROUP_HEADS):
        gate_sc[hd * 16:(hd + 1) * 16, :] = jnp.where(
            valid, gate_t[hd * 16:(hd + 1) * 16, :], -jnp.inf)

    def rank_step(m, cnts):
        tie = jnp.where(n_idx > m, 1, 0)
        out = []
        for hd in range(GROUP_HEADS):
            g = gate_sc[hd * 16:(hd + 1) * 16, :]
            gm = jnp.broadcast_to(gate_sc[pl.ds(hd * 16 + m, 1), :], (16, tm))
            out.append(cnts[hd] + jnp.where(gm > g, 1, jnp.where(gm == g, tie, 0)))
        return tuple(out)

    cnts = lax.fori_loop(0, blk0 + nb - 1, rank_step,
                         tuple(jnp.zeros((16, tm), jnp.int32) for _ in range(GROUP_HEADS)))
    zeros48 = jnp.zeros((HEAD_DIM - 16, tm), F32)
    pair_rows = []
    for hd in range(GROUP_HEADS):
        sel = (valid & (cnts[hd] < MOBA_TOPK)) | (n_idx == tok_blk)
        pair_rows += [jnp.where(sel, 0.0, NEG_BIG), zeros48]
    store_tiles(bias_o, jnp.concatenate(pair_rows, axis=0))

    store_tiles(qf_o, q_feature_major(wqf_ref, gqf_ref, False))
    kf_o[0] = k_token_major(wkf_ref, gkf_ref, False).astype(BF16)
    store_tiles(vf_o, _dot_nt(wvf_ref[...], hb), KEY_TILE)

    f = _dot(hb, wf_ref[...]) + bf_ref[...]
    logf = jnp.minimum(f, 0.0) - jnp.log1p(jnp.exp(-jnp.abs(f)))
    live = lax.broadcasted_iota(jnp.int32, (1, LANES), 1) < GROUP_HEADS
    logf = jnp.where(live, logf, 0.0)
    r_i = lax.broadcasted_iota(jnp.int32, (CUM_BLOCK, CUM_BLOCK), 0)
    c_i = lax.broadcasted_iota(jnp.int32, (CUM_BLOCK, CUM_BLOCK), 1)
    tri = jnp.where(r_i >= c_i, 1.0, 0.0).astype(BF16)
    carry = carry_sc[...]
    cums = []
    for blk in range(tm // CUM_BLOCK):
        parts = _split3(logf[blk * CUM_BLOCK:(blk + 1) * CUM_BLOCK, :])
        c_blk = _dot(tri, parts[0]) + _dot(tri, parts[1]) + _dot(tri, parts[2]) + carry
        carry = c_blk[CUM_BLOCK - 1:CUM_BLOCK, :]
        cums.append(c_blk)
    carry_sc[...] = carry
    nc = jnp.concatenate(cums, axis=0) * (-LOG2E)
    hi, mid, lo = [part.astype(F32) for part in _split3(nc)]
    packed = hi + pltpu.roll(mid, GROUP_HEADS, 1) + pltpu.roll(lo, 2 * GROUP_HEADS, 1)
    ka_o[0] = (_dot(packed.astype(BF16), place_ref[...]) + kones_ref[...]).astype(BF16)

    nc_t = nc.T[0:GROUP_HEADS, :]
    hi_t = nc_t.astype(BF16).astype(F32)
    rem_t = nc_t - hi_t
    mid_t = rem_t.astype(BF16).astype(F32)
    lo_t = (rem_t - mid_t).astype(BF16).astype(F32)
    r8 = lax.broadcasted_iota(jnp.int32, (8, tm), 0)
    zeros56 = jnp.zeros((HEAD_DIM - 8, tm), F32)
    qa_rows = []
    for hd in range(GROUP_HEADS):
        blk = jnp.where(r8 < 3, 1.0,
                        jnp.where(r8 == 3, -hi_t[hd:hd + 1, :],
                                  jnp.where(r8 == 4, -mid_t[hd:hd + 1, :],
                                            jnp.where(r8 == 5, -lo_t[hd:hd + 1, :], 0.0))))
        qa_rows += [blk, zeros56]
    store_tiles(qa_o, jnp.concatenate(qa_rows, axis=0))


def _inproj(x, sc, sh, wts, consts, tm):
    bsz, s, d = x.shape
    nt = s // ATT_TILE
    grid = (bsz, s // tm)
    wqm, wkm, wvm, wqf, wkf, wvf, wf, bfor, gqm, gkm, gqf, gkf = wts
    cos_t, sin_t, rc, rs1, rs2, jmat, place, kones = consts

    def const(a):
        nd = a.ndim
        return pl.BlockSpec(a.shape, lambda b, i, _n=nd: (0,) * _n,
                            pipeline_mode=pl.Buffered(1))

    mod_spec = pl.BlockSpec((None, 1, d), lambda b, i: (b, 0, 0))
    in_specs = [pl.BlockSpec((1, tm, d), lambda b, i: (b, i, 0)), mod_spec, mod_spec]
    in_specs += [const(a) for a in (wqm, wkm, wvm, wqf, wkf, wvf, wf, bfor,
                                    gqm, gkm, gqf, gkf)]
    in_specs += [pl.BlockSpec((ROPE_HALF, tm), lambda b, i: (0, i)),
                 pl.BlockSpec((ROPE_HALF, tm), lambda b, i: (0, i)),
                 pl.BlockSpec((tm, LANES), lambda b, i: (i, 0)),
                 pl.BlockSpec((tm, LANES), lambda b, i: (i, 0)),
                 pl.BlockSpec((tm, LANES), lambda b, i: (i, 0)),
                 const(jmat), const(place), const(kones)]

    tiles = tm // ATT_TILE
    feat_spec = pl.BlockSpec((1, tiles, GROUP_WIDTH, ATT_TILE), lambda b, i: (b, i, 0, 0))
    tok_spec = pl.BlockSpec((1, tm, GROUP_WIDTH), lambda b, i: (b, i, 0))
    feat_shape = jax.ShapeDtypeStruct((bsz, nt, GROUP_WIDTH, ATT_TILE), BF16)
    tok_shape = jax.ShapeDtypeStruct((bsz, s, GROUP_WIDTH), BF16)
    val_spec = pl.BlockSpec((1, tm // KEY_TILE, GROUP_WIDTH, KEY_TILE),
                            lambda b, i: (b, i, 0, 0))
    val_shape = jax.ShapeDtypeStruct((bsz, s // KEY_TILE, GROUP_WIDTH, KEY_TILE), BF16)
    out_specs = [feat_spec, tok_spec, val_spec, feat_spec,
                 feat_spec, tok_spec, val_spec, tok_spec, feat_spec]
    out_shape = [feat_shape, tok_shape, val_shape, feat_shape,
                 feat_shape, tok_shape, val_shape, tok_shape, feat_shape]
    return pl.pallas_call(
        functools.partial(_inproj_kernel, tm=tm),
        grid=grid,
        in_specs=in_specs,
        out_specs=out_specs,
        out_shape=out_shape,
        scratch_shapes=[pltpu.VMEM((GROUP_HEADS * 16, GROUP_WIDTH), F32),
                        pltpu.VMEM((1, LANES), F32),
                        pltpu.VMEM((GROUP_HEADS * 16, tm), F32)],
        compiler_params=pltpu.CompilerParams(
            dimension_semantics=("arbitrary", "arbitrary"),
            vmem_limit_bytes=VMEM_LIMIT),
        name="inproj",
    )(x, sc, sh, wqm, wkm, wvm, wqf, wkf, wvf, wf, bfor, gqm, gkm, gqf, gkf,
      cos_t, sin_t, rc, rs1, rs2, jmat, place, kones)


def _attn_kernel(q_ref, qa_ref, k_ref, ka_ref, v_ref, o_ref, *, t):
    i = pl.program_id(2)
    q = q_ref[0, 0]
    qa = qa_ref[0, 0]
    row = lax.broadcasted_iota(jnp.int32, (PAIR_WIDTH, t), 0)
    rhs = []
    for e in range(2):
        keep = (row >= e * HEAD_DIM) & (row < (e + 1) * HEAD_DIM)
        zero = jnp.zeros_like(q)
        rhs.append(jnp.concatenate([jnp.where(keep, q, zero), jnp.where(keep, qa, zero)],
                                   axis=0))

    def tile_scores(j):
        start = pl.multiple_of(j * t, t)
        lhs = jnp.concatenate([k_ref[0, pl.ds(start, t), :], ka_ref[0, pl.ds(start, t), :]],
                              axis=1)
        return [_dot(lhs, rhs[e]) for e in range(2)]

    def values(j, e):
        rows = slice(e * HEAD_DIM, (e + 1) * HEAD_DIM)
        slabs = t // KEY_TILE
        return jnp.concatenate([v_ref[0, slabs * j + h, rows, :] for h in range(slabs)],
                               axis=1)

    key_i = lax.broadcasted_iota(jnp.int32, (t, t), 0)
    qry_i = lax.broadcasted_iota(jnp.int32, (t, t), 1)
    causal = key_i <= qry_i
    state = []
    for e, s_t in enumerate(tile_scores(i)):
        s_t = jnp.where(causal, s_t, NEG_BIG)
        m = jnp.max(s_t, axis=0, keepdims=True)
        p = jnp.exp2(s_t - m)
        l = jnp.sum(p, axis=0, keepdims=True)
        acc = _dot(values(i, e), p.astype(BF16))
        state += [m, l, acc]

    def body(j, carry):
        out = []
        for e, s_t in enumerate(tile_scores(j)):
            m, l, acc = carry[3 * e:3 * e + 3]
            m_new = jnp.maximum(m, jnp.max(s_t, axis=0, keepdims=True))
            alpha = jnp.exp2(m - m_new)
            p = jnp.exp2(s_t - m_new)
            l = alpha * l + jnp.sum(p, axis=0, keepdims=True)
            acc = alpha * acc + _dot(values(j, e), p.astype(BF16))
            out += [m_new, l, acc]
        return tuple(out)

    state = lax.fori_loop(0, i, body, tuple(state))
    o_t = jnp.concatenate([state[2] / state[1], state[5] / state[4]], axis=0)
    o_ref[0] = o_t.T


def _attn_bounded_kernel(q_ref, qa_ref, k_ref, ka_ref, v_ref, o_ref, p_sc, *, t,
                         ka_const):
    i = pl.program_id(1)
    kt = KEY_TILE
    row = lax.broadcasted_iota(jnp.int32, (PAIR_WIDTH, t), 0)
    ones = jnp.ones((16, kt), BF16)
    key_i = lax.broadcasted_iota(jnp.int32, (kt, t), 0)
    qry_i = lax.broadcasted_iota(jnp.int32, (kt, t), 1)
    assert t == 2 * kt


    def rhs_of(pr):
        feat = slice(pr * PAIR_WIDTH, (pr + 1) * PAIR_WIDTH)
        q = q_ref[0, 0, feat, :]
        qa = qa_ref[0, 0, feat, :]
        out = []
        for e in range(2):
            keep = (row >= e * HEAD_DIM) & (row < (e + 1) * HEAD_DIM)
            zero = jnp.zeros_like(q)
            out.append(jnp.concatenate([jnp.where(keep, q, zero), jnp.where(keep, qa, zero)],
                                       axis=0))
        return out

    for grp in range(N_PAIRS // PAIRS_PER_TRIP):
        prs = list(range(grp * PAIRS_PER_TRIP, (grp + 1) * PAIRS_PER_TRIP))
        rhs = {pr: rhs_of(pr) for pr in prs}
        n_acc = 2 * len(prs)

        def probs_to(slot, jk, mask=None, q0=0, prs=prs, rhs=rhs):
            start = pl.multiple_of(jk * kt, kt)
            for pr in prs:
                feat = slice(pr * PAIR_WIDTH, (pr + 1) * PAIR_WIDTH)
                k_aug = (ka_ref[0, pl.ds(start, kt), :] if ka_const
                         else ka_ref[0, pl.ds(start, kt), feat])
                lhs = jnp.concatenate([k_ref[0, pl.ds(start, kt), feat], k_aug], axis=1)
                for e in range(2):
                    s_t = _dot(lhs, rhs[pr][e][:, q0:])
                    if mask is not None:
                        s_t = jnp.where(mask[:, q0:], s_t, NEG_BIG)
                    p_sc[pr, slot, e, :, q0:] = jnp.exp2(s_t).astype(BF16)

        def values_from(slot, jk, acc, q0=0, prs=prs):
            out = []
            for n, pr in enumerate(prs):
                for e in range(2):
                    r0 = pr * PAIR_WIDTH + e * HEAD_DIM
                    v_e = jnp.concatenate([v_ref[0, jk, r0:r0 + HEAD_DIM, :], ones], axis=0)
                    a = acc[2 * n + e]
                    upd = a[:, q0:] + _dot(v_e, p_sc[pr, slot, e, :, q0:])
                    out.append(upd if q0 == 0 else jnp.concatenate([a[:, :q0], upd], axis=1))
            return out

        d0 = 2 * i
        probs_to(0, d0 + 1, key_i + kt <= qry_i, q0=kt)
        probs_to(1, d0, key_i <= qry_i)
        acc = values_from(0, d0 + 1, [jnp.zeros((HEAD_DIM + 16, t), F32)] * n_acc, q0=kt)

        def body(m, carry, probs_to=probs_to, values_from=values_from):
            acc, j_prev = list(carry[:-1]), carry[-1]
            probs_to(0, 2 * m)
            acc = values_from(1, j_prev, acc)
            probs_to(1, 2 * m + 1)
            acc = values_from(0, 2 * m, acc)
            return tuple(acc) + (2 * m + 1,)

        carry = lax.fori_loop(0, i, body, tuple(acc) + (d0,))
        acc = values_from(1, carry[-1], list(carry[:-1]))
        for n, pr in enumerate(prs):
            o_t = jnp.concatenate([a[0:HEAD_DIM, :] / a[HEAD_DIM:HEAD_DIM + 1, :]
                                   for a in acc[2 * n:2 * n + 2]], axis=0)
            o_ref[0, :, pr * PAIR_WIDTH:(pr + 1) * PAIR_WIDTH] = o_t.T


def _attention_online(q_t, qa, k, ka, v_t, *, ka_const, name):
    bsz, nt, _, t = q_t.shape
    s = k.shape[1]
    grid = (bsz, N_PAIRS, nt)
    if ka_const:
        ka_spec = pl.BlockSpec((1, s, PAIR_WIDTH), lambda b, p, i: (0, 0, 0))
    else:
        ka_spec = pl.BlockSpec((1, s, PAIR_WIDTH), lambda b, p, i: (b, 0, p))
    return pl.pallas_call(
        functools.partial(_attn_kernel, t=t),
        grid=grid,
        in_specs=[pl.BlockSpec((1, 1, PAIR_WIDTH, t), lambda b, p, i: (b, i, p, 0)),
                  pl.BlockSpec((1, 1, PAIR_WIDTH, t), lambda b, p, i: (b, i, p, 0)),
                  pl.BlockSpec((1, s, PAIR_WIDTH), lambda b, p, i: (b, 0, p)),
                  ka_spec,
                  pl.BlockSpec((1, s // KEY_TILE, PAIR_WIDTH, KEY_TILE),
                               lambda b, p, i: (b, 0, p, 0))],
        out_specs=pl.BlockSpec((1, t, PAIR_WIDTH), lambda b, p, i: (b, i, p)),
        out_shape=jax.ShapeDtypeStruct((bsz, s, GROUP_WIDTH), F32),
        compiler_params=pltpu.CompilerParams(
            dimension_semantics=("arbitrary", "arbitrary", "arbitrary"),
            vmem_limit_bytes=VMEM_LIMIT),
        name=name,
    )(q_t, qa, k, ka, v_t)


def _attention_bounded(q_t, qa, k, ka, v_t, *, ka_const, name):
    bsz, nt, _, t = q_t.shape
    s = k.shape[1]
    feat_spec = pl.BlockSpec((1, 1, GROUP_WIDTH, t), lambda b, i: (b, i, 0, 0))
    tok_spec = pl.BlockSpec((1, s, GROUP_WIDTH), lambda b, i: (b, 0, 0))
    ka_spec = pl.BlockSpec((1, s, PAIR_WIDTH), lambda b, i: (0, 0, 0)) if ka_const else tok_spec
    return pl.pallas_call(
        functools.partial(_attn_bounded_kernel, t=t, ka_const=ka_const),
        grid=(bsz, nt),
        in_specs=[feat_spec, feat_spec, tok_spec, ka_spec,
                  pl.BlockSpec((1, s // KEY_TILE, GROUP_WIDTH, KEY_TILE),
                               lambda b, i: (b, 0, 0, 0))],
        out_specs=pl.BlockSpec((1, t, GROUP_WIDTH), lambda b, i: (b, i, 0)),
        out_shape=jax.ShapeDtypeStruct((bsz, s, GROUP_WIDTH), F32),
        scratch_shapes=[pltpu.VMEM((N_PAIRS, 2, 2, KEY_TILE, t), BF16)],
        compiler_params=pltpu.CompilerParams(
            dimension_semantics=("arbitrary", "arbitrary"),
            vmem_limit_bytes=VMEM_LIMIT),
        name=name,
    )(q_t, qa, k, ka, v_t)


def _attention_dispatch(g_q, g_k, q_t, qa, k, ka, v_t, *, ka_const, name):
    bound = HEAD_DIM * jnp.max(jnp.abs(g_q)) * jnp.max(jnp.abs(g_k))
    return lax.cond(
        bound <= MAX_SAFE_EXPONENT,
        functools.partial(_attention_bounded, ka_const=ka_const, name=name + "_bounded"),
        functools.partial(_attention_online, ka_const=ka_const, name=name + "_online"),
        q_t, qa, k, ka, v_t)


def _post_kernel(x_ref, om_ref, of_ref, ga_ref, scm_ref, shm_ref, gm_ref,
                 gom_ref, gof_ref, wout_ref, w1_ref, w2_ref, o_ref, *, ff_chunk):
    def rms(v):
        return v * lax.rsqrt(jnp.mean(v * v, axis=-1, keepdims=True) + EPS)

    mixed = jnp.concatenate([(rms(om_ref[0]) * gom_ref[...]).astype(BF16),
                             (rms(of_ref[0]) * gof_ref[...]).astype(BF16)], axis=1)
    x1 = x_ref[0] + ga_ref[...] * _dot(mixed, wout_ref[...])
    h = (rms(x1) * (1.0 + scm_ref[...]) + shm_ref[...]).astype(BF16)
    d_ff = w1_ref.shape[1]
    y = None
    for c in range(d_ff // ff_chunk):
        hid = _dot(h, w1_ref[:, c * ff_chunk:(c + 1) * ff_chunk])
        hid = jnp.square(jnp.maximum(hid, 0.0)).astype(BF16)
        part = _dot(hid, w2_ref[c * ff_chunk:(c + 1) * ff_chunk, :])
        y = part if y is None else y + part
    o_ref[0] = x1 + gm_ref[...] * y


def _post(x, o_m, o_f, mod4, g_out_m, g_out_f, w_out, w_ff1, w_ff2, tm):
    bsz, s, d = x.shape
    d_ff = w_ff1.shape[1]

    def const(a):
        nd = a.ndim
        return pl.BlockSpec(a.shape, lambda b, i, _n=nd: (0,) * _n,
                            pipeline_mode=pl.Buffered(1))

    def mod_spec(k):
        return pl.BlockSpec((None, None, 1, d), lambda b, i, _k=k: (b, _k, 0, 0))

    tok = lambda w: pl.BlockSpec((1, tm, w), lambda b, i: (b, i, 0))
    return pl.pallas_call(
        functools.partial(_post_kernel, ff_chunk=1024),
        grid=(bsz, s // tm),
        in_specs=[tok(d), tok(GROUP_WIDTH), tok(GROUP_WIDTH),
                  mod_spec(2), mod_spec(4), mod_spec(3), mod_spec(5),
                  const(g_out_m), const(g_out_f), const(w_out), const(w_ff1), const(w_ff2)],
        out_specs=tok(d),
        out_shape=jax.ShapeDtypeStruct((bsz, s, d), F32),
        compiler_params=pltpu.CompilerParams(
            dimension_semantics=("arbitrary", "arbitrary"),
            vmem_limit_bytes=VMEM_LIMIT),
        name="post",
    )(x, o_m, o_f, mod4, mod4, mod4, mod4, g_out_m, g_out_f, w_out, w_ff1, w_ff2)


def _rope_tables(s):
    inv_freq = ROPE_THETA ** (-np.arange(0, ROPE_DIM, 2, dtype=np.float64) / ROPE_DIM)
    ang = np.arange(s, dtype=np.float64)[:, None] * inv_freq[None, :]
    cos, sin = np.cos(ang), np.sin(ang)
    rc = np.ones((s, LANES)); rs1 = np.zeros((s, LANES)); rs2 = np.zeros((s, LANES))
    for base in (0, HEAD_DIM):
        rc[:, base:base + ROPE_HALF] = cos
        rc[:, base + ROPE_HALF:base + ROPE_DIM] = cos
        rs2[:, base:base + ROPE_HALF] = -sin
        rs1[:, base + ROPE_HALF:base + ROPE_DIM] = sin
    f = lambda a: jnp.asarray(a, dtype=F32)
    return f(cos.T), f(sin.T), f(rc), f(rs1), f(rs2)


def _static_mats(s):
    idx = np.arange(256)
    jmat = (idx[:, None] // HEAD_DIM == idx[None, :] // HEAD_DIM) / HEAD_DIM
    place = np.zeros((LANES, GROUP_WIDTH))
    for part in range(3):
        for hd in range(GROUP_HEADS):
            place[part * GROUP_HEADS + hd,
                  (hd // 2) * PAIR_WIDTH + (hd % 2) * HEAD_DIM + part] = 1.0
    kones = np.zeros((1, GROUP_WIDTH))
    for hd in range(GROUP_HEADS):
        kones[0, hd * HEAD_DIM + 3:hd * HEAD_DIM + 6] = 1.0
    ka_moba = np.zeros((1, s, PAIR_WIDTH))
    blk = np.arange(s) // MOBA_BLOCK
    for e in range(2):
        ka_moba[0, np.arange(s), e * HEAD_DIM + blk] = 1.0
    b = lambda a: jnp.asarray(a, dtype=BF16)
    return b(jmat), b(place), jnp.asarray(kones, dtype=F32), b(ka_moba)


def kernel(x, c, w_ada, b_ada, w_in, b_forget, g_qn_moba, g_kn_moba, g_qn_fox, g_kn_fox,
           g_out_moba, g_out_fox, w_out, w_ff1, w_ff2):
    bsz, s, d = x.shape
    depth = w_ada.shape[0]
    w = GROUP_WIDTH
    assert s % MOBA_BLOCK == 0 and s // MOBA_BLOCK <= 16
    tm_in = 1024
    tm_post = 512
    cos_t, sin_t, rc, rs1, rs2 = _rope_tables(s)
    jmat, place, kones, ka_moba = _static_mats(s)
    q_scale = HEAD_DIM ** -0.5 * LOG2E

    for l in range(depth):
        mod = _adaln(c, w_ada[l], b_ada[l])
        mod4 = mod.reshape(bsz, 6, 1, d)

        wl = w_in[l]
        col = lambda k: wl[:, k * w:(k + 1) * w]
        wf = jnp.zeros((d, LANES), F32).at[:, :GROUP_HEADS].set(wl[:, 6 * w:]).astype(BF16)
        bfor = jnp.zeros((1, LANES), F32).at[0, :GROUP_HEADS].set(b_forget[l])
        q_gain = lambda g: jnp.broadcast_to((g * q_scale)[:, None], (HEAD_DIM, tm_in))
        k_gain = lambda g: jnp.tile(g, GROUP_HEADS).reshape(1, w)
        wts = (col(0).T.astype(BF16), col(1).astype(BF16), col(2).T.astype(BF16),
               col(3).T.astype(BF16), col(4).astype(BF16), col(5).T.astype(BF16),
               wf, bfor,
               q_gain(g_qn_moba[l]), k_gain(g_kn_moba[l]),
               q_gain(g_qn_fox[l]), k_gain(g_kn_fox[l]))
        consts = (cos_t, sin_t, rc, rs1, rs2, jmat, place, kones)

        qm_t, km, vm_t, bias_t, qf_t, kf, vf_t, ka_f, qa_f = _inproj(
            x, mod4[:, 1], mod4[:, 0], wts, consts, tm_in)

        o_m = _attention_dispatch(g_qn_moba[l] * q_scale, g_kn_moba[l],
                                  qm_t, bias_t, km, ka_moba, vm_t, ka_const=True,
                                  name="attn_moba")
        o_f = _attention_dispatch(g_qn_fox[l] * q_scale, g_kn_fox[l],
                                  qf_t, qa_f, kf, ka_f, vf_t, ka_const=False,
                                  name="attn_fox")

        x = _post(x, o_m, o_f, mod4,
                  g_out_moba[l].reshape(1, w), g_out_fox[l].reshape(1, w),
                  w_out[l].astype(BF16), w_ff1[l].astype(BF16), w_ff2[l].astype(BF16),
                  tm_post)
    return x
```

```python
import functools
import math

import numpy as np
import jax
import jax.numpy as jnp
from jax import lax
from jax.experimental import pallas as pl
from jax.experimental.pallas import tpu as pltpu

F32 = jnp.float32
BF16 = jnp.bfloat16
HIGHEST = lax.Precision.HIGHEST

HEAD_DIM = 64
GROUP_HEADS = 8
GROUP_WIDTH = GROUP_HEADS * HEAD_DIM
PAIR_WIDTH = 2 * HEAD_DIM
N_PAIRS = GROUP_HEADS // 2
MOBA_BLOCK = 256
MOBA_TOPK = 3
ROPE_THETA = 500000.0
ROPE_DIM = HEAD_DIM // 4
ROPE_HALF = ROPE_DIM // 2
EPS = 1e-6
LOG2E = math.log2(math.e)
NEG_BIG = -1e30
MAX_SAFE_EXPONENT = 60.0
ATT_TILE = 512
KEY_TILE = 256
CUM_BLOCK = 128
PAIRS_PER_TRIP = 4
LANES = 128
VMEM_LIMIT = 48 * 1024 * 1024


def _dot(a, b, precision=None):
    return jnp.dot(a, b, preferred_element_type=F32, precision=precision)


def _split3(x):
    hi = x.astype(BF16)
    rem = x - hi.astype(F32)
    mid = rem.astype(BF16)
    lo = (rem - mid.astype(F32)).astype(BF16)
    return hi, mid, lo


def _dot_nt(a, b, precision=None):
    return lax.dot_general(a, b, (((1,), (1,)), ((), ())),
                           preferred_element_type=F32, precision=precision)


def _adaln_kernel(ct_ref, w_ref, b_ref, o_ref):
    c_t = ct_ref[...]
    s_t = c_t / (1.0 + jnp.exp(-c_t))
    w = w_ref[...]
    rows = [jnp.sum(s_t[:, b:b + 1] * w, axis=0, keepdims=True) for b in range(c_t.shape[1])]
    o_ref[...] = jnp.concatenate(rows, axis=0) + b_ref[...]


def _adaln(c, w_ada, b_ada):
    bsz, d = c.shape
    n = w_ada.shape[1]
    return pl.pallas_call(
        _adaln_kernel,
        grid=(n // d,),
        in_specs=[pl.BlockSpec((d, bsz), lambda j: (0, 0)),
                  pl.BlockSpec((d, d), lambda j: (0, j)),
                  pl.BlockSpec((1, d), lambda j: (0, j))],
        out_specs=pl.BlockSpec((bsz, d), lambda j: (0, j)),
        out_shape=jax.ShapeDtypeStruct((bsz, n), F32),
        compiler_params=pltpu.CompilerParams(dimension_semantics=("arbitrary",)),
        name="adaln",
    )(c.T, w_ada, b_ada.reshape(1, n))


def _inproj_kernel(x_ref, sc_ref, sh_ref,
                   wqm_ref, wkm_ref, wvm_ref, wqf_ref, wkf_ref, wvf_ref, wf_ref, bf_ref,
                   gqm_ref, gkm_ref, gqf_ref, gkf_ref,
                   cos_t_ref, sin_t_ref, rc_ref, rs1_ref, rs2_ref,
                   j_ref, place_ref, kones_ref,
                   qm_o, km_o, vm_o, bias_o, qf_o, kf_o, vf_o, ka_o, qa_o,
                   kmean_sc, carry_sc, gate_sc, *, tm):
    i = pl.program_id(1)
    nb = tm // MOBA_BLOCK

    @pl.when(i == 0)
    def _():
        kmean_sc[...] = jnp.zeros_like(kmean_sc)
        carry_sc[...] = jnp.zeros_like(carry_sc)

    x = x_ref[0]
    ms = jnp.mean(x * x, axis=-1, keepdims=True)
    h = x * lax.rsqrt(ms + EPS) * (1.0 + sc_ref[...]) + sh_ref[...]
    hb = h.astype(BF16)

    def q_feature_major(w_ref, g_ref, rope):
        q_t = _dot_nt(w_ref[...], hb)
        gain = g_ref[...]
        heads = []
        for hd in range(GROUP_HEADS):
            xh = q_t[hd * HEAD_DIM:(hd + 1) * HEAD_DIM, :]
            ss = jnp.sum(xh * xh, axis=0, keepdims=True)
            xh = xh * lax.rsqrt(ss * (1.0 / HEAD_DIM) + EPS) * gain
            if rope:
                x1 = xh[0:ROPE_HALF, :]
                x2 = xh[ROPE_HALF:ROPE_DIM, :]
                cs = cos_t_ref[...]
                sn = sin_t_ref[...]
                xh = jnp.concatenate(
                    [x1 * cs - x2 * sn, x2 * cs + x1 * sn, xh[ROPE_DIM:, :]], axis=0)
            heads.append(xh)
        return jnp.concatenate(heads, axis=0)

    def k_token_major(w_ref, g_ref, rope):
        k = _dot(hb, w_ref[...])
        chunks = []
        for c in range(GROUP_WIDTH // 256):
            seg = k[:, c * 256:(c + 1) * 256]
            msq = _dot((seg * seg).astype(BF16), j_ref[...])
            chunks.append(seg * lax.rsqrt(msq + EPS) * g_ref[:, c * 256:(c + 1) * 256])
        k = jnp.concatenate(chunks, axis=1)
        if rope:
            outs = []
            for c in range(GROUP_WIDTH // LANES):
                seg = k[:, c * LANES:(c + 1) * LANES]
                outs.append(seg * rc_ref[...]
                            + pltpu.roll(seg, ROPE_HALF, 1) * rs1_ref[...]
                            + pltpu.roll(seg, LANES - ROPE_HALF, 1) * rs2_ref[...])
            k = jnp.concatenate(outs, axis=1)
        return k

    def store_tiles(o_ref, val_t, width=ATT_TILE):
        for t in range(tm // width):
            o_ref[0, t] = val_t[:, t * width:(t + 1) * width].astype(o_ref.dtype)

    qm_t = q_feature_major(wqm_ref, gqm_ref, True)
    store_tiles(qm_o, qm_t)
    km = k_token_major(wkm_ref, gkm_ref, True)
    km_o[0] = km.astype(BF16)
    store_tiles(vm_o, _dot_nt(wvm_ref[...], hb), KEY_TILE)

    blk0 = i * nb
    lane_head = lax.broadcasted_iota(jnp.int32, (1, GROUP_WIDTH), 1) // HEAD_DIM
    for sb in range(nb):
        kmean = jnp.mean(km[sb * MOBA_BLOCK:(sb + 1) * MOBA_BLOCK, :], axis=0, keepdims=True)
        for hd in range(GROUP_HEADS):
            row = hd * 16 + blk0 + sb
            kmean_sc[pl.ds(row, 1), :] = jnp.where(lane_head == hd, kmean, 0.0)

    km_hi, km_lo, _ = _split3(kmean_sc[...])
    q_hi, q_lo, _ = _split3(qm_t)
    gate_t = _dot(km_hi, q_hi) + _dot(km_hi, q_lo) + _dot(km_lo, q_hi)
    tok_blk = blk0 + lax.broadcasted_iota(jnp.int32, (16, tm), 1) // MOBA_BLOCK
    n_idx = lax.broadcasted_iota(jnp.int32, (16, tm), 0)
    valid = n_idx < tok_blk
    for hd in range(GROUP_HEADS):
        gate_sc[hd * 16:(hd + 1) * 16, :] = jnp.where(
            valid, gate_t[hd * 16:(hd + 1) * 16, :], -jnp.inf)

    def rank_step(m, cnts):
        tie = jnp.where(n_idx > m, 1, 0)
        out = []
        for hd in range(GROUP_HEADS):
            g = gate_sc[hd * 16:(hd + 1) * 16, :]
            gm = jnp.broadcast_to(gate_sc[pl.ds(hd * 16 + m, 1), :], (16, tm))
            out.append(cnts[hd] + jnp.where(gm > g, 1, jnp.where(gm == g, tie, 0)))
        return tuple(out)

    cnts = lax.fori_loop(0, blk0 + nb - 1, rank_step,
                         tuple(jnp.zeros((16, tm), jnp.int32) for _ in range(GROUP_HEADS)))
    zeros48 = jnp.zeros((HEAD_DIM - 16, tm), F32)
    pair_rows = []
    for hd in range(GROUP_HEADS):
        sel = (valid & (cnts[hd] < MOBA_TOPK)) | (n_idx == tok_blk)
        pair_rows += [jnp.where(sel, 0.0, NEG_BIG), zeros48]
    store_tiles(bias_o, jnp.concatenate(pair_rows, axis=0))

    store_tiles(qf_o, q_feature_major(wqf_ref, gqf_ref, False))
    kf_o[0] = k_token_major(wkf_ref, gkf_ref, False).astype(BF16)
    store_tiles(vf_o, _dot_nt(wvf_ref[...], hb), KEY_TILE)

    f = _dot(hb, wf_ref[...]) + bf_ref[...]
    logf = jnp.minimum(f, 0.0) - jnp.log1p(jnp.exp(-jnp.abs(f)))
    live = lax.broadcasted_iota(jnp.int32, (1, LANES), 1) < GROUP_HEADS
    logf = jnp.where(live, logf, 0.0)
    r_i = lax.broadcasted_iota(jnp.int32, (CUM_BLOCK, CUM_BLOCK), 0)
    c_i = lax.broadcasted_iota(jnp.int32, (CUM_BLOCK, CUM_BLOCK), 1)
    tri = jnp.where(r_i >= c_i, 1.0, 0.0).astype(BF16)
    carry = carry_sc[...]
    cums = []
    for blk in range(tm // CUM_BLOCK):
        parts = _split3(logf[blk * CUM_BLOCK:(blk + 1) * CUM_BLOCK, :])
        c_blk = _dot(tri, parts[0]) + _dot(tri, parts[1]) + _dot(tri, parts[2]) + carry
        carry = c_blk[CUM_BLOCK - 1:CUM_BLOCK, :]
        cums.append(c_blk)
    carry_sc[...] = carry
    nc = jnp.concatenate(cums, axis=0) * (-LOG2E)
    hi, mid, lo = [part.astype(F32) for part in _split3(nc)]
    packed = hi + pltpu.roll(mid, GROUP_HEADS, 1) + pltpu.roll(lo, 2 * GROUP_HEADS, 1)
    ka_o[0] = (_dot(packed.astype(BF16), place_ref[...]) + kones_ref[...]).astype(BF16)

    nc_t = nc.T[0:GROUP_HEADS, :]
    hi_t = nc_t.astype(BF16).astype(F32)
    rem_t = nc_t - hi_t
    mid_t = rem_t.astype(BF16).astype(F32)
    lo_t = (rem_t - mid_t).astype(BF16).astype(F32)
    r8 = lax.broadcasted_iota(jnp.int32, (8, tm), 0)
    zeros56 = jnp.zeros((HEAD_DIM - 8, tm), F32)
    qa_rows = []
    for hd in range(GROUP_HEADS):
        blk = jnp.where(r8 < 3, 1.0,
                        jnp.where(r8 == 3, -hi_t[hd:hd + 1, :],
                                  jnp.where(r8 == 4, -mid_t[hd:hd + 1, :],
                                            jnp.where(r8 == 5, -lo_t[hd:hd + 1, :], 0.0))))
        qa_rows += [blk, zeros56]
    store_tiles(qa_o, jnp.concatenate(qa_rows, axis=0))


def _inproj(x, sc, sh, wts, consts, tm):
    bsz, s, d = x.shape
    nt = s // ATT_TILE
    grid = (bsz, s // tm)
    wqm, wkm, wvm, wqf, wkf, wvf, wf, bfor, gqm, gkm, gqf, gkf = wts
    cos_t, sin_t, rc, rs1, rs2, jmat, place, kones = consts

    def const(a):
        nd = a.ndim
        return pl.BlockSpec(a.shape, lambda b, i, _n=nd: (0,) * _n,
                            pipeline_mode=pl.Buffered(1))

    mod_spec = pl.BlockSpec((None, 1, d), lambda b, i: (b, 0, 0))
    in_specs = [pl.BlockSpec((1, tm, d), lambda b, i: (b, i, 0)), mod_spec, mod_spec]
    in_specs += [const(a) for a in (wqm, wkm, wvm, wqf, wkf, wvf, wf, bfor,
                                    gqm, gkm, gqf, gkf)]
    in_specs += [pl.BlockSpec((ROPE_HALF, tm), lambda b, i: (0, i)),
                 pl.BlockSpec((ROPE_HALF, tm), lambda b, i: (0, i)),
                 pl.BlockSpec((tm, LANES), lambda b, i: (i, 0)),
                 pl.BlockSpec((tm, LANES), lambda b, i: (i, 0)),
                 pl.BlockSpec((tm, LANES), lambda b, i: (i, 0)),
                 const(jmat), const(place), const(kones)]

    tiles = tm // ATT_TILE
    feat_spec = pl.BlockSpec((1, tiles, GROUP_WIDTH, ATT_TILE), lambda b, i: (b, i, 0, 0))
    tok_spec = pl.BlockSpec((1, tm, GROUP_WIDTH), lambda b, i: (b, i, 0))
    feat_shape = jax.ShapeDtypeStruct((bsz, nt, GROUP_WIDTH, ATT_TILE), BF16)
    tok_shape = jax.ShapeDtypeStruct((bsz, s, GROUP_WIDTH), BF16)
    val_spec = pl.BlockSpec((1, tm // KEY_TILE, GROUP_WIDTH, KEY_TILE),
                            lambda b, i: (b, i, 0, 0))
    val_shape = jax.ShapeDtypeStruct((bsz, s // KEY_TILE, GROUP_WIDTH, KEY_TILE), BF16)
    out_specs = [feat_spec, tok_spec, val_spec, feat_spec,
                 feat_spec, tok_spec, val_spec, tok_spec, feat_spec]
    out_shape = [feat_shape, tok_shape, val_shape, feat_shape,
                 feat_shape, tok_shape, val_shape, tok_shape, feat_shape]
    return pl.pallas_call(
        functools.partial(_inproj_kernel, tm=tm),
        grid=grid,
        in_specs=in_specs,
        out_specs=out_specs,
        out_shape=out_shape,
        scratch_shapes=[pltpu.VMEM((GROUP_HEADS * 16, GROUP_WIDTH), F32),
                        pltpu.VMEM((1, LANES), F32),
                        pltpu.VMEM((GROUP_HEADS * 16, tm), F32)],
        compiler_params=pltpu.CompilerParams(
            dimension_semantics=("arbitrary", "arbitrary"),
            vmem_limit_bytes=VMEM_LIMIT),
        name="inproj",
    )(x, sc, sh, wqm, wkm, wvm, wqf, wkf, wvf, wf, bfor, gqm, gkm, gqf, gkf,
      cos_t, sin_t, rc, rs1, rs2, jmat, place, kones)


def _attn_kernel(q_ref, qa_ref, k_ref, ka_ref, v_ref, o_ref, *, t):
    i = pl.program_id(2)
    q = q_ref[0, 0]
    qa = qa_ref[0, 0]
    row = lax.broadcasted_iota(jnp.int32, (PAIR_WIDTH, t), 0)
    rhs = []
    for e in range(2):
        keep = (row >= e * HEAD_DIM) & (row < (e + 1) * HEAD_DIM)
        zero = jnp.zeros_like(q)
        rhs.append(jnp.concatenate([jnp.where(keep, q, zero), jnp.where(keep, qa, zero)],
                                   axis=0))

    def tile_scores(j):
        start = pl.multiple_of(j * t, t)
        lhs = jnp.concatenate([k_ref[0, pl.ds(start, t), :], ka_ref[0, pl.ds(start, t), :]],
                              axis=1)
        return [_dot(lhs, rhs[e]) for e in range(2)]

    def values(j, e):
        rows = slice(e * HEAD_DIM, (e + 1) * HEAD_DIM)
        slabs = t // KEY_TILE
        return jnp.concatenate([v_ref[0, slabs * j + h, rows, :] for h in range(slabs)],
                               axis=1)

    key_i = lax.broadcasted_iota(jnp.int32, (t, t), 0)
    qry_i = lax.broadcasted_iota(jnp.int32, (t, t), 1)
    causal = key_i <= qry_i
    state = []
    for e, s_t in enumerate(tile_scores(i)):
        s_t = jnp.where(causal, s_t, NEG_BIG)
        m = jnp.max(s_t, axis=0, keepdims=True)
        p = jnp.exp2(s_t - m)
        l = jnp.sum(p, axis=0, keepdims=True)
        acc = _dot(values(i, e), p.astype(BF16))
        state += [m, l, acc]

    def body(j, carry):
        out = []
        for e, s_t in enumerate(tile_scores(j)):
            m, l, acc = carry[3 * e:3 * e + 3]
            m_new = jnp.maximum(m, jnp.max(s_t, axis=0, keepdims=True))
            alpha = jnp.exp2(m - m_new)
            p = jnp.exp2(s_t - m_new)
            l = alpha * l + jnp.sum(p, axis=0, keepdims=True)
            acc = alpha * acc + _dot(values(j, e), p.astype(BF16))
            out += [m_new, l, acc]
        return tuple(out)

    state = lax.fori_loop(0, i, body, tuple(state))
    o_t = jnp.concatenate([state[2] / state[1], state[5] / state[4]], axis=0)
    o_ref[0] = o_t.T


def _attn_bounded_kernel(q_ref, qa_ref, k_ref, ka_ref, v_ref, o_ref, p_sc, acc_sc, l_sc, *, t,
                         ka_const):
    i = pl.program_id(1)
    kt = KEY_TILE
    row = lax.broadcasted_iota(jnp.int32, (PAIR_WIDTH, t), 0)
    key_i = lax.broadcasted_iota(jnp.int32, (kt, t), 0)
    qry_i = lax.broadcasted_iota(jnp.int32, (kt, t), 1)
    assert t == 2 * kt


    def rhs_of(pr):
        feat = slice(pr * PAIR_WIDTH, (pr + 1) * PAIR_WIDTH)
        q = q_ref[0, 0, feat, :]
        qa = qa_ref[0, 0, feat, :]
        out = []
        for e in range(2):
            keep = (row >= e * HEAD_DIM) & (row < (e + 1) * HEAD_DIM)
            zero = jnp.zeros_like(q)
            out.append(jnp.concatenate([jnp.where(keep, q, zero), jnp.where(keep, qa, zero)],
                                       axis=0))
        return out

    for grp in range(N_PAIRS // PAIRS_PER_TRIP):
        prs = list(range(grp * PAIRS_PER_TRIP, (grp + 1) * PAIRS_PER_TRIP))
        rhs = {pr: rhs_of(pr) for pr in prs}

        def probs(pr, start, n_keys, row0, mask=None, q0=0, first=False, rhs=rhs):
            feat = slice(pr * PAIR_WIDTH, (pr + 1) * PAIR_WIDTH)
            k_aug = (ka_ref[0, pl.ds(start, n_keys), :] if ka_const
                     else ka_ref[0, pl.ds(start, n_keys), feat])
            lhs = jnp.concatenate([k_ref[0, pl.ds(start, n_keys), feat], k_aug], axis=1)
            for e in range(2):
                s_t = _dot(lhs, rhs[pr][e][:, q0:])
                if mask is not None:
                    s_t = jnp.where(mask[:, q0:], s_t, NEG_BIG)
                p = jnp.exp2(s_t)
                p_sc[pr, e, row0:row0 + n_keys, q0:] = p.astype(BF16)
                part = p[0:8, :]
                for r in range(8, n_keys, 8):
                    part = part + p[r:r + 8, :]
                if first:
                    l_sc[pr, e] = part
                else:
                    l_sc[pr, e, :, q0:] += part

        def values(pr, j):
            for e in range(2):
                rows = slice(pr * PAIR_WIDTH + e * HEAD_DIM, pr * PAIR_WIDTH + (e + 1) * HEAD_DIM)
                v_e = jnp.concatenate([v_ref[0, 2 * j, rows, :], v_ref[0, 2 * j + 1, rows, :]],
                                      axis=1)
                acc_sc[pr, e] += _dot(v_e, p_sc[pr, e])

        diag = pl.multiple_of(i * t, t)
        for pr in prs:
            probs(pr, diag, kt, 0, key_i <= qry_i, first=True)
            probs(pr, diag + kt, kt, kt, key_i + kt <= qry_i, q0=kt)
            for e in range(2):
                p_sc[pr, e, kt:, :kt] = jnp.zeros((kt, kt), BF16)
                acc_sc[pr, e] = jnp.zeros((HEAD_DIM, t), F32)

        def body(j, j_prev, probs=probs, values=values, prs=prs):
            for pr in prs:
                values(pr, j_prev)
                probs(pr, pl.multiple_of(j * t, t), t, 0)
            return j

        j_prev = lax.fori_loop(0, i, body, i)
        for pr in prs:
            values(pr, j_prev)
        for pr in prs:
            o_t = jnp.concatenate(
                [acc_sc[pr, e] / jnp.sum(l_sc[pr, e], axis=0, keepdims=True)
                 for e in range(2)], axis=0)
            o_ref[0, :, pr * PAIR_WIDTH:(pr + 1) * PAIR_WIDTH] = o_t.T


def _attention_online(q_t, qa, k, ka, v_t, *, ka_const, name):
    bsz, nt, _, t = q_t.shape
    s = k.shape[1]
    grid = (bsz, N_PAIRS, nt)
    if ka_const:
        ka_spec = pl.BlockSpec((1, s, PAIR_WIDTH), lambda b, p, i: (0, 0, 0))
    else:
        ka_spec = pl.BlockSpec((1, s, PAIR_WIDTH), lambda b, p, i: (b, 0, p))
    return pl.pallas_call(
        functools.partial(_attn_kernel, t=t),
        grid=grid,
        in_specs=[pl.BlockSpec((1, 1, PAIR_WIDTH, t), lambda b, p, i: (b, i, p, 0)),
                  pl.BlockSpec((1, 1, PAIR_WIDTH, t), lambda b, p, i: (b, i, p, 0)),
                  pl.BlockSpec((1, s, PAIR_WIDTH), lambda b, p, i: (b, 0, p)),
                  ka_spec,
                  pl.BlockSpec((1, s // KEY_TILE, PAIR_WIDTH, KEY_TILE),
                               lambda b, p, i: (b, 0, p, 0))],
        out_specs=pl.BlockSpec((1, t, PAIR_WIDTH), lambda b, p, i: (b, i, p)),
        out_shape=jax.ShapeDtypeStruct((bsz, s, GROUP_WIDTH), F32),
        compiler_params=pltpu.CompilerParams(
            dimension_semantics=("arbitrary", "arbitrary", "arbitrary"),
            vmem_limit_bytes=VMEM_LIMIT),
        name=name,
    )(q_t, qa, k, ka, v_t)


def _attention_bounded(q_t, qa, k, ka, v_t, *, ka_const, name):
    bsz, nt, _, t = q_t.shape
    s = k.shape[1]
    feat_spec = pl.BlockSpec((1, 1, GROUP_WIDTH, t), lambda b, i: (b, i, 0, 0))
    tok_spec = pl.BlockSpec((1, s, GROUP_WIDTH), lambda b, i: (b, 0, 0))
    ka_spec = pl.BlockSpec((1, s, PAIR_WIDTH), lambda b, i: (0, 0, 0)) if ka_const else tok_spec
    return pl.pallas_call(
        functools.partial(_attn_bounded_kernel, t=t, ka_const=ka_const),
        grid=(bsz, nt),
        in_specs=[feat_spec, feat_spec, tok_spec, ka_spec,
                  pl.BlockSpec((1, s // KEY_TILE, GROUP_WIDTH, KEY_TILE),
                               lambda b, i: (b, 0, 0, 0))],
        out_specs=pl.BlockSpec((1, t, GROUP_WIDTH), lambda b, i: (b, i, 0)),
        out_shape=jax.ShapeDtypeStruct((bsz, s, GROUP_WIDTH), F32),
        scratch_shapes=[pltpu.VMEM((N_PAIRS, 2, t, t), BF16),
                        pltpu.VMEM((N_PAIRS, 2, HEAD_DIM, t), F32),
                        pltpu.VMEM((N_PAIRS, 2, 8, t), F32)],
        compiler_params=pltpu.CompilerParams(
            dimension_semantics=("arbitrary", "arbitrary"),
            vmem_limit_bytes=VMEM_LIMIT),
        name=name,
    )(q_t, qa, k, ka, v_t)


def _attention_dispatch(g_q, g_k, q_t, qa, k, ka, v_t, *, ka_const, name):
    bound = HEAD_DIM * jnp.max(jnp.abs(g_q)) * jnp.max(jnp.abs(g_k))
    return lax.cond(
        bound <= MAX_SAFE_EXPONENT,
        functools.partial(_attention_bounded, ka_const=ka_const, name=name + "_bounded"),
        functools.partial(_attention_online, ka_const=ka_const, name=name + "_online"),
        q_t, qa, k, ka, v_t)


def _post_kernel(x_ref, om_ref, of_ref, ga_ref, scm_ref, shm_ref, gm_ref,
                 gom_ref, gof_ref, wout_ref, w1_ref, w2_ref, o_ref, *, ff_chunk):
    def rms(v):
        return v * lax.rsqrt(jnp.mean(v * v, axis=-1, keepdims=True) + EPS)

    mixed = jnp.concatenate([(rms(om_ref[0]) * gom_ref[...]).astype(BF16),
                             (rms(of_ref[0]) * gof_ref[...]).astype(BF16)], axis=1)
    x1 = x_ref[0] + ga_ref[...] * _dot(mixed, wout_ref[...])
    h = (rms(x1) * (1.0 + scm_ref[...]) + shm_ref[...]).astype(BF16)
    d_ff = w1_ref.shape[1]
    y = None
    for c in range(d_ff // ff_chunk):
        hid = _dot(h, w1_ref[:, c * ff_chunk:(c + 1) * ff_chunk])
        hid = jnp.square(jnp.maximum(hid, 0.0)).astype(BF16)
        part = _dot(hid, w2_ref[c * ff_chunk:(c + 1) * ff_chunk, :])
        y = part if y is None else y + part
    o_ref[0] = x1 + gm_ref[...] * y


def _post(x, o_m, o_f, mod4, g_out_m, g_out_f, w_out, w_ff1, w_ff2, tm):
    bsz, s, d = x.shape
    d_ff = w_ff1.shape[1]

    def const(a):
        nd = a.ndim
        return pl.BlockSpec(a.shape, lambda b, i, _n=nd: (0,) * _n,
                            pipeline_mode=pl.Buffered(1))

    def mod_spec(k):
        return pl.BlockSpec((None, None, 1, d), lambda b, i, _k=k: (b, _k, 0, 0))

    tok = lambda w: pl.BlockSpec((1, tm, w), lambda b, i: (b, i, 0))
    return pl.pallas_call(
        functools.partial(_post_kernel, ff_chunk=1024),
        grid=(bsz, s // tm),
        in_specs=[tok(d), tok(GROUP_WIDTH), tok(GROUP_WIDTH),
                  mod_spec(2), mod_spec(4), mod_spec(3), mod_spec(5),
                  const(g_out_m), const(g_out_f), const(w_out), const(w_ff1), const(w_ff2)],
        out_specs=tok(d),
        out_shape=jax.ShapeDtypeStruct((bsz, s, d), F32),
        compiler_params=pltpu.CompilerParams(
            dimension_semantics=("arbitrary", "arbitrary"),
            vmem_limit_bytes=VMEM_LIMIT),
        name="post",
    )(x, o_m, o_f, mod4, mod4, mod4, mod4, g_out_m, g_out_f, w_out, w_ff1, w_ff2)


def _rope_tables(s):
    inv_freq = ROPE_THETA ** (-np.arange(0, ROPE_DIM, 2, dtype=np.float64) / ROPE_DIM)
    ang = np.arange(s, dtype=np.float64)[:, None] * inv_freq[None, :]
    cos, sin = np.cos(ang), np.sin(ang)
    rc = np.ones((s, LANES)); rs1 = np.zeros((s, LANES)); rs2 = np.zeros((s, LANES))
    for base in (0, HEAD_DIM):
        rc[:, base:base + ROPE_HALF] = cos
        rc[:, base + ROPE_HALF:base + ROPE_DIM] = cos
        rs2[:, base:base + ROPE_HALF] = -sin
        rs1[:, base + ROPE_HALF:base + ROPE_DIM] = sin
    f = lambda a: jnp.asarray(a, dtype=F32)
    return f(cos.T), f(sin.T), f(rc), f(rs1), f(rs2)


def _static_mats(s):
    idx = np.arange(256)
    jmat = (idx[:, None] // HEAD_DIM == idx[None, :] // HEAD_DIM) / HEAD_DIM
    place = np.zeros((LANES, GROUP_WIDTH))
    for part in range(3):
        for hd in range(GROUP_HEADS):
            place[part * GROUP_HEADS + hd,
                  (hd // 2) * PAIR_WIDTH + (hd % 2) * HEAD_DIM + part] = 1.0
    kones = np.zeros((1, GROUP_WIDTH))
    for hd in range(GROUP_HEADS):
        kones[0, hd * HEAD_DIM + 3:hd * HEAD_DIM + 6] = 1.0
    ka_moba = np.zeros((1, s, PAIR_WIDTH))
    blk = np.arange(s) // MOBA_BLOCK
    for e in range(2):
        ka_moba[0, np.arange(s), e * HEAD_DIM + blk] = 1.0
    b = lambda a: jnp.asarray(a, dtype=BF16)
    return b(jmat), b(place), jnp.asarray(kones, dtype=F32), b(ka_moba)


def kernel(x, c, w_ada, b_ada, w_in, b_forget, g_qn_moba, g_kn_moba, g_qn_fox, g_kn_fox,
           g_out_moba, g_out_fox, w_out, w_ff1, w_ff2):
    bsz, s, d = x.shape
    depth = w_ada.shape[0]
    w = GROUP_WIDTH
    assert s % MOBA_BLOCK == 0 and s // MOBA_BLOCK <= 16
    tm_in = 1024
    tm_post = 512
    cos_t, sin_t, rc, rs1, rs2 = _rope_tables(s)
    jmat, place, kones, ka_moba = _static_mats(s)
    q_scale = HEAD_DIM ** -0.5 * LOG2E

    for l in range(depth):
        mod = _adaln(c, w_ada[l], b_ada[l])
        mod4 = mod.reshape(bsz, 6, 1, d)

        wl = w_in[l]
        col = lambda k: wl[:, k * w:(k + 1) * w]
        wf = jnp.zeros((d, LANES), F32).at[:, :GROUP_HEADS].set(wl[:, 6 * w:]).astype(BF16)
        bfor = jnp.zeros((1, LANES), F32).at[0, :GROUP_HEADS].set(b_forget[l])
        q_gain = lambda g: jnp.broadcast_to((g * q_scale)[:, None], (HEAD_DIM, tm_in))
        k_gain = lambda g: jnp.tile(g, GROUP_HEADS).reshape(1, w)
        wts = (col(0).T.astype(BF16), col(1).astype(BF16), col(2).T.astype(BF16),
               col(3).T.astype(BF16), col(4).astype(BF16), col(5).T.astype(BF16),
               wf, bfor,
               q_gain(g_qn_moba[l]), k_gain(g_kn_moba[l]),
               q_gain(g_qn_fox[l]), k_gain(g_kn_fox[l]))
        consts = (cos_t, sin_t, rc, rs1, rs2, jmat, place, kones)

        qm_t, km, vm_t, bias_t, qf_t, kf, vf_t, ka_f, qa_f = _inproj(
            x, mod4[:, 1], mod4[:, 0], wts, consts, tm_in)

        o_m = _attention_dispatch(g_qn_moba[l] * q_scale, g_kn_moba[l],
                                  qm_t, bias_t, km, ka_moba, vm_t, ka_const=True,
                                  name="attn_moba")
        o_f = _attention_dispatch(g_qn_fox[l] * q_scale, g_kn_fox[l],
                                  qf_t, qa_f, kf, ka_f, vf_t, ka_const=False,
                                  name="attn_fox")

        x = _post(x, o_m, o_f, mod4,
                  g_out_moba[l].reshape(1, w), g_out_fox[l].reshape(1, w),
                  w_out[l].astype(BF16), w_ff1[l].astype(BF16), w_ff2[l].astype(BF16),
                  tm_post)
    return x
```

```python
import functools
import math

import numpy as np
import jax
import jax.numpy as jnp
from jax import lax
from jax.experimental import pallas as pl
from jax.experimental.pallas import tpu as pltpu

F32 = jnp.float32
BF16 = jnp.bfloat16

HEAD_DIM = 64
GROUP_HEADS = 8
GROUP_WIDTH = GROUP_HEADS * HEAD_DIM
PAIR_WIDTH = 2 * HEAD_DIM
N_PAIRS = GROUP_HEADS // 2
MOBA_BLOCK = 256
MOBA_TOPK = 3
ROPE_THETA = 500000.0
ROPE_DIM = HEAD_DIM // 4
ROPE_HALF = ROPE_DIM // 2
EPS = 1e-6
LOG2E = math.log2(math.e)
NEG_BIG = -1e30
MAX_SAFE_EXPONENT = 60.0
ATT_TILE = 512
KEY_TILE = 256
CUM_BLOCK = 128
PAIRS_PER_TRIP = 4
VALUE_ROWS = 128
LANES = 128
VMEM_LIMIT = 48 * 1024 * 1024


def _dot(a, b):
    return jnp.dot(a, b, preferred_element_type=F32)


def _split3(x):
    hi = x.astype(BF16)
    rem = x - hi.astype(F32)
    mid = rem.astype(BF16)
    lo = (rem - mid.astype(F32)).astype(BF16)
    return hi, mid, lo


def _dot_nt(a, b):
    return lax.dot_general(a, b, (((1,), (1,)), ((), ())), preferred_element_type=F32)


def _adaln_kernel(ct_ref, w_ref, b_ref, o_ref):
    c_t = ct_ref[...]
    s_t = c_t / (1.0 + jnp.exp(-c_t))
    w = w_ref[...]
    rows = [jnp.sum(s_t[:, b:b + 1] * w, axis=0, keepdims=True) for b in range(c_t.shape[1])]
    o_ref[...] = jnp.concatenate(rows, axis=0) + b_ref[...]


def _adaln(c, w_ada, b_ada):
    bsz, d = c.shape
    n = w_ada.shape[1]
    return pl.pallas_call(
        _adaln_kernel,
        grid=(n // d,),
        in_specs=[pl.BlockSpec((d, bsz), lambda j: (0, 0)),
                  pl.BlockSpec((d, d), lambda j: (0, j)),
                  pl.BlockSpec((1, d), lambda j: (0, j))],
        out_specs=pl.BlockSpec((bsz, d), lambda j: (0, j)),
        out_shape=jax.ShapeDtypeStruct((bsz, n), F32),
        compiler_params=pltpu.CompilerParams(dimension_semantics=("arbitrary",)),
        name="adaln",
    )(c.T, w_ada, b_ada.reshape(1, n))


TRANSPOSED_GROUPS = (0, 2, 3, 5)


def _wprep_kernel(w_ref, *o_refs):
    g = pl.program_id(0)
    for k, o_ref in enumerate(o_refs):
        @pl.when(g == k)
        def _(k=k, o_ref=o_ref):
            w = w_ref[...]
            o_ref[...] = (w.T if k in TRANSPOSED_GROUPS else w).astype(BF16)


def _project_weights(w_in_l, width):
    d = w_in_l.shape[0]
    shapes = [(width, d) if k in TRANSPOSED_GROUPS else (d, width) for k in range(6)]
    return tuple(pl.pallas_call(
        _wprep_kernel,
        grid=(6,),
        in_specs=[pl.BlockSpec((d, width), lambda g: (0, g))],
        out_specs=[pl.BlockSpec(s, lambda g: (0, 0)) for s in shapes],
        out_shape=[jax.ShapeDtypeStruct(s, BF16) for s in shapes],
        compiler_params=pltpu.CompilerParams(dimension_semantics=("arbitrary",)),
        name="wprep",
    )(w_in_l))


def _inproj_kernel(x_ref, sc_ref, sh_ref,
                   wqm_ref, wkm_ref, wvm_ref, wqf_ref, wkf_ref, wvf_ref, wf_ref, bf_ref,
                   gqm_ref, gkm_ref, gqf_ref, gkf_ref,
                   cos_t_ref, sin_t_ref, rc_ref, rs1_ref, rs2_ref,
                   j_ref, place_ref, kones_ref,
                   qm_o, km_o, vm_o, bias_o, qf_o, kf_o, vf_o, ka_o, qa_o,
                   kmean_sc, carry_sc, gate_sc, *, tm):
    i = pl.program_id(1)
    nb = tm // MOBA_BLOCK

    @pl.when(i == 0)
    def _():
        kmean_sc[...] = jnp.zeros_like(kmean_sc)
        carry_sc[...] = jnp.zeros_like(carry_sc)

    x = x_ref[0]
    ms = jnp.mean(x * x, axis=-1, keepdims=True)
    h = x * lax.rsqrt(ms + EPS) * (1.0 + sc_ref[...]) + sh_ref[...]
    hb = h.astype(BF16)

    def q_feature_major(w_ref, g_ref, rope):
        q_t = _dot_nt(w_ref[...], hb)
        gain = g_ref[...]
        heads = []
        for hd in range(GROUP_HEADS):
            xh = q_t[hd * HEAD_DIM:(hd + 1) * HEAD_DIM, :]
            ss = jnp.sum(xh * xh, axis=0, keepdims=True)
            xh = xh * lax.rsqrt(ss * (1.0 / HEAD_DIM) + EPS) * gain
            if rope:
                x1 = xh[0:ROPE_HALF, :]
                x2 = xh[ROPE_HALF:ROPE_DIM, :]
                cs = cos_t_ref[...]
                sn = sin_t_ref[...]
                xh = jnp.concatenate(
                    [x1 * cs - x2 * sn, x2 * cs + x1 * sn, xh[ROPE_DIM:, :]], axis=0)
            heads.append(xh)
        return jnp.concatenate(heads, axis=0)

    def k_token_major(w_ref, g_ref, rope):
        k = _dot(hb, w_ref[...])
        chunks = []
        for c in range(GROUP_WIDTH // 256):
            seg = k[:, c * 256:(c + 1) * 256]
            msq = _dot((seg * seg).astype(BF16), j_ref[...])
            chunks.append(seg * lax.rsqrt(msq + EPS) * g_ref[:, c * 256:(c + 1) * 256])
        k = jnp.concatenate(chunks, axis=1)
        if rope:
            outs = []
            for c in range(GROUP_WIDTH // LANES):
                seg = k[:, c * LANES:(c + 1) * LANES]
                outs.append(seg * rc_ref[...]
                            + pltpu.roll(seg, ROPE_HALF, 1) * rs1_ref[...]
                            + pltpu.roll(seg, LANES - ROPE_HALF, 1) * rs2_ref[...])
            k = jnp.concatenate(outs, axis=1)
        return k

    def store_tiles(o_ref, val_t, width=ATT_TILE):
        for t in range(tm // width):
            o_ref[0, t] = val_t[:, t * width:(t + 1) * width].astype(o_ref.dtype)

    qm_t = q_feature_major(wqm_ref, gqm_ref, True)
    store_tiles(qm_o, qm_t)
    km = k_token_major(wkm_ref, gkm_ref, True)
    km_o[0] = km.astype(BF16)
    store_tiles(vm_o, _dot_nt(wvm_ref[...], hb), KEY_TILE)

    blk0 = i * nb
    lane_head = lax.broadcasted_iota(jnp.int32, (1, GROUP_WIDTH), 1) // HEAD_DIM
    for sb in range(nb):
        kmean = jnp.mean(km[sb * MOBA_BLOCK:(sb + 1) * MOBA_BLOCK, :], axis=0, keepdims=True)
        for hd in range(GROUP_HEADS):
            row = hd * 16 + blk0 + sb
            kmean_sc[pl.ds(row, 1), :] = jnp.where(lane_head == hd, kmean, 0.0)

    km_hi, km_lo, _ = _split3(kmean_sc[...])
    q_hi, q_lo, _ = _split3(qm_t)
    gate_t = _dot(km_hi, q_hi) + _dot(km_hi, q_lo) + _dot(km_lo, q_hi)
    tok_blk = blk0 + lax.broadcasted_iota(jnp.int32, (16, tm), 1) // MOBA_BLOCK
    n_idx = lax.broadcasted_iota(jnp.int32, (16, tm), 0)
    valid = n_idx < tok_blk
    for hd in range(GROUP_HEADS):
        gate_sc[hd * 16:(hd + 1) * 16, :] = jnp.where(
            valid, gate_t[hd * 16:(hd + 1) * 16, :], -jnp.inf)

    def rank_step(m, cnts):
        tie = jnp.where(n_idx > m, 1, 0)
        out = []
        for hd in range(GROUP_HEADS):
            g = gate_sc[hd * 16:(hd + 1) * 16, :]
            gm = jnp.broadcast_to(gate_sc[pl.ds(hd * 16 + m, 1), :], (16, tm))
            out.append(cnts[hd] + jnp.where(gm > g, 1, jnp.where(gm == g, tie, 0)))
        return tuple(out)

    cnts = lax.fori_loop(0, blk0 + nb - 1, rank_step,
                         tuple(jnp.zeros((16, tm), jnp.int32) for _ in range(GROUP_HEADS)))
    zeros48 = jnp.zeros((HEAD_DIM - 16, tm), F32)
    pair_rows = []
    for hd in range(GROUP_HEADS):
        sel = (valid & (cnts[hd] < MOBA_TOPK)) | (n_idx == tok_blk)
        pair_rows += [jnp.where(sel, 0.0, NEG_BIG), zeros48]
    store_tiles(bias_o, jnp.concatenate(pair_rows, axis=0))

    store_tiles(qf_o, q_feature_major(wqf_ref, gqf_ref, False))
    kf_o[0] = k_token_major(wkf_ref, gkf_ref, False).astype(BF16)
    store_tiles(vf_o, _dot_nt(wvf_ref[...], hb), KEY_TILE)

    f = _dot(hb, wf_ref[...]) + bf_ref[...]
    logf = jnp.minimum(f, 0.0) - jnp.log1p(jnp.exp(-jnp.abs(f)))
    live = lax.broadcasted_iota(jnp.int32, (1, LANES), 1) < GROUP_HEADS
    logf = jnp.where(live, logf, 0.0)
    r_i = lax.broadcasted_iota(jnp.int32, (CUM_BLOCK, CUM_BLOCK), 0)
    c_i = lax.broadcasted_iota(jnp.int32, (CUM_BLOCK, CUM_BLOCK), 1)
    tri = jnp.where(r_i >= c_i, 1.0, 0.0).astype(BF16)
    carry = carry_sc[...]
    cums = []
    for blk in range(tm // CUM_BLOCK):
        parts = _split3(logf[blk * CUM_BLOCK:(blk + 1) * CUM_BLOCK, :])
        c_blk = _dot(tri, parts[0]) + _dot(tri, parts[1]) + _dot(tri, parts[2]) + carry
        carry = c_blk[CUM_BLOCK - 1:CUM_BLOCK, :]
        cums.append(c_blk)
    carry_sc[...] = carry
    nc = jnp.concatenate(cums, axis=0) * (-LOG2E)
    hi, mid, lo = [part.astype(F32) for part in _split3(nc)]
    packed = hi + pltpu.roll(mid, GROUP_HEADS, 1) + pltpu.roll(lo, 2 * GROUP_HEADS, 1)
    ka_o[0] = (_dot(packed.astype(BF16), place_ref[...]) + kones_ref[...]).astype(BF16)

    nc_t = nc.T[0:GROUP_HEADS, :]
    hi_t = nc_t.astype(BF16).astype(F32)
    rem_t = nc_t - hi_t
    mid_t = rem_t.astype(BF16).astype(F32)
    lo_t = (rem_t - mid_t).astype(BF16).astype(F32)
    r8 = lax.broadcasted_iota(jnp.int32, (8, tm), 0)
    zeros56 = jnp.zeros((HEAD_DIM - 8, tm), F32)
    qa_rows = []
    for hd in range(GROUP_HEADS):
        blk = jnp.where(r8 < 3, 1.0,
                        jnp.where(r8 == 3, -hi_t[hd:hd + 1, :],
                                  jnp.where(r8 == 4, -mid_t[hd:hd + 1, :],
                                            jnp.where(r8 == 5, -lo_t[hd:hd + 1, :], 0.0))))
        qa_rows += [blk, zeros56]
    store_tiles(qa_o, jnp.concatenate(qa_rows, axis=0))


def _inproj(x, sc, sh, wts, consts, tm):
    bsz, s, d = x.shape
    nt = s // ATT_TILE
    grid = (bsz, s // tm)
    wqm, wkm, wvm, wqf, wkf, wvf, wf, bfor, gqm, gkm, gqf, gkf = wts
    cos_t, sin_t, rc, rs1, rs2, jmat, place, kones = consts

    def const(a):
        nd = a.ndim
        return pl.BlockSpec(a.shape, lambda b, i, _n=nd: (0,) * _n,
                            pipeline_mode=pl.Buffered(1))

    mod_spec = pl.BlockSpec((None, 1, d), lambda b, i: (b, 0, 0))
    in_specs = [pl.BlockSpec((1, tm, d), lambda b, i: (b, i, 0)), mod_spec, mod_spec]
    in_specs += [const(a) for a in (wqm, wkm, wvm, wqf, wkf, wvf, wf, bfor,
                                    gqm, gkm, gqf, gkf)]
    in_specs += [pl.BlockSpec((ROPE_HALF, tm), lambda b, i: (0, i)),
                 pl.BlockSpec((ROPE_HALF, tm), lambda b, i: (0, i)),
                 pl.BlockSpec((tm, LANES), lambda b, i: (i, 0)),
                 pl.BlockSpec((tm, LANES), lambda b, i: (i, 0)),
                 pl.BlockSpec((tm, LANES), lambda b, i: (i, 0)),
                 const(jmat), const(place), const(kones)]

    tiles = tm // ATT_TILE
    feat_spec = pl.BlockSpec((1, tiles, GROUP_WIDTH, ATT_TILE), lambda b, i: (b, i, 0, 0))
    tok_spec = pl.BlockSpec((1, tm, GROUP_WIDTH), lambda b, i: (b, i, 0))
    feat_shape = jax.ShapeDtypeStruct((bsz, nt, GROUP_WIDTH, ATT_TILE), BF16)
    tok_shape = jax.ShapeDtypeStruct((bsz, s, GROUP_WIDTH), BF16)
    val_spec = pl.BlockSpec((1, tm // KEY_TILE, GROUP_WIDTH, KEY_TILE),
                            lambda b, i: (b, i, 0, 0))
    val_shape = jax.ShapeDtypeStruct((bsz, s // KEY_TILE, GROUP_WIDTH, KEY_TILE), BF16)
    out_specs = [feat_spec, tok_spec, val_spec, feat_spec,
                 feat_spec, tok_spec, val_spec, tok_spec, feat_spec]
    out_shape = [feat_shape, tok_shape, val_shape, feat_shape,
                 feat_shape, tok_shape, val_shape, tok_shape, feat_shape]
    return pl.pallas_call(
        functools.partial(_inproj_kernel, tm=tm),
        grid=grid,
        in_specs=in_specs,
        out_specs=out_specs,
        out_shape=out_shape,
        scratch_shapes=[pltpu.VMEM((GROUP_HEADS * 16, GROUP_WIDTH), F32),
                        pltpu.VMEM((1, LANES), F32),
                        pltpu.VMEM((GROUP_HEADS * 16, tm), F32)],
        compiler_params=pltpu.CompilerParams(
            dimension_semantics=("arbitrary", "arbitrary"),
            vmem_limit_bytes=VMEM_LIMIT),
        name="inproj",
    )(x, sc, sh, wqm, wkm, wvm, wqf, wkf, wvf, wf, bfor, gqm, gkm, gqf, gkf,
      cos_t, sin_t, rc, rs1, rs2, jmat, place, kones)


def _attn_kernel(q_ref, qa_ref, k_ref, ka_ref, v_ref, o_ref, *, t):
    i = pl.program_id(2)
    q = q_ref[0, 0]
    qa = qa_ref[0, 0]
    row = lax.broadcasted_iota(jnp.int32, (PAIR_WIDTH, t), 0)
    rhs = []
    for e in range(2):
        keep = (row >= e * HEAD_DIM) & (row < (e + 1) * HEAD_DIM)
        zero = jnp.zeros_like(q)
        rhs.append(jnp.concatenate([jnp.where(keep, q, zero), jnp.where(keep, qa, zero)],
                                   axis=0))

    def tile_scores(j):
        start = pl.multiple_of(j * t, t)
        lhs = jnp.concatenate([k_ref[0, pl.ds(start, t), :], ka_ref[0, pl.ds(start, t), :]],
                              axis=1)
        return [_dot(lhs, rhs[e]) for e in range(2)]

    def values(j, e):
        rows = slice(e * HEAD_DIM, (e + 1) * HEAD_DIM)
        slabs = t // KEY_TILE
        return jnp.concatenate([v_ref[0, slabs * j + h, rows, :] for h in range(slabs)],
                               axis=1)

    key_i = lax.broadcasted_iota(jnp.int32, (t, t), 0)
    qry_i = lax.broadcasted_iota(jnp.int32, (t, t), 1)
    causal = key_i <= qry_i
    state = []
    for e, s_t in enumerate(tile_scores(i)):
        s_t = jnp.where(causal, s_t, NEG_BIG)
        m = jnp.max(s_t, axis=0, keepdims=True)
        p = jnp.exp2(s_t - m)
        l = jnp.sum(p, axis=0, keepdims=True)
        acc = _dot(values(i, e), p.astype(BF16))
        state += [m, l, acc]

    def body(j, carry):
        out = []
        for e, s_t in enumerate(tile_scores(j)):
            m, l, acc = carry[3 * e:3 * e + 3]
            m_new = jnp.maximum(m, jnp.max(s_t, axis=0, keepdims=True))
            alpha = jnp.exp2(m - m_new)
            p = jnp.exp2(s_t - m_new)
            l = alpha * l + jnp.sum(p, axis=0, keepdims=True)
            acc = alpha * acc + _dot(values(j, e), p.astype(BF16))
            out += [m_new, l, acc]
        return tuple(out)

    state = lax.fori_loop(0, i, body, tuple(state))
    o_t = jnp.concatenate([state[2] / state[1], state[5] / state[4]], axis=0)
    o_ref[0] = o_t.T


def _attn_bounded_kernel(q_ref, qa_ref, k_ref, ka_ref, v_ref, o_ref, p_sc, acc_sc, l_sc, *, t,
                         ka_const):
    i = pl.program_id(1)
    kt = KEY_TILE
    row = lax.broadcasted_iota(jnp.int32, (PAIR_WIDTH, t), 0)
    key_i = lax.broadcasted_iota(jnp.int32, (kt, t), 0)
    qry_i = lax.broadcasted_iota(jnp.int32, (kt, t), 1)
    assert t == 2 * kt


    def rhs_of(pr):
        feat = slice(pr * PAIR_WIDTH, (pr + 1) * PAIR_WIDTH)
        q = q_ref[0, 0, feat, :]
        qa = qa_ref[0, 0, feat, :]
        out = []
        for e in range(2):
            keep = (row >= e * HEAD_DIM) & (row < (e + 1) * HEAD_DIM)
            zero = jnp.zeros_like(q)
            out.append(jnp.concatenate([jnp.where(keep, q, zero), jnp.where(keep, qa, zero)],
                                       axis=0))
        return out

    for grp in range(N_PAIRS // PAIRS_PER_TRIP):
        prs = list(range(grp * PAIRS_PER_TRIP, (grp + 1) * PAIRS_PER_TRIP))
        rhs = {pr: rhs_of(pr) for pr in prs}

        def probs(pr, start, n_keys, row0, mask=None, q0=0, first=False, rhs=rhs):
            feat = slice(pr * PAIR_WIDTH, (pr + 1) * PAIR_WIDTH)
            k_aug = (ka_ref[0, pl.ds(start, n_keys), :] if ka_const
                     else ka_ref[0, pl.ds(start, n_keys), feat])
            lhs = jnp.concatenate([k_ref[0, pl.ds(start, n_keys), feat], k_aug], axis=1)
            for e in range(2):
                s_t = _dot(lhs, rhs[pr][e][:, q0:])
                if mask is not None:
                    s_t = jnp.where(mask[:, q0:], s_t, NEG_BIG)
                p = jnp.exp2(s_t)
                p_sc[pr, e, row0:row0 + n_keys, q0:] = p.astype(BF16)
                part = p[0:8, :]
                for r in range(8, n_keys, 8):
                    part = part + p[r:r + 8, :]
                if first:
                    l_sc[pr, e] = part
                else:
                    l_sc[pr, e, :, q0:] += part

        def values(pr, j):
            base = min(pr * PAIR_WIDTH, GROUP_WIDTH - VALUE_ROWS)
            rows = slice(base, base + VALUE_ROWS)
            v_rows = jnp.concatenate([v_ref[0, 2 * j, rows, :], v_ref[0, 2 * j + 1, rows, :]],
                                     axis=1)
            for e in range(2):
                upd = _dot(v_rows, p_sc[pr, e])
                r0 = pr * PAIR_WIDTH - base + e * HEAD_DIM
                acc_sc[pr, e] += upd[r0:r0 + HEAD_DIM, :]

        diag = pl.multiple_of(i * t, t)
        for pr in prs:
            probs(pr, diag, kt, 0, key_i <= qry_i, first=True)
            probs(pr, diag + kt, kt, kt, key_i + kt <= qry_i, q0=kt)
            for e in range(2):
                p_sc[pr, e, kt:, :kt] = jnp.zeros((kt, kt), BF16)
                acc_sc[pr, e] = jnp.zeros((HEAD_DIM, t), F32)

        def body(j, j_prev, probs=probs, values=values, prs=prs):
            for pr in prs:
                values(pr, j_prev)
                probs(pr, pl.multiple_of(j * t, t), t, 0)
            return j

        j_prev = lax.fori_loop(0, i, body, i)
        for pr in prs:
            values(pr, j_prev)
        for pr in prs:
            o_t = jnp.concatenate(
                [acc_sc[pr, e] / jnp.sum(l_sc[pr, e], axis=0, keepdims=True)
                 for e in range(2)], axis=0)
            o_ref[0, :, pr * PAIR_WIDTH:(pr + 1) * PAIR_WIDTH] = o_t.T


def _attention_online(q_t, qa, k, ka, v_t, *, ka_const, name):
    bsz, nt, _, t = q_t.shape
    s = k.shape[1]
    grid = (bsz, N_PAIRS, nt)
    if ka_const:
        ka_spec = pl.BlockSpec((1, s, PAIR_WIDTH), lambda b, p, i: (0, 0, 0))
    else:
        ka_spec = pl.BlockSpec((1, s, PAIR_WIDTH), lambda b, p, i: (b, 0, p))
    return pl.pallas_call(
        functools.partial(_attn_kernel, t=t),
        grid=grid,
        in_specs=[pl.BlockSpec((1, 1, PAIR_WIDTH, t), lambda b, p, i: (b, i, p, 0)),
                  pl.BlockSpec((1, 1, PAIR_WIDTH, t), lambda b, p, i: (b, i, p, 0)),
                  pl.BlockSpec((1, s, PAIR_WIDTH), lambda b, p, i: (b, 0, p)),
                  ka_spec,
                  pl.BlockSpec((1, s // KEY_TILE, PAIR_WIDTH, KEY_TILE),
                               lambda b, p, i: (b, 0, p, 0))],
        out_specs=pl.BlockSpec((1, t, PAIR_WIDTH), lambda b, p, i: (b, i, p)),
        out_shape=jax.ShapeDtypeStruct((bsz, s, GROUP_WIDTH), F32),
        compiler_params=pltpu.CompilerParams(
            dimension_semantics=("arbitrary", "arbitrary", "arbitrary"),
            vmem_limit_bytes=VMEM_LIMIT),
        name=name,
    )(q_t, qa, k, ka, v_t)


def _attention_bounded(q_t, qa, k, ka, v_t, *, ka_const, name):
    bsz, nt, _, t = q_t.shape
    s = k.shape[1]
    feat_spec = pl.BlockSpec((1, 1, GROUP_WIDTH, t), lambda b, i: (b, i, 0, 0))
    tok_spec = pl.BlockSpec((1, s, GROUP_WIDTH), lambda b, i: (b, 0, 0))
    ka_spec = pl.BlockSpec((1, s, PAIR_WIDTH), lambda b, i: (0, 0, 0)) if ka_const else tok_spec
    return pl.pallas_call(
        functools.partial(_attn_bounded_kernel, t=t, ka_const=ka_const),
        grid=(bsz, nt),
        in_specs=[feat_spec, feat_spec, tok_spec, ka_spec,
                  pl.BlockSpec((1, s // KEY_TILE, GROUP_WIDTH, KEY_TILE),
                               lambda b, i: (b, 0, 0, 0))],
        out_specs=pl.BlockSpec((1, t, GROUP_WIDTH), lambda b, i: (b, i, 0)),
        out_shape=jax.ShapeDtypeStruct((bsz, s, GROUP_WIDTH), F32),
        scratch_shapes=[pltpu.VMEM((N_PAIRS, 2, t, t), BF16),
                        pltpu.VMEM((N_PAIRS, 2, HEAD_DIM, t), F32),
                        pltpu.VMEM((N_PAIRS, 2, 8, t), F32)],
        compiler_params=pltpu.CompilerParams(
            dimension_semantics=("arbitrary", "arbitrary"),
            vmem_limit_bytes=VMEM_LIMIT),
        name=name,
    )(q_t, qa, k, ka, v_t)


def _attention_dispatch(g_q, g_k, q_t, qa, k, ka, v_t, *, ka_const, name):
    bound = HEAD_DIM * jnp.max(jnp.abs(g_q)) * jnp.max(jnp.abs(g_k))
    return lax.cond(
        bound <= MAX_SAFE_EXPONENT,
        functools.partial(_attention_bounded, ka_const=ka_const, name=name + "_bounded"),
        functools.partial(_attention_online, ka_const=ka_const, name=name + "_online"),
        q_t, qa, k, ka, v_t)


def _post_kernel(x_ref, om_ref, of_ref, ga_ref, scm_ref, shm_ref, gm_ref,
                 gom_ref, gof_ref, wout_ref, w1_ref, w2_ref, o_ref, *, ff_chunk):
    def rms(v):
        return v * lax.rsqrt(jnp.mean(v * v, axis=-1, keepdims=True) + EPS)

    mixed = jnp.concatenate([(rms(om_ref[0]) * gom_ref[...]).astype(BF16),
                             (rms(of_ref[0]) * gof_ref[...]).astype(BF16)], axis=1)
    x1 = x_ref[0] + ga_ref[...] * _dot(mixed, wout_ref[...])
    h = (rms(x1) * (1.0 + scm_ref[...]) + shm_ref[...]).astype(BF16)
    d_ff = w1_ref.shape[1]
    y = None
    for c in range(d_ff // ff_chunk):
        hid = _dot(h, w1_ref[:, c * ff_chunk:(c + 1) * ff_chunk])
        hid = jnp.square(jnp.maximum(hid, 0.0)).astype(BF16)
        part = _dot(hid, w2_ref[c * ff_chunk:(c + 1) * ff_chunk, :])
        y = part if y is None else y + part
    o_ref[0] = x1 + gm_ref[...] * y


def _post(x, o_m, o_f, mod4, g_out_m, g_out_f, w_out, w_ff1, w_ff2, tm):
    bsz, s, d = x.shape
    d_ff = w_ff1.shape[1]

    def const(a):
        nd = a.ndim
        return pl.BlockSpec(a.shape, lambda b, i, _n=nd: (0,) * _n,
                            pipeline_mode=pl.Buffered(1))

    def mod_spec(k):
        return pl.BlockSpec((None, None, 1, d), lambda b, i, _k=k: (b, _k, 0, 0))

    tok = lambda w: pl.BlockSpec((1, tm, w), lambda b, i: (b, i, 0))
    return pl.pallas_call(
        functools.partial(_post_kernel, ff_chunk=1024),
        grid=(bsz, s // tm),
        in_specs=[tok(d), tok(GROUP_WIDTH), tok(GROUP_WIDTH),
                  mod_spec(2), mod_spec(4), mod_spec(3), mod_spec(5),
                  const(g_out_m), const(g_out_f), const(w_out), const(w_ff1), const(w_ff2)],
        out_specs=tok(d),
        out_shape=jax.ShapeDtypeStruct((bsz, s, d), F32),
        compiler_params=pltpu.CompilerParams(
            dimension_semantics=("arbitrary", "arbitrary"),
            vmem_limit_bytes=VMEM_LIMIT),
        name="post",
    )(x, o_m, o_f, mod4, mod4, mod4, mod4, g_out_m, g_out_f, w_out, w_ff1, w_ff2)


def _rope_tables(s):
    inv_freq = ROPE_THETA ** (-np.arange(0, ROPE_DIM, 2, dtype=np.float64) / ROPE_DIM)
    ang = np.arange(s, dtype=np.float64)[:, None] * inv_freq[None, :]
    cos, sin = np.cos(ang), np.sin(ang)
    rc = np.ones((s, LANES)); rs1 = np.zeros((s, LANES)); rs2 = np.zeros((s, LANES))
    for base in (0, HEAD_DIM):
        rc[:, base:base + ROPE_HALF] = cos
        rc[:, base + ROPE_HALF:base + ROPE_DIM] = cos
        rs2[:, base:base + ROPE_HALF] = -sin
        rs1[:, base + ROPE_HALF:base + ROPE_DIM] = sin
    f = lambda a: jnp.asarray(a, dtype=F32)
    return f(cos.T), f(sin.T), f(rc), f(rs1), f(rs2)


def _static_mats(s):
    idx = np.arange(256)
    jmat = (idx[:, None] // HEAD_DIM == idx[None, :] // HEAD_DIM) / HEAD_DIM
    place = np.zeros((LANES, GROUP_WIDTH))
    for part in range(3):
        for hd in range(GROUP_HEADS):
            place[part * GROUP_HEADS + hd,
                  (hd // 2) * PAIR_WIDTH + (hd % 2) * HEAD_DIM + part] = 1.0
    kones = np.zeros((1, GROUP_WIDTH))
    for hd in range(GROUP_HEADS):
        kones[0, hd * HEAD_DIM + 3:hd * HEAD_DIM + 6] = 1.0
    ka_moba = np.zeros((1, s, PAIR_WIDTH))
    blk = np.arange(s) // MOBA_BLOCK
    for e in range(2):
        ka_moba[0, np.arange(s), e * HEAD_DIM + blk] = 1.0
    b = lambda a: jnp.asarray(a, dtype=BF16)
    return b(jmat), b(place), jnp.asarray(kones, dtype=F32), b(ka_moba)


def kernel(x, c, w_ada, b_ada, w_in, b_forget, g_qn_moba, g_kn_moba, g_qn_fox, g_kn_fox,
           g_out_moba, g_out_fox, w_out, w_ff1, w_ff2):
    bsz, s, d = x.shape
    depth = w_ada.shape[0]
    w = GROUP_WIDTH
    assert s % MOBA_BLOCK == 0 and s // MOBA_BLOCK <= 16
    tm_in = 1024
    tm_post = 512
    cos_t, sin_t, rc, rs1, rs2 = _rope_tables(s)
    jmat, place, kones, ka_moba = _static_mats(s)
    q_scale = HEAD_DIM ** -0.5 * LOG2E

    for l in range(depth):
        mod = _adaln(c, w_ada[l], b_ada[l])
        mod4 = mod.reshape(bsz, 6, 1, d)

        wl = w_in[l]
        wf =jnp.zeros((d, LANES), F32).at[:, :GROUP_HEADS].set(wl[:, 6 * w:]).astype(BF16)
        bfor = jnp.zeros((1, LANES), F32).at[0, :GROUP_HEADS].set(b_forget[l])
        q_gain = lambda g: jnp.broadcast_to((g * q_scale)[:, None], (HEAD_DIM, tm_in))
        k_gain = lambda g: jnp.tile(g, GROUP_HEADS).reshape(1, w)
        wts = _project_weights(wl, w) + (
               wf, bfor,
               q_gain(g_qn_moba[l]), k_gain(g_kn_moba[l]),
               q_gain(g_qn_fox[l]), k_gain(g_kn_fox[l]))
        consts = (cos_t, sin_t, rc, rs1, rs2, jmat, place, kones)

        qm_t, km, vm_t, bias_t, qf_t, kf, vf_t, ka_f, qa_f = _inproj(
            x, mod4[:, 1], mod4[:, 0], wts, consts, tm_in)

        o_m = _attention_dispatch(g_qn_moba[l] * q_scale, g_kn_moba[l],
                                  qm_t, bias_t, km, ka_moba, vm_t, ka_const=True,
                                  name="attn_moba")
        o_f = _attention_dispatch(g_qn_fox[l] * q_scale, g_kn_fox[l],
                                  qf_t, qa_f, kf, ka_f, vf_t, ka_const=False,
                                  name="attn_fox")

        x = _post(x, o_m, o_f, mod4,
                  g_out_moba[l].reshape(1, w), g_out_fox[l].reshape(1, w),
                  w_out[l].astype(BF16), w_ff1[l].astype(BF16), w_ff2[l].astype(BF16),
                  tm_post)
    return x
```

```python
import functools
import math

import numpy as np
import jax
import jax.numpy as jnp
from jax import lax
from jax.experimental import pallas as pl
from jax.experimental.pallas import tpu as pltpu

F32 = jnp.float32
BF16 = jnp.bfloat16

HEAD_DIM = 64
GROUP_HEADS = 8
GROUP_WIDTH = GROUP_HEADS * HEAD_DIM
PAIR_WIDTH = 2 * HEAD_DIM
N_PAIRS = GROUP_HEADS // 2
MOBA_BLOCK = 256
MOBA_TOPK = 3
ROPE_THETA = 500000.0
ROPE_DIM = HEAD_DIM // 4
ROPE_HALF = ROPE_DIM // 2
EPS = 1e-6
LOG2E = math.log2(math.e)
NEG_BIG = -1e30
MAX_SAFE_EXPONENT = 60.0
ATT_TILE = 512
KEY_TILE = 256
CUM_BLOCK = 128
PAIRS_PER_TRIP = 4
VALUE_ROWS = 128
LANES = 128
VMEM_LIMIT = 48 * 1024 * 1024


def _dot(a, b):
    return jnp.dot(a, b, preferred_element_type=F32)


def _split3(x):
    hi = x.astype(BF16)
    rem = x - hi.astype(F32)
    mid = rem.astype(BF16)
    lo = (rem - mid.astype(F32)).astype(BF16)
    return hi, mid, lo


def _dot_nt(a, b):
    return lax.dot_general(a, b, (((1,), (1,)), ((), ())), preferred_element_type=F32)


def _adaln_kernel(ct_ref, w_ref, b_ref, o_ref):
    c_t = ct_ref[...]
    s_t = c_t / (1.0 + jnp.exp(-c_t))
    w = w_ref[...]
    rows = [jnp.sum(s_t[:, b:b + 1] * w, axis=0, keepdims=True) for b in range(c_t.shape[1])]
    o_ref[...] = jnp.concatenate(rows, axis=0) + b_ref[...]


def _adaln(c, w_ada, b_ada):
    bsz, d = c.shape
    n = w_ada.shape[1]
    return pl.pallas_call(
        _adaln_kernel,
        grid=(n // d,),
        in_specs=[pl.BlockSpec((d, bsz), lambda j: (0, 0)),
                  pl.BlockSpec((d, d), lambda j: (0, j)),
                  pl.BlockSpec((1, d), lambda j: (0, j))],
        out_specs=pl.BlockSpec((bsz, d), lambda j: (0, j)),
        out_shape=jax.ShapeDtypeStruct((bsz, n), F32),
        compiler_params=pltpu.CompilerParams(dimension_semantics=("arbitrary",)),
        name="adaln",
    )(c.T, w_ada, b_ada.reshape(1, n))


TRANSPOSED_GROUPS = (0, 2, 3, 5)


def _wprep_kernel(w_ref, *o_refs):
    g = pl.program_id(0)
    for k, o_ref in enumerate(o_refs):
        @pl.when(g == k)
        def _(k=k, o_ref=o_ref):
            w = w_ref[...]
            o_ref[...] = (w.T if k in TRANSPOSED_GROUPS else w).astype(BF16)


def _project_weights(w_in, layer, width):
    d = w_in.shape[1]
    shapes = [(width, d) if k in TRANSPOSED_GROUPS else (d, width) for k in range(6)]
    return tuple(pl.pallas_call(
        _wprep_kernel,
        grid=(6,),
        in_specs=[pl.BlockSpec((None, d, width), lambda g: (layer, 0, g))],
        out_specs=[pl.BlockSpec(s, lambda g: (0, 0)) for s in shapes],
        out_shape=[jax.ShapeDtypeStruct(s, BF16) for s in shapes],
        compiler_params=pltpu.CompilerParams(dimension_semantics=("arbitrary",)),
        name="wprep",
    )(w_in))


def _inproj_kernel(x_ref, sc_ref, sh_ref,
                   wqm_ref, wkm_ref, wvm_ref, wqf_ref, wkf_ref, wvf_ref, wf_ref, bf_ref,
                   gqm_ref, gkm_ref, gqf_ref, gkf_ref,
                   cos_t_ref, sin_t_ref, rc_ref, rs1_ref, rs2_ref,
                   j_ref, place_ref, kones_ref,
                   qm_o, km_o, vm_o, bias_o, qf_o, kf_o, vf_o, ka_o, qa_o,
                   kmean_sc, carry_sc, gate_sc, *, tm):
    i = pl.program_id(1)
    nb = tm // MOBA_BLOCK

    @pl.when(i == 0)
    def _():
        kmean_sc[...] = jnp.zeros_like(kmean_sc)
        carry_sc[...] = jnp.zeros_like(carry_sc)

    x = x_ref[0]
    ms = jnp.mean(x * x, axis=-1, keepdims=True)
    h = x * lax.rsqrt(ms + EPS) * (1.0 + sc_ref[...]) + sh_ref[...]
    hb = h.astype(BF16)

    def q_feature_major(w_ref, g_ref, rope):
        q_t = _dot_nt(w_ref[...], hb)
        gain = g_ref[...]
        heads = []
        for hd in range(GROUP_HEADS):
            xh = q_t[hd * HEAD_DIM:(hd + 1) * HEAD_DIM, :]
            ss = jnp.sum(xh * xh, axis=0, keepdims=True)
            xh = xh * lax.rsqrt(ss * (1.0 / HEAD_DIM) + EPS) * gain
            if rope:
                x1 = xh[0:ROPE_HALF, :]
                x2 = xh[ROPE_HALF:ROPE_DIM, :]
                cs = cos_t_ref[...]
                sn = sin_t_ref[...]
                xh = jnp.concatenate(
                    [x1 * cs - x2 * sn, x2 * cs + x1 * sn, xh[ROPE_DIM:, :]], axis=0)
            heads.append(xh)
        return jnp.concatenate(heads, axis=0)

    def k_token_major(w_ref, g_ref, rope):
        k = _dot(hb, w_ref[...])
        chunks = []
        for c in range(GROUP_WIDTH // 256):
            seg = k[:, c * 256:(c + 1) * 256]
            msq = _dot((seg * seg).astype(BF16), j_ref[...])
            chunks.append(seg * lax.rsqrt(msq + EPS) * g_ref[:, c * 256:(c + 1) * 256])
        k = jnp.concatenate(chunks, axis=1)
        if rope:
            outs = []
            for c in range(GROUP_WIDTH // LANES):
                seg = k[:, c * LANES:(c + 1) * LANES]
                outs.append(seg * rc_ref[...]
                            + pltpu.roll(seg, ROPE_HALF, 1) * rs1_ref[...]
                            + pltpu.roll(seg, LANES - ROPE_HALF, 1) * rs2_ref[...])
            k = jnp.concatenate(outs, axis=1)
        return k

    def store_tiles(o_ref, val_t, width=ATT_TILE):
        for t in range(tm // width):
            o_ref[0, t] = val_t[:, t * width:(t + 1) * width].astype(o_ref.dtype)

    qm_t = q_feature_major(wqm_ref, gqm_ref, True)
    store_tiles(qm_o, qm_t)
    km = k_token_major(wkm_ref, gkm_ref, True)
    km_o[0] = km.astype(BF16)
    store_tiles(vm_o, _dot_nt(wvm_ref[...], hb), KEY_TILE)

    blk0 = i * nb
    lane_head = lax.broadcasted_iota(jnp.int32, (1, GROUP_WIDTH), 1) // HEAD_DIM
    for sb in range(nb):
        kmean = jnp.mean(km[sb * MOBA_BLOCK:(sb + 1) * MOBA_BLOCK, :], axis=0, keepdims=True)
        for hd in range(GROUP_HEADS):
            row = hd * 16 + blk0 + sb
            kmean_sc[pl.ds(row, 1), :] = jnp.where(lane_head == hd, kmean, 0.0)

    km_hi, km_lo, _ = _split3(kmean_sc[...])
    q_hi, q_lo, _ = _split3(qm_t)
    gate_t = _dot(km_hi, q_hi) + _dot(km_hi, q_lo) + _dot(km_lo, q_hi)
    tok_blk = blk0 + lax.broadcasted_iota(jnp.int32, (16, tm), 1) // MOBA_BLOCK
    n_idx = lax.broadcasted_iota(jnp.int32, (16, tm), 0)
    valid = n_idx < tok_blk
    for hd in range(GROUP_HEADS):
        gate_sc[hd * 16:(hd + 1) * 16, :] = jnp.where(
            valid, gate_t[hd * 16:(hd + 1) * 16, :], -jnp.inf)

    def rank_step(m, cnts):
        tie = jnp.where(n_idx > m, 1, 0)
        out = []
        for hd in range(GROUP_HEADS):
            g = gate_sc[hd * 16:(hd + 1) * 16, :]
            gm = jnp.broadcast_to(gate_sc[pl.ds(hd * 16 + m, 1), :], (16, tm))
            out.append(cnts[hd] + jnp.where(gm > g, 1, jnp.where(gm == g, tie, 0)))
        return tuple(out)

    cnts = lax.fori_loop(0, blk0 + nb - 1, rank_step,
                         tuple(jnp.zeros((16, tm), jnp.int32) for _ in range(GROUP_HEADS)))
    zeros48 = jnp.zeros((HEAD_DIM - 16, tm), F32)
    pair_rows = []
    for hd in range(GROUP_HEADS):
        sel = (valid & (cnts[hd] < MOBA_TOPK)) | (n_idx == tok_blk)
        pair_rows += [jnp.where(sel, 0.0, NEG_BIG), zeros48]
    store_tiles(bias_o, jnp.concatenate(pair_rows, axis=0))

    store_tiles(qf_o, q_feature_major(wqf_ref, gqf_ref, False))
    kf_o[0] = k_token_major(wkf_ref, gkf_ref, False).astype(BF16)
    store_tiles(vf_o, _dot_nt(wvf_ref[...], hb), KEY_TILE)

    f = _dot(hb, wf_ref[...]) + bf_ref[...]
    logf = jnp.minimum(f, 0.0) - jnp.log1p(jnp.exp(-jnp.abs(f)))
    live = lax.broadcasted_iota(jnp.int32, (1, LANES), 1) < GROUP_HEADS
    logf = jnp.where(live, logf, 0.0)
    r_i = lax.broadcasted_iota(jnp.int32, (CUM_BLOCK, CUM_BLOCK), 0)
    c_i = lax.broadcasted_iota(jnp.int32, (CUM_BLOCK, CUM_BLOCK), 1)
    tri = jnp.where(r_i >= c_i, 1.0, 0.0).astype(BF16)
    carry = carry_sc[...]
    cums = []
    for blk in range(tm // CUM_BLOCK):
        parts = _split3(logf[blk * CUM_BLOCK:(blk + 1) * CUM_BLOCK, :])
        c_blk = _dot(tri, parts[0]) + _dot(tri, parts[1]) + _dot(tri, parts[2]) + carry
        carry = c_blk[CUM_BLOCK - 1:CUM_BLOCK, :]
        cums.append(c_blk)
    carry_sc[...] = carry
    nc = jnp.concatenate(cums, axis=0) * (-LOG2E)
    hi, mid, lo = [part.astype(F32) for part in _split3(nc)]
    packed = hi + pltpu.roll(mid, GROUP_HEADS, 1) + pltpu.roll(lo, 2 * GROUP_HEADS, 1)
    ka_o[0] = (_dot(packed.astype(BF16), place_ref[...]) + kones_ref[...]).astype(BF16)

    nc_t = nc.T[0:GROUP_HEADS, :]
    hi_t = nc_t.astype(BF16).astype(F32)
    rem_t = nc_t - hi_t
    mid_t = rem_t.astype(BF16).astype(F32)
    lo_t = (rem_t - mid_t).astype(BF16).astype(F32)
    r8 = lax.broadcasted_iota(jnp.int32, (8, tm), 0)
    zeros56 = jnp.zeros((HEAD_DIM - 8, tm), F32)
    qa_rows = []
    for hd in range(GROUP_HEADS):
        blk = jnp.where(r8 < 3, 1.0,
                        jnp.where(r8 == 3, -hi_t[hd:hd + 1, :],
                                  jnp.where(r8 == 4, -mid_t[hd:hd + 1, :],
                                            jnp.where(r8 == 5, -lo_t[hd:hd + 1, :], 0.0))))
        qa_rows += [blk, zeros56]
    store_tiles(qa_o, jnp.concatenate(qa_rows, axis=0))


def _inproj(x, sc, sh, wts, consts, tm):
    bsz, s, d = x.shape
    nt = s // ATT_TILE
    grid = (bsz, s // tm)
    wqm, wkm, wvm, wqf, wkf, wvf, wf, bfor, gqm, gkm, gqf, gkf = wts
    cos_t, sin_t, rc, rs1, rs2, jmat, place, kones = consts

    def const(a):
        nd = a.ndim
        return pl.BlockSpec(a.shape, lambda b, i, _n=nd: (0,) * _n,
                            pipeline_mode=pl.Buffered(1))

    mod_spec = pl.BlockSpec((None, 1, d), lambda b, i: (b, 0, 0))
    in_specs = [pl.BlockSpec((1, tm, d), lambda b, i: (b, i, 0)), mod_spec, mod_spec]
    in_specs += [const(a) for a in (wqm, wkm, wvm, wqf, wkf, wvf, wf, bfor,
                                    gqm, gkm, gqf, gkf)]
    in_specs += [pl.BlockSpec((ROPE_HALF, tm), lambda b, i: (0, i)),
                 pl.BlockSpec((ROPE_HALF, tm), lambda b, i: (0, i)),
                 pl.BlockSpec((tm, LANES), lambda b, i: (i, 0)),
                 pl.BlockSpec((tm, LANES), lambda b, i: (i, 0)),
                 pl.BlockSpec((tm, LANES), lambda b, i: (i, 0)),
                 const(jmat), const(place), const(kones)]

    tiles = tm // ATT_TILE
    feat_spec = pl.BlockSpec((1, tiles, GROUP_WIDTH, ATT_TILE), lambda b, i: (b, i, 0, 0))
    tok_spec = pl.BlockSpec((1, tm, GROUP_WIDTH), lambda b, i: (b, i, 0))
    feat_shape = jax.ShapeDtypeStruct((bsz, nt, GROUP_WIDTH, ATT_TILE), BF16)
    tok_shape = jax.ShapeDtypeStruct((bsz, s, GROUP_WIDTH), BF16)
    val_spec = pl.BlockSpec((1, tm // KEY_TILE, GROUP_WIDTH, KEY_TILE),
                            lambda b, i: (b, i, 0, 0))
    val_shape = jax.ShapeDtypeStruct((bsz, s // KEY_TILE, GROUP_WIDTH, KEY_TILE), BF16)
    out_specs = [feat_spec, tok_spec, val_spec, feat_spec,
                 feat_spec, tok_spec, val_spec, tok_spec, feat_spec]
    out_shape = [feat_shape, tok_shape, val_shape, feat_shape,
                 feat_shape, tok_shape, val_shape, tok_shape, feat_shape]
    return pl.pallas_call(
        functools.partial(_inproj_kernel, tm=tm),
        grid=grid,
        in_specs=in_specs,
        out_specs=out_specs,
        out_shape=out_shape,
        scratch_shapes=[pltpu.VMEM((GROUP_HEADS * 16, GROUP_WIDTH), F32),
                        pltpu.VMEM((1, LANES), F32),
                        pltpu.VMEM((GROUP_HEADS * 16, tm), F32)],
        compiler_params=pltpu.CompilerParams(
            dimension_semantics=("arbitrary", "arbitrary"),
            vmem_limit_bytes=VMEM_LIMIT),
        name="inproj",
    )(x, sc, sh, wqm, wkm, wvm, wqf, wkf, wvf, wf, bfor, gqm, gkm, gqf, gkf,
      cos_t, sin_t, rc, rs1, rs2, jmat, place, kones)


def _attn_kernel(q_ref, qa_ref, k_ref, ka_ref, v_ref, o_ref, *, t):
    i = pl.program_id(2)
    q = q_ref[0, 0]
    qa = qa_ref[0, 0]
    row = lax.broadcasted_iota(jnp.int32, (PAIR_WIDTH, t), 0)
    rhs = []
    for e in range(2):
        keep = (row >= e * HEAD_DIM) & (row < (e + 1) * HEAD_DIM)
        zero = jnp.zeros_like(q)
        rhs.append(jnp.concatenate([jnp.where(keep, q, zero), jnp.where(keep, qa, zero)],
                                   axis=0))

    def tile_scores(j):
        start = pl.multiple_of(j * t, t)
        lhs = jnp.concatenate([k_ref[0, pl.ds(start, t), :], ka_ref[0, pl.ds(start, t), :]],
                              axis=1)
        return [_dot(lhs, rhs[e]) for e in range(2)]

    def values(j, e):
        rows = slice(e * HEAD_DIM, (e + 1) * HEAD_DIM)
        slabs = t // KEY_TILE
        return jnp.concatenate([v_ref[0, slabs * j + h, rows, :] for h in range(slabs)],
                               axis=1)

    key_i = lax.broadcasted_iota(jnp.int32, (t, t), 0)
    qry_i = lax.broadcasted_iota(jnp.int32, (t, t), 1)
    causal = key_i <= qry_i
    state = []
    for e, s_t in enumerate(tile_scores(i)):
        s_t = jnp.where(causal, s_t, NEG_BIG)
        m = jnp.max(s_t, axis=0, keepdims=True)
        p = jnp.exp2(s_t - m)
        l = jnp.sum(p, axis=0, keepdims=True)
        acc = _dot(values(i, e), p.astype(BF16))
        state += [m, l, acc]

    def body(j, carry):
        out = []
        for e, s_t in enumerate(tile_scores(j)):
            m, l, acc = carry[3 * e:3 * e + 3]
            m_new = jnp.maximum(m, jnp.max(s_t, axis=0, keepdims=True))
            alpha = jnp.exp2(m - m_new)
            p = jnp.exp2(s_t - m_new)
            l = alpha * l + jnp.sum(p, axis=0, keepdims=True)
            acc = alpha * acc + _dot(values(j, e), p.astype(BF16))
            out += [m_new, l, acc]
        return tuple(out)

    state = lax.fori_loop(0, i, body, tuple(state))
    o_t = jnp.concatenate([state[2] / state[1], state[5] / state[4]], axis=0)
    o_ref[0] = o_t.T


def _attn_bounded_kernel(q_ref, qa_ref, k_ref, ka_ref, v_ref, o_ref, p_sc, acc_sc, l_sc, *, t,
                         ka_const):
    i = pl.program_id(1)
    kt = KEY_TILE
    row = lax.broadcasted_iota(jnp.int32, (PAIR_WIDTH, t), 0)
    key_i = lax.broadcasted_iota(jnp.int32, (kt, t), 0)
    qry_i = lax.broadcasted_iota(jnp.int32, (kt, t), 1)
    assert t == 2 * kt


    def rhs_of(pr):
        feat = slice(pr * PAIR_WIDTH, (pr + 1) * PAIR_WIDTH)
        q = q_ref[0, 0, feat, :]
        qa = qa_ref[0, 0, feat, :]
        out = []
        for e in range(2):
            keep = (row >= e * HEAD_DIM) & (row < (e + 1) * HEAD_DIM)
            zero = jnp.zeros_like(q)
            out.append(jnp.concatenate([jnp.where(keep, q, zero), jnp.where(keep, qa, zero)],
                                       axis=0))
        return out

    for grp in range(N_PAIRS // PAIRS_PER_TRIP):
        prs = list(range(grp * PAIRS_PER_TRIP, (grp + 1) * PAIRS_PER_TRIP))
        rhs = {pr: rhs_of(pr) for pr in prs}

        def probs(pr, start, n_keys, row0, mask=None, q0=0, first=False, rhs=rhs):
            feat = slice(pr * PAIR_WIDTH, (pr + 1) * PAIR_WIDTH)
            k_aug = (ka_ref[0, pl.ds(start, n_keys), :] if ka_const
                     else ka_ref[0, pl.ds(start, n_keys), feat])
            lhs = jnp.concatenate([k_ref[0, pl.ds(start, n_keys), feat], k_aug], axis=1)
            for e in range(2):
                s_t = _dot(lhs, rhs[pr][e][:, q0:])
                if mask is not None:
                    s_t = jnp.where(mask[:, q0:], s_t, NEG_BIG)
                p = jnp.exp2(s_t)
                p_sc[pr, e, row0:row0 + n_keys, q0:] = p.astype(BF16)
                part = p[0:8, :]
                for r in range(8, n_keys, 8):
                    part = part + p[r:r + 8, :]
                if first:
                    l_sc[pr, e] = part
                else:
                    l_sc[pr, e, :, q0:] += part

        def values(pr, j):
            base = min(pr * PAIR_WIDTH, GROUP_WIDTH - VALUE_ROWS)
            rows = slice(base, base + VALUE_ROWS)
            v_rows = jnp.concatenate([v_ref[0, 2 * j, rows, :], v_ref[0, 2 * j + 1, rows, :]],
                                     axis=1)
            for e in range(2):
                upd = _dot(v_rows, p_sc[pr, e])
                r0 = pr * PAIR_WIDTH - base + e * HEAD_DIM
                acc_sc[pr, e] += upd[r0:r0 + HEAD_DIM, :]

        diag = pl.multiple_of(i * t, t)
        for pr in prs:
            probs(pr, diag, kt, 0, key_i <= qry_i, first=True)
            probs(pr, diag + kt, kt, kt, key_i + kt <= qry_i, q0=kt)
            for e in range(2):
                p_sc[pr, e, kt:, :kt] = jnp.zeros((kt, kt), BF16)
                acc_sc[pr, e] = jnp.zeros((HEAD_DIM, t), F32)

        def body(j, j_prev, probs=probs, values=values, prs=prs):
            for pr in prs:
                values(pr, j_prev)
                probs(pr, pl.multiple_of(j * t, t), t, 0)
            return j

        j_prev = lax.fori_loop(0, i, body, i)
        for pr in prs:
            values(pr, j_prev)
        for pr in prs:
            o_t = jnp.concatenate(
                [acc_sc[pr, e] / jnp.sum(l_sc[pr, e], axis=0, keepdims=True)
                 for e in range(2)], axis=0)
            o_ref[0, :, pr * PAIR_WIDTH:(pr + 1) * PAIR_WIDTH] = o_t.T


def _attention_online(q_t, qa, k, ka, v_t, *, ka_const, name):
    bsz, nt, _, t = q_t.shape
    s = k.shape[1]
    grid = (bsz, N_PAIRS, nt)
    if ka_const:
        ka_spec = pl.BlockSpec((1, s, PAIR_WIDTH), lambda b, p, i: (0, 0, 0))
    else:
        ka_spec = pl.BlockSpec((1, s, PAIR_WIDTH), lambda b, p, i: (b, 0, p))
    return pl.pallas_call(
        functools.partial(_attn_kernel, t=t),
        grid=grid,
        in_specs=[pl.BlockSpec((1, 1, PAIR_WIDTH, t), lambda b, p, i: (b, i, p, 0)),
                  pl.BlockSpec((1, 1, PAIR_WIDTH, t), lambda b, p, i: (b, i, p, 0)),
                  pl.BlockSpec((1, s, PAIR_WIDTH), lambda b, p, i: (b, 0, p)),
                  ka_spec,
                  pl.BlockSpec((1, s // KEY_TILE, PAIR_WIDTH, KEY_TILE),
                               lambda b, p, i: (b, 0, p, 0))],
        out_specs=pl.BlockSpec((1, t, PAIR_WIDTH), lambda b, p, i: (b, i, p)),
        out_shape=jax.ShapeDtypeStruct((bsz, s, GROUP_WIDTH), F32),
        compiler_params=pltpu.CompilerParams(
            dimension_semantics=("arbitrary", "arbitrary", "arbitrary"),
            vmem_limit_bytes=VMEM_LIMIT),
        name=name,
    )(q_t, qa, k, ka, v_t)


def _attention_bounded(q_t, qa, k, ka, v_t, *, ka_const, name):
    bsz, nt, _, t = q_t.shape
    s = k.shape[1]
    feat_spec = pl.BlockSpec((1, 1, GROUP_WIDTH, t), lambda b, i: (b, i, 0, 0))
    tok_spec = pl.BlockSpec((1, s, GROUP_WIDTH), lambda b, i: (b, 0, 0))
    ka_spec = pl.BlockSpec((1, s, PAIR_WIDTH), lambda b, i: (0, 0, 0)) if ka_const else tok_spec
    return pl.pallas_call(
        functools.partial(_attn_bounded_kernel, t=t, ka_const=ka_const),
        grid=(bsz, nt),
        in_specs=[feat_spec, feat_spec, tok_spec, ka_spec,
                  pl.BlockSpec((1, s // KEY_TILE, GROUP_WIDTH, KEY_TILE),
                               lambda b, i: (b, 0, 0, 0))],
        out_specs=pl.BlockSpec((1, t, GROUP_WIDTH), lambda b, i: (b, i, 0)),
        out_shape=jax.ShapeDtypeStruct((bsz, s, GROUP_WIDTH), F32),
        scratch_shapes=[pltpu.VMEM((N_PAIRS, 2, t, t), BF16),
                        pltpu.VMEM((N_PAIRS, 2, HEAD_DIM, t), F32),
                        pltpu.VMEM((N_PAIRS, 2, 8, t), F32)],
        compiler_params=pltpu.CompilerParams(
            dimension_semantics=("arbitrary", "arbitrary"),
            vmem_limit_bytes=VMEM_LIMIT),
        name=name,
    )(q_t, qa, k, ka, v_t)


def _attention_dispatch(g_q, g_k, q_t, qa, k, ka, v_t, *, ka_const, name):
    bound = HEAD_DIM * jnp.max(jnp.abs(g_q)) * jnp.max(jnp.abs(g_k))
    return lax.cond(
        bound <= MAX_SAFE_EXPONENT,
        functools.partial(_attention_bounded, ka_const=ka_const, name=name + "_bounded"),
        functools.partial(_attention_online, ka_const=ka_const, name=name + "_online"),
        q_t, qa, k, ka, v_t)


def _post_kernel(x_ref, om_ref, of_ref, ga_ref, scm_ref, shm_ref, gm_ref,
                 gom_ref, gof_ref, wout_ref, w1_ref, w2_ref, o_ref, *, ff_chunk):
    def rms(v):
        return v * lax.rsqrt(jnp.mean(v * v, axis=-1, keepdims=True) + EPS)

    mixed = jnp.concatenate([(rms(om_ref[0]) * gom_ref[...]).astype(BF16),
                             (rms(of_ref[0]) * gof_ref[...]).astype(BF16)], axis=1)
    x1 = x_ref[0] + ga_ref[...] * _dot(mixed, wout_ref[...])
    h = (rms(x1) * (1.0 + scm_ref[...]) + shm_ref[...]).astype(BF16)
    d_ff = w1_ref.shape[1]
    y = None
    for c in range(d_ff // ff_chunk):
        hid = _dot(h, w1_ref[:, c * ff_chunk:(c + 1) * ff_chunk])
        hid = jnp.square(jnp.maximum(hid, 0.0)).astype(BF16)
        part = _dot(hid, w2_ref[c * ff_chunk:(c + 1) * ff_chunk, :])
        y = part if y is None else y + part
    o_ref[0] = x1 + gm_ref[...] * y


def _post(x, o_m, o_f, mod4, g_out_m, g_out_f, w_out, w_ff1, w_ff2, tm):
    bsz, s, d = x.shape
    d_ff = w_ff1.shape[1]

    def const(a):
        nd = a.ndim
        return pl.BlockSpec(a.shape, lambda b, i, _n=nd: (0,) * _n,
                            pipeline_mode=pl.Buffered(1))

    def mod_spec(k):
        return pl.BlockSpec((None, None, 1, d), lambda b, i, _k=k: (b, _k, 0, 0))

    tok = lambda w: pl.BlockSpec((1, tm, w), lambda b, i: (b, i, 0))
    return pl.pallas_call(
        functools.partial(_post_kernel, ff_chunk=1024),
        grid=(bsz, s // tm),
        in_specs=[tok(d), tok(GROUP_WIDTH), tok(GROUP_WIDTH),
                  mod_spec(2), mod_spec(4), mod_spec(3), mod_spec(5),
                  const(g_out_m), const(g_out_f), const(w_out), const(w_ff1), const(w_ff2)],
        out_specs=tok(d),
        out_shape=jax.ShapeDtypeStruct((bsz, s, d), F32),
        compiler_params=pltpu.CompilerParams(
            dimension_semantics=("arbitrary", "arbitrary"),
            vmem_limit_bytes=VMEM_LIMIT),
        name="post",
    )(x, o_m, o_f, mod4, mod4, mod4, mod4, g_out_m, g_out_f, w_out, w_ff1, w_ff2)


def _rope_tables(s):
    inv_freq = ROPE_THETA ** (-np.arange(0, ROPE_DIM, 2, dtype=np.float64) / ROPE_DIM)
    ang = np.arange(s, dtype=np.float64)[:, None] * inv_freq[None, :]
    cos, sin = np.cos(ang), np.sin(ang)
    rc = np.ones((s, LANES)); rs1 = np.zeros((s, LANES)); rs2 = np.zeros((s, LANES))
    for base in (0, HEAD_DIM):
        rc[:, base:base + ROPE_HALF] = cos
        rc[:, base + ROPE_HALF:base + ROPE_DIM] = cos
        rs2[:, base:base + ROPE_HALF] = -sin
        rs1[:, base + ROPE_HALF:base + ROPE_DIM] = sin
    f = lambda a: jnp.asarray(a, dtype=F32)
    return f(cos.T), f(sin.T), f(rc), f(rs1), f(rs2)


def _static_mats(s):
    idx = np.arange(256)
    jmat = (idx[:, None] // HEAD_DIM == idx[None, :] // HEAD_DIM) / HEAD_DIM
    place = np.zeros((LANES, GROUP_WIDTH))
    for part in range(3):
        for hd in range(GROUP_HEADS):
            place[part * GROUP_HEADS + hd,
                  (hd // 2) * PAIR_WIDTH + (hd % 2) * HEAD_DIM + part] = 1.0
    kones = np.zeros((1, GROUP_WIDTH))
    for hd in range(GROUP_HEADS):
        kones[0, hd * HEAD_DIM + 3:hd * HEAD_DIM + 6] = 1.0
    ka_moba = np.zeros((1, s, PAIR_WIDTH))
    blk = np.arange(s) // MOBA_BLOCK
    for e in range(2):
        ka_moba[0, np.arange(s), e * HEAD_DIM + blk] = 1.0
    b = lambda a: jnp.asarray(a, dtype=BF16)
    return b(jmat), b(place), jnp.asarray(kones, dtype=F32), b(ka_moba)


def kernel(x, c, w_ada, b_ada, w_in, b_forget, g_qn_moba, g_kn_moba, g_qn_fox, g_kn_fox,
           g_out_moba, g_out_fox, w_out, w_ff1, w_ff2):
    bsz, s, d = x.shape
    depth = w_ada.shape[0]
    w = GROUP_WIDTH
    assert s % MOBA_BLOCK == 0 and s // MOBA_BLOCK <= 16
    tm_in = 1024
    tm_post = 512
    cos_t, sin_t, rc, rs1, rs2 = _rope_tables(s)
    jmat, place, kones, ka_moba = _static_mats(s)
    q_scale = HEAD_DIM ** -0.5 * LOG2E

    for l in range(depth):
        mod = _adaln(c, w_ada[l], b_ada[l])
        mod4 = mod.reshape(bsz, 6, 1, d)

        wl = w_in[l]
        wf =jnp.zeros((d, LANES), F32).at[:, :GROUP_HEADS].set(wl[:, 6 * w:]).astype(BF16)
        bfor = jnp.zeros((1, LANES), F32).at[0, :GROUP_HEADS].set(b_forget[l])
        q_gain = lambda g: jnp.broadcast_to((g * q_scale)[:, None], (HEAD_DIM, tm_in))
        k_gain = lambda g: jnp.tile(g, GROUP_HEADS).reshape(1, w)
        wts = _project_weights(w_in, l, w) + (
               wf, bfor,
               q_gain(g_qn_moba[l]), k_gain(g_kn_moba[l]),
               q_gain(g_qn_fox[l]), k_gain(g_kn_fox[l]))
        consts = (cos_t, sin_t, rc, rs1, rs2, jmat, place, kones)

        qm_t, km, vm_t, bias_t, qf_t, kf, vf_t, ka_f, qa_f = _inproj(
            x, mod4[:, 1], mod4[:, 0], wts, consts, tm_in)

        o_m = _attention_dispatch(g_qn_moba[l] * q_scale, g_kn_moba[l],
                                  qm_t, bias_t, km, ka_moba, vm_t, ka_const=True,
                                  name="attn_moba")
        o_f = _attention_dispatch(g_qn_fox[l] * q_scale, g_kn_fox[l],
                                  qf_t, qa_f, kf, ka_f, vf_t, ka_const=False,
                                  name="attn_fox")

        x = _post(x, o_m, o_f, mod4,
                  g_out_moba[l].reshape(1, w), g_out_fox[l].reshape(1, w),
                  w_out[l].astype(BF16), w_ff1[l].astype(BF16), w_ff2[l].astype(BF16),
                  tm_post)
    return x
```

```python
import functools
import math

import numpy as np
import jax
import jax.numpy as jnp
from jax import lax
from jax.experimental import pallas as pl
from jax.experimental.pallas import tpu as pltpu

F32 = jnp.float32
BF16 = jnp.bfloat16

HEAD_DIM = 64
GROUP_HEADS = 8
GROUP_WIDTH = GROUP_HEADS * HEAD_DIM
PAIR_WIDTH = 2 * HEAD_DIM
N_PAIRS = GROUP_HEADS // 2
MOBA_BLOCK = 256
MOBA_TOPK = 3
ROPE_THETA = 500000.0
ROPE_DIM = HEAD_DIM // 4
ROPE_HALF = ROPE_DIM // 2
EPS = 1e-6
LOG2E = math.log2(math.e)
NEG_BIG = -1e30
MAX_SAFE_EXPONENT = 60.0
ATT_TILE = 512
KEY_TILE = 256
CUM_BLOCK = 128
PAIRS_PER_TRIP = 4
VALUE_ROWS = 128
LANES = 128
VMEM_LIMIT = 48 * 1024 * 1024


def _dot(a, b):
    return jnp.dot(a, b, preferred_element_type=F32)


def _split3(x):
    hi = x.astype(BF16)
    rem = x - hi.astype(F32)
    mid = rem.astype(BF16)
    lo = (rem - mid.astype(F32)).astype(BF16)
    return hi, mid, lo


def _dot_nt(a, b):
    return lax.dot_general(a, b, (((1,), (1,)), ((), ())), preferred_element_type=F32)


def _adaln_kernel(ct_ref, w_ref, b_ref, o_ref):
    c_t = ct_ref[...]
    s_t = c_t / (1.0 + jnp.exp(-c_t))
    w = w_ref[...]
    rows = [jnp.sum(s_t[:, b:b + 1] * w, axis=0, keepdims=True) for b in range(c_t.shape[1])]
    o_ref[...] = jnp.concatenate(rows, axis=0) + b_ref[...]


def _adaln(c, w_ada, b_ada):
    bsz, d = c.shape
    n = w_ada.shape[1]
    return pl.pallas_call(
        _adaln_kernel,
        grid=(n // d,),
        in_specs=[pl.BlockSpec((d, bsz), lambda j: (0, 0)),
                  pl.BlockSpec((d, d), lambda j: (0, j)),
                  pl.BlockSpec((1, d), lambda j: (0, j))],
        out_specs=pl.BlockSpec((bsz, d), lambda j: (0, j)),
        out_shape=jax.ShapeDtypeStruct((bsz, n), F32),
        compiler_params=pltpu.CompilerParams(dimension_semantics=("arbitrary",)),
        name="adaln",
    )(c.T, w_ada, b_ada.reshape(1, n))


TRANSPOSED_GROUPS = (0, 2, 3, 5)


def _wprep_kernel(w_ref, *o_refs):
    g = pl.program_id(0)
    for k, o_ref in enumerate(o_refs):
        @pl.when(g == k)
        def _(k=k, o_ref=o_ref):
            w_t = w_ref[...]
            o_ref[...] = (w_t if k in TRANSPOSED_GROUPS else w_t.T).astype(BF16)


def _project_weights(w_in, layer, width):
    d = w_in.shape[1]
    shapes = [(width, d) if k in TRANSPOSED_GROUPS else (d, width) for k in range(6)]
    w_in = jnp.swapaxes(w_in, 1, 2)
    return tuple(pl.pallas_call(
        _wprep_kernel,
        grid=(6,),
        in_specs=[pl.BlockSpec((None, width, d), lambda g: (layer, g, 0))],
        out_specs=[pl.BlockSpec(s, lambda g: (0, 0)) for s in shapes],
        out_shape=[jax.ShapeDtypeStruct(s, BF16) for s in shapes],
        compiler_params=pltpu.CompilerParams(dimension_semantics=("arbitrary",)),
        name="wprep",
    )(w_in))


def _inproj_kernel(x_ref, sc_ref, sh_ref,
                   wqm_ref, wkm_ref, wvm_ref, wqf_ref, wkf_ref, wvf_ref, wf_ref, bf_ref,
                   gqm_ref, gkm_ref, gqf_ref, gkf_ref,
                   cos_t_ref, sin_t_ref, rc_ref, rs1_ref, rs2_ref,
                   j_ref, place_ref, kones_ref,
                   qm_o, km_o, vm_o, bias_o, qf_o, kf_o, vf_o, ka_o, qa_o,
                   kmean_sc, carry_sc, gate_sc, *, tm):
    i = pl.program_id(1)
    nb = tm // MOBA_BLOCK

    @pl.when(i == 0)
    def _():
        kmean_sc[...] = jnp.zeros_like(kmean_sc)
        carry_sc[...] = jnp.zeros_like(carry_sc)

    x = x_ref[0]
    ms = jnp.mean(x * x, axis=-1, keepdims=True)
    h = x * lax.rsqrt(ms + EPS) * (1.0 + sc_ref[...]) + sh_ref[...]
    hb = h.astype(BF16)

    def q_feature_major(w_ref, g_ref, rope):
        q_t = _dot_nt(w_ref[...], hb)
        gain = g_ref[...]
        heads = []
        for hd in range(GROUP_HEADS):
            xh = q_t[hd * HEAD_DIM:(hd + 1) * HEAD_DIM, :]
            ss = jnp.sum(xh * xh, axis=0, keepdims=True)
            xh = xh * lax.rsqrt(ss * (1.0 / HEAD_DIM) + EPS) * gain
            if rope:
                x1 = xh[0:ROPE_HALF, :]
                x2 = xh[ROPE_HALF:ROPE_DIM, :]
                cs = cos_t_ref[...]
                sn = sin_t_ref[...]
                xh = jnp.concatenate(
                    [x1 * cs - x2 * sn, x2 * cs + x1 * sn, xh[ROPE_DIM:, :]], axis=0)
            heads.append(xh)
        return jnp.concatenate(heads, axis=0)

    def k_token_major(w_ref, g_ref, rope):
        k = _dot(hb, w_ref[...])
        chunks = []
        for c in range(GROUP_WIDTH // 256):
            seg = k[:, c * 256:(c + 1) * 256]
            msq = _dot((seg * seg).astype(BF16), j_ref[...])
            chunks.append(seg * lax.rsqrt(msq + EPS) * g_ref[:, c * 256:(c + 1) * 256])
        k = jnp.concatenate(chunks, axis=1)
        if rope:
            outs = []
            for c in range(GROUP_WIDTH // LANES):
                seg = k[:, c * LANES:(c + 1) * LANES]
                outs.append(seg * rc_ref[...]
                            + pltpu.roll(seg, ROPE_HALF, 1) * rs1_ref[...]
                            + pltpu.roll(seg, LANES - ROPE_HALF, 1) * rs2_ref[...])
            k = jnp.concatenate(outs, axis=1)
        return k

    def store_tiles(o_ref, val_t, width=ATT_TILE):
        for t in range(tm // width):
            o_ref[0, t] = val_t[:, t * width:(t + 1) * width].astype(o_ref.dtype)

    qm_t = q_feature_major(wqm_ref, gqm_ref, True)
    store_tiles(qm_o, qm_t)
    km = k_token_major(wkm_ref, gkm_ref, True)
    km_o[0] = km.astype(BF16)
    store_tiles(vm_o, _dot_nt(wvm_ref[...], hb), KEY_TILE)

    blk0 = i * nb
    lane_head = lax.broadcasted_iota(jnp.int32, (1, GROUP_WIDTH), 1) // HEAD_DIM
    for sb in range(nb):
        kmean = jnp.mean(km[sb * MOBA_BLOCK:(sb + 1) * MOBA_BLOCK, :], axis=0, keepdims=True)
        for hd in range(GROUP_HEADS):
            row = hd * 16 + blk0 + sb
            kmean_sc[pl.ds(row, 1), :] = jnp.where(lane_head == hd, kmean, 0.0)

    km_hi, km_lo, _ = _split3(kmean_sc[...])
    q_hi, q_lo, _ = _split3(qm_t)
    gate_t = _dot(km_hi, q_hi) + _dot(km_hi, q_lo) + _dot(km_lo, q_hi)
    tok_blk = blk0 + lax.broadcasted_iota(jnp.int32, (16, tm), 1) // MOBA_BLOCK
    n_idx = lax.broadcasted_iota(jnp.int32, (16, tm), 0)
    valid = n_idx < tok_blk
    for hd in range(GROUP_HEADS):
        gate_sc[hd * 16:(hd + 1) * 16, :] = jnp.where(
            valid, gate_t[hd * 16:(hd + 1) * 16, :], -jnp.inf)

    def rank_step(m, cnts):
        tie = jnp.where(n_idx > m, 1, 0)
        out = []
        for hd in range(GROUP_HEADS):
            g = gate_sc[hd * 16:(hd + 1) * 16, :]
            gm = jnp.broadcast_to(gate_sc[pl.ds(hd * 16 + m, 1), :], (16, tm))
            out.append(cnts[hd] + jnp.where(gm > g, 1, jnp.where(gm == g, tie, 0)))
        return tuple(out)

    cnts = lax.fori_loop(0, blk0 + nb - 1, rank_step,
                         tuple(jnp.zeros((16, tm), jnp.int32) for _ in range(GROUP_HEADS)))
    zeros48 = jnp.zeros((HEAD_DIM - 16, tm), F32)
    pair_rows = []
    for hd in range(GROUP_HEADS):
        sel = (valid & (cnts[hd] < MOBA_TOPK)) | (n_idx == tok_blk)
        pair_rows += [jnp.where(sel, 0.0, NEG_BIG), zeros48]
    store_tiles(bias_o, jnp.concatenate(pair_rows, axis=0))

    store_tiles(qf_o, q_feature_major(wqf_ref, gqf_ref, False))
    kf_o[0] = k_token_major(wkf_ref, gkf_ref, False).astype(BF16)
    store_tiles(vf_o, _dot_nt(wvf_ref[...], hb), KEY_TILE)

    f = _dot(hb, wf_ref[...]) + bf_ref[...]
    logf = jnp.minimum(f, 0.0) - jnp.log1p(jnp.exp(-jnp.abs(f)))
    live = lax.broadcasted_iota(jnp.int32, (1, LANES), 1) < GROUP_HEADS
    logf = jnp.where(live, logf, 0.0)
    r_i = lax.broadcasted_iota(jnp.int32, (CUM_BLOCK, CUM_BLOCK), 0)
    c_i = lax.broadcasted_iota(jnp.int32, (CUM_BLOCK, CUM_BLOCK), 1)
    tri = jnp.where(r_i >= c_i, 1.0, 0.0).astype(BF16)
    carry = carry_sc[...]
    cums = []
    for blk in range(tm // CUM_BLOCK):
        parts = _split3(logf[blk * CUM_BLOCK:(blk + 1) * CUM_BLOCK, :])
        c_blk = _dot(tri, parts[0]) + _dot(tri, parts[1]) + _dot(tri, parts[2]) + carry
        carry = c_blk[CUM_BLOCK - 1:CUM_BLOCK, :]
        cums.append(c_blk)
    carry_sc[...] = carry
    nc = jnp.concatenate(cums, axis=0) * (-LOG2E)
    hi, mid, lo = [part.astype(F32) for part in _split3(nc)]
    packed = hi + pltpu.roll(mid, GROUP_HEADS, 1) + pltpu.roll(lo, 2 * GROUP_HEADS, 1)
    ka_o[0] = (_dot(packed.astype(BF16), place_ref[...]) + kones_ref[...]).astype(BF16)

    nc_t = nc.T[0:GROUP_HEADS, :]
    hi_t = nc_t.astype(BF16).astype(F32)
    rem_t = nc_t - hi_t
    mid_t = rem_t.astype(BF16).astype(F32)
    lo_t = (rem_t - mid_t).astype(BF16).astype(F32)
    r8 = lax.broadcasted_iota(jnp.int32, (8, tm), 0)
    zeros56 = jnp.zeros((HEAD_DIM - 8, tm), F32)
    qa_rows = []
    for hd in range(GROUP_HEADS):
        blk = jnp.where(r8 < 3, 1.0,
                        jnp.where(r8 == 3, -hi_t[hd:hd + 1, :],
                                  jnp.where(r8 == 4, -mid_t[hd:hd + 1, :],
                                            jnp.where(r8 == 5, -lo_t[hd:hd + 1, :], 0.0))))
        qa_rows += [blk, zeros56]
    store_tiles(qa_o, jnp.concatenate(qa_rows, axis=0))


def _inproj(x, sc, sh, wts, consts, tm):
    bsz, s, d = x.shape
    nt = s // ATT_TILE
    grid = (bsz, s // tm)
    wqm, wkm, wvm, wqf, wkf, wvf, wf, bfor, gqm, gkm, gqf, gkf = wts
    cos_t, sin_t, rc, rs1, rs2, jmat, place, kones = consts

    def const(a):
        nd = a.ndim
        return pl.BlockSpec(a.shape, lambda b, i, _n=nd: (0,) * _n,
                            pipeline_mode=pl.Buffered(1))

    mod_spec = pl.BlockSpec((None, 1, d), lambda b, i: (b, 0, 0))
    in_specs = [pl.BlockSpec((1, tm, d), lambda b, i: (b, i, 0)), mod_spec, mod_spec]
    in_specs += [const(a) for a in (wqm, wkm, wvm, wqf, wkf, wvf, wf, bfor,
                                    gqm, gkm, gqf, gkf)]
    in_specs += [pl.BlockSpec((ROPE_HALF, tm), lambda b, i: (0, i)),
                 pl.BlockSpec((ROPE_HALF, tm), lambda b, i: (0, i)),
                 pl.BlockSpec((tm, LANES), lambda b, i: (i, 0)),
                 pl.BlockSpec((tm, LANES), lambda b, i: (i, 0)),
                 pl.BlockSpec((tm, LANES), lambda b, i: (i, 0)),
                 const(jmat), const(place), const(kones)]

    tiles = tm // ATT_TILE
    feat_spec = pl.BlockSpec((1, tiles, GROUP_WIDTH, ATT_TILE), lambda b, i: (b, i, 0, 0))
    tok_spec = pl.BlockSpec((1, tm, GROUP_WIDTH), lambda b, i: (b, i, 0))
    feat_shape = jax.ShapeDtypeStruct((bsz, nt, GROUP_WIDTH, ATT_TILE), BF16)
    tok_shape = jax.ShapeDtypeStruct((bsz, s, GROUP_WIDTH), BF16)
    val_spec = pl.BlockSpec((1, tm // KEY_TILE, GROUP_WIDTH, KEY_TILE),
                            lambda b, i: (b, i, 0, 0))
    val_shape = jax.ShapeDtypeStruct((bsz, s // KEY_TILE, GROUP_WIDTH, KEY_TILE), BF16)
    out_specs = [feat_spec, tok_spec, val_spec, feat_spec,
                 feat_spec, tok_spec, val_spec, tok_spec, feat_spec]
    out_shape = [feat_shape, tok_shape, val_shape, feat_shape,
                 feat_shape, tok_shape, val_shape, tok_shape, feat_shape]
    return pl.pallas_call(
        functools.partial(_inproj_kernel, tm=tm),
        grid=grid,
        in_specs=in_specs,
        out_specs=out_specs,
        out_shape=out_shape,
        scratch_shapes=[pltpu.VMEM((GROUP_HEADS * 16, GROUP_WIDTH), F32),
                        pltpu.VMEM((1, LANES), F32),
                        pltpu.VMEM((GROUP_HEADS * 16, tm), F32)],
        compiler_params=pltpu.CompilerParams(
            dimension_semantics=("arbitrary", "arbitrary"),
            vmem_limit_bytes=VMEM_LIMIT),
        name="inproj",
    )(x, sc, sh, wqm, wkm, wvm, wqf, wkf, wvf, wf, bfor, gqm, gkm, gqf, gkf,
      cos_t, sin_t, rc, rs1, rs2, jmat, place, kones)


def _attn_kernel(q_ref, qa_ref, k_ref, ka_ref, v_ref, o_ref, *, t):
    i = pl.program_id(2)
    q = q_ref[0, 0]
    qa = qa_ref[0, 0]
    row = lax.broadcasted_iota(jnp.int32, (PAIR_WIDTH, t), 0)
    rhs = []
    for e in range(2):
        keep = (row >= e * HEAD_DIM) & (row < (e + 1) * HEAD_DIM)
        zero = jnp.zeros_like(q)
        rhs.append(jnp.concatenate([jnp.where(keep, q, zero), jnp.where(keep, qa, zero)],
                                   axis=0))

    def tile_scores(j):
        start = pl.multiple_of(j * t, t)
        lhs = jnp.concatenate([k_ref[0, pl.ds(start, t), :], ka_ref[0, pl.ds(start, t), :]],
                              axis=1)
        return [_dot(lhs, rhs[e]) for e in range(2)]

    def values(j, e):
        rows = slice(e * HEAD_DIM, (e + 1) * HEAD_DIM)
        slabs = t // KEY_TILE
        return jnp.concatenate([v_ref[0, slabs * j + h, rows, :] for h in range(slabs)],
                               axis=1)

    key_i = lax.broadcasted_iota(jnp.int32, (t, t), 0)
    qry_i = lax.broadcasted_iota(jnp.int32, (t, t), 1)
    causal = key_i <= qry_i
    state = []
    for e, s_t in enumerate(tile_scores(i)):
        s_t = jnp.where(causal, s_t, NEG_BIG)
        m = jnp.max(s_t, axis=0, keepdims=True)
        p = jnp.exp2(s_t - m)
        l = jnp.sum(p, axis=0, keepdims=True)
        acc = _dot(values(i, e), p.astype(BF16))
        state += [m, l, acc]

    def body(j, carry):
        out = []
        for e, s_t in enumerate(tile_scores(j)):
            m, l, acc = carry[3 * e:3 * e + 3]
            m_new = jnp.maximum(m, jnp.max(s_t, axis=0, keepdims=True))
            alpha = jnp.exp2(m - m_new)
            p = jnp.exp2(s_t - m_new)
            l = alpha * l + jnp.sum(p, axis=0, keepdims=True)
            acc = alpha * acc + _dot(values(j, e), p.astype(BF16))
            out += [m_new, l, acc]
        return tuple(out)

    state = lax.fori_loop(0, i, body, tuple(state))
    o_t = jnp.concatenate([state[2] / state[1], state[5] / state[4]], axis=0)
    o_ref[0] = o_t.T


def _attn_bounded_kernel(q_ref, qa_ref, k_ref, ka_ref, v_ref, o_ref, p_sc, acc_sc, l_sc, *, t,
                         ka_const):
    i = pl.program_id(1)
    kt = KEY_TILE
    row = lax.broadcasted_iota(jnp.int32, (PAIR_WIDTH, t), 0)
    key_i = lax.broadcasted_iota(jnp.int32, (kt, t), 0)
    qry_i = lax.broadcasted_iota(jnp.int32, (kt, t), 1)
    assert t == 2 * kt


    def rhs_of(pr):
        feat = slice(pr * PAIR_WIDTH, (pr + 1) * PAIR_WIDTH)
        q = q_ref[0, 0, feat, :]
        qa = qa_ref[0, 0, feat, :]
        out = []
        for e in range(2):
            keep = (row >= e * HEAD_DIM) & (row < (e + 1) * HEAD_DIM)
            zero = jnp.zeros_like(q)
            out.append(jnp.concatenate([jnp.where(keep, q, zero), jnp.where(keep, qa, zero)],
                                       axis=0))
        return out

    for grp in range(N_PAIRS // PAIRS_PER_TRIP):
        prs = list(range(grp * PAIRS_PER_TRIP, (grp + 1) * PAIRS_PER_TRIP))
        rhs = {pr: rhs_of(pr) for pr in prs}

        def probs(pr, start, n_keys, row0, mask=None, q0=0, first=False, rhs=rhs):
            feat = slice(pr * PAIR_WIDTH, (pr + 1) * PAIR_WIDTH)
            k_aug = (ka_ref[0, pl.ds(start, n_keys), :] if ka_const
                     else ka_ref[0, pl.ds(start, n_keys), feat])
            lhs = jnp.concatenate([k_ref[0, pl.ds(start, n_keys), feat], k_aug], axis=1)
            for e in range(2):
                s_t = _dot(lhs, rhs[pr][e][:, q0:])
                if mask is not None:
                    s_t = jnp.where(mask[:, q0:], s_t, NEG_BIG)
                p = jnp.exp2(s_t)
                p_sc[pr, e, row0:row0 + n_keys, q0:] = p.astype(BF16)
                part = p[0:8, :]
                for r in range(8, n_keys, 8):
                    part = part + p[r:r + 8, :]
                if first:
                    l_sc[pr, e] = part
                else:
                    l_sc[pr, e, :, q0:] += part

        def values(pr, j):
            base = min(pr * PAIR_WIDTH, GROUP_WIDTH - VALUE_ROWS)
            rows = slice(base, base + VALUE_ROWS)
            v_rows = jnp.concatenate([v_ref[0, 2 * j, rows, :], v_ref[0, 2 * j + 1, rows, :]],
                                     axis=1)
            for e in range(2):
                upd = _dot(v_rows, p_sc[pr, e])
                r0 = pr * PAIR_WIDTH - base + e * HEAD_DIM
                acc_sc[pr, e] += upd[r0:r0 + HEAD_DIM, :]

        diag = pl.multiple_of(i * t, t)
        for pr in prs:
            probs(pr, diag, kt, 0, key_i <= qry_i, first=True)
            probs(pr, diag + kt, kt, kt, key_i + kt <= qry_i, q0=kt)
            for e in range(2):
                p_sc[pr, e, kt:, :kt] = jnp.zeros((kt, kt), BF16)
                acc_sc[pr, e] = jnp.zeros((HEAD_DIM, t), F32)

        def body(j, j_prev, probs=probs, values=values, prs=prs):
            for pr in prs:
                values(pr, j_prev)
                probs(pr, pl.multiple_of(j * t, t), t, 0)
            return j

        j_prev = lax.fori_loop(0, i, body, i)
        for pr in prs:
            values(pr, j_prev)
        for pr in prs:
            o_t = jnp.concatenate(
                [acc_sc[pr, e] / jnp.sum(l_sc[pr, e], axis=0, keepdims=True)
                 for e in range(2)], axis=0)
            o_ref[0, :, pr * PAIR_WIDTH:(pr + 1) * PAIR_WIDTH] = o_t.T


def _attention_online(q_t, qa, k, ka, v_t, *, ka_const, name):
    bsz, nt, _, t = q_t.shape
    s = k.shape[1]
    grid = (bsz, N_PAIRS, nt)
    if ka_const:
        ka_spec = pl.BlockSpec((1, s, PAIR_WIDTH), lambda b, p, i: (0, 0, 0))
    else:
        ka_spec = pl.BlockSpec((1, s, PAIR_WIDTH), lambda b, p, i: (b, 0, p))
    return pl.pallas_call(
        functools.partial(_attn_kernel, t=t),
        grid=grid,
        in_specs=[pl.BlockSpec((1, 1, PAIR_WIDTH, t), lambda b, p, i: (b, i, p, 0)),
                  pl.BlockSpec((1, 1, PAIR_WIDTH, t), lambda b, p, i: (b, i, p, 0)),
                  pl.BlockSpec((1, s, PAIR_WIDTH), lambda b, p, i: (b, 0, p)),
                  ka_spec,
                  pl.BlockSpec((1, s // KEY_TILE, PAIR_WIDTH, KEY_TILE),
                               lambda b, p, i: (b, 0, p, 0))],
        out_specs=pl.BlockSpec((1, t, PAIR_WIDTH), lambda b, p, i: (b, i, p)),
        out_shape=jax.ShapeDtypeStruct((bsz, s, GROUP_WIDTH), F32),
        compiler_params=pltpu.CompilerParams(
            dimension_semantics=("arbitrary", "arbitrary", "arbitrary"),
            vmem_limit_bytes=VMEM_LIMIT),
        name=name,
    )(q_t, qa, k, ka, v_t)


def _attention_bounded(q_t, qa, k, ka, v_t, *, ka_const, name):
    bsz, nt, _, t = q_t.shape
    s = k.shape[1]
    feat_spec = pl.BlockSpec((1, 1, GROUP_WIDTH, t), lambda b, i: (b, i, 0, 0))
    tok_spec = pl.BlockSpec((1, s, GROUP_WIDTH), lambda b, i: (b, 0, 0))
    ka_spec = pl.BlockSpec((1, s, PAIR_WIDTH), lambda b, i: (0, 0, 0)) if ka_const else tok_spec
    return pl.pallas_call(
        functools.partial(_attn_bounded_kernel, t=t, ka_const=ka_const),
        grid=(bsz, nt),
        in_specs=[feat_spec, feat_spec, tok_spec, ka_spec,
                  pl.BlockSpec((1, s // KEY_TILE, GROUP_WIDTH, KEY_TILE),
                               lambda b, i: (b, 0, 0, 0))],
        out_specs=pl.BlockSpec((1, t, GROUP_WIDTH), lambda b, i: (b, i, 0)),
        out_shape=jax.ShapeDtypeStruct((bsz, s, GROUP_WIDTH), F32),
        scratch_shapes=[pltpu.VMEM((N_PAIRS, 2, t, t), BF16),
                        pltpu.VMEM((N_PAIRS, 2, HEAD_DIM, t), F32),
                        pltpu.VMEM((N_PAIRS, 2, 8, t), F32)],
        compiler_params=pltpu.CompilerParams(
            dimension_semantics=("arbitrary", "arbitrary"),
            vmem_limit_bytes=VMEM_LIMIT),
        name=name,
    )(q_t, qa, k, ka, v_t)


def _attention_dispatch(g_q, g_k, q_t, qa, k, ka, v_t, *, ka_const, name):
    bound = HEAD_DIM * jnp.max(jnp.abs(g_q)) * jnp.max(jnp.abs(g_k))
    return lax.cond(
        bound <= MAX_SAFE_EXPONENT,
        functools.partial(_attention_bounded, ka_const=ka_const, name=name + "_bounded"),
        functools.partial(_attention_online, ka_const=ka_const, name=name + "_online"),
        q_t, qa, k, ka, v_t)


def _post_kernel(x_ref, om_ref, of_ref, ga_ref, scm_ref, shm_ref, gm_ref,
                 gom_ref, gof_ref, wout_ref, w1_ref, w2_ref, o_ref, *, ff_chunk):
    def rms(v):
        return v * lax.rsqrt(jnp.mean(v * v, axis=-1, keepdims=True) + EPS)

    mixed = jnp.concatenate([(rms(om_ref[0]) * gom_ref[...]).astype(BF16),
                             (rms(of_ref[0]) * gof_ref[...]).astype(BF16)], axis=1)
    x1 = x_ref[0] + ga_ref[...] * _dot(mixed, wout_ref[...])
    h = (rms(x1) * (1.0 + scm_ref[...]) + shm_ref[...]).astype(BF16)
    d_ff = w1_ref.shape[1]
    y = None
    for c in range(d_ff // ff_chunk):
        hid = _dot(h, w1_ref[:, c * ff_chunk:(c + 1) * ff_chunk])
        hid = jnp.square(jnp.maximum(hid, 0.0)).astype(BF16)
        part = _dot(hid, w2_ref[c * ff_chunk:(c + 1) * ff_chunk, :])
        y = part if y is None else y + part
    o_ref[0] = x1 + gm_ref[...] * y


def _post(x, o_m, o_f, mod4, g_out_m, g_out_f, w_out, w_ff1, w_ff2, tm):
    bsz, s, d = x.shape
    d_ff = w_ff1.shape[1]

    def const(a):
        nd = a.ndim
        return pl.BlockSpec(a.shape, lambda b, i, _n=nd: (0,) * _n,
                            pipeline_mode=pl.Buffered(1))

    def mod_spec(k):
        return pl.BlockSpec((None, None, 1, d), lambda b, i, _k=k: (b, _k, 0, 0))

    tok = lambda w: pl.BlockSpec((1, tm, w), lambda b, i: (b, i, 0))
    return pl.pallas_call(
        functools.partial(_post_kernel, ff_chunk=1024),
        grid=(bsz, s // tm),
        in_specs=[tok(d), tok(GROUP_WIDTH), tok(GROUP_WIDTH),
                  mod_spec(2), mod_spec(4), mod_spec(3), mod_spec(5),
                  const(g_out_m), const(g_out_f), const(w_out), const(w_ff1), const(w_ff2)],
        out_specs=tok(d),
        out_shape=jax.ShapeDtypeStruct((bsz, s, d), F32),
        compiler_params=pltpu.CompilerParams(
            dimension_semantics=("arbitrary", "arbitrary"),
            vmem_limit_bytes=VMEM_LIMIT),
        name="post",
    )(x, o_m, o_f, mod4, mod4, mod4, mod4, g_out_m, g_out_f, w_out, w_ff1, w_ff2)


def _rope_tables(s):
    inv_freq = ROPE_THETA ** (-np.arange(0, ROPE_DIM, 2, dtype=np.float64) / ROPE_DIM)
    ang = np.arange(s, dtype=np.float64)[:, None] * inv_freq[None, :]
    cos, sin = np.cos(ang), np.sin(ang)
    rc = np.ones((s, LANES)); rs1 = np.zeros((s, LANES)); rs2 = np.zeros((s, LANES))
    for base in (0, HEAD_DIM):
        rc[:, base:base + ROPE_HALF] = cos
        rc[:, base + ROPE_HALF:base + ROPE_DIM] = cos
        rs2[:, base:base + ROPE_HALF] = -sin
        rs1[:, base + ROPE_HALF:base + ROPE_DIM] = sin
    f = lambda a: jnp.asarray(a, dtype=F32)
    return f(cos.T), f(sin.T), f(rc), f(rs1), f(rs2)


def _static_mats(s):
    idx = np.arange(256)
    jmat = (idx[:, None] // HEAD_DIM == idx[None, :] // HEAD_DIM) / HEAD_DIM
    place = np.zeros((LANES, GROUP_WIDTH))
    for part in range(3):
        for hd in range(GROUP_HEADS):
            place[part * GROUP_HEADS + hd,
                  (hd // 2) * PAIR_WIDTH + (hd % 2) * HEAD_DIM + part] = 1.0
    kones = np.zeros((1, GROUP_WIDTH))
    for hd in range(GROUP_HEADS):
        kones[0, hd * HEAD_DIM + 3:hd * HEAD_DIM + 6] = 1.0
    ka_moba = np.zeros((1, s, PAIR_WIDTH))
    blk = np.arange(s) // MOBA_BLOCK
    for e in range(2):
        ka_moba[0, np.arange(s), e * HEAD_DIM + blk] = 1.0
    b = lambda a: jnp.asarray(a, dtype=BF16)
    return b(jmat), b(place), jnp.asarray(kones, dtype=F32), b(ka_moba)


def kernel(x, c, w_ada, b_ada, w_in, b_forget, g_qn_moba, g_kn_moba, g_qn_fox, g_kn_fox,
           g_out_moba, g_out_fox, w_out, w_ff1, w_ff2):
    bsz, s, d = x.shape
    depth = w_ada.shape[0]
    w = GROUP_WIDTH
    assert s % MOBA_BLOCK == 0 and s // MOBA_BLOCK <= 16
    tm_in = 1024
    tm_post = 512
    cos_t, sin_t, rc, rs1, rs2 = _rope_tables(s)
    jmat, place, kones, ka_moba = _static_mats(s)
    q_scale = HEAD_DIM ** -0.5 * LOG2E

    for l in range(depth):
        mod = _adaln(c, w_ada[l], b_ada[l])
        mod4 = mod.reshape(bsz, 6, 1, d)

        wf_t = jnp.swapaxes(w_in, 1, 2)[l, 6 * w:, :]
        wf = jnp.zeros((d, LANES), F32).at[:, :GROUP_HEADS].set(wf_t.T).astype(BF16)
        bfor = jnp.zeros((1, LANES), F32).at[0, :GROUP_HEADS].set(b_forget[l])
        q_gain = lambda g: jnp.broadcast_to((g * q_scale)[:, None], (HEAD_DIM, tm_in))
        k_gain = lambda g: jnp.tile(g, GROUP_HEADS).reshape(1, w)
        wts = _project_weights(w_in, l, w) + (
               wf, bfor,
               q_gain(g_qn_moba[l]), k_gain(g_kn_moba[l]),
               q_gain(g_qn_fox[l]), k_gain(g_kn_fox[l]))
        consts = (cos_t, sin_t, rc, rs1, rs2, jmat, place, kones)

        qm_t, km, vm_t, bias_t, qf_t, kf, vf_t, ka_f, qa_f = _inproj(
            x, mod4[:, 1], mod4[:, 0], wts, consts, tm_in)

        o_m = _attention_dispatch(g_qn_moba[l] * q_scale, g_kn_moba[l],
                                  qm_t, bias_t, km, ka_moba, vm_t, ka_const=True,
                                  name="attn_moba")
        o_f = _attention_dispatch(g_qn_fox[l] * q_scale, g_kn_fox[l],
                                  qf_t, qa_f, kf, ka_f, vf_t, ka_const=False,
                                  name="attn_fox")

        x = _post(x, o_m, o_f, mod4,
                  g_out_moba[l].reshape(1, w), g_out_fox[l].reshape(1, w),
                  w_out[l].astype(BF16), w_ff1[l].astype(BF16), w_ff2[l].astype(BF16),
                  tm_post)
    return x
```

```python
import functools
import math

import numpy as np
import jax
import jax.numpy as jnp
from jax import lax
from jax.experimental import pallas as pl
from jax.experimental.pallas import tpu as pltpu

F32 = jnp.float32
BF16 = jnp.bfloat16

HEAD_DIM = 64
GROUP_HEADS = 8
GROUP_WIDTH = GROUP_HEADS * HEAD_DIM
PAIR_WIDTH = 2 * HEAD_DIM
N_PAIRS = GROUP_HEADS // 2
MOBA_BLOCK = 256
MOBA_TOPK = 3
ROPE_THETA = 500000.0
ROPE_DIM = HEAD_DIM // 4
ROPE_HALF = ROPE_DIM // 2
EPS = 1e-6
LOG2E = math.log2(math.e)
NEG_BIG = -1e30
MAX_SAFE_EXPONENT = 60.0
ATT_TILE = 512
KEY_TILE = 256
CUM_BLOCK = 128
PAIRS_PER_TRIP = 4
VALUE_ROWS = 128
LANES = 128
VMEM_LIMIT = 48 * 1024 * 1024


def _dot(a, b):
    return jnp.dot(a, b, preferred_element_type=F32)


def _split3(x):
    hi = x.astype(BF16)
    rem = x - hi.astype(F32)
    mid = rem.astype(BF16)
    lo = (rem - mid.astype(F32)).astype(BF16)
    return hi, mid, lo


def _dot_nt(a, b):
    return lax.dot_general(a, b, (((1,), (1,)), ((), ())), preferred_element_type=F32)


def _adaln_kernel(ct_ref, w_ref, b_ref, o_ref):
    c_t = ct_ref[...]
    s_t = c_t / (1.0 + jnp.exp(-c_t))
    w = w_ref[...]
    rows = [jnp.sum(s_t[:, b:b + 1] * w, axis=0, keepdims=True) for b in range(c_t.shape[1])]
    o_ref[...] = jnp.concatenate(rows, axis=0) + b_ref[...]


def _adaln(c, w_ada, b_ada):
    bsz, d = c.shape
    n = w_ada.shape[1]
    return pl.pallas_call(
        _adaln_kernel,
        grid=(n // d,),
        in_specs=[pl.BlockSpec((d, bsz), lambda j: (0, 0)),
                  pl.BlockSpec((d, d), lambda j: (0, j)),
                  pl.BlockSpec((1, d), lambda j: (0, j))],
        out_specs=pl.BlockSpec((bsz, d), lambda j: (0, j)),
        out_shape=jax.ShapeDtypeStruct((bsz, n), F32),
        compiler_params=pltpu.CompilerParams(dimension_semantics=("arbitrary",)),
        name="adaln",
    )(c.T, w_ada, b_ada.reshape(1, n))


TRANSPOSED_GROUPS = (0, 2, 3, 5)


def _wprep_kernel(w_ref, *o_refs):
    g = pl.program_id(0)
    for k, o_ref in enumerate(o_refs):
        @pl.when(g == k)
        def _(k=k, o_ref=o_ref):
            w = w_ref[...]
            o_ref[...] = (w.T if k in TRANSPOSED_GROUPS else w).astype(BF16)


def _project_weights(w_in, layer, width):
    d = w_in.shape[1]
    shapes = [(width, d) if k in TRANSPOSED_GROUPS else (d, width) for k in range(6)]
    return tuple(pl.pallas_call(
        _wprep_kernel,
        grid=(6,),
        in_specs=[pl.BlockSpec((None, d, width), lambda g: (layer, 0, g))],
        out_specs=[pl.BlockSpec(s, lambda g: (0, 0)) for s in shapes],
        out_shape=[jax.ShapeDtypeStruct(s, BF16) for s in shapes],
        compiler_params=pltpu.CompilerParams(dimension_semantics=("arbitrary",)),
        name="wprep",
    )(w_in))


def _inproj_kernel(x_ref, sc_ref, sh_ref,
                   wqm_ref, wkm_ref, wvm_ref, wqf_ref, wkf_ref, wvf_ref, wf_ref, bf_ref,
                   gqm_ref, gkm_ref, gqf_ref, gkf_ref,
                   cos_t_ref, sin_t_ref, rc_ref, rs1_ref, rs2_ref,
                   j_ref, place_ref, kones_ref,
                   qm_o, km_o, vm_o, bias_o, qf_o, kf_o, vf_o, ka_o, qa_o,
                   kmean_sc, carry_sc, gate_sc, *, tm):
    i = pl.program_id(1)
    nb = tm // MOBA_BLOCK

    @pl.when(i == 0)
    def _():
        kmean_sc[...] = jnp.zeros_like(kmean_sc)
        carry_sc[...] = jnp.zeros_like(carry_sc)

    x = x_ref[0]
    ms = jnp.mean(x * x, axis=-1, keepdims=True)
    h = x * lax.rsqrt(ms + EPS) * (1.0 + sc_ref[...]) + sh_ref[...]
    hb = h.astype(BF16)

    def q_feature_major(w_ref, g_ref, rope):
        q_t = _dot_nt(w_ref[...], hb)
        gain = g_ref[...]
        heads = []
        for hd in range(GROUP_HEADS):
            xh = q_t[hd * HEAD_DIM:(hd + 1) * HEAD_DIM, :]
            ss = jnp.sum(xh * xh, axis=0, keepdims=True)
            xh = xh * lax.rsqrt(ss * (1.0 / HEAD_DIM) + EPS) * gain
            if rope:
                x1 = xh[0:ROPE_HALF, :]
                x2 = xh[ROPE_HALF:ROPE_DIM, :]
                cs = cos_t_ref[...]
                sn = sin_t_ref[...]
                xh = jnp.concatenate(
                    [x1 * cs - x2 * sn, x2 * cs + x1 * sn, xh[ROPE_DIM:, :]], axis=0)
            heads.append(xh)
        return jnp.concatenate(heads, axis=0)

    def k_token_major(w_ref, g_ref, rope):
        k = _dot(hb, w_ref[...])
        chunks = []
        for c in range(GROUP_WIDTH // 256):
            seg = k[:, c * 256:(c + 1) * 256]
            msq = _dot((seg * seg).astype(BF16), j_ref[...])
            chunks.append(seg * lax.rsqrt(msq + EPS) * g_ref[:, c * 256:(c + 1) * 256])
        k = jnp.concatenate(chunks, axis=1)
        if rope:
            outs = []
            for c in range(GROUP_WIDTH // LANES):
                seg = k[:, c * LANES:(c + 1) * LANES]
                outs.append(seg * rc_ref[...]
                            + pltpu.roll(seg, ROPE_HALF, 1) * rs1_ref[...]
                            + pltpu.roll(seg, LANES - ROPE_HALF, 1) * rs2_ref[...])
            k = jnp.concatenate(outs, axis=1)
        return k

    def store_tiles(o_ref, val_t, width=ATT_TILE):
        for t in range(tm // width):
            o_ref[0, t] = val_t[:, t * width:(t + 1) * width].astype(o_ref.dtype)

    qm_t = q_feature_major(wqm_ref, gqm_ref, True)
    store_tiles(qm_o, qm_t)
    km = k_token_major(wkm_ref, gkm_ref, True)
    km_o[0] = km.astype(BF16)
    store_tiles(vm_o, _dot_nt(wvm_ref[...], hb), KEY_TILE)

    blk0 = i * nb
    lane_head = lax.broadcasted_iota(jnp.int32, (1, GROUP_WIDTH), 1) // HEAD_DIM
    for sb in range(nb):
        kmean = jnp.mean(km[sb * MOBA_BLOCK:(sb + 1) * MOBA_BLOCK, :], axis=0, keepdims=True)
        for hd in range(GROUP_HEADS):
            row = hd * 16 + blk0 + sb
            kmean_sc[pl.ds(row, 1), :] = jnp.where(lane_head == hd, kmean, 0.0)

    km_hi, km_lo, _ = _split3(kmean_sc[...])
    q_hi, q_lo, _ = _split3(qm_t)
    gate_t = _dot(km_hi, q_hi) + _dot(km_hi, q_lo) + _dot(km_lo, q_hi)
    tok_blk = blk0 + lax.broadcasted_iota(jnp.int32, (16, tm), 1) // MOBA_BLOCK
    n_idx = lax.broadcasted_iota(jnp.int32, (16, tm), 0)
    valid = n_idx < tok_blk
    for hd in range(GROUP_HEADS):
        gate_sc[hd * 16:(hd + 1) * 16, :] = jnp.where(
            valid, gate_t[hd * 16:(hd + 1) * 16, :], -jnp.inf)

    def rank_step(m, cnts):
        tie = jnp.where(n_idx > m, 1, 0)
        out = []
        for hd in range(GROUP_HEADS):
            g = gate_sc[hd * 16:(hd + 1) * 16, :]
            gm = jnp.broadcast_to(gate_sc[pl.ds(hd * 16 + m, 1), :], (16, tm))
            out.append(cnts[hd] + jnp.where(gm > g, 1, jnp.where(gm == g, tie, 0)))
        return tuple(out)

    cnts = lax.fori_loop(0, blk0 + nb - 1, rank_step,
                         tuple(jnp.zeros((16, tm), jnp.int32) for _ in range(GROUP_HEADS)))
    zeros48 = jnp.zeros((HEAD_DIM - 16, tm), F32)
    pair_rows = []
    for hd in range(GROUP_HEADS):
        sel = (valid & (cnts[hd] < MOBA_TOPK)) | (n_idx == tok_blk)
        pair_rows += [jnp.where(sel, 0.0, NEG_BIG), zeros48]
    store_tiles(bias_o, jnp.concatenate(pair_rows, axis=0))

    store_tiles(qf_o, q_feature_major(wqf_ref, gqf_ref, False))
    kf_o[0] = k_token_major(wkf_ref, gkf_ref, False).astype(BF16)
    store_tiles(vf_o, _dot_nt(wvf_ref[...], hb), KEY_TILE)

    f = _dot(hb, wf_ref[...]) + bf_ref[...]
    logf = jnp.minimum(f, 0.0) - jnp.log1p(jnp.exp(-jnp.abs(f)))
    live = lax.broadcasted_iota(jnp.int32, (1, LANES), 1) < GROUP_HEADS
    logf = jnp.where(live, logf, 0.0)
    r_i = lax.broadcasted_iota(jnp.int32, (CUM_BLOCK, CUM_BLOCK), 0)
    c_i = lax.broadcasted_iota(jnp.int32, (CUM_BLOCK, CUM_BLOCK), 1)
    tri = jnp.where(r_i >= c_i, 1.0, 0.0).astype(BF16)
    n_blk = tm // CUM_BLOCK
    wide = jnp.concatenate([logf[b * CUM_BLOCK:(b + 1) * CUM_BLOCK, :] for b in range(n_blk)],
                           axis=1)
    parts = _split3(wide)
    prefix = _dot(tri, parts[0]) + _dot(tri, parts[1]) + _dot(tri, parts[2])
    carry = carry_sc[...]
    cums = []
    for blk in range(n_blk):
        c_blk = prefix[:, blk * LANES:(blk + 1) * LANES] + carry
        carry = c_blk[CUM_BLOCK - 1:CUM_BLOCK, :]
        cums.append(c_blk)
    carry_sc[...] = carry
    nc = jnp.concatenate(cums, axis=0) * (-LOG2E)
    hi, mid, lo = [part.astype(F32) for part in _split3(nc)]
    packed = hi + pltpu.roll(mid, GROUP_HEADS, 1) + pltpu.roll(lo, 2 * GROUP_HEADS, 1)
    ka_o[0] = (_dot(packed.astype(BF16), place_ref[...]) + kones_ref[...]).astype(BF16)

    nc_t = nc.T[0:GROUP_HEADS, :]
    hi_t = nc_t.astype(BF16).astype(F32)
    rem_t = nc_t - hi_t
    mid_t = rem_t.astype(BF16).astype(F32)
    lo_t = (rem_t - mid_t).astype(BF16).astype(F32)
    r8 = lax.broadcasted_iota(jnp.int32, (8, tm), 0)
    zeros56 = jnp.zeros((HEAD_DIM - 8, tm), F32)
    qa_rows = []
    for hd in range(GROUP_HEADS):
        blk = jnp.where(r8 < 3, 1.0,
                        jnp.where(r8 == 3, -hi_t[hd:hd + 1, :],
                                  jnp.where(r8 == 4, -mid_t[hd:hd + 1, :],
                                            jnp.where(r8 == 5, -lo_t[hd:hd + 1, :], 0.0))))
        qa_rows += [blk, zeros56]
    store_tiles(qa_o, jnp.concatenate(qa_rows, axis=0))


def _inproj(x, sc, sh, wts, consts, tm):
    bsz, s, d = x.shape
    nt = s // ATT_TILE
    grid = (bsz, s // tm)
    wqm, wkm, wvm, wqf, wkf, wvf, wf, bfor, gqm, gkm, gqf, gkf = wts
    cos_t, sin_t, rc, rs1, rs2, jmat, place, kones = consts

    def const(a):
        nd = a.ndim
        return pl.BlockSpec(a.shape, lambda b, i, _n=nd: (0,) * _n,
                            pipeline_mode=pl.Buffered(1))

    mod_spec = pl.BlockSpec((None, 1, d), lambda b, i: (b, 0, 0))
    in_specs = [pl.BlockSpec((1, tm, d), lambda b, i: (b, i, 0)), mod_spec, mod_spec]
    in_specs += [const(a) for a in (wqm, wkm, wvm, wqf, wkf, wvf, wf, bfor,
                                    gqm, gkm, gqf, gkf)]
    in_specs += [pl.BlockSpec((ROPE_HALF, tm), lambda b, i: (0, i)),
                 pl.BlockSpec((ROPE_HALF, tm), lambda b, i: (0, i)),
                 pl.BlockSpec((tm, LANES), lambda b, i: (i, 0)),
                 pl.BlockSpec((tm, LANES), lambda b, i: (i, 0)),
                 pl.BlockSpec((tm, LANES), lambda b, i: (i, 0)),
                 const(jmat), const(place), const(kones)]

    tiles = tm // ATT_TILE
    feat_spec = pl.BlockSpec((1, tiles, GROUP_WIDTH, ATT_TILE), lambda b, i: (b, i, 0, 0))
    tok_spec = pl.BlockSpec((1, tm, GROUP_WIDTH), lambda b, i: (b, i, 0))
    feat_shape = jax.ShapeDtypeStruct((bsz, nt, GROUP_WIDTH, ATT_TILE), BF16)
    tok_shape = jax.ShapeDtypeStruct((bsz, s, GROUP_WIDTH), BF16)
    val_spec = pl.BlockSpec((1, tm // KEY_TILE, GROUP_WIDTH, KEY_TILE),
                            lambda b, i: (b, i, 0, 0))
    val_shape = jax.ShapeDtypeStruct((bsz, s // KEY_TILE, GROUP_WIDTH, KEY_TILE), BF16)
    out_specs = [feat_spec, tok_spec, val_spec, feat_spec,
                 feat_spec, tok_spec, val_spec, tok_spec, feat_spec]
    out_shape = [feat_shape, tok_shape, val_shape, feat_shape,
                 feat_shape, tok_shape, val_shape, tok_shape, feat_shape]
    return pl.pallas_call(
        functools.partial(_inproj_kernel, tm=tm),
        grid=grid,
        in_specs=in_specs,
        out_specs=out_specs,
        out_shape=out_shape,
        scratch_shapes=[pltpu.VMEM((GROUP_HEADS * 16, GROUP_WIDTH), F32),
                        pltpu.VMEM((1, LANES), F32),
                        pltpu.VMEM((GROUP_HEADS * 16, tm), F32)],
        compiler_params=pltpu.CompilerParams(
            dimension_semantics=("arbitrary", "arbitrary"),
            vmem_limit_bytes=VMEM_LIMIT),
        name="inproj",
    )(x, sc, sh, wqm, wkm, wvm, wqf, wkf, wvf, wf, bfor, gqm, gkm, gqf, gkf,
      cos_t, sin_t, rc, rs1, rs2, jmat, place, kones)


def _attn_kernel(q_ref, qa_ref, k_ref, ka_ref, v_ref, o_ref, *, t):
    i = pl.program_id(2)
    q = q_ref[0, 0]
    qa = qa_ref[0, 0]
    row = lax.broadcasted_iota(jnp.int32, (PAIR_WIDTH, t), 0)
    rhs = []
    for e in range(2):
        keep = (row >= e * HEAD_DIM) & (row < (e + 1) * HEAD_DIM)
        zero = jnp.zeros_like(q)
        rhs.append(jnp.concatenate([jnp.where(keep, q, zero), jnp.where(keep, qa, zero)],
                                   axis=0))

    def tile_scores(j):
        start = pl.multiple_of(j * t, t)
        lhs = jnp.concatenate([k_ref[0, pl.ds(start, t), :], ka_ref[0, pl.ds(start, t), :]],
                              axis=1)
        return [_dot(lhs, rhs[e]) for e in range(2)]

    def values(j, e):
        rows = slice(e * HEAD_DIM, (e + 1) * HEAD_DIM)
        slabs = t // KEY_TILE
        return jnp.concatenate([v_ref[0, slabs * j + h, rows, :] for h in range(slabs)],
                               axis=1)

    key_i = lax.broadcasted_iota(jnp.int32, (t, t), 0)
    qry_i = lax.broadcasted_iota(jnp.int32, (t, t), 1)
    causal = key_i <= qry_i
    state = []
    for e, s_t in enumerate(tile_scores(i)):
        s_t = jnp.where(causal, s_t, NEG_BIG)
        m = jnp.max(s_t, axis=0, keepdims=True)
        p = jnp.exp2(s_t - m)
        l = jnp.sum(p, axis=0, keepdims=True)
        acc = _dot(values(i, e), p.astype(BF16))
        state += [m, l, acc]

    def body(j, carry):
        out = []
        for e, s_t in enumerate(tile_scores(j)):
            m, l, acc = carry[3 * e:3 * e + 3]
            m_new = jnp.maximum(m, jnp.max(s_t, axis=0, keepdims=True))
            alpha = jnp.exp2(m - m_new)
            p = jnp.exp2(s_t - m_new)
            l = alpha * l + jnp.sum(p, axis=0, keepdims=True)
            acc = alpha * acc + _dot(values(j, e), p.astype(BF16))
            out += [m_new, l, acc]
        return tuple(out)

    state = lax.fori_loop(0, i, body, tuple(state))
    o_t = jnp.concatenate([state[2] / state[1], state[5] / state[4]], axis=0)
    o_ref[0] = o_t.T


def _attn_bounded_kernel(q_ref, qa_ref, k_ref, ka_ref, v_ref, o_ref, p_sc, acc_sc, l_sc, *, t,
                         ka_const):
    i = pl.program_id(1)
    kt = KEY_TILE
    row = lax.broadcasted_iota(jnp.int32, (PAIR_WIDTH, t), 0)
    key_i = lax.broadcasted_iota(jnp.int32, (kt, t), 0)
    qry_i = lax.broadcasted_iota(jnp.int32, (kt, t), 1)
    assert t == 2 * kt


    def rhs_of(pr):
        feat = slice(pr * PAIR_WIDTH, (pr + 1) * PAIR_WIDTH)
        q = q_ref[0, 0, feat, :]
        qa = qa_ref[0, 0, feat, :]
        out = []
        for e in range(2):
            keep = (row >= e * HEAD_DIM) & (row < (e + 1) * HEAD_DIM)
            zero = jnp.zeros_like(q)
            out.append(jnp.concatenate([jnp.where(keep, q, zero), jnp.where(keep, qa, zero)],
                                       axis=0))
        return out

    for grp in range(N_PAIRS // PAIRS_PER_TRIP):
        prs = list(range(grp * PAIRS_PER_TRIP, (grp + 1) * PAIRS_PER_TRIP))
        rhs = {pr: rhs_of(pr) for pr in prs}

        def probs(pr, start, n_keys, row0, mask=None, q0=0, first=False, rhs=rhs):
            feat = slice(pr * PAIR_WIDTH, (pr + 1) * PAIR_WIDTH)
            k_aug = (ka_ref[0, pl.ds(start, n_keys), :] if ka_const
                     else ka_ref[0, pl.ds(start, n_keys), feat])
            lhs = jnp.concatenate([k_ref[0, pl.ds(start, n_keys), feat], k_aug], axis=1)
            for e in range(2):
                s_t = _dot(lhs, rhs[pr][e][:, q0:])
                if mask is not None:
                    s_t = jnp.where(mask[:, q0:], s_t, NEG_BIG)
                p = jnp.exp2(s_t)
                p_sc[pr, e, row0:row0 + n_keys, q0:] = p.astype(BF16)
                part = p[0:8, :]
                for r in range(8, n_keys, 8):
                    part = part + p[r:r + 8, :]
                if first:
                    l_sc[pr, e] = part
                else:
                    l_sc[pr, e, :, q0:] += part

        def values(pr, j):
            base = min(pr * PAIR_WIDTH, GROUP_WIDTH - VALUE_ROWS)
            rows = slice(base, base + VALUE_ROWS)
            v_rows = jnp.concatenate([v_ref[0, 2 * j, rows, :], v_ref[0, 2 * j + 1, rows, :]],
                                     axis=1)
            for e in range(2):
                upd = _dot(v_rows, p_sc[pr, e])
                r0 = pr * PAIR_WIDTH - base + e * HEAD_DIM
                acc_sc[pr, e] += upd[r0:r0 + HEAD_DIM, :]

        diag = pl.multiple_of(i * t, t)
        for pr in prs:
            probs(pr, diag, kt, 0, key_i <= qry_i, first=True)
            probs(pr, diag + kt, kt, kt, key_i + kt <= qry_i, q0=kt)
            for e in range(2):
                p_sc[pr, e, kt:, :kt] = jnp.zeros((kt, kt), BF16)
                acc_sc[pr, e] = jnp.zeros((HEAD_DIM, t), F32)

        def trip(pr, j, j_prev, rhs=rhs):
            start = pl.multiple_of(j * t, t)
            feat = slice(pr * PAIR_WIDTH, (pr + 1) * PAIR_WIDTH)
            k_aug = (ka_ref[0, pl.ds(start, t), :] if ka_const
                     else ka_ref[0, pl.ds(start, t), feat])
            lhs = jnp.concatenate([k_ref[0, pl.ds(start, t), feat], k_aug], axis=1)
            for e in range(2):
                rows = slice(pr * PAIR_WIDTH + e * HEAD_DIM, pr * PAIR_WIDTH + (e + 1) * HEAD_DIM)
                for qh in range(t // kt):
                    qs = slice(qh * kt, (qh + 1) * kt)
                    upd = (_dot(v_ref[0, 2 * j_prev, rows, :], p_sc[pr, e, 0:kt, qs])
                           + _dot(v_ref[0, 2 * j_prev + 1, rows, :], p_sc[pr, e, kt:, qs]))
                    acc_sc[pr, e, :, qs] += upd
                    p = jnp.exp2(_dot(lhs, rhs[pr][e][:, qs]))
                    p_sc[pr, e, :, qs] = p.astype(BF16)
                    part = p[0:8, :]
                    for r in range(8, t, 8):
                        part = part + p[r:r + 8, :]
                    l_sc[pr, e, :, qs] += part

        def body(j, j_prev, trip=trip, prs=prs):
            for pr in prs:
                trip(pr, j, j_prev)
            return j

        j_prev = lax.fori_loop(0, i, body, i)
        for pr in prs:
            values(pr, j_prev)
        for pr in prs:
            o_t = jnp.concatenate(
                [acc_sc[pr, e] / jnp.sum(l_sc[pr, e], axis=0, keepdims=True)
                 for e in range(2)], axis=0)
            o_ref[0, :, pr * PAIR_WIDTH:(pr + 1) * PAIR_WIDTH] = o_t.T


def _attention_online(q_t, qa, k, ka, v_t, *, ka_const, name):
    bsz, nt, _, t = q_t.shape
    s = k.shape[1]
    grid = (bsz, N_PAIRS, nt)
    if ka_const:
        ka_spec = pl.BlockSpec((1, s, PAIR_WIDTH), lambda b, p, i: (0, 0, 0))
    else:
        ka_spec = pl.BlockSpec((1, s, PAIR_WIDTH), lambda b, p, i: (b, 0, p))
    return pl.pallas_call(
        functools.partial(_attn_kernel, t=t),
        grid=grid,
        in_specs=[pl.BlockSpec((1, 1, PAIR_WIDTH, t), lambda b, p, i: (b, i, p, 0)),
                  pl.BlockSpec((1, 1, PAIR_WIDTH, t), lambda b, p, i: (b, i, p, 0)),
                  pl.BlockSpec((1, s, PAIR_WIDTH), lambda b, p, i: (b, 0, p)),
                  ka_spec,
                  pl.BlockSpec((1, s // KEY_TILE, PAIR_WIDTH, KEY_TILE),
                               lambda b, p, i: (b, 0, p, 0))],
        out_specs=pl.BlockSpec((1, t, PAIR_WIDTH), lambda b, p, i: (b, i, p)),
        out_shape=jax.ShapeDtypeStruct((bsz, s, GROUP_WIDTH), F32),
        compiler_params=pltpu.CompilerParams(
            dimension_semantics=("arbitrary", "arbitrary", "arbitrary"),
            vmem_limit_bytes=VMEM_LIMIT),
        name=name,
    )(q_t, qa, k, ka, v_t)


def _attention_bounded(q_t, qa, k, ka, v_t, *, ka_const, name):
    bsz, nt, _, t = q_t.shape
    s = k.shape[1]
    feat_spec = pl.BlockSpec((1, 1, GROUP_WIDTH, t), lambda b, i: (b, i, 0, 0))
    tok_spec = pl.BlockSpec((1, s, GROUP_WIDTH), lambda b, i: (b, 0, 0))
    ka_spec = pl.BlockSpec((1, s, PAIR_WIDTH), lambda b, i: (0, 0, 0)) if ka_const else tok_spec
    return pl.pallas_call(
        functools.partial(_attn_bounded_kernel, t=t, ka_const=ka_const),
        grid=(bsz, nt),
        in_specs=[feat_spec, feat_spec, tok_spec, ka_spec,
                  pl.BlockSpec((1, s // KEY_TILE, GROUP_WIDTH, KEY_TILE),
                               lambda b, i: (b, 0, 0, 0))],
        out_specs=pl.BlockSpec((1, t, GROUP_WIDTH), lambda b, i: (b, i, 0)),
        out_shape=jax.ShapeDtypeStruct((bsz, s, GROUP_WIDTH), F32),
        scratch_shapes=[pltpu.VMEM((N_PAIRS, 2, t, t), BF16),
                        pltpu.VMEM((N_PAIRS, 2, HEAD_DIM, t), F32),
                        pltpu.VMEM((N_PAIRS, 2, 8, t), F32)],
        compiler_params=pltpu.CompilerParams(
            dimension_semantics=("arbitrary", "arbitrary"),
            vmem_limit_bytes=VMEM_LIMIT),
        name=name,
    )(q_t, qa, k, ka, v_t)


def _attention_dispatch(g_q, g_k, q_t, qa, k, ka, v_t, *, ka_const, name):
    bound = HEAD_DIM * jnp.max(jnp.abs(g_q)) * jnp.max(jnp.abs(g_k))
    return lax.cond(
        bound <= MAX_SAFE_EXPONENT,
        functools.partial(_attention_bounded, ka_const=ka_const, name=name + "_bounded"),
        functools.partial(_attention_online, ka_const=ka_const, name=name + "_online"),
        q_t, qa, k, ka, v_t)


def _post_kernel(x_ref, om_ref, of_ref, ga_ref, scm_ref, shm_ref, gm_ref,
                 gom_ref, gof_ref, wout_ref, w1_ref, w2_ref, o_ref, *, ff_chunk):
    def rms(v):
        return v * lax.rsqrt(jnp.mean(v * v, axis=-1, keepdims=True) + EPS)

    mixed = jnp.concatenate([(rms(om_ref[0]) * gom_ref[...]).astype(BF16),
                             (rms(of_ref[0]) * gof_ref[...]).astype(BF16)], axis=1)
    x1 = x_ref[0] + ga_ref[...] * _dot(mixed, wout_ref[...])
    h = (rms(x1) * (1.0 + scm_ref[...]) + shm_ref[...]).astype(BF16)
    d_ff = w1_ref.shape[1]
    y = None
    for c in range(d_ff // ff_chunk):
        hid = _dot(h, w1_ref[:, c * ff_chunk:(c + 1) * ff_chunk])
        hid = jnp.square(jnp.maximum(hid, 0.0)).astype(BF16)
        part = _dot(hid, w2_ref[c * ff_chunk:(c + 1) * ff_chunk, :])
        y = part if y is None else y + part
    o_ref[0] = x1 + gm_ref[...] * y


def _post(x, o_m, o_f, mod4, g_out_m, g_out_f, w_out, w_ff1, w_ff2, tm):
    bsz, s, d = x.shape
    d_ff = w_ff1.shape[1]

    def const(a):
        nd = a.ndim
        return pl.BlockSpec(a.shape, lambda b, i, _n=nd: (0,) * _n,
                            pipeline_mode=pl.Buffered(1))

    def mod_spec(k):
        return pl.BlockSpec((None, None, 1, d), lambda b, i, _k=k: (b, _k, 0, 0))

    tok = lambda w: pl.BlockSpec((1, tm, w), lambda b, i: (b, i, 0))
    return pl.pallas_call(
        functools.partial(_post_kernel, ff_chunk=1024),
        grid=(bsz, s // tm),
        in_specs=[tok(d), tok(GROUP_WIDTH), tok(GROUP_WIDTH),
                  mod_spec(2), mod_spec(4), mod_spec(3), mod_spec(5),
                  const(g_out_m), const(g_out_f), const(w_out), const(w_ff1), const(w_ff2)],
        out_specs=tok(d),
        out_shape=jax.ShapeDtypeStruct((bsz, s, d), F32),
        compiler_params=pltpu.CompilerParams(
            dimension_semantics=("arbitrary", "arbitrary"),
            vmem_limit_bytes=VMEM_LIMIT),
        name="post",
    )(x, o_m, o_f, mod4, mod4, mod4, mod4, g_out_m, g_out_f, w_out, w_ff1, w_ff2)


def _rope_tables(s):
    inv_freq = ROPE_THETA ** (-np.arange(0, ROPE_DIM, 2, dtype=np.float64) / ROPE_DIM)
    ang = np.arange(s, dtype=np.float64)[:, None] * inv_freq[None, :]
    cos, sin = np.cos(ang), np.sin(ang)
    rc = np.ones((s, LANES)); rs1 = np.zeros((s, LANES)); rs2 = np.zeros((s, LANES))
    for base in (0, HEAD_DIM):
        rc[:, base:base + ROPE_HALF] = cos
        rc[:, base + ROPE_HALF:base + ROPE_DIM] = cos
        rs2[:, base:base + ROPE_HALF] = -sin
        rs1[:, base + ROPE_HALF:base + ROPE_DIM] = sin
    f = lambda a: jnp.asarray(a, dtype=F32)
    return f(cos.T), f(sin.T), f(rc), f(rs1), f(rs2)


def _static_mats(s):
    idx = np.arange(256)
    jmat = (idx[:, None] // HEAD_DIM == idx[None, :] // HEAD_DIM) / HEAD_DIM
    place = np.zeros((LANES, GROUP_WIDTH))
    for part in range(3):
        for hd in range(GROUP_HEADS):
            place[part * GROUP_HEADS + hd,
                  (hd // 2) * PAIR_WIDTH + (hd % 2) * HEAD_DIM + part] = 1.0
    kones = np.zeros((1, GROUP_WIDTH))
    for hd in range(GROUP_HEADS):
        kones[0, hd * HEAD_DIM + 3:hd * HEAD_DIM + 6] = 1.0
    ka_moba = np.zeros((1, s, PAIR_WIDTH))
    blk = np.arange(s) // MOBA_BLOCK
    for e in range(2):
        ka_moba[0, np.arange(s), e * HEAD_DIM + blk] = 1.0
    b = lambda a: jnp.asarray(a, dtype=BF16)
    return b(jmat), b(place), jnp.asarray(kones, dtype=F32), b(ka_moba)


def kernel(x, c, w_ada, b_ada, w_in, b_forget, g_qn_moba, g_kn_moba, g_qn_fox, g_kn_fox,
           g_out_moba, g_out_fox, w_out, w_ff1, w_ff2):
    bsz, s, d = x.shape
    depth = w_ada.shape[0]
    w = GROUP_WIDTH
    assert s % MOBA_BLOCK == 0 and s // MOBA_BLOCK <= 16
    tm_in = 1024
    tm_post = 512
    cos_t, sin_t, rc, rs1, rs2 = _rope_tables(s)
    jmat, place, kones, ka_moba = _static_mats(s)
    q_scale = HEAD_DIM ** -0.5 * LOG2E

    for l in range(depth):
        mod = _adaln(c, w_ada[l], b_ada[l])
        mod4 = mod.reshape(bsz, 6, 1, d)

        wl = w_in[l]
        wf =jnp.zeros((d, LANES), F32).at[:, :GROUP_HEADS].set(wl[:, 6 * w:]).astype(BF16)
        bfor = jnp.zeros((1, LANES), F32).at[0, :GROUP_HEADS].set(b_forget[l])
        q_gain = lambda g: jnp.broadcast_to((g * q_scale)[:, None], (HEAD_DIM, tm_in))
        k_gain = lambda g: jnp.tile(g, GROUP_HEADS).reshape(1, w)
        wts = _project_weights(w_in, l, w) + (
               wf, bfor,
               q_gain(g_qn_moba[l]), k_gain(g_kn_moba[l]),
               q_gain(g_qn_fox[l]), k_gain(g_kn_fox[l]))
        consts = (cos_t, sin_t, rc, rs1, rs2, jmat, place, kones)

        qm_t, km, vm_t, bias_t, qf_t, kf, vf_t, ka_f, qa_f = _inproj(
            x, mod4[:, 1], mod4[:, 0], wts, consts, tm_in)

        o_m = _attention_dispatch(g_qn_moba[l] * q_scale, g_kn_moba[l],
                                  qm_t, bias_t, km, ka_moba, vm_t, ka_const=True,
                                  name="attn_moba")
        o_f = _attention_dispatch(g_qn_fox[l] * q_scale, g_kn_fox[l],
                                  qf_t, qa_f, kf, ka_f, vf_t, ka_const=False,
                                  name="attn_fox")

        x = _post(x, o_m, o_f, mod4,
                  g_out_moba[l].reshape(1, w), g_out_fox[l].reshape(1, w),
                  w_out[l].astype(BF16), w_ff1[l].astype(BF16), w_ff2[l].astype(BF16),
                  tm_post)
    return x
```

```python
import functools
import math

import numpy as np
import jax
import jax.numpy as jnp
from jax import lax
from jax.experimental import pallas as pl
from jax.experimental.pallas import tpu as pltpu

F32 = jnp.float32
BF16 = jnp.bfloat16

HEAD_DIM = 64
GROUP_HEADS = 8
GROUP_WIDTH = GROUP_HEADS * HEAD_DIM
PAIR_WIDTH = 2 * HEAD_DIM
N_PAIRS = GROUP_HEADS // 2
MOBA_BLOCK = 256
MOBA_TOPK = 3
ROPE_THETA = 500000.0
ROPE_DIM = HEAD_DIM // 4
ROPE_HALF = ROPE_DIM // 2
EPS = 1e-6
LOG2E = math.log2(math.e)
NEG_BIG = -1e30
MAX_SAFE_EXPONENT = 60.0
ATT_TILE = 512
KEY_TILE = 256
CUM_BLOCK = 128
PAIRS_PER_TRIP = 4
LANES = 128
VMEM_LIMIT = 48 * 1024 * 1024
POST_VMEM_LIMIT = 58 * 1024 * 1024


def _dot(a, b):
    return jnp.dot(a, b, preferred_element_type=F32)


def _split3(x):
    hi = x.astype(BF16)
    rem = x - hi.astype(F32)
    mid = rem.astype(BF16)
    lo = (rem - mid.astype(F32)).astype(BF16)
    return hi, mid, lo


def _dot_nt(a, b):
    return lax.dot_general(a, b, (((1,), (1,)), ((), ())), preferred_element_type=F32)


def _adaln_kernel(ct_ref, w_ref, b_ref, o_ref):
    c_t = ct_ref[...]
    s_t = c_t / (1.0 + jnp.exp(-c_t))
    w = w_ref[...]
    rows = [jnp.sum(s_t[:, b:b + 1] * w, axis=0, keepdims=True) for b in range(c_t.shape[1])]
    o_ref[...] = jnp.concatenate(rows, axis=0) + b_ref[...]


def _adaln(c, w_ada, b_ada):
    bsz, d = c.shape
    n = w_ada.shape[1]
    return pl.pallas_call(
        _adaln_kernel,
        grid=(n // d,),
        in_specs=[pl.BlockSpec((d, bsz), lambda j: (0, 0)),
                  pl.BlockSpec((d, d), lambda j: (0, j)),
                  pl.BlockSpec((1, d), lambda j: (0, j))],
        out_specs=pl.BlockSpec((bsz, d), lambda j: (0, j)),
        out_shape=jax.ShapeDtypeStruct((bsz, n), F32),
        compiler_params=pltpu.CompilerParams(dimension_semantics=("arbitrary",)),
        name="adaln",
    )(c.T, w_ada, b_ada.reshape(1, n))


TRANSPOSED_GROUPS = (0, 2, 3, 5)


def _wprep_kernel(w_ref, *o_refs):
    g = pl.program_id(0)
    for k, o_ref in enumerate(o_refs):
        @pl.when(g == k)
        def _(k=k, o_ref=o_ref):
            w = w_ref[...]
            o_ref[...] = (w.T if k in TRANSPOSED_GROUPS else w).astype(BF16)


def _project_weights(w_in, layer, width):
    d = w_in.shape[1]
    shapes = [(width, d) if k in TRANSPOSED_GROUPS else (d, width) for k in range(6)]
    return tuple(pl.pallas_call(
        _wprep_kernel,
        grid=(6,),
        in_specs=[pl.BlockSpec((None, d, width), lambda g: (layer, 0, g))],
        out_specs=[pl.BlockSpec(s, lambda g: (0, 0)) for s in shapes],
        out_shape=[jax.ShapeDtypeStruct(s, BF16) for s in shapes],
        compiler_params=pltpu.CompilerParams(dimension_semantics=("arbitrary",)),
        name="wprep",
    )(w_in))


def _inproj_kernel(x_ref, sc_ref, sh_ref,
                   wqm_ref, wkm_ref, wvm_ref, wqf_ref, wkf_ref, wvf_ref, wf_ref, bf_ref,
                   gqm_ref, gkm_ref, gqf_ref, gkf_ref,
                   cos_t_ref, sin_t_ref, rc_ref, rs1_ref, rs2_ref,
                   j_ref, place_ref, kones_ref,
                   qm_o, km_o, vm_o, bias_o, qf_o, kf_o, vf_o, ka_o, qa_o,
                   kmean_sc, carry_sc, gate_sc, *, tm):
    i = pl.program_id(1)
    nb = tm // MOBA_BLOCK

    @pl.when(i == 0)
    def _():
        kmean_sc[...] = jnp.zeros_like(kmean_sc)
        carry_sc[...] = jnp.zeros_like(carry_sc)

    x = x_ref[0]
    ms = jnp.mean(x * x, axis=-1, keepdims=True)
    h = x * lax.rsqrt(ms + EPS) * (1.0 + sc_ref[...]) + sh_ref[...]
    hb = h.astype(BF16)

    def q_feature_major(w_ref, g_ref, rope):
        q_t = _dot_nt(w_ref[...], hb)
        gain = g_ref[...]
        heads = []
        for hd in range(GROUP_HEADS):
            xh = q_t[hd * HEAD_DIM:(hd + 1) * HEAD_DIM, :]
            ss = jnp.sum(xh * xh, axis=0, keepdims=True)
            xh = xh * lax.rsqrt(ss * (1.0 / HEAD_DIM) + EPS) * gain
            if rope:
                x1 = xh[0:ROPE_HALF, :]
                x2 = xh[ROPE_HALF:ROPE_DIM, :]
                cs = cos_t_ref[...]
                sn = sin_t_ref[...]
                xh = jnp.concatenate(
                    [x1 * cs - x2 * sn, x2 * cs + x1 * sn, xh[ROPE_DIM:, :]], axis=0)
            heads.append(xh)
        return jnp.concatenate(heads, axis=0)

    def k_token_major(w_ref, g_ref, rope):
        k = _dot(hb, w_ref[...])
        chunks = []
        for c in range(GROUP_WIDTH // 256):
            seg = k[:, c * 256:(c + 1) * 256]
            msq = _dot((seg * seg).astype(BF16), j_ref[...])
            chunks.append(seg * lax.rsqrt(msq + EPS) * g_ref[:, c * 256:(c + 1) * 256])
        k = jnp.concatenate(chunks, axis=1)
        if rope:
            outs = []
            for c in range(GROUP_WIDTH // LANES):
                seg = k[:, c * LANES:(c + 1) * LANES]
                outs.append(seg * rc_ref[...]
                            + pltpu.roll(seg, ROPE_HALF, 1) * rs1_ref[...]
                            + pltpu.roll(seg, LANES - ROPE_HALF, 1) * rs2_ref[...])
            k = jnp.concatenate(outs, axis=1)
        return k

    def store_tiles(o_ref, val_t, width=ATT_TILE):
        for t in range(tm // width):
            o_ref[0, t] = val_t[:, t * width:(t + 1) * width].astype(o_ref.dtype)

    qm_t = q_feature_major(wqm_ref, gqm_ref, True)
    store_tiles(qm_o, qm_t)
    km = k_token_major(wkm_ref, gkm_ref, True)
    km_o[0] = km.astype(BF16)
    store_tiles(vm_o, _dot_nt(wvm_ref[...], hb), KEY_TILE)

    blk0 = i * nb
    lane_head = lax.broadcasted_iota(jnp.int32, (1, GROUP_WIDTH), 1) // HEAD_DIM
    for sb in range(nb):
        kmean = jnp.mean(km[sb * MOBA_BLOCK:(sb + 1) * MOBA_BLOCK, :], axis=0, keepdims=True)
        for hd in range(GROUP_HEADS):
            row = hd * 16 + blk0 + sb
            kmean_sc[pl.ds(row, 1), :] = jnp.where(lane_head == hd, kmean, 0.0)

    km_hi, km_lo, _ = _split3(kmean_sc[...])
    q_hi, q_lo, _ = _split3(qm_t)
    gate_t = _dot(km_hi, q_hi) + _dot(km_hi, q_lo) + _dot(km_lo, q_hi)
    tok_blk = blk0 + lax.broadcasted_iota(jnp.int32, (16, tm), 1) // MOBA_BLOCK
    n_idx = lax.broadcasted_iota(jnp.int32, (16, tm), 0)
    valid = n_idx < tok_blk
    for hd in range(GROUP_HEADS):
        gate_sc[hd * 16:(hd + 1) * 16, :] = jnp.where(
            valid, gate_t[hd * 16:(hd + 1) * 16, :], -jnp.inf)

    def rank_step(m, cnts):
        tie = jnp.where(n_idx > m, 1, 0)
        out = []
        for hd in range(GROUP_HEADS):
            g = gate_sc[hd * 16:(hd + 1) * 16, :]
            gm = jnp.broadcast_to(gate_sc[pl.ds(hd * 16 + m, 1), :], (16, tm))
            out.append(cnts[hd] + jnp.where(gm > g, 1, jnp.where(gm == g, tie, 0)))
        return tuple(out)

    cnts = lax.fori_loop(0, blk0 + nb - 1, rank_step,
                         tuple(jnp.zeros((16, tm), jnp.int32) for _ in range(GROUP_HEADS)))
    zeros48 = jnp.zeros((HEAD_DIM - 16, tm), F32)
    pair_rows = []
    for hd in range(GROUP_HEADS):
        sel = (valid & (cnts[hd] < MOBA_TOPK)) | (n_idx == tok_blk)
        pair_rows += [jnp.where(sel, 0.0, NEG_BIG), zeros48]
    store_tiles(bias_o, jnp.concatenate(pair_rows, axis=0))

    store_tiles(qf_o, q_feature_major(wqf_ref, gqf_ref, False))
    kf_o[0] = k_token_major(wkf_ref, gkf_ref, False).astype(BF16)
    store_tiles(vf_o, _dot_nt(wvf_ref[...], hb), KEY_TILE)

    f = _dot(hb, wf_ref[...]) + bf_ref[...]
    logf = jnp.minimum(f, 0.0) - jnp.log1p(jnp.exp(-jnp.abs(f)))
    live = lax.broadcasted_iota(jnp.int32, (1, LANES), 1) < GROUP_HEADS
    logf = jnp.where(live, logf, 0.0)
    r_i = lax.broadcasted_iota(jnp.int32, (CUM_BLOCK, CUM_BLOCK), 0)
    c_i = lax.broadcasted_iota(jnp.int32, (CUM_BLOCK, CUM_BLOCK), 1)
    tri = jnp.where(r_i >= c_i, 1.0, 0.0).astype(BF16)
    n_blk = tm // CUM_BLOCK
    wide = jnp.concatenate([logf[b * CUM_BLOCK:(b + 1) * CUM_BLOCK, :] for b in range(n_blk)],
                           axis=1)
    parts = _split3(wide)
    prefix = _dot(tri, parts[0]) + _dot(tri, parts[1]) + _dot(tri, parts[2])
    carry = carry_sc[...]
    cums = []
    for blk in range(n_blk):
        c_blk = prefix[:, blk * LANES:(blk + 1) * LANES] + carry
        carry = c_blk[CUM_BLOCK - 1:CUM_BLOCK, :]
        cums.append(c_blk)
    carry_sc[...] = carry
    nc = jnp.concatenate(cums, axis=0) * (-LOG2E)
    hi, mid, lo = [part.astype(F32) for part in _split3(nc)]
    packed = hi + pltpu.roll(mid, GROUP_HEADS, 1) + pltpu.roll(lo, 2 * GROUP_HEADS, 1)
    ka_o[0] = (_dot(packed.astype(BF16), place_ref[...]) + kones_ref[...]).astype(BF16)

    nc_t = nc.T[0:GROUP_HEADS, :]
    hi_t = nc_t.astype(BF16).astype(F32)
    rem_t = nc_t - hi_t
    mid_t = rem_t.astype(BF16).astype(F32)
    lo_t = (rem_t - mid_t).astype(BF16).astype(F32)
    r8 = lax.broadcasted_iota(jnp.int32, (8, tm), 0)
    zeros56 = jnp.zeros((HEAD_DIM - 8, tm), F32)
    qa_rows = []
    for hd in range(GROUP_HEADS):
        blk = jnp.where(r8 < 3, 1.0,
                        jnp.where(r8 == 3, -hi_t[hd:hd + 1, :],
                                  jnp.where(r8 == 4, -mid_t[hd:hd + 1, :],
                                            jnp.where(r8 == 5, -lo_t[hd:hd + 1, :], 0.0))))
        qa_rows += [blk, zeros56]
    store_tiles(qa_o, jnp.concatenate(qa_rows, axis=0))


def _inproj(x, sc, sh, wts, consts, tm):
    bsz, s, d = x.shape
    nt = s // ATT_TILE
    grid = (bsz, s // tm)
    wqm, wkm, wvm, wqf, wkf, wvf, wf, bfor, gqm, gkm, gqf, gkf = wts
    cos_t, sin_t, rc, rs1, rs2, jmat, place, kones = consts

    def const(a):
        nd = a.ndim
        return pl.BlockSpec(a.shape, lambda b, i, _n=nd: (0,) * _n,
                            pipeline_mode=pl.Buffered(1))

    mod_spec = pl.BlockSpec((None, 1, d), lambda b, i: (b, 0, 0))
    in_specs = [pl.BlockSpec((1, tm, d), lambda b, i: (b, i, 0)), mod_spec, mod_spec]
    in_specs += [const(a) for a in (wqm, wkm, wvm, wqf, wkf, wvf, wf, bfor,
                                    gqm, gkm, gqf, gkf)]
    in_specs += [pl.BlockSpec((ROPE_HALF, tm), lambda b, i: (0, i)),
                 pl.BlockSpec((ROPE_HALF, tm), lambda b, i: (0, i)),
                 pl.BlockSpec((tm, LANES), lambda b, i: (i, 0)),
                 pl.BlockSpec((tm, LANES), lambda b, i: (i, 0)),
                 pl.BlockSpec((tm, LANES), lambda b, i: (i, 0)),
                 const(jmat), const(place), const(kones)]

    tiles = tm // ATT_TILE
    feat_spec = pl.BlockSpec((1, tiles, GROUP_WIDTH, ATT_TILE), lambda b, i: (b, i, 0, 0))
    tok_spec = pl.BlockSpec((1, tm, GROUP_WIDTH), lambda b, i: (b, i, 0))
    feat_shape = jax.ShapeDtypeStruct((bsz, nt, GROUP_WIDTH, ATT_TILE), BF16)
    tok_shape = jax.ShapeDtypeStruct((bsz, s, GROUP_WIDTH), BF16)
    val_spec = pl.BlockSpec((1, tm // KEY_TILE, GROUP_WIDTH, KEY_TILE),
                            lambda b, i: (b, i, 0, 0))
    val_shape = jax.ShapeDtypeStruct((bsz, s // KEY_TILE, GROUP_WIDTH, KEY_TILE), BF16)
    out_specs = [feat_spec, tok_spec, val_spec, feat_spec,
                 feat_spec, tok_spec, val_spec, tok_spec, feat_spec]
    out_shape = [feat_shape, tok_shape, val_shape, feat_shape,
                 feat_shape, tok_shape, val_shape, tok_shape, feat_shape]
    return pl.pallas_call(
        functools.partial(_inproj_kernel, tm=tm),
        grid=grid,
        in_specs=in_specs,
        out_specs=out_specs,
        out_shape=out_shape,
        scratch_shapes=[pltpu.VMEM((GROUP_HEADS * 16, GROUP_WIDTH), F32),
                        pltpu.VMEM((1, LANES), F32),
                        pltpu.VMEM((GROUP_HEADS * 16, tm), F32)],
        compiler_params=pltpu.CompilerParams(
            dimension_semantics=("arbitrary", "arbitrary"),
            vmem_limit_bytes=VMEM_LIMIT),
        name="inproj",
    )(x, sc, sh, wqm, wkm, wvm, wqf, wkf, wvf, wf, bfor, gqm, gkm, gqf, gkf,
      cos_t, sin_t, rc, rs1, rs2, jmat, place, kones)


def _attn_kernel(q_ref, qa_ref, k_ref, ka_ref, v_ref, o_ref, *, t):
    i = pl.program_id(2)
    q = q_ref[0, 0]
    qa = qa_ref[0, 0]
    row = lax.broadcasted_iota(jnp.int32, (PAIR_WIDTH, t), 0)
    rhs = []
    for e in range(2):
        keep = (row >= e * HEAD_DIM) & (row < (e + 1) * HEAD_DIM)
        zero = jnp.zeros_like(q)
        rhs.append(jnp.concatenate([jnp.where(keep, q, zero), jnp.where(keep, qa, zero)],
                                   axis=0))

    def tile_scores(j):
        start = pl.multiple_of(j * t, t)
        lhs = jnp.concatenate([k_ref[0, pl.ds(start, t), :], ka_ref[0, pl.ds(start, t), :]],
                              axis=1)
        return [_dot(lhs, rhs[e]) for e in range(2)]

    def values(j, e):
        rows = slice(e * HEAD_DIM, (e + 1) * HEAD_DIM)
        slabs = t // KEY_TILE
        return jnp.concatenate([v_ref[0, slabs * j + h, rows, :] for h in range(slabs)],
                               axis=1)

    key_i = lax.broadcasted_iota(jnp.int32, (t, t), 0)
    qry_i = lax.broadcasted_iota(jnp.int32, (t, t), 1)
    causal = key_i <= qry_i
    state = []
    for e, s_t in enumerate(tile_scores(i)):
        s_t = jnp.where(causal, s_t, NEG_BIG)
        m = jnp.max(s_t, axis=0, keepdims=True)
        p = jnp.exp2(s_t - m)
        l = jnp.sum(p, axis=0, keepdims=True)
        acc = _dot(values(i, e), p.astype(BF16))
        state += [m, l, acc]

    def body(j, carry):
        out = []
        for e, s_t in enumerate(tile_scores(j)):
            m, l, acc = carry[3 * e:3 * e + 3]
            m_new = jnp.maximum(m, jnp.max(s_t, axis=0, keepdims=True))
            alpha = jnp.exp2(m - m_new)
            p = jnp.exp2(s_t - m_new)
            l = alpha * l + jnp.sum(p, axis=0, keepdims=True)
            acc = alpha * acc + _dot(values(j, e), p.astype(BF16))
            out += [m_new, l, acc]
        return tuple(out)

    state = lax.fori_loop(0, i, body, tuple(state))
    o_t = jnp.concatenate([state[2] / state[1], state[5] / state[4]], axis=0)
    o_ref[0] = o_t.T


def _attn_bounded_kernel(q_ref, qa_ref, k_ref, ka_ref, v_ref, o_ref, p_sc, acc_sc, l_sc, *, t,
                         ka_const):
    i = pl.program_id(1)
    kt = KEY_TILE
    row = lax.broadcasted_iota(jnp.int32, (PAIR_WIDTH, t), 0)
    key_i = lax.broadcasted_iota(jnp.int32, (kt, t), 0)
    qry_i = lax.broadcasted_iota(jnp.int32, (kt, t), 1)
    assert t == 2 * kt


    def rhs_of(pr):
        feat = slice(pr * PAIR_WIDTH, (pr + 1) * PAIR_WIDTH)
        q = q_ref[0, 0, feat, :]
        qa = qa_ref[0, 0, feat, :]
        out = []
        for e in range(2):
            keep = (row >= e * HEAD_DIM) & (row < (e + 1) * HEAD_DIM)
            zero = jnp.zeros_like(q)
            out.append(jnp.concatenate([jnp.where(keep, q, zero), jnp.where(keep, qa, zero)],
                                       axis=0))
        return out

    for grp in range(N_PAIRS // PAIRS_PER_TRIP):
        prs = list(range(grp * PAIRS_PER_TRIP, (grp + 1) * PAIRS_PER_TRIP))
        rhs = {pr: rhs_of(pr) for pr in prs}

        def probs(pr, start, n_keys, row0, mask=None, q0=0, first=False, rhs=rhs):
            feat = slice(pr * PAIR_WIDTH, (pr + 1) * PAIR_WIDTH)
            k_aug = (ka_ref[0, pl.ds(start, n_keys), :] if ka_const
                     else ka_ref[0, pl.ds(start, n_keys), feat])
            lhs = jnp.concatenate([k_ref[0, pl.ds(start, n_keys), feat], k_aug], axis=1)
            for e in range(2):
                s_t = _dot(lhs, rhs[pr][e][:, q0:])
                if mask is not None:
                    s_t = jnp.where(mask[:, q0:], s_t, NEG_BIG)
                p = jnp.exp2(s_t)
                p_sc[pr, e, row0:row0 + n_keys, q0:] = p.astype(BF16)
                part = p[0:8, :]
                for r in range(8, n_keys, 8):
                    part = part + p[r:r + 8, :]
                if first:
                    l_sc[pr, e] = part
                else:
                    l_sc[pr, e, :, q0:] += part

        def values(pr, j):
            rows = slice(pr * PAIR_WIDTH, (pr + 1) * PAIR_WIDTH)
            v_pair = jnp.concatenate([v_ref[0, 2 * j, rows, :], v_ref[0, 2 * j + 1, rows, :]],
                                     axis=1)
            for e in range(2):
                upd = _dot(v_pair, p_sc[pr, e])
                acc_sc[pr, e] += upd[e * HEAD_DIM:(e + 1) * HEAD_DIM, :]

        diag = pl.multiple_of(i * t, t)
        for pr in prs:
            probs(pr, diag, kt, 0, key_i <= qry_i, first=True)
            probs(pr, diag + kt, kt, kt, key_i + kt <= qry_i, q0=kt)
            for e in range(2):
                p_sc[pr, e, kt:, :kt] = jnp.zeros((kt, kt), BF16)
                acc_sc[pr, e] = jnp.zeros((HEAD_DIM, t), F32)

        def body(j, j_prev, probs=probs, values=values, prs=prs):
            for pr in prs:
                values(pr, j_prev)
                probs(pr, pl.multiple_of(j * t, t), t, 0)
            return j

        j_prev = lax.fori_loop(0, i, body, i)
        for pr in prs:
            values(pr, j_prev)
        for pr in prs:
            o_t = jnp.concatenate(
                [acc_sc[pr, e] / jnp.sum(l_sc[pr, e], axis=0, keepdims=True)
                 for e in range(2)], axis=0)
            o_ref[0, :, pr * PAIR_WIDTH:(pr + 1) * PAIR_WIDTH] = o_t.T


def _attention_online(q_t, qa, k, ka, v_t, *, ka_const, name):
    bsz, nt, _, t = q_t.shape
    s = k.shape[1]
    grid = (bsz, N_PAIRS, nt)
    if ka_const:
        ka_spec = pl.BlockSpec((1, s, PAIR_WIDTH), lambda b, p, i: (0, 0, 0))
    else:
        ka_spec = pl.BlockSpec((1, s, PAIR_WIDTH), lambda b, p, i: (b, 0, p))
    return pl.pallas_call(
        functools.partial(_attn_kernel, t=t),
        grid=grid,
        in_specs=[pl.BlockSpec((1, 1, PAIR_WIDTH, t), lambda b, p, i: (b, i, p, 0)),
                  pl.BlockSpec((1, 1, PAIR_WIDTH, t), lambda b, p, i: (b, i, p, 0)),
                  pl.BlockSpec((1, s, PAIR_WIDTH), lambda b, p, i: (b, 0, p)),
                  ka_spec,
                  pl.BlockSpec((1, s // KEY_TILE, PAIR_WIDTH, KEY_TILE),
                               lambda b, p, i: (b, 0, p, 0))],
        out_specs=pl.BlockSpec((1, t, PAIR_WIDTH), lambda b, p, i: (b, i, p)),
        out_shape=jax.ShapeDtypeStruct((bsz, s, GROUP_WIDTH), F32),
        compiler_params=pltpu.CompilerParams(
            dimension_semantics=("arbitrary", "arbitrary", "arbitrary"),
            vmem_limit_bytes=VMEM_LIMIT),
        name=name,
    )(q_t, qa, k, ka, v_t)


def _attention_bounded(q_t, qa, k, ka, v_t, *, ka_const, name):
    bsz, nt, _, t = q_t.shape
    s = k.shape[1]
    feat_spec = pl.BlockSpec((1, 1, GROUP_WIDTH, t), lambda b, i: (b, i, 0, 0))
    tok_spec = pl.BlockSpec((1, s, GROUP_WIDTH), lambda b, i: (b, 0, 0))
    ka_spec = pl.BlockSpec((1, s, PAIR_WIDTH), lambda b, i: (0, 0, 0)) if ka_const else tok_spec
    return pl.pallas_call(
        functools.partial(_attn_bounded_kernel, t=t, ka_const=ka_const),
        grid=(bsz, nt),
        in_specs=[feat_spec, feat_spec, tok_spec, ka_spec,
                  pl.BlockSpec((1, s // KEY_TILE, GROUP_WIDTH, KEY_TILE),
                               lambda b, i: (b, 0, 0, 0))],
        out_specs=pl.BlockSpec((1, t, GROUP_WIDTH), lambda b, i: (b, i, 0)),
        out_shape=jax.ShapeDtypeStruct((bsz, s, GROUP_WIDTH), F32),
        scratch_shapes=[pltpu.VMEM((N_PAIRS, 2, t, t), BF16),
                        pltpu.VMEM((N_PAIRS, 2, HEAD_DIM, t), F32),
                        pltpu.VMEM((N_PAIRS, 2, 8, t), F32)],
        compiler_params=pltpu.CompilerParams(
            dimension_semantics=("arbitrary", "arbitrary"),
            vmem_limit_bytes=VMEM_LIMIT),
        name=name,
    )(q_t, qa, k, ka, v_t)


def _attention_dispatch(g_q, g_k, q_t, qa, k, ka, v_t, *, ka_const, name):
    bound = HEAD_DIM * jnp.max(jnp.abs(g_q)) * jnp.max(jnp.abs(g_k))
    return lax.cond(
        bound <= MAX_SAFE_EXPONENT,
        functools.partial(_attention_bounded, ka_const=ka_const, name=name + "_bounded"),
        functools.partial(_attention_online, ka_const=ka_const, name=name + "_online"),
        q_t, qa, k, ka, v_t)


def _post_kernel(x_ref, om_ref, of_ref, ga_ref, scm_ref, shm_ref, gm_ref,
                 gom_ref, gof_ref, wout_ref, w1_ref, w2_ref, o_ref, *, ff_chunk):
    def rms(v):
        return v * lax.rsqrt(jnp.mean(v * v, axis=-1, keepdims=True) + EPS)

    mixed = jnp.concatenate([rms(om_ref[0]) * gom_ref[...],
                             rms(of_ref[0]) * gof_ref[...]], axis=1)
    x1 = x_ref[0] + ga_ref[...] * _dot(mixed, wout_ref[...])
    h = rms(x1) * (1.0 + scm_ref[...]) + shm_ref[...]
    d_ff = w1_ref.shape[1]
    y = None
    for c in range(d_ff // ff_chunk):
        hid = _dot(h, w1_ref[:, c * ff_chunk:(c + 1) * ff_chunk])
        hid = jnp.square(jnp.maximum(hid, 0.0))
        part = _dot(hid, w2_ref[c * ff_chunk:(c + 1) * ff_chunk, :])
        y = part if y is None else y + part
    o_ref[0] = x1 + gm_ref[...] * y


def _post(x, o_m, o_f, mod4, g_out_m, g_out_f, w_out, w_ff1, w_ff2, tm):
    bsz, s, d = x.shape
    d_ff = w_ff1.shape[1]

    def const(a):
        nd = a.ndim
        return pl.BlockSpec(a.shape, lambda b, i, _n=nd: (0,) * _n,
                            pipeline_mode=pl.Buffered(1))

    def mod_spec(k):
        return pl.BlockSpec((None, None, 1, d), lambda b, i, _k=k: (b, _k, 0, 0))

    tok = lambda w: pl.BlockSpec((1, tm, w), lambda b, i: (b, i, 0))
    return pl.pallas_call(
        functools.partial(_post_kernel, ff_chunk=1024),
        grid=(bsz, s // tm),
        in_specs=[tok(d), tok(GROUP_WIDTH), tok(GROUP_WIDTH),
                  mod_spec(2), mod_spec(4), mod_spec(3), mod_spec(5),
                  const(g_out_m), const(g_out_f), const(w_out), const(w_ff1), const(w_ff2)],
        out_specs=tok(d),
        out_shape=jax.ShapeDtypeStruct((bsz, s, d), F32),
        compiler_params=pltpu.CompilerParams(
            dimension_semantics=("arbitrary", "arbitrary"),
            vmem_limit_bytes=POST_VMEM_LIMIT),
        name="post",
    )(x, o_m, o_f, mod4, mod4, mod4, mod4, g_out_m, g_out_f, w_out, w_ff1, w_ff2)


def _rope_tables(s):
    inv_freq = ROPE_THETA ** (-np.arange(0, ROPE_DIM, 2, dtype=np.float64) / ROPE_DIM)
    ang = np.arange(s, dtype=np.float64)[:, None] * inv_freq[None, :]
    cos, sin = np.cos(ang), np.sin(ang)
    rc = np.ones((s, LANES)); rs1 = np.zeros((s, LANES)); rs2 = np.zeros((s, LANES))
    for base in (0, HEAD_DIM):
        rc[:, base:base + ROPE_HALF] = cos
        rc[:, base + ROPE_HALF:base + ROPE_DIM] = cos
        rs2[:, base:base + ROPE_HALF] = -sin
        rs1[:, base + ROPE_HALF:base + ROPE_DIM] = sin
    f = lambda a: jnp.asarray(a, dtype=F32)
    return f(cos.T), f(sin.T), f(rc), f(rs1), f(rs2)


def _static_mats(s):
    idx = np.arange(256)
    jmat = (idx[:, None] // HEAD_DIM == idx[None, :] // HEAD_DIM) / HEAD_DIM
    place = np.zeros((LANES, GROUP_WIDTH))
    for part in range(3):
        for hd in range(GROUP_HEADS):
            place[part * GROUP_HEADS + hd,
                  (hd // 2) * PAIR_WIDTH + (hd % 2) * HEAD_DIM + part] = 1.0
    kones = np.zeros((1, GROUP_WIDTH))
    for hd in range(GROUP_HEADS):
        kones[0, hd * HEAD_DIM + 3:hd * HEAD_DIM + 6] = 1.0
    ka_moba = np.zeros((1, s, PAIR_WIDTH))
    blk = np.arange(s) // MOBA_BLOCK
    for e in range(2):
        ka_moba[0, np.arange(s), e * HEAD_DIM + blk] = 1.0
    b = lambda a: jnp.asarray(a, dtype=BF16)
    return b(jmat), b(place), jnp.asarray(kones, dtype=F32), b(ka_moba)


def kernel(x, c, w_ada, b_ada, w_in, b_forget, g_qn_moba, g_kn_moba, g_qn_fox, g_kn_fox,
           g_out_moba, g_out_fox, w_out, w_ff1, w_ff2):
    bsz, s, d = x.shape
    depth = w_ada.shape[0]
    w = GROUP_WIDTH
    assert s % MOBA_BLOCK == 0 and s // MOBA_BLOCK <= 16
    tm_in = 1024
    tm_post = 512
    cos_t, sin_t, rc, rs1, rs2 = _rope_tables(s)
    jmat, place, kones, ka_moba = _static_mats(s)
    q_scale = HEAD_DIM ** -0.5 * LOG2E

    for l in range(depth):
        mod = _adaln(c, w_ada[l], b_ada[l])
        mod4 = mod.reshape(bsz, 6, 1, d)

        wl = w_in[l]
        wf =jnp.zeros((d, LANES), F32).at[:, :GROUP_HEADS].set(wl[:, 6 * w:]).astype(BF16)
        bfor = jnp.zeros((1, LANES), F32).at[0, :GROUP_HEADS].set(b_forget[l])
        q_gain = lambda g: jnp.broadcast_to((g * q_scale)[:, None], (HEAD_DIM, tm_in))
        k_gain = lambda g: jnp.tile(g, GROUP_HEADS).reshape(1, w)
        wts = _project_weights(w_in, l, w) + (
               wf, bfor,
               q_gain(g_qn_moba[l]), k_gain(g_kn_moba[l]),
               q_gain(g_qn_fox[l]), k_gain(g_kn_fox[l]))
        consts = (cos_t, sin_t, rc, rs1, rs2, jmat, place, kones)

        qm_t, km, vm_t, bias_t, qf_t, kf, vf_t, ka_f, qa_f = _inproj(
            x, mod4[:, 1], mod4[:, 0], wts, consts, tm_in)

        o_m = _attention_dispatch(g_qn_moba[l] * q_scale, g_kn_moba[l],
                                  qm_t, bias_t, km, ka_moba, vm_t, ka_const=True,
                                  name="attn_moba")
        o_f = _attention_dispatch(g_qn_fox[l] * q_scale, g_kn_fox[l],
                                  qf_t, qa_f, kf, ka_f, vf_t, ka_const=False,
                                  name="attn_fox")

        x = _post(x, o_m, o_f, mod4,
                  g_out_moba[l].reshape(1, w), g_out_fox[l].reshape(1, w),
                  w_out[l], w_ff1[l], w_ff2[l],
                  tm_post)
    return x
```

```python
import functools
import math

import numpy as np
import jax
import jax.numpy as jnp
from jax import lax
from jax.experimental import pallas as pl
from jax.experimental.pallas import tpu as pltpu

F32 = jnp.float32
BF16 = jnp.bfloat16

HEAD_DIM = 64
GROUP_HEADS = 8
GROUP_WIDTH = GROUP_HEADS * HEAD_DIM
PAIR_WIDTH = 2 * HEAD_DIM
N_PAIRS = GROUP_HEADS // 2
MOBA_BLOCK = 256
MOBA_TOPK = 3
ROPE_THETA = 500000.0
ROPE_DIM = HEAD_DIM // 4
ROPE_HALF = ROPE_DIM // 2
EPS = 1e-6
LOG2E = math.log2(math.e)
NEG_BIG = -1e30
MAX_SAFE_EXPONENT = 60.0
ATT_TILE = 512
KEY_TILE = 256
CUM_BLOCK = 128
PAIRS_PER_TRIP = 4
LANES = 128
VMEM_LIMIT = 48 * 1024 * 1024
POST_VMEM_LIMIT = 58 * 1024 * 1024


def _dot(a, b):
    return jnp.dot(a, b, preferred_element_type=F32)


def _split3(x):
    hi = x.astype(BF16)
    rem = x - hi.astype(F32)
    mid = rem.astype(BF16)
    lo = (rem - mid.astype(F32)).astype(BF16)
    return hi, mid, lo


def _dot_nt(a, b):
    return lax.dot_general(a, b, (((1,), (1,)), ((), ())), preferred_element_type=F32)


def _adaln_kernel(ct_ref, w_ref, b_ref, o_ref):
    c_t = ct_ref[...]
    s_t = c_t / (1.0 + jnp.exp(-c_t))
    w = w_ref[...]
    rows = [jnp.sum(s_t[:, b:b + 1] * w, axis=0, keepdims=True) for b in range(c_t.shape[1])]
    o_ref[...] = jnp.concatenate(rows, axis=0) + b_ref[...]


def _adaln(c, w_ada, b_ada):
    bsz, d = c.shape
    n = w_ada.shape[1]
    return pl.pallas_call(
        _adaln_kernel,
        grid=(n // d,),
        in_specs=[pl.BlockSpec((d, bsz), lambda j: (0, 0)),
                  pl.BlockSpec((d, d), lambda j: (0, j)),
                  pl.BlockSpec((1, d), lambda j: (0, j))],
        out_specs=pl.BlockSpec((bsz, d), lambda j: (0, j)),
        out_shape=jax.ShapeDtypeStruct((bsz, n), F32),
        compiler_params=pltpu.CompilerParams(dimension_semantics=("arbitrary",)),
        name="adaln",
    )(c.T, w_ada, b_ada.reshape(1, n))


TRANSPOSED_GROUPS = (0, 2, 3, 5)


def _wprep_kernel(w_ref, *o_refs):
    g = pl.program_id(0)
    for k, o_ref in enumerate(o_refs):
        @pl.when(g == k)
        def _(k=k, o_ref=o_ref):
            w = w_ref[...]
            o_ref[...] = (w.T if k in TRANSPOSED_GROUPS else w).astype(BF16)


def _project_weights(w_in, layer, width):
    d = w_in.shape[1]
    shapes = [(width, d) if k in TRANSPOSED_GROUPS else (d, width) for k in range(6)]
    return tuple(pl.pallas_call(
        _wprep_kernel,
        grid=(6,),
        in_specs=[pl.BlockSpec((None, d, width), lambda g: (layer, 0, g))],
        out_specs=[pl.BlockSpec(s, lambda g: (0, 0)) for s in shapes],
        out_shape=[jax.ShapeDtypeStruct(s, BF16) for s in shapes],
        compiler_params=pltpu.CompilerParams(dimension_semantics=("arbitrary",)),
        name="wprep",
    )(w_in))


def _inproj_kernel(x_ref, sc_ref, sh_ref,
                   wqm_ref, wkm_ref, wvm_ref, wqf_ref, wkf_ref, wvf_ref, wf_ref, bf_ref,
                   gqm_ref, gkm_ref, gqf_ref, gkf_ref,
                   cos_t_ref, sin_t_ref, rc_ref, rs1_ref, rs2_ref,
                   j_ref, place_ref, kones_ref,
                   qm_o, km_o, vm_o, bias_o, qf_o, kf_o, vf_o, ka_o, qa_o,
                   kmean_sc, carry_sc, gate_sc, *, tm):
    i = pl.program_id(1)
    nb = tm // MOBA_BLOCK

    @pl.when(i == 0)
    def _():
        kmean_sc[...] = jnp.zeros_like(kmean_sc)
        carry_sc[...] = jnp.zeros_like(carry_sc)

    x = x_ref[0]
    ms = jnp.mean(x * x, axis=-1, keepdims=True)
    h = x * lax.rsqrt(ms + EPS) * (1.0 + sc_ref[...]) + sh_ref[...]
    hb = h.astype(BF16)

    def q_feature_major(w_ref, g_ref, rope):
        q_t = _dot_nt(w_ref[...], hb)
        gain = g_ref[...]
        heads = []
        for hd in range(GROUP_HEADS):
            xh = q_t[hd * HEAD_DIM:(hd + 1) * HEAD_DIM, :]
            ss = jnp.sum(xh * xh, axis=0, keepdims=True)
            xh = xh * lax.rsqrt(ss * (1.0 / HEAD_DIM) + EPS) * gain
            if rope:
                x1 = xh[0:ROPE_HALF, :]
                x2 = xh[ROPE_HALF:ROPE_DIM, :]
                cs = cos_t_ref[...]
                sn = sin_t_ref[...]
                xh = jnp.concatenate(
                    [x1 * cs - x2 * sn, x2 * cs + x1 * sn, xh[ROPE_DIM:, :]], axis=0)
            heads.append(xh)
        return jnp.concatenate(heads, axis=0)

    def k_token_major(w_ref, g_ref, rope):
        k = _dot(hb, w_ref[...])
        chunks = []
        for c in range(GROUP_WIDTH // 256):
            seg = k[:, c * 256:(c + 1) * 256]
            msq = _dot((seg * seg).astype(BF16), j_ref[...])
            chunks.append(seg * lax.rsqrt(msq + EPS) * g_ref[:, c * 256:(c + 1) * 256])
        k = jnp.concatenate(chunks, axis=1)
        if rope:
            outs = []
            for c in range(GROUP_WIDTH // LANES):
                seg = k[:, c * LANES:(c + 1) * LANES]
                outs.append(seg * rc_ref[...]
                            + pltpu.roll(seg, ROPE_HALF, 1) * rs1_ref[...]
                            + pltpu.roll(seg, LANES - ROPE_HALF, 1) * rs2_ref[...])
            k = jnp.concatenate(outs, axis=1)
        return k

    def store_tiles(o_ref, val_t, width=ATT_TILE):
        for t in range(tm // width):
            o_ref[0, t] = val_t[:, t * width:(t + 1) * width].astype(o_ref.dtype)

    qm_t = q_feature_major(wqm_ref, gqm_ref, True)
    store_tiles(qm_o, qm_t)
    km = k_token_major(wkm_ref, gkm_ref, True)
    km_o[0] = km.astype(BF16)
    store_tiles(vm_o, _dot_nt(wvm_ref[...], hb), KEY_TILE)

    blk0 = i * nb
    lane_head = lax.broadcasted_iota(jnp.int32, (1, GROUP_WIDTH), 1) // HEAD_DIM
    for sb in range(nb):
        kmean = jnp.mean(km[sb * MOBA_BLOCK:(sb + 1) * MOBA_BLOCK, :], axis=0, keepdims=True)
        for hd in range(GROUP_HEADS):
            row = hd * 16 + blk0 + sb
            kmean_sc[pl.ds(row, 1), :] = jnp.where(lane_head == hd, kmean, 0.0)

    km_hi, km_lo, _ = _split3(kmean_sc[...])
    q_hi, q_lo, _ = _split3(qm_t)
    gate_t = _dot(km_hi, q_hi) + _dot(km_hi, q_lo) + _dot(km_lo, q_hi)
    tok_blk = blk0 + lax.broadcasted_iota(jnp.int32, (16, tm), 1) // MOBA_BLOCK
    n_idx = lax.broadcasted_iota(jnp.int32, (16, tm), 0)
    valid = n_idx < tok_blk
    for hd in range(GROUP_HEADS):
        gate_sc[hd * 16:(hd + 1) * 16, :] = jnp.where(
            valid, gate_t[hd * 16:(hd + 1) * 16, :], -jnp.inf)

    def rank_step(m, cnts):
        tie = jnp.where(n_idx > m, 1, 0)
        out = []
        for hd in range(GROUP_HEADS):
            g = gate_sc[hd * 16:(hd + 1) * 16, :]
            gm = jnp.broadcast_to(gate_sc[pl.ds(hd * 16 + m, 1), :], (16, tm))
            out.append(cnts[hd] + jnp.where(gm > g, 1, jnp.where(gm == g, tie, 0)))
        return tuple(out)

    cnts = lax.fori_loop(0, blk0 + nb - 1, rank_step,
                         tuple(jnp.zeros((16, tm), jnp.int32) for _ in range(GROUP_HEADS)))
    zeros48 = jnp.zeros((HEAD_DIM - 16, tm), F32)
    pair_rows = []
    for hd in range(GROUP_HEADS):
        sel = (valid & (cnts[hd] < MOBA_TOPK)) | (n_idx == tok_blk)
        pair_rows += [jnp.where(sel, 0.0, NEG_BIG), zeros48]
    store_tiles(bias_o, jnp.concatenate(pair_rows, axis=0))

    store_tiles(qf_o, q_feature_major(wqf_ref, gqf_ref, False))
    kf_o[0] = k_token_major(wkf_ref, gkf_ref, False).astype(BF16)
    store_tiles(vf_o, _dot_nt(wvf_ref[...], hb), KEY_TILE)

    f = _dot(hb, wf_ref[...]) + bf_ref[...]
    logf = jnp.minimum(f, 0.0) - jnp.log1p(jnp.exp(-jnp.abs(f)))
    live = lax.broadcasted_iota(jnp.int32, (1, LANES), 1) < GROUP_HEADS
    logf = jnp.where(live, logf, 0.0)
    r_i = lax.broadcasted_iota(jnp.int32, (CUM_BLOCK, CUM_BLOCK), 0)
    c_i = lax.broadcasted_iota(jnp.int32, (CUM_BLOCK, CUM_BLOCK), 1)
    tri = jnp.where(r_i >= c_i, 1.0, 0.0).astype(BF16)
    n_blk = tm // CUM_BLOCK
    wide = jnp.concatenate([logf[b * CUM_BLOCK:(b + 1) * CUM_BLOCK, :] for b in range(n_blk)],
                           axis=1)
    parts = _split3(wide)
    prefix = _dot(tri, parts[0]) + _dot(tri, parts[1]) + _dot(tri, parts[2])
    carry = carry_sc[...]
    cums = []
    for blk in range(n_blk):
        c_blk = prefix[:, blk * LANES:(blk + 1) * LANES] + carry
        carry = c_blk[CUM_BLOCK - 1:CUM_BLOCK, :]
        cums.append(c_blk)
    carry_sc[...] = carry
    nc = jnp.concatenate(cums, axis=0) * (-LOG2E)
    hi, mid, lo = [part.astype(F32) for part in _split3(nc)]
    packed = hi + pltpu.roll(mid, GROUP_HEADS, 1) + pltpu.roll(lo, 2 * GROUP_HEADS, 1)
    ka_o[0] = (_dot(packed.astype(BF16), place_ref[...]) + kones_ref[...]).astype(BF16)

    nc_t = nc.T[0:GROUP_HEADS, :]
    hi_t = nc_t.astype(BF16).astype(F32)
    rem_t = nc_t - hi_t
    mid_t = rem_t.astype(BF16).astype(F32)
    lo_t = (rem_t - mid_t).astype(BF16).astype(F32)
    r8 = lax.broadcasted_iota(jnp.int32, (8, tm), 0)
    zeros56 = jnp.zeros((HEAD_DIM - 8, tm), F32)
    qa_rows = []
    for hd in range(GROUP_HEADS):
        blk = jnp.where(r8 < 3, 1.0,
                        jnp.where(r8 == 3, -hi_t[hd:hd + 1, :],
                                  jnp.where(r8 == 4, -mid_t[hd:hd + 1, :],
                                            jnp.where(r8 == 5, -lo_t[hd:hd + 1, :], 0.0))))
        qa_rows += [blk, zeros56]
    store_tiles(qa_o, jnp.concatenate(qa_rows, axis=0))


def _inproj(x, sc, sh, wts, consts, tm):
    bsz, s, d = x.shape
    nt = s // ATT_TILE
    grid = (bsz, s // tm)
    wqm, wkm, wvm, wqf, wkf, wvf, wf, bfor, gqm, gkm, gqf, gkf = wts
    cos_t, sin_t, rc, rs1, rs2, jmat, place, kones = consts

    def const(a):
        nd = a.ndim
        return pl.BlockSpec(a.shape, lambda b, i, _n=nd: (0,) * _n,
                            pipeline_mode=pl.Buffered(1))

    mod_spec = pl.BlockSpec((None, 1, d), lambda b, i: (b, 0, 0))
    in_specs = [pl.BlockSpec((1, tm, d), lambda b, i: (b, i, 0)), mod_spec, mod_spec]
    in_specs += [const(a) for a in (wqm, wkm, wvm, wqf, wkf, wvf, wf, bfor,
                                    gqm, gkm, gqf, gkf)]
    in_specs += [pl.BlockSpec((ROPE_HALF, tm), lambda b, i: (0, i)),
                 pl.BlockSpec((ROPE_HALF, tm), lambda b, i: (0, i)),
                 pl.BlockSpec((tm, LANES), lambda b, i: (i, 0)),
                 pl.BlockSpec((tm, LANES), lambda b, i: (i, 0)),
                 pl.BlockSpec((tm, LANES), lambda b, i: (i, 0)),
                 const(jmat), const(place), const(kones)]

    tiles = tm // ATT_TILE
    feat_spec = pl.BlockSpec((1, tiles, GROUP_WIDTH, ATT_TILE), lambda b, i: (b, i, 0, 0))
    tok_spec = pl.BlockSpec((1, tm, GROUP_WIDTH), lambda b, i: (b, i, 0))
    feat_shape = jax.ShapeDtypeStruct((bsz, nt, GROUP_WIDTH, ATT_TILE), BF16)
    tok_shape = jax.ShapeDtypeStruct((bsz, s, GROUP_WIDTH), BF16)
    val_spec = pl.BlockSpec((1, tm // KEY_TILE, GROUP_WIDTH, KEY_TILE),
                            lambda b, i: (b, i, 0, 0))
    val_shape = jax.ShapeDtypeStruct((bsz, s // KEY_TILE, GROUP_WIDTH, KEY_TILE), BF16)
    out_specs = [feat_spec, tok_spec, val_spec, feat_spec,
                 feat_spec, tok_spec, val_spec, tok_spec, feat_spec]
    out_shape = [feat_shape, tok_shape, val_shape, feat_shape,
                 feat_shape, tok_shape, val_shape, tok_shape, feat_shape]
    return pl.pallas_call(
        functools.partial(_inproj_kernel, tm=tm),
        grid=grid,
        in_specs=in_specs,
        out_specs=out_specs,
        out_shape=out_shape,
        scratch_shapes=[pltpu.VMEM((GROUP_HEADS * 16, GROUP_WIDTH), F32),
                        pltpu.VMEM((1, LANES), F32),
                        pltpu.VMEM((GROUP_HEADS * 16, tm), F32)],
        compiler_params=pltpu.CompilerParams(
            dimension_semantics=("arbitrary", "arbitrary"),
            vmem_limit_bytes=VMEM_LIMIT),
        name="inproj",
    )(x, sc, sh, wqm, wkm, wvm, wqf, wkf, wvf, wf, bfor, gqm, gkm, gqf, gkf,
      cos_t, sin_t, rc, rs1, rs2, jmat, place, kones)


def _attn_kernel(q_ref, qa_ref, k_ref, ka_ref, v_ref, o_ref, *, t):
    i = pl.program_id(2)
    q = q_ref[0, 0]
    qa = qa_ref[0, 0]
    row = lax.broadcasted_iota(jnp.int32, (PAIR_WIDTH, t), 0)
    rhs = []
    for e in range(2):
        keep = (row >= e * HEAD_DIM) & (row < (e + 1) * HEAD_DIM)
        zero = jnp.zeros_like(q)
        rhs.append(jnp.concatenate([jnp.where(keep, q, zero), jnp.where(keep, qa, zero)],
                                   axis=0))

    def tile_scores(j):
        start = pl.multiple_of(j * t, t)
        lhs = jnp.concatenate([k_ref[0, pl.ds(start, t), :], ka_ref[0, pl.ds(start, t), :]],
                              axis=1)
        return [_dot(lhs, rhs[e]) for e in range(2)]

    def values(j, e):
        rows = slice(e * HEAD_DIM, (e + 1) * HEAD_DIM)
        slabs = t // KEY_TILE
        return jnp.concatenate([v_ref[0, slabs * j + h, rows, :] for h in range(slabs)],
                               axis=1)

    key_i = lax.broadcasted_iota(jnp.int32, (t, t), 0)
    qry_i = lax.broadcasted_iota(jnp.int32, (t, t), 1)
    causal = key_i <= qry_i
    state = []
    for e, s_t in enumerate(tile_scores(i)):
        s_t = jnp.where(causal, s_t, NEG_BIG)
        m = jnp.max(s_t, axis=0, keepdims=True)
        p = jnp.exp2(s_t - m)
        l = jnp.sum(p, axis=0, keepdims=True)
        acc = _dot(values(i, e), p.astype(BF16))
        state += [m, l, acc]

    def body(j, carry):
        out = []
        for e, s_t in enumerate(tile_scores(j)):
            m, l, acc = carry[3 * e:3 * e + 3]
            m_new = jnp.maximum(m, jnp.max(s_t, axis=0, keepdims=True))
            alpha = jnp.exp2(m - m_new)
            p = jnp.exp2(s_t - m_new)
            l = alpha * l + jnp.sum(p, axis=0, keepdims=True)
            acc = alpha * acc + _dot(values(j, e), p.astype(BF16))
            out += [m_new, l, acc]
        return tuple(out)

    state = lax.fori_loop(0, i, body, tuple(state))
    o_t = jnp.concatenate([state[2] / state[1], state[5] / state[4]], axis=0)
    o_ref[0] = o_t.T


def _attn_bounded_kernel(q_ref, qa_ref, k_ref, ka_ref, v_ref, o_ref, p_sc, acc_sc, l_sc, *, t,
                         ka_const):
    i = pl.program_id(1)
    kt = KEY_TILE
    row = lax.broadcasted_iota(jnp.int32, (PAIR_WIDTH, t), 0)
    key_i = lax.broadcasted_iota(jnp.int32, (kt, t), 0)
    qry_i = lax.broadcasted_iota(jnp.int32, (kt, t), 1)
    assert t == 2 * kt


    def rhs_of(pr):
        feat = slice(pr * PAIR_WIDTH, (pr + 1) * PAIR_WIDTH)
        q = q_ref[0, 0, feat, :]
        qa = qa_ref[0, 0, feat, :]
        out = []
        for e in range(2):
            keep = (row >= e * HEAD_DIM) & (row < (e + 1) * HEAD_DIM)
            zero = jnp.zeros_like(q)
            out.append(jnp.concatenate([jnp.where(keep, q, zero), jnp.where(keep, qa, zero)],
                                       axis=0))
        return out

    for grp in range(N_PAIRS // PAIRS_PER_TRIP):
        prs = list(range(grp * PAIRS_PER_TRIP, (grp + 1) * PAIRS_PER_TRIP))
        rhs = {pr: rhs_of(pr) for pr in prs}

        def probs(pr, start, n_keys, row0, mask=None, q0=0, first=False, rhs=rhs):
            feat = slice(pr * PAIR_WIDTH, (pr + 1) * PAIR_WIDTH)
            k_aug = (ka_ref[0, pl.ds(start, n_keys), :] if ka_const
                     else ka_ref[0, pl.ds(start, n_keys), feat])
            lhs = jnp.concatenate([k_ref[0, pl.ds(start, n_keys), feat], k_aug], axis=1)
            for e in range(2):
                s_t = _dot(lhs, rhs[pr][e][:, q0:])
                if mask is not None:
                    s_t = jnp.where(mask[:, q0:], s_t, NEG_BIG)
                p = jnp.exp2(s_t)
                p_sc[pr, e, row0:row0 + n_keys, q0:] = p.astype(BF16)
                part = p[0:8, :]
                for r in range(8, n_keys, 8):
                    part = part + p[r:r + 8, :]
                if first:
                    l_sc[pr, e] = part
                else:
                    l_sc[pr, e, :, q0:] += part

        def values(pr, j):
            rows = slice(pr * PAIR_WIDTH, (pr + 1) * PAIR_WIDTH)
            v_pair = jnp.concatenate([v_ref[0, 2 * j, rows, :], v_ref[0, 2 * j + 1, rows, :]],
                                     axis=1)
            for e in range(2):
                upd = _dot(v_pair, p_sc[pr, e])
                acc_sc[pr, e] += upd[e * HEAD_DIM:(e + 1) * HEAD_DIM, :]

        diag = pl.multiple_of(i * t, t)
        for pr in prs:
            probs(pr, diag, kt, 0, key_i <= qry_i, first=True)
            probs(pr, diag + kt, kt, kt, key_i + kt <= qry_i, q0=kt)
            for e in range(2):
                p_sc[pr, e, kt:, :kt] = jnp.zeros((kt, kt), BF16)
                acc_sc[pr, e] = jnp.zeros((HEAD_DIM, t), F32)

        def body(j, j_prev, probs=probs, values=values, prs=prs):
            for pr in prs:
                values(pr, j_prev)
                probs(pr, pl.multiple_of(j * t, t), t, 0)
            return j

        def body2(m, j_prev, body=body):
            return body(2 * m + 1, body(2 * m, j_prev))

        pairs_of_tiles = i // 2
        j_prev = lax.fori_loop(0, pairs_of_tiles, body2, i)
        j_prev = lax.fori_loop(2 * pairs_of_tiles, i, body, j_prev)
        for pr in prs:
            values(pr, j_prev)
        for pr in prs:
            o_t = jnp.concatenate(
                [acc_sc[pr, e] / jnp.sum(l_sc[pr, e], axis=0, keepdims=True)
                 for e in range(2)], axis=0)
            o_ref[0, :, pr * PAIR_WIDTH:(pr + 1) * PAIR_WIDTH] = o_t.T


def _attention_online(q_t, qa, k, ka, v_t, *, ka_const, name):
    bsz, nt, _, t = q_t.shape
    s = k.shape[1]
    grid = (bsz, N_PAIRS, nt)
    if ka_const:
        ka_spec = pl.BlockSpec((1, s, PAIR_WIDTH), lambda b, p, i: (0, 0, 0))
    else:
        ka_spec = pl.BlockSpec((1, s, PAIR_WIDTH), lambda b, p, i: (b, 0, p))
    return pl.pallas_call(
        functools.partial(_attn_kernel, t=t),
        grid=grid,
        in_specs=[pl.BlockSpec((1, 1, PAIR_WIDTH, t), lambda b, p, i: (b, i, p, 0)),
                  pl.BlockSpec((1, 1, PAIR_WIDTH, t), lambda b, p, i: (b, i, p, 0)),
                  pl.BlockSpec((1, s, PAIR_WIDTH), lambda b, p, i: (b, 0, p)),
                  ka_spec,
                  pl.BlockSpec((1, s // KEY_TILE, PAIR_WIDTH, KEY_TILE),
                               lambda b, p, i: (b, 0, p, 0))],
        out_specs=pl.BlockSpec((1, t, PAIR_WIDTH), lambda b, p, i: (b, i, p)),
        out_shape=jax.ShapeDtypeStruct((bsz, s, GROUP_WIDTH), F32),
        compiler_params=pltpu.CompilerParams(
            dimension_semantics=("arbitrary", "arbitrary", "arbitrary"),
            vmem_limit_bytes=VMEM_LIMIT),
        name=name,
    )(q_t, qa, k, ka, v_t)


def _attention_bounded(q_t, qa, k, ka, v_t, *, ka_const, name):
    bsz, nt, _, t = q_t.shape
    s = k.shape[1]
    feat_spec = pl.BlockSpec((1, 1, GROUP_WIDTH, t), lambda b, i: (b, i, 0, 0))
    tok_spec = pl.BlockSpec((1, s, GROUP_WIDTH), lambda b, i: (b, 0, 0))
    ka_spec = pl.BlockSpec((1, s, PAIR_WIDTH), lambda b, i: (0, 0, 0)) if ka_const else tok_spec
    return pl.pallas_call(
        functools.partial(_attn_bounded_kernel, t=t, ka_const=ka_const),
        grid=(bsz, nt),
        in_specs=[feat_spec, feat_spec, tok_spec, ka_spec,
                  pl.BlockSpec((1, s // KEY_TILE, GROUP_WIDTH, KEY_TILE),
                               lambda b, i: (b, 0, 0, 0))],
        out_specs=pl.BlockSpec((1, t, GROUP_WIDTH), lambda b, i: (b, i, 0)),
        out_shape=jax.ShapeDtypeStruct((bsz, s, GROUP_WIDTH), F32),
        scratch_shapes=[pltpu.VMEM((N_PAIRS, 2, t, t), BF16),
                        pltpu.VMEM((N_PAIRS, 2, HEAD_DIM, t), F32),
                        pltpu.VMEM((N_PAIRS, 2, 8, t), F32)],
        compiler_params=pltpu.CompilerParams(
            dimension_semantics=("arbitrary", "arbitrary"),
            vmem_limit_bytes=VMEM_LIMIT),
        name=name,
    )(q_t, qa, k, ka, v_t)


def _attention_dispatch(g_q, g_k, q_t, qa, k, ka, v_t, *, ka_const, name):
    bound = HEAD_DIM * jnp.max(jnp.abs(g_q)) * jnp.max(jnp.abs(g_k))
    return lax.cond(
        bound <= MAX_SAFE_EXPONENT,
        functools.partial(_attention_bounded, ka_const=ka_const, name=name + "_bounded"),
        functools.partial(_attention_online, ka_const=ka_const, name=name + "_online"),
        q_t, qa, k, ka, v_t)


def _post_kernel(x_ref, om_ref, of_ref, ga_ref, scm_ref, shm_ref, gm_ref,
                 gom_ref, gof_ref, wout_ref, w1_ref, w2_ref, o_ref, *, ff_chunk):
    def rms(v):
        return v * lax.rsqrt(jnp.mean(v * v, axis=-1, keepdims=True) + EPS)

    mixed = jnp.concatenate([rms(om_ref[0]) * gom_ref[...],
                             rms(of_ref[0]) * gof_ref[...]], axis=1)
    x1 = x_ref[0] + ga_ref[...] * _dot(mixed, wout_ref[...])
    h = rms(x1) * (1.0 + scm_ref[...]) + shm_ref[...]
    d_ff = w1_ref.shape[1]
    y = None
    for c in range(d_ff // ff_chunk):
        hid = _dot(h, w1_ref[:, c * ff_chunk:(c + 1) * ff_chunk])
        hid = jnp.square(jnp.maximum(hid, 0.0))
        part = _dot(hid, w2_ref[c * ff_chunk:(c + 1) * ff_chunk, :])
        y = part if y is None else y + part
    o_ref[0] = x1 + gm_ref[...] * y


def _post(x, o_m, o_f, mod4, g_out_m, g_out_f, w_out, w_ff1, w_ff2, tm):
    bsz, s, d = x.shape
    d_ff = w_ff1.shape[1]

    def const(a):
        nd = a.ndim
        return pl.BlockSpec(a.shape, lambda b, i, _n=nd: (0,) * _n,
                            pipeline_mode=pl.Buffered(1))

    def mod_spec(k):
        return pl.BlockSpec((None, None, 1, d), lambda b, i, _k=k: (b, _k, 0, 0))

    tok = lambda w: pl.BlockSpec((1, tm, w), lambda b, i: (b, i, 0))
    return pl.pallas_call(
        functools.partial(_post_kernel, ff_chunk=1024),
        grid=(bsz, s // tm),
        in_specs=[tok(d), tok(GROUP_WIDTH), tok(GROUP_WIDTH),
                  mod_spec(2), mod_spec(4), mod_spec(3), mod_spec(5),
                  const(g_out_m), const(g_out_f), const(w_out), const(w_ff1), const(w_ff2)],
        out_specs=tok(d),
        out_shape=jax.ShapeDtypeStruct((bsz, s, d), F32),
        compiler_params=pltpu.CompilerParams(
            dimension_semantics=("arbitrary", "arbitrary"),
            vmem_limit_bytes=POST_VMEM_LIMIT),
        name="post",
    )(x, o_m, o_f, mod4, mod4, mod4, mod4, g_out_m, g_out_f, w_out, w_ff1, w_ff2)


def _rope_tables(s):
    inv_freq = ROPE_THETA ** (-np.arange(0, ROPE_DIM, 2, dtype=np.float64) / ROPE_DIM)
    ang = np.arange(s, dtype=np.float64)[:, None] * inv_freq[None, :]
    cos, sin = np.cos(ang), np.sin(ang)
    rc = np.ones((s, LANES)); rs1 = np.zeros((s, LANES)); rs2 = np.zeros((s, LANES))
    for base in (0, HEAD_DIM):
        rc[:, base:base + ROPE_HALF] = cos
        rc[:, base + ROPE_HALF:base + ROPE_DIM] = cos
        rs2[:, base:base + ROPE_HALF] = -sin
        rs1[:, base + ROPE_HALF:base + ROPE_DIM] = sin
    f = lambda a: jnp.asarray(a, dtype=F32)
    return f(cos.T), f(sin.T), f(rc), f(rs1), f(rs2)


def _static_mats(s):
    idx = np.arange(256)
    jmat = (idx[:, None] // HEAD_DIM == idx[None, :] // HEAD_DIM) / HEAD_DIM
    place = np.zeros((LANES, GROUP_WIDTH))
    for part in range(3):
        for hd in range(GROUP_HEADS):
            place[part * GROUP_HEADS + hd,
                  (hd // 2) * PAIR_WIDTH + (hd % 2) * HEAD_DIM + part] = 1.0
    kones = np.zeros((1, GROUP_WIDTH))
    for hd in range(GROUP_HEADS):
        kones[0, hd * HEAD_DIM + 3:hd * HEAD_DIM + 6] = 1.0
    ka_moba = np.zeros((1, s, PAIR_WIDTH))
    blk = np.arange(s) // MOBA_BLOCK
    for e in range(2):
        ka_moba[0, np.arange(s), e * HEAD_DIM + blk] = 1.0
    b = lambda a: jnp.asarray(a, dtype=BF16)
    return b(jmat), b(place), jnp.asarray(kones, dtype=F32), b(ka_moba)


def kernel(x, c, w_ada, b_ada, w_in, b_forget, g_qn_moba, g_kn_moba, g_qn_fox, g_kn_fox,
           g_out_moba, g_out_fox, w_out, w_ff1, w_ff2):
    bsz, s, d = x.shape
    depth = w_ada.shape[0]
    w = GROUP_WIDTH
    assert s % MOBA_BLOCK == 0 and s // MOBA_BLOCK <= 16
    tm_in = 1024
    tm_post = 512
    cos_t, sin_t, rc, rs1, rs2 = _rope_tables(s)
    jmat, place, kones, ka_moba = _static_mats(s)
    q_scale = HEAD_DIM ** -0.5 * LOG2E

    for l in range(depth):
        mod = _adaln(c, w_ada[l], b_ada[l])
        mod4 = mod.reshape(bsz, 6, 1, d)

        wl = w_in[l]
        wf =jnp.zeros((d, LANES), F32).at[:, :GROUP_HEADS].set(wl[:, 6 * w:]).astype(BF16)
        bfor = jnp.zeros((1, LANES), F32).at[0, :GROUP_HEADS].set(b_forget[l])
        q_gain = lambda g: jnp.broadcast_to((g * q_scale)[:, None], (HEAD_DIM, tm_in))
        k_gain = lambda g: jnp.tile(g, GROUP_HEADS).reshape(1, w)
        wts = _project_weights(w_in, l, w) + (
               wf, bfor,
               q_gain(g_qn_moba[l]), k_gain(g_kn_moba[l]),
               q_gain(g_qn_fox[l]), k_gain(g_kn_fox[l]))
        consts = (cos_t, sin_t, rc, rs1, rs2, jmat, place, kones)

        qm_t, km, vm_t, bias_t, qf_t, kf, vf_t, ka_f, qa_f = _inproj(
            x, mod4[:, 1], mod4[:, 0], wts, consts, tm_in)

        o_m = _attention_dispatch(g_qn_moba[l] * q_scale, g_kn_moba[l],
                                  qm_t, bias_t, km, ka_moba, vm_t, ka_const=True,
                                  name="attn_moba")
        o_f = _attention_dispatch(g_qn_fox[l] * q_scale, g_kn_fox[l],
                                  qf_t, qa_f, kf, ka_f, vf_t, ka_const=False,
                                  name="attn_fox")

        x = _post(x, o_m, o_f, mod4,
                  g_out_moba[l].reshape(1, w), g_out_fox[l].reshape(1, w),
                  w_out[l], w_ff1[l], w_ff2[l],
                  tm_post)
    return x
```

```python
import functools
import math

import numpy as np
import jax
import jax.numpy as jnp
from jax import lax
from jax.experimental import pallas as pl
from jax.experimental.pallas import tpu as pltpu

F32 = jnp.float32
BF16 = jnp.bfloat16

HEAD_DIM = 64
GROUP_HEADS = 8
GROUP_WIDTH = GROUP_HEADS * HEAD_DIM
PAIR_WIDTH = 2 * HEAD_DIM
N_PAIRS = GROUP_HEADS // 2
MOBA_BLOCK = 256
MOBA_TOPK = 3
ROPE_THETA = 500000.0
ROPE_DIM = HEAD_DIM // 4
ROPE_HALF = ROPE_DIM // 2
EPS = 1e-6
LOG2E = math.log2(math.e)
NEG_BIG = -1e30
MAX_SAFE_EXPONENT = 60.0
ATT_TILE = 512
KEY_TILE = 256
CUM_BLOCK = 128
PAIRS_PER_TRIP = 4
LANES = 128
VMEM_LIMIT = 48 * 1024 * 1024
POST_VMEM_LIMIT = 58 * 1024 * 1024


def _dot(a, b):
    return jnp.dot(a, b, preferred_element_type=F32)


def _split3(x):
    hi = x.astype(BF16)
    rem = x - hi.astype(F32)
    mid = rem.astype(BF16)
    lo = (rem - mid.astype(F32)).astype(BF16)
    return hi, mid, lo


def _dot_nt(a, b):
    return lax.dot_general(a, b, (((1,), (1,)), ((), ())), preferred_element_type=F32)


def _adaln_kernel(ct_ref, w_ref, b_ref, o_ref):
    c_t = ct_ref[...]
    s_t = c_t / (1.0 + jnp.exp(-c_t))
    w = w_ref[...]
    rows = [jnp.sum(s_t[:, b:b + 1] * w, axis=0, keepdims=True) for b in range(c_t.shape[1])]
    o_ref[...] = jnp.concatenate(rows, axis=0) + b_ref[...]


def _adaln(c, w_ada, b_ada):
    bsz, d = c.shape
    n = w_ada.shape[1]
    return pl.pallas_call(
        _adaln_kernel,
        grid=(n // d,),
        in_specs=[pl.BlockSpec((d, bsz), lambda j: (0, 0)),
                  pl.BlockSpec((d, d), lambda j: (0, j)),
                  pl.BlockSpec((1, d), lambda j: (0, j))],
        out_specs=pl.BlockSpec((bsz, d), lambda j: (0, j)),
        out_shape=jax.ShapeDtypeStruct((bsz, n), F32),
        compiler_params=pltpu.CompilerParams(dimension_semantics=("arbitrary",)),
        name="adaln",
    )(c.T, w_ada, b_ada.reshape(1, n))


TRANSPOSED_GROUPS = (0, 2, 3, 5)


def _wprep_kernel(w_ref, *o_refs):
    g = pl.program_id(0)
    for k, o_ref in enumerate(o_refs):
        @pl.when(g == k)
        def _(k=k, o_ref=o_ref):
            w = w_ref[...]
            o_ref[...] = (w.T if k in TRANSPOSED_GROUPS else w).astype(BF16)


def _project_weights(w_in, layer, width):
    d = w_in.shape[1]
    shapes = [(width, d) if k in TRANSPOSED_GROUPS else (d, width) for k in range(6)]
    return tuple(pl.pallas_call(
        _wprep_kernel,
        grid=(6,),
        in_specs=[pl.BlockSpec((None, d, width), lambda g: (layer, 0, g))],
        out_specs=[pl.BlockSpec(s, lambda g: (0, 0)) for s in shapes],
        out_shape=[jax.ShapeDtypeStruct(s, BF16) for s in shapes],
        compiler_params=pltpu.CompilerParams(dimension_semantics=("arbitrary",)),
        name="wprep",
    )(w_in))


def _inproj_kernel(x_ref, sc_ref, sh_ref,
                   wqm_ref, wkm_ref, wvm_ref, wqf_ref, wkf_ref, wvf_ref, wf_ref, bf_ref,
                   gqm_ref, gkm_ref, gqf_ref, gkf_ref,
                   cos_t_ref, sin_t_ref, rc_ref, rs1_ref, rs2_ref,
                   j_ref, place_ref, kones_ref,
                   qm_o, km_o, vm_o, bias_o, qf_o, kf_o, vf_o, ka_o, qa_o,
                   kmean_sc, carry_sc, gate_sc, *, tm):
    i = pl.program_id(1)
    nb = tm // MOBA_BLOCK

    @pl.when(i == 0)
    def _():
        kmean_sc[...] = jnp.zeros_like(kmean_sc)
        carry_sc[...] = jnp.zeros_like(carry_sc)

    x = x_ref[0]
    ms = jnp.mean(x * x, axis=-1, keepdims=True)
    h = x * lax.rsqrt(ms + EPS) * (1.0 + sc_ref[...]) + sh_ref[...]
    hb = h.astype(BF16)

    def q_feature_major(w_ref, g_ref, rope):
        q_t = _dot_nt(w_ref[...], hb)
        gain = g_ref[...]
        heads = []
        for hd in range(GROUP_HEADS):
            xh = q_t[hd * HEAD_DIM:(hd + 1) * HEAD_DIM, :]
            ss = jnp.sum(xh * xh, axis=0, keepdims=True)
            xh = xh * lax.rsqrt(ss * (1.0 / HEAD_DIM) + EPS) * gain
            if rope:
                x1 = xh[0:ROPE_HALF, :]
                x2 = xh[ROPE_HALF:ROPE_DIM, :]
                cs = cos_t_ref[...]
                sn = sin_t_ref[...]
                xh = jnp.concatenate(
                    [x1 * cs - x2 * sn, x2 * cs + x1 * sn, xh[ROPE_DIM:, :]], axis=0)
            heads.append(xh)
        return jnp.concatenate(heads, axis=0)

    def k_token_major(w_ref, g_ref, rope):
        k = _dot(hb, w_ref[...])
        chunks = []
        for c in range(GROUP_WIDTH // 256):
            seg = k[:, c * 256:(c + 1) * 256]
            msq = _dot((seg * seg).astype(BF16), j_ref[...])
            chunks.append(seg * lax.rsqrt(msq + EPS) * g_ref[:, c * 256:(c + 1) * 256])
        k = jnp.concatenate(chunks, axis=1)
        if rope:
            outs = []
            for c in range(GROUP_WIDTH // LANES):
                seg = k[:, c * LANES:(c + 1) * LANES]
                outs.append(seg * rc_ref[...]
                            + pltpu.roll(seg, ROPE_HALF, 1) * rs1_ref[...]
                            + pltpu.roll(seg, LANES - ROPE_HALF, 1) * rs2_ref[...])
            k = jnp.concatenate(outs, axis=1)
        return k

    def store_tiles(o_ref, val_t, width=ATT_TILE):
        for t in range(tm // width):
            o_ref[0, t] = val_t[:, t * width:(t + 1) * width].astype(o_ref.dtype)

    qm_t = q_feature_major(wqm_ref, gqm_ref, True)
    store_tiles(qm_o, qm_t)
    km = k_token_major(wkm_ref, gkm_ref, True)
    km_o[0] = km.astype(BF16)
    store_tiles(vm_o, _dot_nt(wvm_ref[...], hb), KEY_TILE)

    blk0 = i * nb
    lane_head = lax.broadcasted_iota(jnp.int32, (1, GROUP_WIDTH), 1) // HEAD_DIM
    for sb in range(nb):
        kmean = jnp.mean(km[sb * MOBA_BLOCK:(sb + 1) * MOBA_BLOCK, :], axis=0, keepdims=True)
        for hd in range(GROUP_HEADS):
            row = hd * 16 + blk0 + sb
            kmean_sc[pl.ds(row, 1), :] = jnp.where(lane_head == hd, kmean, 0.0)

    km_hi, km_lo, _ = _split3(kmean_sc[...])
    q_hi, q_lo, _ = _split3(qm_t)
    gate_t = _dot(km_hi, q_hi) + _dot(km_hi, q_lo) + _dot(km_lo, q_hi)
    tok_blk = blk0 + lax.broadcasted_iota(jnp.int32, (16, tm), 1) // MOBA_BLOCK
    n_idx = lax.broadcasted_iota(jnp.int32, (16, tm), 0)
    valid = n_idx < tok_blk
    for hd in range(GROUP_HEADS):
        gate_sc[hd * 16:(hd + 1) * 16, :] = jnp.where(
            valid, gate_t[hd * 16:(hd + 1) * 16, :], -jnp.inf)

    def rank_step(m, cnts):
        tie = jnp.where(n_idx > m, 1, 0)
        out = []
        for hd in range(GROUP_HEADS):
            g = gate_sc[hd * 16:(hd + 1) * 16, :]
            gm = jnp.broadcast_to(gate_sc[pl.ds(hd * 16 + m, 1), :], (16, tm))
            out.append(cnts[hd] + jnp.where(gm > g, 1, jnp.where(gm == g, tie, 0)))
        return tuple(out)

    cnts = lax.fori_loop(0, blk0 + nb - 1, rank_step,
                         tuple(jnp.zeros((16, tm), jnp.int32) for _ in range(GROUP_HEADS)))
    zeros48 = jnp.zeros((HEAD_DIM - 16, tm), F32)
    pair_rows = []
    for hd in range(GROUP_HEADS):
        sel = (valid & (cnts[hd] < MOBA_TOPK)) | (n_idx == tok_blk)
        pair_rows += [jnp.where(sel, 0.0, NEG_BIG), zeros48]
    store_tiles(bias_o, jnp.concatenate(pair_rows, axis=0))

    store_tiles(qf_o, q_feature_major(wqf_ref, gqf_ref, False))
    kf_o[0] = k_token_major(wkf_ref, gkf_ref, False).astype(BF16)
    store_tiles(vf_o, _dot_nt(wvf_ref[...], hb), KEY_TILE)

    f = _dot(hb, wf_ref[...]) + bf_ref[...]
    logf = jnp.minimum(f, 0.0) - jnp.log1p(jnp.exp(-jnp.abs(f)))
    live = lax.broadcasted_iota(jnp.int32, (1, LANES), 1) < GROUP_HEADS
    logf = jnp.where(live, logf, 0.0)
    r_i = lax.broadcasted_iota(jnp.int32, (CUM_BLOCK, CUM_BLOCK), 0)
    c_i = lax.broadcasted_iota(jnp.int32, (CUM_BLOCK, CUM_BLOCK), 1)
    tri = jnp.where(r_i >= c_i, 1.0, 0.0).astype(BF16)
    n_blk = tm // CUM_BLOCK
    wide = jnp.concatenate([logf[b * CUM_BLOCK:(b + 1) * CUM_BLOCK, :] for b in range(n_blk)],
                           axis=1)
    parts = _split3(wide)
    prefix = _dot(tri, parts[0]) + _dot(tri, parts[1]) + _dot(tri, parts[2])
    carry = carry_sc[...]
    cums = []
    for blk in range(n_blk):
        c_blk = prefix[:, blk * LANES:(blk + 1) * LANES] + carry
        carry = c_blk[CUM_BLOCK - 1:CUM_BLOCK, :]
        cums.append(c_blk)
    carry_sc[...] = carry
    nc = jnp.concatenate(cums, axis=0) * (-LOG2E)
    hi, mid, lo = [part.astype(F32) for part in _split3(nc)]
    packed = hi + pltpu.roll(mid, GROUP_HEADS, 1) + pltpu.roll(lo, 2 * GROUP_HEADS, 1)
    ka_o[0] = (_dot(packed.astype(BF16), place_ref[...]) + kones_ref[...]).astype(BF16)

    nc_t = nc.T[0:GROUP_HEADS, :]
    hi_t = nc_t.astype(BF16).astype(F32)
    rem_t = nc_t - hi_t
    mid_t = rem_t.astype(BF16).astype(F32)
    lo_t = (rem_t - mid_t).astype(BF16).astype(F32)
    r8 = lax.broadcasted_iota(jnp.int32, (8, tm), 0)
    zeros56 = jnp.zeros((HEAD_DIM - 8, tm), F32)
    qa_rows = []
    for hd in range(GROUP_HEADS):
        blk = jnp.where(r8 < 3, 1.0,
                        jnp.where(r8 == 3, -hi_t[hd:hd + 1, :],
                                  jnp.where(r8 == 4, -mid_t[hd:hd + 1, :],
                                            jnp.where(r8 == 5, -lo_t[hd:hd + 1, :], 0.0))))
        qa_rows += [blk, zeros56]
    store_tiles(qa_o, jnp.concatenate(qa_rows, axis=0))


def _inproj(x, sc, sh, wts, consts, tm):
    bsz, s, d = x.shape
    nt = s // ATT_TILE
    grid = (bsz, s // tm)
    wqm, wkm, wvm, wqf, wkf, wvf, wf, bfor, gqm, gkm, gqf, gkf = wts
    cos_t, sin_t, rc, rs1, rs2, jmat, place, kones = consts

    def const(a):
        nd = a.ndim
        return pl.BlockSpec(a.shape, lambda b, i, _n=nd: (0,) * _n,
                            pipeline_mode=pl.Buffered(1))

    mod_spec = pl.BlockSpec((None, 1, d), lambda b, i: (b, 0, 0))
    in_specs = [pl.BlockSpec((1, tm, d), lambda b, i: (b, i, 0)), mod_spec, mod_spec]
    in_specs += [const(a) for a in (wqm, wkm, wvm, wqf, wkf, wvf, wf, bfor,
                                    gqm, gkm, gqf, gkf)]
    in_specs += [pl.BlockSpec((ROPE_HALF, tm), lambda b, i: (0, i)),
                 pl.BlockSpec((ROPE_HALF, tm), lambda b, i: (0, i)),
                 pl.BlockSpec((tm, LANES), lambda b, i: (i, 0)),
                 pl.BlockSpec((tm, LANES), lambda b, i: (i, 0)),
                 pl.BlockSpec((tm, LANES), lambda b, i: (i, 0)),
                 const(jmat), const(place), const(kones)]

    tiles = tm // ATT_TILE
    feat_spec = pl.BlockSpec((1, tiles, GROUP_WIDTH, ATT_TILE), lambda b, i: (b, i, 0, 0))
    tok_spec = pl.BlockSpec((1, tm, GROUP_WIDTH), lambda b, i: (b, i, 0))
    feat_shape = jax.ShapeDtypeStruct((bsz, nt, GROUP_WIDTH, ATT_TILE), BF16)
    tok_shape = jax.ShapeDtypeStruct((bsz, s, GROUP_WIDTH), BF16)
    val_spec = pl.BlockSpec((1, tm // KEY_TILE, GROUP_WIDTH, KEY_TILE),
                            lambda b, i: (b, i, 0, 0))
    val_shape = jax.ShapeDtypeStruct((bsz, s // KEY_TILE, GROUP_WIDTH, KEY_TILE), BF16)
    out_specs = [feat_spec, tok_spec, val_spec, feat_spec,
                 feat_spec, tok_spec, val_spec, tok_spec, feat_spec]
    out_shape = [feat_shape, tok_shape, val_shape, feat_shape,
                 feat_shape, tok_shape, val_shape, tok_shape, feat_shape]
    return pl.pallas_call(
        functools.partial(_inproj_kernel, tm=tm),
        grid=grid,
        in_specs=in_specs,
        out_specs=out_specs,
        out_shape=out_shape,
        scratch_shapes=[pltpu.VMEM((GROUP_HEADS * 16, GROUP_WIDTH), F32),
                        pltpu.VMEM((1, LANES), F32),
                        pltpu.VMEM((GROUP_HEADS * 16, tm), F32)],
        compiler_params=pltpu.CompilerParams(
            dimension_semantics=("arbitrary", "arbitrary"),
            vmem_limit_bytes=VMEM_LIMIT),
        name="inproj",
    )(x, sc, sh, wqm, wkm, wvm, wqf, wkf, wvf, wf, bfor, gqm, gkm, gqf, gkf,
      cos_t, sin_t, rc, rs1, rs2, jmat, place, kones)


def _attn_kernel(q_ref, qa_ref, k_ref, ka_ref, v_ref, o_ref, *, t):
    i = pl.program_id(2)
    q = q_ref[0, 0]
    qa = qa_ref[0, 0]
    row = lax.broadcasted_iota(jnp.int32, (PAIR_WIDTH, t), 0)
    rhs = []
    for e in range(2):
        keep = (row >= e * HEAD_DIM) & (row < (e + 1) * HEAD_DIM)
        zero = jnp.zeros_like(q)
        rhs.append(jnp.concatenate([jnp.where(keep, q, zero), jnp.where(keep, qa, zero)],
                                   axis=0))

    def tile_scores(j):
        start = pl.multiple_of(j * t, t)
        lhs = jnp.concatenate([k_ref[0, pl.ds(start, t), :], ka_ref[0, pl.ds(start, t), :]],
                              axis=1)
        return [_dot(lhs, rhs[e]) for e in range(2)]

    def values(j, e):
        rows = slice(e * HEAD_DIM, (e + 1) * HEAD_DIM)
        slabs = t // KEY_TILE
        return jnp.concatenate([v_ref[0, slabs * j + h, rows, :] for h in range(slabs)],
                               axis=1)

    key_i = lax.broadcasted_iota(jnp.int32, (t, t), 0)
    qry_i = lax.broadcasted_iota(jnp.int32, (t, t), 1)
    causal = key_i <= qry_i
    state = []
    for e, s_t in enumerate(tile_scores(i)):
        s_t = jnp.where(causal, s_t, NEG_BIG)
        m = jnp.max(s_t, axis=0, keepdims=True)
        p = jnp.exp2(s_t - m)
        l = jnp.sum(p, axis=0, keepdims=True)
        acc = _dot(values(i, e), p.astype(BF16))
        state += [m, l, acc]

    def body(j, carry):
        out = []
        for e, s_t in enumerate(tile_scores(j)):
            m, l, acc = carry[3 * e:3 * e + 3]
            m_new = jnp.maximum(m, jnp.max(s_t, axis=0, keepdims=True))
            alpha = jnp.exp2(m - m_new)
            p = jnp.exp2(s_t - m_new)
            l = alpha * l + jnp.sum(p, axis=0, keepdims=True)
            acc = alpha * acc + _dot(values(j, e), p.astype(BF16))
            out += [m_new, l, acc]
        return tuple(out)

    state = lax.fori_loop(0, i, body, tuple(state))
    o_t = jnp.concatenate([state[2] / state[1], state[5] / state[4]], axis=0)
    o_ref[0] = o_t.T


def _attn_bounded_kernel(q_ref, qa_ref, k_ref, ka_ref, v_ref, o_ref, p_sc, acc_sc, l_sc, *, t,
                         ka_const):
    i = pl.program_id(1)
    kt = KEY_TILE
    row = lax.broadcasted_iota(jnp.int32, (PAIR_WIDTH, t), 0)
    key_i = lax.broadcasted_iota(jnp.int32, (kt, t), 0)
    qry_i = lax.broadcasted_iota(jnp.int32, (kt, t), 1)
    assert t == 2 * kt


    def rhs_of(pr):
        feat = slice(pr * PAIR_WIDTH, (pr + 1) * PAIR_WIDTH)
        q = q_ref[0, 0, feat, :]
        qa = qa_ref[0, 0, feat, :]
        out = []
        for e in range(2):
            keep = (row >= e * HEAD_DIM) & (row < (e + 1) * HEAD_DIM)
            zero = jnp.zeros_like(q)
            out.append(jnp.concatenate([jnp.where(keep, q, zero), jnp.where(keep, qa, zero)],
                                       axis=0))
        return out

    for grp in range(N_PAIRS // PAIRS_PER_TRIP):
        prs = list(range(grp * PAIRS_PER_TRIP, (grp + 1) * PAIRS_PER_TRIP))
        rhs = {pr: rhs_of(pr) for pr in prs}

        def probs(pr, start, n_keys, row0, mask=None, q0=0, first=False, rhs=rhs):
            feat = slice(pr * PAIR_WIDTH, (pr + 1) * PAIR_WIDTH)
            k_aug = (ka_ref[0, pl.ds(start, n_keys), :] if ka_const
                     else ka_ref[0, pl.ds(start, n_keys), feat])
            lhs = jnp.concatenate([k_ref[0, pl.ds(start, n_keys), feat], k_aug], axis=1)
            for e in range(2):
                s_t = _dot(lhs, rhs[pr][e][:, q0:])
                if mask is not None:
                    s_t = jnp.where(mask[:, q0:], s_t, NEG_BIG)
                p = jnp.exp2(s_t)
                p_sc[pr, e, row0:row0 + n_keys, q0:] = p.astype(BF16)
                part = p[0:8, :]
                for r in range(8, n_keys, 8):
                    part = part + p[r:r + 8, :]
                if first:
                    l_sc[pr, e] = part
                else:
                    l_sc[pr, e, :, q0:] += part

        def values(pr, j):
            rows = slice(pr * PAIR_WIDTH, (pr + 1) * PAIR_WIDTH)
            v_pair = jnp.concatenate([v_ref[0, 2 * j, rows, :], v_ref[0, 2 * j + 1, rows, :]],
                                     axis=1)
            for e in range(2):
                upd = _dot(v_pair, p_sc[pr, e])
                acc_sc[pr, e] += upd[e * HEAD_DIM:(e + 1) * HEAD_DIM, :]

        diag = pl.multiple_of(i * t, t)
        for pr in prs:
            probs(pr, diag, kt, 0, key_i <= qry_i, first=True)
            probs(pr, diag + kt, kt, kt, key_i + kt <= qry_i, q0=kt)
            for e in range(2):
                p_sc[pr, e, kt:, :kt] = jnp.zeros((kt, kt), BF16)
                acc_sc[pr, e] = jnp.zeros((HEAD_DIM, t), F32)

        def body(j, j_prev, probs=probs, values=values, prs=prs):
            for pr in prs:
                values(pr, j_prev)
                probs(pr, pl.multiple_of(j * t, t), t, 0)
            return j

        def unrolled(n, body=body):
            def trip(m, j_prev):
                for u in range(n):
                    j_prev = body(n * m + u, j_prev)
                return j_prev
            return trip

        done = 0
        j_prev = i
        for n in (4, 2, 1):
            trips = (i - done) // n
            j_prev = lax.fori_loop(done // n, done // n + trips, unrolled(n), j_prev)
            done = done + trips * n
        for pr in prs:
            values(pr, j_prev)
        for pr in prs:
            o_t = jnp.concatenate(
                [acc_sc[pr, e] / jnp.sum(l_sc[pr, e], axis=0, keepdims=True)
                 for e in range(2)], axis=0)
            o_ref[0, :, pr * PAIR_WIDTH:(pr + 1) * PAIR_WIDTH] = o_t.T


def _attention_online(q_t, qa, k, ka, v_t, *, ka_const, name):
    bsz, nt, _, t = q_t.shape
    s = k.shape[1]
    grid = (bsz, N_PAIRS, nt)
    if ka_const:
        ka_spec = pl.BlockSpec((1, s, PAIR_WIDTH), lambda b, p, i: (0, 0, 0))
    else:
        ka_spec = pl.BlockSpec((1, s, PAIR_WIDTH), lambda b, p, i: (b, 0, p))
    return pl.pallas_call(
        functools.partial(_attn_kernel, t=t),
        grid=grid,
        in_specs=[pl.BlockSpec((1, 1, PAIR_WIDTH, t), lambda b, p, i: (b, i, p, 0)),
                  pl.BlockSpec((1, 1, PAIR_WIDTH, t), lambda b, p, i: (b, i, p, 0)),
                  pl.BlockSpec((1, s, PAIR_WIDTH), lambda b, p, i: (b, 0, p)),
                  ka_spec,
                  pl.BlockSpec((1, s // KEY_TILE, PAIR_WIDTH, KEY_TILE),
                               lambda b, p, i: (b, 0, p, 0))],
        out_specs=pl.BlockSpec((1, t, PAIR_WIDTH), lambda b, p, i: (b, i, p)),
        out_shape=jax.ShapeDtypeStruct((bsz, s, GROUP_WIDTH), F32),
        compiler_params=pltpu.CompilerParams(
            dimension_semantics=("arbitrary", "arbitrary", "arbitrary"),
            vmem_limit_bytes=VMEM_LIMIT),
        name=name,
    )(q_t, qa, k, ka, v_t)


def _attention_bounded(q_t, qa, k, ka, v_t, *, ka_const, name):
    bsz, nt, _, t = q_t.shape
    s = k.shape[1]
    feat_spec = pl.BlockSpec((1, 1, GROUP_WIDTH, t), lambda b, i: (b, i, 0, 0))
    tok_spec = pl.BlockSpec((1, s, GROUP_WIDTH), lambda b, i: (b, 0, 0))
    ka_spec = pl.BlockSpec((1, s, PAIR_WIDTH), lambda b, i: (0, 0, 0)) if ka_const else tok_spec
    return pl.pallas_call(
        functools.partial(_attn_bounded_kernel, t=t, ka_const=ka_const),
        grid=(bsz, nt),
        in_specs=[feat_spec, feat_spec, tok_spec, ka_spec,
                  pl.BlockSpec((1, s // KEY_TILE, GROUP_WIDTH, KEY_TILE),
                               lambda b, i: (b, 0, 0, 0))],
        out_specs=pl.BlockSpec((1, t, GROUP_WIDTH), lambda b, i: (b, i, 0)),
        out_shape=jax.ShapeDtypeStruct((bsz, s, GROUP_WIDTH), F32),
        scratch_shapes=[pltpu.VMEM((N_PAIRS, 2, t, t), BF16),
                        pltpu.VMEM((N_PAIRS, 2, HEAD_DIM, t), F32),
                        pltpu.VMEM((N_PAIRS, 2, 8, t), F32)],
        compiler_params=pltpu.CompilerParams(
            dimension_semantics=("arbitrary", "arbitrary"),
            vmem_limit_bytes=VMEM_LIMIT),
        name=name,
    )(q_t, qa, k, ka, v_t)


def _attention_dispatch(g_q, g_k, q_t, qa, k, ka, v_t, *, ka_const, name):
    bound = HEAD_DIM * jnp.max(jnp.abs(g_q)) * jnp.max(jnp.abs(g_k))
    return lax.cond(
        bound <= MAX_SAFE_EXPONENT,
        functools.partial(_attention_bounded, ka_const=ka_const, name=name + "_bounded"),
        functools.partial(_attention_online, ka_const=ka_const, name=name + "_online"),
        q_t, qa, k, ka, v_t)


def _post_kernel(x_ref, om_ref, of_ref, ga_ref, scm_ref, shm_ref, gm_ref,
                 gom_ref, gof_ref, wout_ref, w1_ref, w2_ref, o_ref, *, ff_chunk):
    def rms(v):
        return v * lax.rsqrt(jnp.mean(v * v, axis=-1, keepdims=True) + EPS)

    mixed = jnp.concatenate([rms(om_ref[0]) * gom_ref[...],
                             rms(of_ref[0]) * gof_ref[...]], axis=1)
    x1 = x_ref[0] + ga_ref[...] * _dot(mixed, wout_ref[...])
    h = rms(x1) * (1.0 + scm_ref[...]) + shm_ref[...]
    d_ff = w1_ref.shape[1]
    y = None
    for c in range(d_ff // ff_chunk):
        hid = _dot(h, w1_ref[:, c * ff_chunk:(c + 1) * ff_chunk])
        hid = jnp.square(jnp.maximum(hid, 0.0))
        part = _dot(hid, w2_ref[c * ff_chunk:(c + 1) * ff_chunk, :])
        y = part if y is None else y + part
    o_ref[0] = x1 + gm_ref[...] * y


def _post(x, o_m, o_f, mod4, g_out_m, g_out_f, w_out, w_ff1, w_ff2, tm):
    bsz, s, d = x.shape
    d_ff = w_ff1.shape[1]

    def const(a):
        nd = a.ndim
        return pl.BlockSpec(a.shape, lambda b, i, _n=nd: (0,) * _n,
                            pipeline_mode=pl.Buffered(1))

    def mod_spec(k):
        return pl.BlockSpec((None, None, 1, d), lambda b, i, _k=k: (b, _k, 0, 0))

    tok = lambda w: pl.BlockSpec((1, tm, w), lambda b, i: (b, i, 0))
    return pl.pallas_call(
        functools.partial(_post_kernel, ff_chunk=1024),
        grid=(bsz, s // tm),
        in_specs=[tok(d), tok(GROUP_WIDTH), tok(GROUP_WIDTH),
                  mod_spec(2), mod_spec(4), mod_spec(3), mod_spec(5),
                  const(g_out_m), const(g_out_f), const(w_out), const(w_ff1), const(w_ff2)],
        out_specs=tok(d),
        out_shape=jax.ShapeDtypeStruct((bsz, s, d), F32),
        compiler_params=pltpu.CompilerParams(
            dimension_semantics=("arbitrary", "arbitrary"),
            vmem_limit_bytes=POST_VMEM_LIMIT),
        name="post",
    )(x, o_m, o_f, mod4, mod4, mod4, mod4, g_out_m, g_out_f, w_out, w_ff1, w_ff2)


def _rope_tables(s):
    inv_freq = ROPE_THETA ** (-np.arange(0, ROPE_DIM, 2, dtype=np.float64) / ROPE_DIM)
    ang = np.arange(s, dtype=np.float64)[:, None] * inv_freq[None, :]
    cos, sin = np.cos(ang), np.sin(ang)
    rc = np.ones((s, LANES)); rs1 = np.zeros((s, LANES)); rs2 = np.zeros((s, LANES))
    for base in (0, HEAD_DIM):
        rc[:, base:base + ROPE_HALF] = cos
        rc[:, base + ROPE_HALF:base + ROPE_DIM] = cos
        rs2[:, base:base + ROPE_HALF] = -sin
        rs1[:, base + ROPE_HALF:base + ROPE_DIM] = sin
    f = lambda a: jnp.asarray(a, dtype=F32)
    return f(cos.T), f(sin.T), f(rc), f(rs1), f(rs2)


def _static_mats(s):
    idx = np.arange(256)
    jmat = (idx[:, None] // HEAD_DIM == idx[None, :] // HEAD_DIM) / HEAD_DIM
    place = np.zeros((LANES, GROUP_WIDTH))
    for part in range(3):
        for hd in range(GROUP_HEADS):
            place[part * GROUP_HEADS + hd,
                  (hd // 2) * PAIR_WIDTH + (hd % 2) * HEAD_DIM + part] = 1.0
    kones = np.zeros((1, GROUP_WIDTH))
    for hd in range(GROUP_HEADS):
        kones[0, hd * HEAD_DIM + 3:hd * HEAD_DIM + 6] = 1.0
    ka_moba = np.zeros((1, s, PAIR_WIDTH))
    blk = np.arange(s) // MOBA_BLOCK
    for e in range(2):
        ka_moba[0, np.arange(s), e * HEAD_DIM + blk] = 1.0
    b = lambda a: jnp.asarray(a, dtype=BF16)
    return b(jmat), b(place), jnp.asarray(kones, dtype=F32), b(ka_moba)


def kernel(x, c, w_ada, b_ada, w_in, b_forget, g_qn_moba, g_kn_moba, g_qn_fox, g_kn_fox,
           g_out_moba, g_out_fox, w_out, w_ff1, w_ff2):
    bsz, s, d = x.shape
    depth = w_ada.shape[0]
    w = GROUP_WIDTH
    assert s % MOBA_BLOCK == 0 and s // MOBA_BLOCK <= 16
    tm_in = 1024
    tm_post = 512
    cos_t, sin_t, rc, rs1, rs2 = _rope_tables(s)
    jmat, place, kones, ka_moba = _static_mats(s)
    q_scale = HEAD_DIM ** -0.5 * LOG2E

    for l in range(depth):
        mod = _adaln(c, w_ada[l], b_ada[l])
        mod4 = mod.reshape(bsz, 6, 1, d)

        wl = w_in[l]
        wf =jnp.zeros((d, LANES), F32).at[:, :GROUP_HEADS].set(wl[:, 6 * w:]).astype(BF16)
        bfor = jnp.zeros((1, LANES), F32).at[0, :GROUP_HEADS].set(b_forget[l])
        q_gain = lambda g: jnp.broadcast_to((g * q_scale)[:, None], (HEAD_DIM, tm_in))
        k_gain = lambda g: jnp.tile(g, GROUP_HEADS).reshape(1, w)
        wts = _project_weights(w_in, l, w) + (
               wf, bfor,
               q_gain(g_qn_moba[l]), k_gain(g_kn_moba[l]),
               q_gain(g_qn_fox[l]), k_gain(g_kn_fox[l]))
        consts = (cos_t, sin_t, rc, rs1, rs2, jmat, place, kones)

        qm_t, km, vm_t, bias_t, qf_t, kf, vf_t, ka_f, qa_f = _inproj(
            x, mod4[:, 1], mod4[:, 0], wts, consts, tm_in)

        o_m = _attention_dispatch(g_qn_moba[l] * q_scale, g_kn_moba[l],
                                  qm_t, bias_t, km, ka_moba, vm_t, ka_const=True,
                                  name="attn_moba")
        o_f = _attention_dispatch(g_qn_fox[l] * q_scale, g_kn_fox[l],
                                  qf_t, qa_f, kf, ka_f, vf_t, ka_const=False,
                                  name="attn_fox")

        x = _post(x, o_m, o_f, mod4,
                  g_out_moba[l].reshape(1, w), g_out_fox[l].reshape(1, w),
                  w_out[l], w_ff1[l], w_ff2[l],
                  tm_post)
    return x
```

```python
import functools
import math

import numpy as np
import jax
import jax.numpy as jnp
from jax import lax
from jax.experimental import pallas as pl
from jax.experimental.pallas import tpu as pltpu

F32 = jnp.float32
BF16 = jnp.bfloat16

HEAD_DIM = 64
GROUP_HEADS = 8
GROUP_WIDTH = GROUP_HEADS * HEAD_DIM
PAIR_WIDTH = 2 * HEAD_DIM
N_PAIRS = GROUP_HEADS // 2
MOBA_BLOCK = 256
MOBA_TOPK = 3
ROPE_THETA = 500000.0
ROPE_DIM = HEAD_DIM // 4
ROPE_HALF = ROPE_DIM // 2
EPS = 1e-6
LOG2E = math.log2(math.e)
NEG_BIG = -1e30
MAX_SAFE_EXPONENT = 60.0
ATT_TILE = 512
KEY_TILE = 256
CUM_BLOCK = 128
PAIRS_PER_TRIP = 4
LANES = 128
VMEM_LIMIT = 48 * 1024 * 1024
POST_VMEM_LIMIT = 58 * 1024 * 1024


def _dot(a, b):
    return jnp.dot(a, b, preferred_element_type=F32)


def _split3(x):
    hi = x.astype(BF16)
    rem = x - hi.astype(F32)
    mid = rem.astype(BF16)
    lo = (rem - mid.astype(F32)).astype(BF16)
    return hi, mid, lo


def _dot_nt(a, b):
    return lax.dot_general(a, b, (((1,), (1,)), ((), ())), preferred_element_type=F32)


def _adaln_kernel(ct_ref, w_ref, b_ref, o_ref):
    c_t = ct_ref[...]
    s_t = c_t / (1.0 + jnp.exp(-c_t))
    w = w_ref[...]
    rows = [jnp.sum(s_t[:, b:b + 1] * w, axis=0, keepdims=True) for b in range(c_t.shape[1])]
    o_ref[...] = jnp.concatenate(rows, axis=0) + b_ref[...]


def _adaln(c, w_ada, b_ada):
    bsz, d = c.shape
    n = w_ada.shape[1]
    return pl.pallas_call(
        _adaln_kernel,
        grid=(n // d,),
        in_specs=[pl.BlockSpec((d, bsz), lambda j: (0, 0)),
                  pl.BlockSpec((d, d), lambda j: (0, j)),
                  pl.BlockSpec((1, d), lambda j: (0, j))],
        out_specs=pl.BlockSpec((bsz, d), lambda j: (0, j)),
        out_shape=jax.ShapeDtypeStruct((bsz, n), F32),
        compiler_params=pltpu.CompilerParams(dimension_semantics=("arbitrary",)),
        name="adaln",
    )(c.T, w_ada, b_ada.reshape(1, n))


TRANSPOSED_GROUPS = (0, 2, 3, 5)


def _wprep_kernel(w_ref, *o_refs):
    g = pl.program_id(0)
    for k, o_ref in enumerate(o_refs):
        @pl.when(g == k)
        def _(k=k, o_ref=o_ref):
            w = w_ref[...]
            o_ref[...] = (w.T if k in TRANSPOSED_GROUPS else w).astype(BF16)


def _project_weights(w_in, layer, width):
    d = w_in.shape[1]
    shapes = [(width, d) if k in TRANSPOSED_GROUPS else (d, width) for k in range(6)]
    return tuple(pl.pallas_call(
        _wprep_kernel,
        grid=(6,),
        in_specs=[pl.BlockSpec((None, d, width), lambda g: (layer, 0, g))],
        out_specs=[pl.BlockSpec(s, lambda g: (0, 0)) for s in shapes],
        out_shape=[jax.ShapeDtypeStruct(s, BF16) for s in shapes],
        compiler_params=pltpu.CompilerParams(dimension_semantics=("arbitrary",)),
        name="wprep",
    )(w_in))


def _inproj_kernel(x_ref, sc_ref, sh_ref,
                   wqm_ref, wkm_ref, wvm_ref, wqf_ref, wkf_ref, wvf_ref, wf_ref, bf_ref,
                   gqm_ref, gkm_ref, gqf_ref, gkf_ref,
                   cos_t_ref, sin_t_ref, rc_ref, rs1_ref, rs2_ref,
                   j_ref, place_ref, kones_ref,
                   qm_o, km_o, vm_o, bias_o, qf_o, kf_o, vf_o, ka_o, qa_o,
                   kmean_sc, carry_sc, gate_sc, *, tm):
    i = pl.program_id(1)
    nb = tm // MOBA_BLOCK

    @pl.when(i == 0)
    def _():
        kmean_sc[...] = jnp.zeros_like(kmean_sc)
        carry_sc[...] = jnp.zeros_like(carry_sc)

    x = x_ref[0]
    ms = jnp.mean(x * x, axis=-1, keepdims=True)
    h = x * lax.rsqrt(ms + EPS) * (1.0 + sc_ref[...]) + sh_ref[...]
    hb = h.astype(BF16)

    def q_feature_major(w_ref, g_ref, rope):
        q_t = _dot_nt(w_ref[...], hb)
        gain = g_ref[...]
        heads = []
        for hd in range(GROUP_HEADS):
            xh = q_t[hd * HEAD_DIM:(hd + 1) * HEAD_DIM, :]
            ss = jnp.sum(xh * xh, axis=0, keepdims=True)
            xh = xh * lax.rsqrt(ss * (1.0 / HEAD_DIM) + EPS) * gain
            if rope:
                x1 = xh[0:ROPE_HALF, :]
                x2 = xh[ROPE_HALF:ROPE_DIM, :]
                cs = cos_t_ref[...]
                sn = sin_t_ref[...]
                xh = jnp.concatenate(
                    [x1 * cs - x2 * sn, x2 * cs + x1 * sn, xh[ROPE_DIM:, :]], axis=0)
            heads.append(xh)
        return jnp.concatenate(heads, axis=0)

    def k_token_major(w_ref, g_ref, rope):
        k = _dot(hb, w_ref[...])
        chunks = []
        for c in range(GROUP_WIDTH // 256):
            seg = k[:, c * 256:(c + 1) * 256]
            msq = _dot((seg * seg).astype(BF16), j_ref[...])
            chunks.append(seg * lax.rsqrt(msq + EPS) * g_ref[:, c * 256:(c + 1) * 256])
        k = jnp.concatenate(chunks, axis=1)
        if rope:
            outs = []
            for c in range(GROUP_WIDTH // LANES):
                seg = k[:, c * LANES:(c + 1) * LANES]
                outs.append(seg * rc_ref[...]
                            + pltpu.roll(seg, ROPE_HALF, 1) * rs1_ref[...]
                            + pltpu.roll(seg, LANES - ROPE_HALF, 1) * rs2_ref[...])
            k = jnp.concatenate(outs, axis=1)
        return k

    def store_tiles(o_ref, val_t, width=ATT_TILE):
        for t in range(tm // width):
            o_ref[0, t] = val_t[:, t * width:(t + 1) * width].astype(o_ref.dtype)

    qm_t = q_feature_major(wqm_ref, gqm_ref, True)
    store_tiles(qm_o, qm_t)
    km = k_token_major(wkm_ref, gkm_ref, True)
    km_o[0] = km.astype(BF16)
    store_tiles(vm_o, _dot_nt(wvm_ref[...], hb), KEY_TILE)

    blk0 = i * nb
    lane_head = lax.broadcasted_iota(jnp.int32, (1, GROUP_WIDTH), 1) // HEAD_DIM
    for sb in range(nb):
        kmean = jnp.mean(km[sb * MOBA_BLOCK:(sb + 1) * MOBA_BLOCK, :], axis=0, keepdims=True)
        for hd in range(GROUP_HEADS):
            row = hd * 16 + blk0 + sb
            kmean_sc[pl.ds(row, 1), :] = jnp.where(lane_head == hd, kmean, 0.0)

    km_hi, km_lo, _ = _split3(kmean_sc[...])
    q_hi, q_lo, _ = _split3(qm_t)
    gate_t = _dot(km_hi, q_hi) + _dot(km_hi, q_lo) + _dot(km_lo, q_hi)
    tok_blk = blk0 + lax.broadcasted_iota(jnp.int32, (16, tm), 1) // MOBA_BLOCK
    n_idx = lax.broadcasted_iota(jnp.int32, (16, tm), 0)
    valid = n_idx < tok_blk
    for hd in range(GROUP_HEADS):
        gate_sc[hd * 16:(hd + 1) * 16, :] = jnp.where(
            valid, gate_t[hd * 16:(hd + 1) * 16, :], -jnp.inf)

    store_tiles(qf_o, q_feature_major(wqf_ref, gqf_ref, False))
    kf_o[0] = k_token_major(wkf_ref, gkf_ref, False).astype(BF16)
    store_tiles(vf_o, _dot_nt(wvf_ref[...], hb), KEY_TILE)

    f = _dot(hb, wf_ref[...]) + bf_ref[...]
    logf = jnp.minimum(f, 0.0) - jnp.log1p(jnp.exp(-jnp.abs(f)))
    live = lax.broadcasted_iota(jnp.int32, (1, LANES), 1) < GROUP_HEADS
    logf = jnp.where(live, logf, 0.0)
    r_i = lax.broadcasted_iota(jnp.int32, (CUM_BLOCK, CUM_BLOCK), 0)
    c_i = lax.broadcasted_iota(jnp.int32, (CUM_BLOCK, CUM_BLOCK), 1)
    tri = jnp.where(r_i >= c_i, 1.0, 0.0).astype(BF16)
    n_blk = tm // CUM_BLOCK
    wide = jnp.concatenate([logf[b * CUM_BLOCK:(b + 1) * CUM_BLOCK, :] for b in range(n_blk)],
                           axis=1)
    parts = _split3(wide)
    prefix = _dot(tri, parts[0]) + _dot(tri, parts[1]) + _dot(tri, parts[2])
    carry = carry_sc[...]
    cums = []
    for blk in range(n_blk):
        c_blk = prefix[:, blk * LANES:(blk + 1) * LANES] + carry
        carry = c_blk[CUM_BLOCK - 1:CUM_BLOCK, :]
        cums.append(c_blk)
    carry_sc[...] = carry
    nc = jnp.concatenate(cums, axis=0) * (-LOG2E)
    hi, mid, lo = [part.astype(F32) for part in _split3(nc)]
    packed = hi + pltpu.roll(mid, GROUP_HEADS, 1) + pltpu.roll(lo, 2 * GROUP_HEADS, 1)
    ka_o[0] = (_dot(packed.astype(BF16), place_ref[...]) + kones_ref[...]).astype(BF16)

    nc_t = nc.T[0:GROUP_HEADS, :]
    hi_t = nc_t.astype(BF16).astype(F32)
    rem_t = nc_t - hi_t
    mid_t = rem_t.astype(BF16).astype(F32)
    lo_t = (rem_t - mid_t).astype(BF16).astype(F32)
    r8 = lax.broadcasted_iota(jnp.int32, (8, tm), 0)
    zeros56 = jnp.zeros((HEAD_DIM - 8, tm), F32)
    qa_rows = []
    for hd in range(GROUP_HEADS):
        blk = jnp.where(r8 < 3, 1.0,
                        jnp.where(r8 == 3, -hi_t[hd:hd + 1, :],
                                  jnp.where(r8 == 4, -mid_t[hd:hd + 1, :],
                                            jnp.where(r8 == 5, -lo_t[hd:hd + 1, :], 0.0))))
        qa_rows += [blk, zeros56]
    store_tiles(qa_o, jnp.concatenate(qa_rows, axis=0))

    def rank_step(m, cnts):
        tie = jnp.where(n_idx > m, 1, 0)
        out = []
        for hd in range(GROUP_HEADS):
            g = gate_sc[hd * 16:(hd + 1) * 16, :]
            gm = jnp.broadcast_to(gate_sc[pl.ds(hd * 16 + m, 1), :], (16, tm))
            out.append(cnts[hd] + jnp.where(gm > g, 1, jnp.where(gm == g, tie, 0)))
        return tuple(out)

    cnts = lax.fori_loop(0, blk0 + nb - 1, rank_step,
                         tuple(jnp.zeros((16, tm), jnp.int32) for _ in range(GROUP_HEADS)))
    zeros48 = jnp.zeros((HEAD_DIM - 16, tm), F32)
    pair_rows = []
    for hd in range(GROUP_HEADS):
        sel = (valid & (cnts[hd] < MOBA_TOPK)) | (n_idx == tok_blk)
        pair_rows += [jnp.where(sel, 0.0, NEG_BIG), zeros48]
    store_tiles(bias_o, jnp.concatenate(pair_rows, axis=0))


def _inproj(x, sc, sh, wts, consts, tm):
    bsz, s, d = x.shape
    nt = s // ATT_TILE
    grid = (bsz, s // tm)
    wqm, wkm, wvm, wqf, wkf, wvf, wf, bfor, gqm, gkm, gqf, gkf = wts
    cos_t, sin_t, rc, rs1, rs2, jmat, place, kones = consts

    def const(a):
        nd = a.ndim
        return pl.BlockSpec(a.shape, lambda b, i, _n=nd: (0,) * _n,
                            pipeline_mode=pl.Buffered(1))

    mod_spec = pl.BlockSpec((None, 1, d), lambda b, i: (b, 0, 0))
    in_specs = [pl.BlockSpec((1, tm, d), lambda b, i: (b, i, 0)), mod_spec, mod_spec]
    in_specs += [const(a) for a in (wqm, wkm, wvm, wqf, wkf, wvf, wf, bfor,
                                    gqm, gkm, gqf, gkf)]
    in_specs += [pl.BlockSpec((ROPE_HALF, tm), lambda b, i: (0, i)),
                 pl.BlockSpec((ROPE_HALF, tm), lambda b, i: (0, i)),
                 pl.BlockSpec((tm, LANES), lambda b, i: (i, 0)),
                 pl.BlockSpec((tm, LANES), lambda b, i: (i, 0)),
                 pl.BlockSpec((tm, LANES), lambda b, i: (i, 0)),
                 const(jmat), const(place), const(kones)]

    tiles = tm // ATT_TILE
    feat_spec = pl.BlockSpec((1, tiles, GROUP_WIDTH, ATT_TILE), lambda b, i: (b, i, 0, 0))
    tok_spec = pl.BlockSpec((1, tm, GROUP_WIDTH), lambda b, i: (b, i, 0))
    feat_shape = jax.ShapeDtypeStruct((bsz, nt, GROUP_WIDTH, ATT_TILE), BF16)
    tok_shape = jax.ShapeDtypeStruct((bsz, s, GROUP_WIDTH), BF16)
    val_spec = pl.BlockSpec((1, tm // KEY_TILE, GROUP_WIDTH, KEY_TILE),
                            lambda b, i: (b, i, 0, 0))
    val_shape = jax.ShapeDtypeStruct((bsz, s // KEY_TILE, GROUP_WIDTH, KEY_TILE), BF16)
    out_specs = [feat_spec, tok_spec, val_spec, feat_spec,
                 feat_spec, tok_spec, val_spec, tok_spec, feat_spec]
    out_shape = [feat_shape, tok_shape, val_shape, feat_shape,
                 feat_shape, tok_shape, val_shape, tok_shape, feat_shape]
    return pl.pallas_call(
        functools.partial(_inproj_kernel, tm=tm),
        grid=grid,
        in_specs=in_specs,
        out_specs=out_specs,
        out_shape=out_shape,
        scratch_shapes=[pltpu.VMEM((GROUP_HEADS * 16, GROUP_WIDTH), F32),
                        pltpu.VMEM((1, LANES), F32),
                        pltpu.VMEM((GROUP_HEADS * 16, tm), F32)],
        compiler_params=pltpu.CompilerParams(
            dimension_semantics=("arbitrary", "arbitrary"),
            vmem_limit_bytes=VMEM_LIMIT),
        name="inproj",
    )(x, sc, sh, wqm, wkm, wvm, wqf, wkf, wvf, wf, bfor, gqm, gkm, gqf, gkf,
      cos_t, sin_t, rc, rs1, rs2, jmat, place, kones)


def _attn_kernel(q_ref, qa_ref, k_ref, ka_ref, v_ref, o_ref, *, t):
    i = pl.program_id(2)
    q = q_ref[0, 0]
    qa = qa_ref[0, 0]
    row = lax.broadcasted_iota(jnp.int32, (PAIR_WIDTH, t), 0)
    rhs = []
    for e in range(2):
        keep = (row >= e * HEAD_DIM) & (row < (e + 1) * HEAD_DIM)
        zero = jnp.zeros_like(q)
        rhs.append(jnp.concatenate([jnp.where(keep, q, zero), jnp.where(keep, qa, zero)],
                                   axis=0))

    def tile_scores(j):
        start = pl.multiple_of(j * t, t)
        lhs = jnp.concatenate([k_ref[0, pl.ds(start, t), :], ka_ref[0, pl.ds(start, t), :]],
                              axis=1)
        return [_dot(lhs, rhs[e]) for e in range(2)]

    def values(j, e):
        rows = slice(e * HEAD_DIM, (e + 1) * HEAD_DIM)
        slabs = t // KEY_TILE
        return jnp.concatenate([v_ref[0, slabs * j + h, rows, :] for h in range(slabs)],
                               axis=1)

    key_i = lax.broadcasted_iota(jnp.int32, (t, t), 0)
    qry_i = lax.broadcasted_iota(jnp.int32, (t, t), 1)
    causal = key_i <= qry_i
    state = []
    for e, s_t in enumerate(tile_scores(i)):
        s_t = jnp.where(causal, s_t, NEG_BIG)
        m = jnp.max(s_t, axis=0, keepdims=True)
        p = jnp.exp2(s_t - m)
        l = jnp.sum(p, axis=0, keepdims=True)
        acc = _dot(values(i, e), p.astype(BF16))
        state += [m, l, acc]

    def body(j, carry):
        out = []
        for e, s_t in enumerate(tile_scores(j)):
            m, l, acc = carry[3 * e:3 * e + 3]
            m_new = jnp.maximum(m, jnp.max(s_t, axis=0, keepdims=True))
            alpha = jnp.exp2(m - m_new)
            p = jnp.exp2(s_t - m_new)
            l = alpha * l + jnp.sum(p, axis=0, keepdims=True)
            acc = alpha * acc + _dot(values(j, e), p.astype(BF16))
            out += [m_new, l, acc]
        return tuple(out)

    state = lax.fori_loop(0, i, body, tuple(state))
    o_t = jnp.concatenate([state[2] / state[1], state[5] / state[4]], axis=0)
    o_ref[0] = o_t.T


def _attn_bounded_kernel(q_ref, qa_ref, k_ref, ka_ref, v_ref, o_ref, p_sc, acc_sc, l_sc, *, t,
                         ka_const):
    i = pl.program_id(1)
    kt = KEY_TILE
    row = lax.broadcasted_iota(jnp.int32, (PAIR_WIDTH, t), 0)
    key_i = lax.broadcasted_iota(jnp.int32, (kt, t), 0)
    qry_i = lax.broadcasted_iota(jnp.int32, (kt, t), 1)
    assert t == 2 * kt


    def rhs_of(pr):
        feat = slice(pr * PAIR_WIDTH, (pr + 1) * PAIR_WIDTH)
        q = q_ref[0, 0, feat, :]
        qa = qa_ref[0, 0, feat, :]
        out = []
        for e in range(2):
            keep = (row >= e * HEAD_DIM) & (row < (e + 1) * HEAD_DIM)
            zero = jnp.zeros_like(q)
            out.append(jnp.concatenate([jnp.where(keep, q, zero), jnp.where(keep, qa, zero)],
                                       axis=0))
        return out

    for grp in range(N_PAIRS // PAIRS_PER_TRIP):
        prs = list(range(grp * PAIRS_PER_TRIP, (grp + 1) * PAIRS_PER_TRIP))
        rhs = {pr: rhs_of(pr) for pr in prs}

        def probs(pr, start, n_keys, row0, mask=None, q0=0, first=False, rhs=rhs):
            feat = slice(pr * PAIR_WIDTH, (pr + 1) * PAIR_WIDTH)
            k_aug = (ka_ref[0, pl.ds(start, n_keys), :] if ka_const
                     else ka_ref[0, pl.ds(start, n_keys), feat])
            lhs = jnp.concatenate([k_ref[0, pl.ds(start, n_keys), feat], k_aug], axis=1)
            for e in range(2):
                s_t = _dot(lhs, rhs[pr][e][:, q0:])
                if mask is not None:
                    s_t = jnp.where(mask[:, q0:], s_t, NEG_BIG)
                p = jnp.exp2(s_t)
                p_sc[pr, e, row0:row0 + n_keys, q0:] = p.astype(BF16)
                part = p[0:8, :]
                for r in range(8, n_keys, 8):
                    part = part + p[r:r + 8, :]
                if first:
                    l_sc[pr, e] = part
                else:
                    l_sc[pr, e, :, q0:] += part

        def values(pr, j):
            rows = slice(pr * PAIR_WIDTH, (pr + 1) * PAIR_WIDTH)
            v_pair = jnp.concatenate([v_ref[0, 2 * j, rows, :], v_ref[0, 2 * j + 1, rows, :]],
                                     axis=1)
            for e in range(2):
                upd = _dot(v_pair, p_sc[pr, e])
                acc_sc[pr, e] += upd[e * HEAD_DIM:(e + 1) * HEAD_DIM, :]

        diag = pl.multiple_of(i * t, t)
        for pr in prs:
            probs(pr, diag, kt, 0, key_i <= qry_i, first=True)
            probs(pr, diag + kt, kt, kt, key_i + kt <= qry_i, q0=kt)
            for e in range(2):
                p_sc[pr, e, kt:, :kt] = jnp.zeros((kt, kt), BF16)
                acc_sc[pr, e] = jnp.zeros((HEAD_DIM, t), F32)

        def body(j, j_prev, probs=probs, values=values, prs=prs):
            for pr in prs:
                values(pr, j_prev)
                probs(pr, pl.multiple_of(j * t, t), t, 0)
            return j

        def unrolled(n, body=body):
            def trip(m, j_prev):
                for u in range(n):
                    j_prev = body(n * m + u, j_prev)
                return j_prev
            return trip

        done = 0
        j_prev = i
        for n in (4, 2, 1):
            trips = (i - done) // n
            j_prev = lax.fori_loop(done // n, done // n + trips, unrolled(n), j_prev)
            done = done + trips * n
        for pr in prs:
            values(pr, j_prev)
        for pr in prs:
            o_t = jnp.concatenate(
                [acc_sc[pr, e] / jnp.sum(l_sc[pr, e], axis=0, keepdims=True)
                 for e in range(2)], axis=0)
            o_ref[0, :, pr * PAIR_WIDTH:(pr + 1) * PAIR_WIDTH] = o_t.T


def _attention_online(q_t, qa, k, ka, v_t, *, ka_const, name):
    bsz, nt, _, t = q_t.shape
    s = k.shape[1]
    grid = (bsz, N_PAIRS, nt)
    if ka_const:
        ka_spec = pl.BlockSpec((1, s, PAIR_WIDTH), lambda b, p, i: (0, 0, 0))
    else:
        ka_spec = pl.BlockSpec((1, s, PAIR_WIDTH), lambda b, p, i: (b, 0, p))
    return pl.pallas_call(
        functools.partial(_attn_kernel, t=t),
        grid=grid,
        in_specs=[pl.BlockSpec((1, 1, PAIR_WIDTH, t), lambda b, p, i: (b, i, p, 0)),
                  pl.BlockSpec((1, 1, PAIR_WIDTH, t), lambda b, p, i: (b, i, p, 0)),
                  pl.BlockSpec((1, s, PAIR_WIDTH), lambda b, p, i: (b, 0, p)),
                  ka_spec,
                  pl.BlockSpec((1, s // KEY_TILE, PAIR_WIDTH, KEY_TILE),
                               lambda b, p, i: (b, 0, p, 0))],
        out_specs=pl.BlockSpec((1, t, PAIR_WIDTH), lambda b, p, i: (b, i, p)),
        out_shape=jax.ShapeDtypeStruct((bsz, s, GROUP_WIDTH), F32),
        compiler_params=pltpu.CompilerParams(
            dimension_semantics=("arbitrary", "arbitrary", "arbitrary"),
            vmem_limit_bytes=VMEM_LIMIT),
        name=name,
    )(q_t, qa, k, ka, v_t)


def _attention_bounded(q_t, qa, k, ka, v_t, *, ka_const, name):
    bsz, nt, _, t = q_t.shape
    s = k.shape[1]
    feat_spec = pl.BlockSpec((1, 1, GROUP_WIDTH, t), lambda b, i: (b, i, 0, 0))
    tok_spec = pl.BlockSpec((1, s, GROUP_WIDTH), lambda b, i: (b, 0, 0))
    ka_spec = pl.BlockSpec((1, s, PAIR_WIDTH), lambda b, i: (0, 0, 0)) if ka_const else tok_spec
    return pl.pallas_call(
        functools.partial(_attn_bounded_kernel, t=t, ka_const=ka_const),
        grid=(bsz, nt),
        in_specs=[feat_spec, feat_spec, tok_spec, ka_spec,
                  pl.BlockSpec((1, s // KEY_TILE, GROUP_WIDTH, KEY_TILE),
                               lambda b, i: (b, 0, 0, 0))],
        out_specs=pl.BlockSpec((1, t, GROUP_WIDTH), lambda b, i: (b, i, 0)),
        out_shape=jax.ShapeDtypeStruct((bsz, s, GROUP_WIDTH), F32),
        scratch_shapes=[pltpu.VMEM((N_PAIRS, 2, t, t), BF16),
                        pltpu.VMEM((N_PAIRS, 2, HEAD_DIM, t), F32),
                        pltpu.VMEM((N_PAIRS, 2, 8, t), F32)],
        compiler_params=pltpu.CompilerParams(
            dimension_semantics=("arbitrary", "arbitrary"),
            vmem_limit_bytes=VMEM_LIMIT),
        name=name,
    )(q_t, qa, k, ka, v_t)


def _attention_dispatch(g_q, g_k, q_t, qa, k, ka, v_t, *, ka_const, name):
    bound = HEAD_DIM * jnp.max(jnp.abs(g_q)) * jnp.max(jnp.abs(g_k))
    return lax.cond(
        bound <= MAX_SAFE_EXPONENT,
        functools.partial(_attention_bounded, ka_const=ka_const, name=name + "_bounded"),
        functools.partial(_attention_online, ka_const=ka_const, name=name + "_online"),
        q_t, qa, k, ka, v_t)


def _post_kernel(x_ref, om_ref, of_ref, ga_ref, scm_ref, shm_ref, gm_ref,
                 gom_ref, gof_ref, wout_ref, w1_ref, w2_ref, o_ref, *, ff_chunk):
    def rms(v):
        return v * lax.rsqrt(jnp.mean(v * v, axis=-1, keepdims=True) + EPS)

    mixed = jnp.concatenate([rms(om_ref[0]) * gom_ref[...],
                             rms(of_ref[0]) * gof_ref[...]], axis=1)
    x1 = x_ref[0] + ga_ref[...] * _dot(mixed, wout_ref[...])
    h = rms(x1) * (1.0 + scm_ref[...]) + shm_ref[...]
    d_ff = w1_ref.shape[1]
    y = None
    for c in range(d_ff // ff_chunk):
        hid = _dot(h, w1_ref[:, c * ff_chunk:(c + 1) * ff_chunk])
        hid = jnp.square(jnp.maximum(hid, 0.0))
        part = _dot(hid, w2_ref[c * ff_chunk:(c + 1) * ff_chunk, :])
        y = part if y is None else y + part
    o_ref[0] = x1 + gm_ref[...] * y


def _post(x, o_m, o_f, mod4, g_out_m, g_out_f, w_out, w_ff1, w_ff2, tm):
    bsz, s, d = x.shape
    d_ff = w_ff1.shape[1]

    def const(a):
        nd = a.ndim
        return pl.BlockSpec(a.shape, lambda b, i, _n=nd: (0,) * _n,
                            pipeline_mode=pl.Buffered(1))

    def mod_spec(k):
        return pl.BlockSpec((None, None, 1, d), lambda b, i, _k=k: (b, _k, 0, 0))

    tok = lambda w: pl.BlockSpec((1, tm, w), lambda b, i: (b, i, 0))
    return pl.pallas_call(
        functools.partial(_post_kernel, ff_chunk=1024),
        grid=(bsz, s // tm),
        in_specs=[tok(d), tok(GROUP_WIDTH), tok(GROUP_WIDTH),
                  mod_spec(2), mod_spec(4), mod_spec(3), mod_spec(5),
                  const(g_out_m), const(g_out_f), const(w_out), const(w_ff1), const(w_ff2)],
        out_specs=tok(d),
        out_shape=jax.ShapeDtypeStruct((bsz, s, d), F32),
        compiler_params=pltpu.CompilerParams(
            dimension_semantics=("arbitrary", "arbitrary"),
            vmem_limit_bytes=POST_VMEM_LIMIT),
        name="post",
    )(x, o_m, o_f, mod4, mod4, mod4, mod4, g_out_m, g_out_f, w_out, w_ff1, w_ff2)


def _rope_tables(s):
    inv_freq = ROPE_THETA ** (-np.arange(0, ROPE_DIM, 2, dtype=np.float64) / ROPE_DIM)
    ang = np.arange(s, dtype=np.float64)[:, None] * inv_freq[None, :]
    cos, sin = np.cos(ang), np.sin(ang)
    rc = np.ones((s, LANES)); rs1 = np.zeros((s, LANES)); rs2 = np.zeros((s, LANES))
    for base in (0, HEAD_DIM):
        rc[:, base:base + ROPE_HALF] = cos
        rc[:, base + ROPE_HALF:base + ROPE_DIM] = cos
        rs2[:, base:base + ROPE_HALF] = -sin
        rs1[:, base + ROPE_HALF:base + ROPE_DIM] = sin
    f = lambda a: jnp.asarray(a, dtype=F32)
    return f(cos.T), f(sin.T), f(rc), f(rs1), f(rs2)


def _static_mats(s):
    idx = np.arange(256)
    jmat = (idx[:, None] // HEAD_DIM == idx[None, :] // HEAD_DIM) / HEAD_DIM
    place = np.zeros((LANES, GROUP_WIDTH))
    for part in range(3):
        for hd in range(GROUP_HEADS):
            place[part * GROUP_HEADS + hd,
                  (hd // 2) * PAIR_WIDTH + (hd % 2) * HEAD_DIM + part] = 1.0
    kones = np.zeros((1, GROUP_WIDTH))
    for hd in range(GROUP_HEADS):
        kones[0, hd * HEAD_DIM + 3:hd * HEAD_DIM + 6] = 1.0
    ka_moba = np.zeros((1, s, PAIR_WIDTH))
    blk = np.arange(s) // MOBA_BLOCK
    for e in range(2):
        ka_moba[0, np.arange(s), e * HEAD_DIM + blk] = 1.0
    b = lambda a: jnp.asarray(a, dtype=BF16)
    return b(jmat), b(place), jnp.asarray(kones, dtype=F32), b(ka_moba)


def kernel(x, c, w_ada, b_ada, w_in, b_forget, g_qn_moba, g_kn_moba, g_qn_fox, g_kn_fox,
           g_out_moba, g_out_fox, w_out, w_ff1, w_ff2):
    bsz, s, d = x.shape
    depth = w_ada.shape[0]
    w = GROUP_WIDTH
    assert s % MOBA_BLOCK == 0 and s // MOBA_BLOCK <= 16
    tm_in = 1024
    tm_post = 512
    cos_t, sin_t, rc, rs1, rs2 = _rope_tables(s)
    jmat, place, kones, ka_moba = _static_mats(s)
    q_scale = HEAD_DIM ** -0.5 * LOG2E

    for l in range(depth):
        mod = _adaln(c, w_ada[l], b_ada[l])
        mod4 = mod.reshape(bsz, 6, 1, d)

        wl = w_in[l]
        wf =jnp.zeros((d, LANES), F32).at[:, :GROUP_HEADS].set(wl[:, 6 * w:]).astype(BF16)
        bfor = jnp.zeros((1, LANES), F32).at[0, :GROUP_HEADS].set(b_forget[l])
        q_gain = lambda g: jnp.broadcast_to((g * q_scale)[:, None], (HEAD_DIM, tm_in))
        k_gain = lambda g: jnp.tile(g, GROUP_HEADS).reshape(1, w)
        wts = _project_weights(w_in, l, w) + (
               wf, bfor,
               q_gain(g_qn_moba[l]), k_gain(g_kn_moba[l]),
               q_gain(g_qn_fox[l]), k_gain(g_kn_fox[l]))
        consts = (cos_t, sin_t, rc, rs1, rs2, jmat, place, kones)

        qm_t, km, vm_t, bias_t, qf_t, kf, vf_t, ka_f, qa_f = _inproj(
            x, mod4[:, 1], mod4[:, 0], wts, consts, tm_in)

        o_m = _attention_dispatch(g_qn_moba[l] * q_scale, g_kn_moba[l],
                                  qm_t, bias_t, km, ka_moba, vm_t, ka_const=True,
                                  name="attn_moba")
        o_f = _attention_dispatch(g_qn_fox[l] * q_scale, g_kn_fox[l],
                                  qf_t, qa_f, kf, ka_f, vf_t, ka_const=False,
                                  name="attn_fox")

        x = _post(x, o_m, o_f, mod4,
                  g_out_moba[l].reshape(1, w), g_out_fox[l].reshape(1, w),
                  w_out[l], w_ff1[l], w_ff2[l],
                  tm_post)
    return x
```

```python
import functools
import math

import numpy as np
import jax
import jax.numpy as jnp
from jax import lax
from jax.experimental import pallas as pl
from jax.experimental.pallas import tpu as pltpu

F32 = jnp.float32
BF16 = jnp.bfloat16

HEAD_DIM = 64
GROUP_HEADS = 8
GROUP_WIDTH = GROUP_HEADS * HEAD_DIM
PAIR_WIDTH = 2 * HEAD_DIM
N_PAIRS = GROUP_HEADS // 2
MOBA_BLOCK = 256
MOBA_TOPK = 3
ROPE_THETA = 500000.0
ROPE_DIM = HEAD_DIM // 4
ROPE_HALF = ROPE_DIM // 2
EPS = 1e-6
LOG2E = math.log2(math.e)
NEG_BIG = -1e30
MAX_SAFE_EXPONENT = 60.0
ATT_TILE = 512
KEY_TILE = 256
CUM_BLOCK = 128
PAIRS_PER_TRIP = 4
LANES = 128
VMEM_LIMIT = 48 * 1024 * 1024
POST_VMEM_LIMIT = 58 * 1024 * 1024


def _dot(a, b):
    return jnp.dot(a, b, preferred_element_type=F32)


def _split3(x):
    hi = x.astype(BF16)
    rem = x - hi.astype(F32)
    mid = rem.astype(BF16)
    lo = (rem - mid.astype(F32)).astype(BF16)
    return hi, mid, lo


def _dot_nt(a, b):
    return lax.dot_general(a, b, (((1,), (1,)), ((), ())), preferred_element_type=F32)


def _adaln_kernel(ct_ref, w_ref, b_ref, o_ref):
    c_t = ct_ref[...]
    s_t = c_t / (1.0 + jnp.exp(-c_t))
    w = w_ref[...]
    rows = [jnp.sum(s_t[:, b:b + 1] * w, axis=0, keepdims=True) for b in range(c_t.shape[1])]
    o_ref[...] = jnp.concatenate(rows, axis=0) + b_ref[...]


def _adaln(c, w_ada, b_ada):
    bsz, d = c.shape
    n = w_ada.shape[1]
    return pl.pallas_call(
        _adaln_kernel,
        grid=(n // d,),
        in_specs=[pl.BlockSpec((d, bsz), lambda j: (0, 0)),
                  pl.BlockSpec((d, d), lambda j: (0, j)),
                  pl.BlockSpec((1, d), lambda j: (0, j))],
        out_specs=pl.BlockSpec((bsz, d), lambda j: (0, j)),
        out_shape=jax.ShapeDtypeStruct((bsz, n), F32),
        compiler_params=pltpu.CompilerParams(dimension_semantics=("arbitrary",)),
        name="adaln",
    )(c.T, w_ada, b_ada.reshape(1, n))


TRANSPOSED_GROUPS = (0, 2, 3, 5)


def _wprep_kernel(w_ref, *o_refs):
    g = pl.program_id(0)
    for k, o_ref in enumerate(o_refs):
        @pl.when(g == k)
        def _(k=k, o_ref=o_ref):
            w = w_ref[...]
            o_ref[...] = (w.T if k in TRANSPOSED_GROUPS else w).astype(BF16)


def _project_weights(w_in, layer, width):
    d = w_in.shape[1]
    shapes = [(width, d) if k in TRANSPOSED_GROUPS else (d, width) for k in range(6)]
    return tuple(pl.pallas_call(
        _wprep_kernel,
        grid=(6,),
        in_specs=[pl.BlockSpec((None, d, width), lambda g: (layer, 0, g))],
        out_specs=[pl.BlockSpec(s, lambda g: (0, 0)) for s in shapes],
        out_shape=[jax.ShapeDtypeStruct(s, BF16) for s in shapes],
        compiler_params=pltpu.CompilerParams(dimension_semantics=("arbitrary",)),
        name="wprep",
    )(w_in))


def _inproj_kernel(x_ref, sc_ref, sh_ref,
                   wqm_ref, wkm_ref, wvm_ref, wqf_ref, wkf_ref, wvf_ref, wf_ref, bf_ref,
                   gqm_ref, gkm_ref, gqf_ref, gkf_ref,
                   cos_t_ref, sin_t_ref, rc_ref, rs1_ref, rs2_ref,
                   j_ref, place_ref, kones_ref,
                   qm_o, km_o, vm_o, bias_o, qf_o, kf_o, vf_o, ka_o, qa_o,
                   kmean_sc, carry_sc, gate_sc, *, tm):
    i = pl.program_id(1)
    nb = tm // MOBA_BLOCK

    @pl.when(i == 0)
    def _():
        kmean_sc[...] = jnp.zeros_like(kmean_sc)
        carry_sc[...] = jnp.zeros_like(carry_sc)

    x = x_ref[0]
    ms = jnp.mean(x * x, axis=-1, keepdims=True)
    h = x * lax.rsqrt(ms + EPS) * (1.0 + sc_ref[...]) + sh_ref[...]
    hb = h.astype(BF16)

    def q_feature_major(w_ref, g_ref, rope):
        q_t = _dot_nt(w_ref[...], hb)
        gain = g_ref[...]
        heads = []
        for hd in range(GROUP_HEADS):
            xh = q_t[hd * HEAD_DIM:(hd + 1) * HEAD_DIM, :]
            ss = jnp.sum(xh * xh, axis=0, keepdims=True)
            xh = xh * lax.rsqrt(ss * (1.0 / HEAD_DIM) + EPS) * gain
            if rope:
                x1 = xh[0:ROPE_HALF, :]
                x2 = xh[ROPE_HALF:ROPE_DIM, :]
                cs = cos_t_ref[...]
                sn = sin_t_ref[...]
                xh = jnp.concatenate(
                    [x1 * cs - x2 * sn, x2 * cs + x1 * sn, xh[ROPE_DIM:, :]], axis=0)
            heads.append(xh)
        return jnp.concatenate(heads, axis=0)

    def k_token_major(w_ref, g_ref, rope):
        k = _dot(hb, w_ref[...])
        chunks = []
        for c in range(GROUP_WIDTH // 256):
            seg = k[:, c * 256:(c + 1) * 256]
            msq = _dot((seg * seg).astype(BF16), j_ref[...])
            chunks.append(seg * lax.rsqrt(msq + EPS) * g_ref[:, c * 256:(c + 1) * 256])
        k = jnp.concatenate(chunks, axis=1)
        if rope:
            outs = []
            for c in range(GROUP_WIDTH // LANES):
                seg = k[:, c * LANES:(c + 1) * LANES]
                outs.append(seg * rc_ref[...]
                            + pltpu.roll(seg, ROPE_HALF, 1) * rs1_ref[...]
                            + pltpu.roll(seg, LANES - ROPE_HALF, 1) * rs2_ref[...])
            k = jnp.concatenate(outs, axis=1)
        return k

    def store_tiles(o_ref, val_t, width=ATT_TILE):
        for t in range(tm // width):
            o_ref[0, t] = val_t[:, t * width:(t + 1) * width].astype(o_ref.dtype)

    qm_t = q_feature_major(wqm_ref, gqm_ref, True)
    store_tiles(qm_o, qm_t)
    km = k_token_major(wkm_ref, gkm_ref, True)
    km_o[0] = km.astype(BF16)
    store_tiles(vm_o, _dot_nt(wvm_ref[...], hb), KEY_TILE)

    blk0 = i * nb
    lane_head = lax.broadcasted_iota(jnp.int32, (1, GROUP_WIDTH), 1) // HEAD_DIM
    for sb in range(nb):
        kmean = jnp.mean(km[sb * MOBA_BLOCK:(sb + 1) * MOBA_BLOCK, :], axis=0, keepdims=True)
        for hd in range(GROUP_HEADS):
            row = hd * 16 + blk0 + sb
            kmean_sc[pl.ds(row, 1), :] = jnp.where(lane_head == hd, kmean, 0.0)

    km_hi, km_lo, _ = _split3(kmean_sc[...])
    q_hi, q_lo, _ = _split3(qm_t)
    gate_t = _dot(km_hi, q_hi) + _dot(km_hi, q_lo) + _dot(km_lo, q_hi)
    tok_blk = blk0 + lax.broadcasted_iota(jnp.int32, (16, tm), 1) // MOBA_BLOCK
    n_idx = lax.broadcasted_iota(jnp.int32, (16, tm), 0)
    valid = n_idx < tok_blk
    for hd in range(GROUP_HEADS):
        gate_sc[hd * 16:(hd + 1) * 16, :] = jnp.where(
            valid, gate_t[hd * 16:(hd + 1) * 16, :], -jnp.inf)

    store_tiles(qf_o, q_feature_major(wqf_ref, gqf_ref, False))
    kf_o[0] = k_token_major(wkf_ref, gkf_ref, False).astype(BF16)
    store_tiles(vf_o, _dot_nt(wvf_ref[...], hb), KEY_TILE)

    f = _dot(hb, wf_ref[...]) + bf_ref[...]
    logf = jnp.minimum(f, 0.0) - jnp.log1p(jnp.exp(-jnp.abs(f)))
    live = lax.broadcasted_iota(jnp.int32, (1, LANES), 1) < GROUP_HEADS
    logf = jnp.where(live, logf, 0.0)
    r_i = lax.broadcasted_iota(jnp.int32, (CUM_BLOCK, CUM_BLOCK), 0)
    c_i = lax.broadcasted_iota(jnp.int32, (CUM_BLOCK, CUM_BLOCK), 1)
    tri = jnp.where(r_i >= c_i, 1.0, 0.0).astype(BF16)
    n_blk = tm // CUM_BLOCK
    wide = jnp.concatenate([logf[b * CUM_BLOCK:(b + 1) * CUM_BLOCK, :] for b in range(n_blk)],
                           axis=1)
    parts = _split3(wide)
    prefix = _dot(tri, parts[0]) + _dot(tri, parts[1]) + _dot(tri, parts[2])
    carry = carry_sc[...]
    cums = []
    for blk in range(n_blk):
        c_blk = prefix[:, blk * LANES:(blk + 1) * LANES] + carry
        carry = c_blk[CUM_BLOCK - 1:CUM_BLOCK, :]
        cums.append(c_blk)
    carry_sc[...] = carry
    nc = jnp.concatenate(cums, axis=0) * (-LOG2E)
    hi, mid, lo = [part.astype(F32) for part in _split3(nc)]
    packed = hi + pltpu.roll(mid, GROUP_HEADS, 1) + pltpu.roll(lo, 2 * GROUP_HEADS, 1)
    ka_o[0] = (_dot(packed.astype(BF16), place_ref[...]) + kones_ref[...]).astype(BF16)

    nc_t = nc.T[0:GROUP_HEADS, :]
    hi_t = nc_t.astype(BF16).astype(F32)
    rem_t = nc_t - hi_t
    mid_t = rem_t.astype(BF16).astype(F32)
    lo_t = (rem_t - mid_t).astype(BF16).astype(F32)
    r8 = lax.broadcasted_iota(jnp.int32, (8, tm), 0)
    zeros56 = jnp.zeros((HEAD_DIM - 8, tm), F32)
    qa_rows = []
    for hd in range(GROUP_HEADS):
        blk = jnp.where(r8 < 3, 1.0,
                        jnp.where(r8 == 3, -hi_t[hd:hd + 1, :],
                                  jnp.where(r8 == 4, -mid_t[hd:hd + 1, :],
                                            jnp.where(r8 == 5, -lo_t[hd:hd + 1, :], 0.0))))
        qa_rows += [blk, zeros56]
    store_tiles(qa_o, jnp.concatenate(qa_rows, axis=0))

    def rank_step(m, cnts):
        tie = jnp.where(n_idx > m, 1, 0)
        out = []
        for hd in range(GROUP_HEADS):
            g = gate_sc[hd * 16:(hd + 1) * 16, :]
            gm = jnp.broadcast_to(gate_sc[pl.ds(hd * 16 + m, 1), :], (16, tm))
            out.append(cnts[hd] + jnp.where(gm > g, 1, jnp.where(gm == g, tie, 0)))
        return tuple(out)

    cnts = lax.fori_loop(0, blk0 + nb - 1, rank_step,
                         tuple(jnp.zeros((16, tm), jnp.int32) for _ in range(GROUP_HEADS)))
    zeros48 = jnp.zeros((HEAD_DIM - 16, tm), F32)
    pair_rows = []
    for hd in range(GROUP_HEADS):
        sel = (valid & (cnts[hd] < MOBA_TOPK)) | (n_idx == tok_blk)
        pair_rows += [jnp.where(sel, 0.0, NEG_BIG), zeros48]
    store_tiles(bias_o, jnp.concatenate(pair_rows, axis=0))


def _inproj(x, sc, sh, wts, consts, tm):
    bsz, s, d = x.shape
    nt = s // ATT_TILE
    grid = (bsz, s // tm)
    wqm, wkm, wvm, wqf, wkf, wvf, wf, bfor, gqm, gkm, gqf, gkf = wts
    cos_t, sin_t, rc, rs1, rs2, jmat, place, kones = consts

    def const(a):
        nd = a.ndim
        return pl.BlockSpec(a.shape, lambda b, i, _n=nd: (0,) * _n,
                            pipeline_mode=pl.Buffered(1))

    mod_spec = pl.BlockSpec((None, 1, d), lambda b, i: (b, 0, 0))
    in_specs = [pl.BlockSpec((1, tm, d), lambda b, i: (b, i, 0)), mod_spec, mod_spec]
    in_specs += [const(a) for a in (wqm, wkm, wvm, wqf, wkf, wvf, wf, bfor,
                                    gqm, gkm, gqf, gkf)]
    in_specs += [pl.BlockSpec((ROPE_HALF, tm), lambda b, i: (0, i)),
                 pl.BlockSpec((ROPE_HALF, tm), lambda b, i: (0, i)),
                 pl.BlockSpec((tm, LANES), lambda b, i: (i, 0)),
                 pl.BlockSpec((tm, LANES), lambda b, i: (i, 0)),
                 pl.BlockSpec((tm, LANES), lambda b, i: (i, 0)),
                 const(jmat), const(place), const(kones)]

    tiles = tm // ATT_TILE
    feat_spec = pl.BlockSpec((1, tiles, GROUP_WIDTH, ATT_TILE), lambda b, i: (b, i, 0, 0))
    tok_spec = pl.BlockSpec((1, tm, GROUP_WIDTH), lambda b, i: (b, i, 0))
    feat_shape = jax.ShapeDtypeStruct((bsz, nt, GROUP_WIDTH, ATT_TILE), BF16)
    tok_shape = jax.ShapeDtypeStruct((bsz, s, GROUP_WIDTH), BF16)
    val_spec = pl.BlockSpec((1, tm // KEY_TILE, GROUP_WIDTH, KEY_TILE),
                            lambda b, i: (b, i, 0, 0))
    val_shape = jax.ShapeDtypeStruct((bsz, s // KEY_TILE, GROUP_WIDTH, KEY_TILE), BF16)
    out_specs = [feat_spec, tok_spec, val_spec, feat_spec,
                 feat_spec, tok_spec, val_spec, tok_spec, feat_spec]
    out_shape = [feat_shape, tok_shape, val_shape, feat_shape,
                 feat_shape, tok_shape, val_shape, tok_shape, feat_shape]
    return pl.pallas_call(
        functools.partial(_inproj_kernel, tm=tm),
        grid=grid,
        in_specs=in_specs,
        out_specs=out_specs,
        out_shape=out_shape,
        scratch_shapes=[pltpu.VMEM((GROUP_HEADS * 16, GROUP_WIDTH), F32),
                        pltpu.VMEM((1, LANES), F32),
                        pltpu.VMEM((GROUP_HEADS * 16, tm), F32)],
        compiler_params=pltpu.CompilerParams(
            dimension_semantics=("arbitrary", "arbitrary"),
            vmem_limit_bytes=VMEM_LIMIT),
        name="inproj",
    )(x, sc, sh, wqm, wkm, wvm, wqf, wkf, wvf, wf, bfor, gqm, gkm, gqf, gkf,
      cos_t, sin_t, rc, rs1, rs2, jmat, place, kones)


def _attn_kernel(q_ref, qa_ref, k_ref, ka_ref, v_ref, o_ref, *, t):
    i = pl.program_id(2)
    q = q_ref[0, 0]
    qa = qa_ref[0, 0]
    row = lax.broadcasted_iota(jnp.int32, (PAIR_WIDTH, t), 0)
    rhs = []
    for e in range(2):
        keep = (row >= e * HEAD_DIM) & (row < (e + 1) * HEAD_DIM)
        zero = jnp.zeros_like(q)
        rhs.append(jnp.concatenate([jnp.where(keep, q, zero), jnp.where(keep, qa, zero)],
                                   axis=0))

    def tile_scores(j):
        start = pl.multiple_of(j * t, t)
        lhs = jnp.concatenate([k_ref[0, pl.ds(start, t), :], ka_ref[0, pl.ds(start, t), :]],
                              axis=1)
        return [_dot(lhs, rhs[e]) for e in range(2)]

    def values(j, e):
        rows = slice(e * HEAD_DIM, (e + 1) * HEAD_DIM)
        slabs = t // KEY_TILE
        return jnp.concatenate([v_ref[0, slabs * j + h, rows, :] for h in range(slabs)],
                               axis=1)

    key_i = lax.broadcasted_iota(jnp.int32, (t, t), 0)
    qry_i = lax.broadcasted_iota(jnp.int32, (t, t), 1)
    causal = key_i <= qry_i
    state = []
    for e, s_t in enumerate(tile_scores(i)):
        s_t = jnp.where(causal, s_t, NEG_BIG)
        m = jnp.max(s_t, axis=0, keepdims=True)
        p = jnp.exp2(s_t - m)
        l = jnp.sum(p, axis=0, keepdims=True)
        acc = _dot(values(i, e), p.astype(BF16))
        state += [m, l, acc]

    def body(j, carry):
        out = []
        for e, s_t in enumerate(tile_scores(j)):
            m, l, acc = carry[3 * e:3 * e + 3]
            m_new = jnp.maximum(m, jnp.max(s_t, axis=0, keepdims=True))
            alpha = jnp.exp2(m - m_new)
            p = jnp.exp2(s_t - m_new)
            l = alpha * l + jnp.sum(p, axis=0, keepdims=True)
            acc = alpha * acc + _dot(values(j, e), p.astype(BF16))
            out += [m_new, l, acc]
        return tuple(out)

    state = lax.fori_loop(0, i, body, tuple(state))
    o_t = jnp.concatenate([state[2] / state[1], state[5] / state[4]], axis=0)
    o_ref[0] = o_t.T


def _attn_bounded_kernel(q_ref, qa_ref, k_ref, ka_ref, v_ref, o_ref, p_sc, acc_sc, l_sc, *, t,
                         ka_const):
    i = pl.program_id(1)
    kt = KEY_TILE
    row = lax.broadcasted_iota(jnp.int32, (PAIR_WIDTH, t), 0)
    key_i = lax.broadcasted_iota(jnp.int32, (kt, t), 0)
    qry_i = lax.broadcasted_iota(jnp.int32, (kt, t), 1)
    assert t == 2 * kt


    def rhs_of(pr):
        feat = slice(pr * PAIR_WIDTH, (pr + 1) * PAIR_WIDTH)
        q = q_ref[0, 0, feat, :]
        qa = qa_ref[0, 0, feat, :]
        out = []
        for e in range(2):
            keep = (row >= e * HEAD_DIM) & (row < (e + 1) * HEAD_DIM)
            zero = jnp.zeros_like(q)
            out.append(jnp.concatenate([jnp.where(keep, q, zero), jnp.where(keep, qa, zero)],
                                       axis=0))
        return out

    for grp in range(N_PAIRS // PAIRS_PER_TRIP):
        prs = list(range(grp * PAIRS_PER_TRIP, (grp + 1) * PAIRS_PER_TRIP))
        rhs = {pr: rhs_of(pr) for pr in prs}

        def probs(pr, start, n_keys, row0, mask=None, q0=0, first=False, rhs=rhs):
            feat = slice(pr * PAIR_WIDTH, (pr + 1) * PAIR_WIDTH)
            k_aug = (ka_ref[0, pl.ds(start, n_keys), :] if ka_const
                     else ka_ref[0, pl.ds(start, n_keys), feat])
            lhs = jnp.concatenate([k_ref[0, pl.ds(start, n_keys), feat], k_aug], axis=1)
            for e in range(2):
                s_t = _dot(lhs, rhs[pr][e][:, q0:])
                if mask is not None:
                    s_t = jnp.where(mask[:, q0:], s_t, NEG_BIG)
                p = jnp.exp2(s_t)
                p_sc[pr, e, row0:row0 + n_keys, q0:] = p.astype(BF16)
                part = p[0:8, :]
                for r in range(8, n_keys, 8):
                    part = part + p[r:r + 8, :]
                if first:
                    l_sc[pr, e] = part
                else:
                    l_sc[pr, e, :, q0:] += part

        def values(pr, j):
            rows = slice(pr * PAIR_WIDTH, (pr + 1) * PAIR_WIDTH)
            v_pair = jnp.concatenate([v_ref[0, 2 * j, rows, :], v_ref[0, 2 * j + 1, rows, :]],
                                     axis=1)
            for e in range(2):
                acc_sc[pr, e] += _dot(v_pair[e * HEAD_DIM:(e + 1) * HEAD_DIM, :], p_sc[pr, e])

        diag = pl.multiple_of(i * t, t)
        for pr in prs:
            probs(pr, diag, kt, 0, key_i <= qry_i, first=True)
            probs(pr, diag + kt, kt, kt, key_i + kt <= qry_i, q0=kt)
            for e in range(2):
                p_sc[pr, e, kt:, :kt] = jnp.zeros((kt, kt), BF16)
                acc_sc[pr, e] = jnp.zeros((HEAD_DIM, t), F32)

        def body(j, j_prev, probs=probs, values=values, prs=prs):
            for pr in prs:
                values(pr, j_prev)
                probs(pr, pl.multiple_of(j * t, t), t, 0)
            return j

        def unrolled(n, body=body):
            def trip(m, j_prev):
                for u in range(n):
                    j_prev = body(n * m + u, j_prev)
                return j_prev
            return trip

        done = 0
        j_prev = i
        for n in (4, 2, 1):
            trips = (i - done) // n
            j_prev = lax.fori_loop(done // n, done // n + trips, unrolled(n), j_prev)
            done = done + trips * n
        for pr in prs:
            values(pr, j_prev)
        for pr in prs:
            o_t = jnp.concatenate(
                [acc_sc[pr, e] / jnp.sum(l_sc[pr, e], axis=0, keepdims=True)
                 for e in range(2)], axis=0)
            o_ref[0, :, pr * PAIR_WIDTH:(pr + 1) * PAIR_WIDTH] = o_t.T


def _attention_online(q_t, qa, k, ka, v_t, *, ka_const, name):
    bsz, nt, _, t = q_t.shape
    s = k.shape[1]
    grid = (bsz, N_PAIRS, nt)
    if ka_const:
        ka_spec = pl.BlockSpec((1, s, PAIR_WIDTH), lambda b, p, i: (0, 0, 0))
    else:
        ka_spec = pl.BlockSpec((1, s, PAIR_WIDTH), lambda b, p, i: (b, 0, p))
    return pl.pallas_call(
        functools.partial(_attn_kernel, t=t),
        grid=grid,
        in_specs=[pl.BlockSpec((1, 1, PAIR_WIDTH, t), lambda b, p, i: (b, i, p, 0)),
                  pl.BlockSpec((1, 1, PAIR_WIDTH, t), lambda b, p, i: (b, i, p, 0)),
                  pl.BlockSpec((1, s, PAIR_WIDTH), lambda b, p, i: (b, 0, p)),
                  ka_spec,
                  pl.BlockSpec((1, s // KEY_TILE, PAIR_WIDTH, KEY_TILE),
                               lambda b, p, i: (b, 0, p, 0))],
        out_specs=pl.BlockSpec((1, t, PAIR_WIDTH), lambda b, p, i: (b, i, p)),
        out_shape=jax.ShapeDtypeStruct((bsz, s, GROUP_WIDTH), F32),
        compiler_params=pltpu.CompilerParams(
            dimension_semantics=("arbitrary", "arbitrary", "arbitrary"),
            vmem_limit_bytes=VMEM_LIMIT),
        name=name,
    )(q_t, qa, k, ka, v_t)


def _attention_bounded(q_t, qa, k, ka, v_t, *, ka_const, name):
    bsz, nt, _, t = q_t.shape
    s = k.shape[1]
    feat_spec = pl.BlockSpec((1, 1, GROUP_WIDTH, t), lambda b, i: (b, i, 0, 0))
    tok_spec = pl.BlockSpec((1, s, GROUP_WIDTH), lambda b, i: (b, 0, 0))
    ka_spec = pl.BlockSpec((1, s, PAIR_WIDTH), lambda b, i: (0, 0, 0)) if ka_const else tok_spec
    return pl.pallas_call(
        functools.partial(_attn_bounded_kernel, t=t, ka_const=ka_const),
        grid=(bsz, nt),
        in_specs=[feat_spec, feat_spec, tok_spec, ka_spec,
                  pl.BlockSpec((1, s // KEY_TILE, GROUP_WIDTH, KEY_TILE),
                               lambda b, i: (b, 0, 0, 0))],
        out_specs=pl.BlockSpec((1, t, GROUP_WIDTH), lambda b, i: (b, i, 0)),
        out_shape=jax.ShapeDtypeStruct((bsz, s, GROUP_WIDTH), F32),
        scratch_shapes=[pltpu.VMEM((N_PAIRS, 2, t, t), BF16),
                        pltpu.VMEM((N_PAIRS, 2, HEAD_DIM, t), F32),
                        pltpu.VMEM((N_PAIRS, 2, 8, t), F32)],
        compiler_params=pltpu.CompilerParams(
            dimension_semantics=("arbitrary", "arbitrary"),
            vmem_limit_bytes=VMEM_LIMIT),
        name=name,
    )(q_t, qa, k, ka, v_t)


def _attention_dispatch(g_q, g_k, q_t, qa, k, ka, v_t, *, ka_const, name):
    bound = HEAD_DIM * jnp.max(jnp.abs(g_q)) * jnp.max(jnp.abs(g_k))
    return lax.cond(
        bound <= MAX_SAFE_EXPONENT,
        functools.partial(_attention_bounded, ka_const=ka_const, name=name + "_bounded"),
        functools.partial(_attention_online, ka_const=ka_const, name=name + "_online"),
        q_t, qa, k, ka, v_t)


def _post_kernel(x_ref, om_ref, of_ref, ga_ref, scm_ref, shm_ref, gm_ref,
                 gom_ref, gof_ref, wout_ref, w1_ref, w2_ref, o_ref, *, ff_chunk):
    def rms(v):
        return v * lax.rsqrt(jnp.mean(v * v, axis=-1, keepdims=True) + EPS)

    mixed = jnp.concatenate([rms(om_ref[0]) * gom_ref[...],
                             rms(of_ref[0]) * gof_ref[...]], axis=1)
    x1 = x_ref[0] + ga_ref[...] * _dot(mixed, wout_ref[...])
    h = rms(x1) * (1.0 + scm_ref[...]) + shm_ref[...]
    d_ff = w1_ref.shape[1]
    y = None
    for c in range(d_ff // ff_chunk):
        hid = _dot(h, w1_ref[:, c * ff_chunk:(c + 1) * ff_chunk])
        hid = jnp.square(jnp.maximum(hid, 0.0))
        part = _dot(hid, w2_ref[c * ff_chunk:(c + 1) * ff_chunk, :])
        y = part if y is None else y + part
    o_ref[0] = x1 + gm_ref[...] * y


def _post(x, o_m, o_f, mod4, g_out_m, g_out_f, w_out, w_ff1, w_ff2, tm):
    bsz, s, d = x.shape
    d_ff = w_ff1.shape[1]

    def const(a):
        nd = a.ndim
        return pl.BlockSpec(a.shape, lambda b, i, _n=nd: (0,) * _n,
                            pipeline_mode=pl.Buffered(1))

    def mod_spec(k):
        return pl.BlockSpec((None, None, 1, d), lambda b, i, _k=k: (b, _k, 0, 0))

    tok = lambda w: pl.BlockSpec((1, tm, w), lambda b, i: (b, i, 0))
    return pl.pallas_call(
        functools.partial(_post_kernel, ff_chunk=1024),
        grid=(bsz, s // tm),
        in_specs=[tok(d), tok(GROUP_WIDTH), tok(GROUP_WIDTH),
                  mod_spec(2), mod_spec(4), mod_spec(3), mod_spec(5),
                  const(g_out_m), const(g_out_f), const(w_out), const(w_ff1), const(w_ff2)],
        out_specs=tok(d),
        out_shape=jax.ShapeDtypeStruct((bsz, s, d), F32),
        compiler_params=pltpu.CompilerParams(
            dimension_semantics=("arbitrary", "arbitrary"),
            vmem_limit_bytes=POST_VMEM_LIMIT),
        name="post",
    )(x, o_m, o_f, mod4, mod4, mod4, mod4, g_out_m, g_out_f, w_out, w_ff1, w_ff2)


def _rope_tables(s):
    inv_freq = ROPE_THETA ** (-np.arange(0, ROPE_DIM, 2, dtype=np.float64) / ROPE_DIM)
    ang = np.arange(s, dtype=np.float64)[:, None] * inv_freq[None, :]
    cos, sin = np.cos(ang), np.sin(ang)
    rc = np.ones((s, LANES)); rs1 = np.zeros((s, LANES)); rs2 = np.zeros((s, LANES))
    for base in (0, HEAD_DIM):
        rc[:, base:base + ROPE_HALF] = cos
        rc[:, base + ROPE_HALF:base + ROPE_DIM] = cos
        rs2[:, base:base + ROPE_HALF] = -sin
        rs1[:, base + ROPE_HALF:base + ROPE_DIM] = sin
    f = lambda a: jnp.asarray(a, dtype=F32)
    return f(cos.T), f(sin.T), f(rc), f(rs1), f(rs2)


def _static_mats(s):
    idx = np.arange(256)
    jmat = (idx[:, None] // HEAD_DIM == idx[None, :] // HEAD_DIM) / HEAD_DIM
    place = np.zeros((LANES, GROUP_WIDTH))
    for part in range(3):
        for hd in range(GROUP_HEADS):
            place[part * GROUP_HEADS + hd,
                  (hd // 2) * PAIR_WIDTH + (hd % 2) * HEAD_DIM + part] = 1.0
    kones = np.zeros((1, GROUP_WIDTH))
    for hd in range(GROUP_HEADS):
        kones[0, hd * HEAD_DIM + 3:hd * HEAD_DIM + 6] = 1.0
    ka_moba = np.zeros((1, s, PAIR_WIDTH))
    blk = np.arange(s) // MOBA_BLOCK
    for e in range(2):
        ka_moba[0, np.arange(s), e * HEAD_DIM + blk] = 1.0
    b = lambda a: jnp.asarray(a, dtype=BF16)
    return b(jmat), b(place), jnp.asarray(kones, dtype=F32), b(ka_moba)


def kernel(x, c, w_ada, b_ada, w_in, b_forget, g_qn_moba, g_kn_moba, g_qn_fox, g_kn_fox,
           g_out_moba, g_out_fox, w_out, w_ff1, w_ff2):
    bsz, s, d = x.shape
    depth = w_ada.shape[0]
    w = GROUP_WIDTH
    assert s % MOBA_BLOCK == 0 and s // MOBA_BLOCK <= 16
    tm_in = 1024
    tm_post = 512
    cos_t, sin_t, rc, rs1, rs2 = _rope_tables(s)
    jmat, place, kones, ka_moba = _static_mats(s)
    q_scale = HEAD_DIM ** -0.5 * LOG2E

    for l in range(depth):
        mod = _adaln(c, w_ada[l], b_ada[l])
        mod4 = mod.reshape(bsz, 6, 1, d)

        wl = w_in[l]
        wf =jnp.zeros((d, LANES), F32).at[:, :GROUP_HEADS].set(wl[:, 6 * w:]).astype(BF16)
        bfor = jnp.zeros((1, LANES), F32).at[0, :GROUP_HEADS].set(b_forget[l])
        q_gain = lambda g: jnp.broadcast_to((g * q_scale)[:, None], (HEAD_DIM, tm_in))
        k_gain = lambda g: jnp.tile(g, GROUP_HEADS).reshape(1, w)
        wts = _project_weights(w_in, l, w) + (
               wf, bfor,
               q_gain(g_qn_moba[l]), k_gain(g_kn_moba[l]),
               q_gain(g_qn_fox[l]), k_gain(g_kn_fox[l]))
        consts = (cos_t, sin_t, rc, rs1, rs2, jmat, place, kones)

        qm_t, km, vm_t, bias_t, qf_t, kf, vf_t, ka_f, qa_f = _inproj(
            x, mod4[:, 1], mod4[:, 0], wts, consts, tm_in)

        o_m = _attention_dispatch(g_qn_moba[l] * q_scale, g_kn_moba[l],
                                  qm_t, bias_t, km, ka_moba, vm_t, ka_const=True,
                                  name="attn_moba")
        o_f = _attention_dispatch(g_qn_fox[l] * q_scale, g_kn_fox[l],
                                  qf_t, qa_f, kf, ka_f, vf_t, ka_const=False,
                                  name="attn_fox")

        x = _post(x, o_m, o_f, mod4,
                  g_out_moba[l].reshape(1, w), g_out_fox[l].reshape(1, w),
                  w_out[l], w_ff1[l], w_ff2[l],
                  tm_post)
    return x
```

```python
import functools
import math

import numpy as np
import jax
import jax.numpy as jnp
from jax import lax
from jax.experimental import pallas as pl
from jax.experimental.pallas import tpu as pltpu

F32 = jnp.float32
BF16 = jnp.bfloat16

HEAD_DIM = 64
GROUP_HEADS = 8
GROUP_WIDTH = GROUP_HEADS * HEAD_DIM
PAIR_WIDTH = 2 * HEAD_DIM
N_PAIRS = GROUP_HEADS // 2
MOBA_BLOCK = 256
MOBA_TOPK = 3
ROPE_THETA = 500000.0
ROPE_DIM = HEAD_DIM // 4
ROPE_HALF = ROPE_DIM // 2
EPS = 1e-6
LOG2E = math.log2(math.e)
NEG_BIG = -1e30
MAX_SAFE_EXPONENT = 60.0
ATT_TILE = 512
KEY_TILE = 256
CUM_BLOCK = 128
NORM_CHUNKS = 2
PAIRS_PER_TRIP = 4
LANES = 128
VMEM_LIMIT = 48 * 1024 * 1024
POST_VMEM_LIMIT = 58 * 1024 * 1024


def _dot(a, b):
    return jnp.dot(a, b, preferred_element_type=F32)


def _split3(x):
    hi = x.astype(BF16)
    rem = x - hi.astype(F32)
    mid = rem.astype(BF16)
    lo = (rem - mid.astype(F32)).astype(BF16)
    return hi, mid, lo


def _dot_nt(a, b):
    return lax.dot_general(a, b, (((1,), (1,)), ((), ())), preferred_element_type=F32)


def _adaln_kernel(ct_ref, w_ref, b_ref, o_ref):
    c_t = ct_ref[...]
    s_t = c_t / (1.0 + jnp.exp(-c_t))
    w = w_ref[...]
    rows = [jnp.sum(s_t[:, b:b + 1] * w, axis=0, keepdims=True) for b in range(c_t.shape[1])]
    o_ref[...] = jnp.concatenate(rows, axis=0) + b_ref[...]


def _adaln(c, w_ada, b_ada):
    bsz, d = c.shape
    n = w_ada.shape[1]
    return pl.pallas_call(
        _adaln_kernel,
        grid=(n // d,),
        in_specs=[pl.BlockSpec((d, bsz), lambda j: (0, 0)),
                  pl.BlockSpec((d, d), lambda j: (0, j)),
                  pl.BlockSpec((1, d), lambda j: (0, j))],
        out_specs=pl.BlockSpec((bsz, d), lambda j: (0, j)),
        out_shape=jax.ShapeDtypeStruct((bsz, n), F32),
        compiler_params=pltpu.CompilerParams(dimension_semantics=("arbitrary",)),
        name="adaln",
    )(c.T, w_ada, b_ada.reshape(1, n))


TRANSPOSED_GROUPS = (0, 2, 3, 5)


def _wprep_kernel(w_ref, *o_refs):
    g = pl.program_id(0)
    for k, o_ref in enumerate(o_refs):
        @pl.when(g == k)
        def _(k=k, o_ref=o_ref):
            w = w_ref[...]
            o_ref[...] = (w.T if k in TRANSPOSED_GROUPS else w).astype(BF16)


def _project_weights(w_in, layer, width):
    d = w_in.shape[1]
    shapes = [(width, d) if k in TRANSPOSED_GROUPS else (d, width) for k in range(6)]
    return tuple(pl.pallas_call(
        _wprep_kernel,
        grid=(6,),
        in_specs=[pl.BlockSpec((None, d, width), lambda g: (layer, 0, g))],
        out_specs=[pl.BlockSpec(s, lambda g: (0, 0)) for s in shapes],
        out_shape=[jax.ShapeDtypeStruct(s, BF16) for s in shapes],
        compiler_params=pltpu.CompilerParams(dimension_semantics=("arbitrary",)),
        name="wprep",
    )(w_in))


def _inproj_kernel(x_ref, sc_ref, sh_ref,
                   wqm_ref, wkm_ref, wvm_ref, wqf_ref, wkf_ref, wvf_ref, wf_ref, bf_ref,
                   gqm_ref, gkm_ref, gqf_ref, gkf_ref,
                   cos_t_ref, sin_t_ref, rc_ref, rs1_ref, rs2_ref,
                   j_ref, place_ref, kones_ref,
                   qm_o, km_o, vm_o, bias_o, qf_o, kf_o, vf_o, ka_o, qa_o,
                   kmean_sc, carry_sc, gate_sc, *, tm):
    i = pl.program_id(1)
    nb = tm // MOBA_BLOCK

    @pl.when(i == 0)
    def _():
        kmean_sc[...] = jnp.zeros_like(kmean_sc)
        carry_sc[...] = jnp.zeros_like(carry_sc)

    rows_per_chunk = tm // NORM_CHUNKS
    hb_chunks = []
    for ch in range(NORM_CHUNKS):
        x = x_ref[0, ch * rows_per_chunk:(ch + 1) * rows_per_chunk, :]
        ms = jnp.mean(x * x, axis=-1, keepdims=True)
        h = x * lax.rsqrt(ms + EPS) * (1.0 + sc_ref[...]) + sh_ref[...]
        hb_chunks.append(h.astype(BF16))

    def proj(w):
        return jnp.concatenate([_dot(hb, w) for hb in hb_chunks], axis=0)

    def proj_t(w_t):
        return jnp.concatenate([_dot_nt(w_t, hb) for hb in hb_chunks], axis=1)

    def q_feature_major(w_ref, g_ref, rope):
        q_t = proj_t(w_ref[...])
        gain = g_ref[...]
        heads = []
        for hd in range(GROUP_HEADS):
            xh = q_t[hd * HEAD_DIM:(hd + 1) * HEAD_DIM, :]
            ss = jnp.sum(xh * xh, axis=0, keepdims=True)
            xh = xh * lax.rsqrt(ss * (1.0 / HEAD_DIM) + EPS) * gain
            if rope:
                x1 = xh[0:ROPE_HALF, :]
                x2 = xh[ROPE_HALF:ROPE_DIM, :]
                cs = cos_t_ref[...]
                sn = sin_t_ref[...]
                xh = jnp.concatenate(
                    [x1 * cs - x2 * sn, x2 * cs + x1 * sn, xh[ROPE_DIM:, :]], axis=0)
            heads.append(xh)
        return jnp.concatenate(heads, axis=0)

    def k_token_major(w_ref, g_ref, rope):
        k = proj(w_ref[...])
        chunks = []
        for c in range(GROUP_WIDTH // 256):
            seg = k[:, c * 256:(c + 1) * 256]
            msq = _dot((seg * seg).astype(BF16), j_ref[...])
            chunks.append(seg * lax.rsqrt(msq + EPS) * g_ref[:, c * 256:(c + 1) * 256])
        k = jnp.concatenate(chunks, axis=1)
        if rope:
            outs = []
            for c in range(GROUP_WIDTH // LANES):
                seg = k[:, c * LANES:(c + 1) * LANES]
                outs.append(seg * rc_ref[...]
                            + pltpu.roll(seg, ROPE_HALF, 1) * rs1_ref[...]
                            + pltpu.roll(seg, LANES - ROPE_HALF, 1) * rs2_ref[...])
            k = jnp.concatenate(outs, axis=1)
        return k

    def store_tiles(o_ref, val_t, width=ATT_TILE):
        for t in range(tm // width):
            o_ref[0, t] = val_t[:, t * width:(t + 1) * width].astype(o_ref.dtype)

    qm_t = q_feature_major(wqm_ref, gqm_ref, True)
    store_tiles(qm_o, qm_t)
    km = k_token_major(wkm_ref, gkm_ref, True)
    km_o[0] = km.astype(BF16)
    store_tiles(vm_o, proj_t(wvm_ref[...]), KEY_TILE)

    blk0 = i * nb
    lane_head = lax.broadcasted_iota(jnp.int32, (1, GROUP_WIDTH), 1) // HEAD_DIM
    for sb in range(nb):
        kmean = jnp.mean(km[sb * MOBA_BLOCK:(sb + 1) * MOBA_BLOCK, :], axis=0, keepdims=True)
        for hd in range(GROUP_HEADS):
            row = hd * 16 + blk0 + sb
            kmean_sc[pl.ds(row, 1), :] = jnp.where(lane_head == hd, kmean, 0.0)

    km_hi, km_lo, _ = _split3(kmean_sc[...])
    q_hi, q_lo, _ = _split3(qm_t)
    gate_t = _dot(km_hi, q_hi) + _dot(km_hi, q_lo) + _dot(km_lo, q_hi)
    tok_blk = blk0 + lax.broadcasted_iota(jnp.int32, (16, tm), 1) // MOBA_BLOCK
    n_idx = lax.broadcasted_iota(jnp.int32, (16, tm), 0)
    valid = n_idx < tok_blk
    for hd in range(GROUP_HEADS):
        gate_sc[hd * 16:(hd + 1) * 16, :] = jnp.where(
            valid, gate_t[hd * 16:(hd + 1) * 16, :], -jnp.inf)

    store_tiles(qf_o, q_feature_major(wqf_ref, gqf_ref, False))
    kf_o[0] = k_token_major(wkf_ref, gkf_ref, False).astype(BF16)
    store_tiles(vf_o, proj_t(wvf_ref[...]), KEY_TILE)

    f = proj(wf_ref[...]) + bf_ref[...]
    logf = jnp.minimum(f, 0.0) - jnp.log1p(jnp.exp(-jnp.abs(f)))
    live = lax.broadcasted_iota(jnp.int32, (1, LANES), 1) < GROUP_HEADS
    logf = jnp.where(live, logf, 0.0)
    r_i = lax.broadcasted_iota(jnp.int32, (CUM_BLOCK, CUM_BLOCK), 0)
    c_i = lax.broadcasted_iota(jnp.int32, (CUM_BLOCK, CUM_BLOCK), 1)
    tri = jnp.where(r_i >= c_i, 1.0, 0.0).astype(BF16)
    n_blk = tm // CUM_BLOCK
    wide = jnp.concatenate([logf[b * CUM_BLOCK:(b + 1) * CUM_BLOCK, :] for b in range(n_blk)],
                           axis=1)
    parts = _split3(wide)
    prefix = _dot(tri, parts[0]) + _dot(tri, parts[1]) + _dot(tri, parts[2])
    carry = carry_sc[...]
    cums = []
    for blk in range(n_blk):
        c_blk = prefix[:, blk * LANES:(blk + 1) * LANES] + carry
        carry = c_blk[CUM_BLOCK - 1:CUM_BLOCK, :]
        cums.append(c_blk)
    carry_sc[...] = carry
    nc = jnp.concatenate(cums, axis=0) * (-LOG2E)
    hi, mid, lo = [part.astype(F32) for part in _split3(nc)]
    packed = hi + pltpu.roll(mid, GROUP_HEADS, 1) + pltpu.roll(lo, 2 * GROUP_HEADS, 1)
    ka_o[0] = (_dot(packed.astype(BF16), place_ref[...]) + kones_ref[...]).astype(BF16)

    nc_t = nc.T[0:GROUP_HEADS, :]
    hi_t = nc_t.astype(BF16).astype(F32)
    rem_t = nc_t - hi_t
    mid_t = rem_t.astype(BF16).astype(F32)
    lo_t = (rem_t - mid_t).astype(BF16).astype(F32)
    r8 = lax.broadcasted_iota(jnp.int32, (8, tm), 0)
    zeros56 = jnp.zeros((HEAD_DIM - 8, tm), F32)
    qa_rows = []
    for hd in range(GROUP_HEADS):
        blk = jnp.where(r8 < 3, 1.0,
                        jnp.where(r8 == 3, -hi_t[hd:hd + 1, :],
                                  jnp.where(r8 == 4, -mid_t[hd:hd + 1, :],
                                            jnp.where(r8 == 5, -lo_t[hd:hd + 1, :], 0.0))))
        qa_rows += [blk, zeros56]
    store_tiles(qa_o, jnp.concatenate(qa_rows, axis=0))

    def rank_step(m, cnts):
        tie = jnp.where(n_idx > m, 1, 0)
        out = []
        for hd in range(GROUP_HEADS):
            g = gate_sc[hd * 16:(hd + 1) * 16, :]
            gm = jnp.broadcast_to(gate_sc[pl.ds(hd * 16 + m, 1), :], (16, tm))
            out.append(cnts[hd] + jnp.where(gm > g, 1, jnp.where(gm == g, tie, 0)))
        return tuple(out)

    cnts = lax.fori_loop(0, blk0 + nb - 1, rank_step,
                         tuple(jnp.zeros((16, tm), jnp.int32) for _ in range(GROUP_HEADS)))
    zeros48 = jnp.zeros((HEAD_DIM - 16, tm), F32)
    pair_rows = []
    for hd in range(GROUP_HEADS):
        sel = (valid & (cnts[hd] < MOBA_TOPK)) | (n_idx == tok_blk)
        pair_rows += [jnp.where(sel, 0.0, NEG_BIG), zeros48]
    store_tiles(bias_o, jnp.concatenate(pair_rows, axis=0))


def _inproj(x, sc, sh, wts, consts, tm):
    bsz, s, d = x.shape
    nt = s // ATT_TILE
    grid = (bsz, s // tm)
    wqm, wkm, wvm, wqf, wkf, wvf, wf, bfor, gqm, gkm, gqf, gkf = wts
    cos_t, sin_t, rc, rs1, rs2, jmat, place, kones = consts

    def const(a):
        nd = a.ndim
        return pl.BlockSpec(a.shape, lambda b, i, _n=nd: (0,) * _n,
                            pipeline_mode=pl.Buffered(1))

    mod_spec = pl.BlockSpec((None, 1, d), lambda b, i: (b, 0, 0))
    in_specs = [pl.BlockSpec((1, tm, d), lambda b, i: (b, i, 0)), mod_spec, mod_spec]
    in_specs += [const(a) for a in (wqm, wkm, wvm, wqf, wkf, wvf, wf, bfor,
                                    gqm, gkm, gqf, gkf)]
    in_specs += [pl.BlockSpec((ROPE_HALF, tm), lambda b, i: (0, i)),
                 pl.BlockSpec((ROPE_HALF, tm), lambda b, i: (0, i)),
                 pl.BlockSpec((tm, LANES), lambda b, i: (i, 0)),
                 pl.BlockSpec((tm, LANES), lambda b, i: (i, 0)),
                 pl.BlockSpec((tm, LANES), lambda b, i: (i, 0)),
                 const(jmat), const(place), const(kones)]

    tiles = tm // ATT_TILE
    feat_spec = pl.BlockSpec((1, tiles, GROUP_WIDTH, ATT_TILE), lambda b, i: (b, i, 0, 0))
    tok_spec = pl.BlockSpec((1, tm, GROUP_WIDTH), lambda b, i: (b, i, 0))
    feat_shape = jax.ShapeDtypeStruct((bsz, nt, GROUP_WIDTH, ATT_TILE), BF16)
    tok_shape = jax.ShapeDtypeStruct((bsz, s, GROUP_WIDTH), BF16)
    val_spec = pl.BlockSpec((1, tm // KEY_TILE, GROUP_WIDTH, KEY_TILE),
                            lambda b, i: (b, i, 0, 0))
    val_shape = jax.ShapeDtypeStruct((bsz, s // KEY_TILE, GROUP_WIDTH, KEY_TILE), BF16)
    out_specs = [feat_spec, tok_spec, val_spec, feat_spec,
                 feat_spec, tok_spec, val_spec, tok_spec, feat_spec]
    out_shape = [feat_shape, tok_shape, val_shape, feat_shape,
                 feat_shape, tok_shape, val_shape, tok_shape, feat_shape]
    return pl.pallas_call(
        functools.partial(_inproj_kernel, tm=tm),
        grid=grid,
        in_specs=in_specs,
        out_specs=out_specs,
        out_shape=out_shape,
        scratch_shapes=[pltpu.VMEM((GROUP_HEADS * 16, GROUP_WIDTH), F32),
                        pltpu.VMEM((1, LANES), F32),
                        pltpu.VMEM((GROUP_HEADS * 16, tm), F32)],
        compiler_params=pltpu.CompilerParams(
            dimension_semantics=("arbitrary", "arbitrary"),
            vmem_limit_bytes=VMEM_LIMIT),
        name="inproj",
    )(x, sc, sh, wqm, wkm, wvm, wqf, wkf, wvf, wf, bfor, gqm, gkm, gqf, gkf,
      cos_t, sin_t, rc, rs1, rs2, jmat, place, kones)


def _attn_kernel(q_ref, qa_ref, k_ref, ka_ref, v_ref, o_ref, *, t):
    i = pl.program_id(2)
    q = q_ref[0, 0]
    qa = qa_ref[0, 0]
    row = lax.broadcasted_iota(jnp.int32, (PAIR_WIDTH, t), 0)
    rhs = []
    for e in range(2):
        keep = (row >= e * HEAD_DIM) & (row < (e + 1) * HEAD_DIM)
        zero = jnp.zeros_like(q)
        rhs.append(jnp.concatenate([jnp.where(keep, q, zero), jnp.where(keep, qa, zero)],
                                   axis=0))

    def tile_scores(j):
        start = pl.multiple_of(j * t, t)
        lhs = jnp.concatenate([k_ref[0, pl.ds(start, t), :], ka_ref[0, pl.ds(start, t), :]],
                              axis=1)
        return [_dot(lhs, rhs[e]) for e in range(2)]

    def values(j, e):
        rows = slice(e * HEAD_DIM, (e + 1) * HEAD_DIM)
        slabs = t // KEY_TILE
        return jnp.concatenate([v_ref[0, slabs * j + h, rows, :] for h in range(slabs)],
                               axis=1)

    key_i = lax.broadcasted_iota(jnp.int32, (t, t), 0)
    qry_i = lax.broadcasted_iota(jnp.int32, (t, t), 1)
    causal = key_i <= qry_i
    state = []
    for e, s_t in enumerate(tile_scores(i)):
        s_t = jnp.where(causal, s_t, NEG_BIG)
        m = jnp.max(s_t, axis=0, keepdims=True)
        p = jnp.exp2(s_t - m)
        l = jnp.sum(p, axis=0, keepdims=True)
        acc = _dot(values(i, e), p.astype(BF16))
        state += [m, l, acc]

    def body(j, carry):
        out = []
        for e, s_t in enumerate(tile_scores(j)):
            m, l, acc = carry[3 * e:3 * e + 3]
            m_new = jnp.maximum(m, jnp.max(s_t, axis=0, keepdims=True))
            alpha = jnp.exp2(m - m_new)
            p = jnp.exp2(s_t - m_new)
            l = alpha * l + jnp.sum(p, axis=0, keepdims=True)
            acc = alpha * acc + _dot(values(j, e), p.astype(BF16))
            out += [m_new, l, acc]
        return tuple(out)

    state = lax.fori_loop(0, i, body, tuple(state))
    o_t = jnp.concatenate([state[2] / state[1], state[5] / state[4]], axis=0)
    o_ref[0] = o_t.T


def _attn_bounded_kernel(q_ref, qa_ref, k_ref, ka_ref, v_ref, o_ref, p_sc, acc_sc, l_sc, *, t,
                         ka_const):
    i = pl.program_id(1)
    kt = KEY_TILE
    row = lax.broadcasted_iota(jnp.int32, (PAIR_WIDTH, t), 0)
    key_i = lax.broadcasted_iota(jnp.int32, (kt, t), 0)
    qry_i = lax.broadcasted_iota(jnp.int32, (kt, t), 1)
    assert t == 2 * kt


    def rhs_of(pr):
        feat = slice(pr * PAIR_WIDTH, (pr + 1) * PAIR_WIDTH)
        q = q_ref[0, 0, feat, :]
        qa = qa_ref[0, 0, feat, :]
        out = []
        for e in range(2):
            keep = (row >= e * HEAD_DIM) & (row < (e + 1) * HEAD_DIM)
            zero = jnp.zeros_like(q)
            out.append(jnp.concatenate([jnp.where(keep, q, zero), jnp.where(keep, qa, zero)],
                                       axis=0))
        return out

    for grp in range(N_PAIRS // PAIRS_PER_TRIP):
        prs = list(range(grp * PAIRS_PER_TRIP, (grp + 1) * PAIRS_PER_TRIP))
        rhs = {pr: rhs_of(pr) for pr in prs}

        def probs(pr, start, n_keys, row0, mask=None, q0=0, first=False, rhs=rhs):
            feat = slice(pr * PAIR_WIDTH, (pr + 1) * PAIR_WIDTH)
            k_aug = (ka_ref[0, pl.ds(start, n_keys), :] if ka_const
                     else ka_ref[0, pl.ds(start, n_keys), feat])
            lhs = jnp.concatenate([k_ref[0, pl.ds(start, n_keys), feat], k_aug], axis=1)
            for e in range(2):
                s_t = _dot(lhs, rhs[pr][e][:, q0:])
                if mask is not None:
                    s_t = jnp.where(mask[:, q0:], s_t, NEG_BIG)
                p = jnp.exp2(s_t)
                p_sc[pr, e, row0:row0 + n_keys, q0:] = p.astype(BF16)
                part = p[0:8, :]
                for r in range(8, n_keys, 8):
                    part = part + p[r:r + 8, :]
                if first:
                    l_sc[pr, e] = part
                else:
                    l_sc[pr, e, :, q0:] += part

        def values(pr, j):
            rows = slice(pr * PAIR_WIDTH, (pr + 1) * PAIR_WIDTH)
            v_pair = jnp.concatenate([v_ref[0, 2 * j, rows, :], v_ref[0, 2 * j + 1, rows, :]],
                                     axis=1)
            for e in range(2):
                upd = _dot(v_pair, p_sc[pr, e])
                acc_sc[pr, e] += upd[e * HEAD_DIM:(e + 1) * HEAD_DIM, :]

        diag = pl.multiple_of(i * t, t)
        for pr in prs:
            probs(pr, diag, kt, 0, key_i <= qry_i, first=True)
            probs(pr, diag + kt, kt, kt, key_i + kt <= qry_i, q0=kt)
            for e in range(2):
                p_sc[pr, e, kt:, :kt] = jnp.zeros((kt, kt), BF16)
                acc_sc[pr, e] = jnp.zeros((HEAD_DIM, t), F32)

        def body(j, j_prev, probs=probs, values=values, prs=prs):
            for pr in prs:
                values(pr, j_prev)
                probs(pr, pl.multiple_of(j * t, t), t, 0)
            return j

        def unrolled(n, body=body):
            def trip(m, j_prev):
                for u in range(n):
                    j_prev = body(n * m + u, j_prev)
                return j_prev
            return trip

        done = 0
        j_prev = i
        for n in (4, 2, 1):
            trips = (i - done) // n
            j_prev = lax.fori_loop(done // n, done // n + trips, unrolled(n), j_prev)
            done = done + trips * n
        for pr in prs:
            values(pr, j_prev)
        for pr in prs:
            o_t = jnp.concatenate(
                [acc_sc[pr, e] / jnp.sum(l_sc[pr, e], axis=0, keepdims=True)
                 for e in range(2)], axis=0)
            o_ref[0, :, pr * PAIR_WIDTH:(pr + 1) * PAIR_WIDTH] = o_t.T


def _attention_online(q_t, qa, k, ka, v_t, *, ka_const, name):
    bsz, nt, _, t = q_t.shape
    s = k.shape[1]
    grid = (bsz, N_PAIRS, nt)
    if ka_const:
        ka_spec = pl.BlockSpec((1, s, PAIR_WIDTH), lambda b, p, i: (0, 0, 0))
    else:
        ka_spec = pl.BlockSpec((1, s, PAIR_WIDTH), lambda b, p, i: (b, 0, p))
    return pl.pallas_call(
        functools.partial(_attn_kernel, t=t),
        grid=grid,
        in_specs=[pl.BlockSpec((1, 1, PAIR_WIDTH, t), lambda b, p, i: (b, i, p, 0)),
                  pl.BlockSpec((1, 1, PAIR_WIDTH, t), lambda b, p, i: (b, i, p, 0)),
                  pl.BlockSpec((1, s, PAIR_WIDTH), lambda b, p, i: (b, 0, p)),
                  ka_spec,
                  pl.BlockSpec((1, s // KEY_TILE, PAIR_WIDTH, KEY_TILE),
                               lambda b, p, i: (b, 0, p, 0))],
        out_specs=pl.BlockSpec((1, t, PAIR_WIDTH), lambda b, p, i: (b, i, p)),
        out_shape=jax.ShapeDtypeStruct((bsz, s, GROUP_WIDTH), F32),
        compiler_params=pltpu.CompilerParams(
            dimension_semantics=("arbitrary", "arbitrary", "arbitrary"),
            vmem_limit_bytes=VMEM_LIMIT),
        name=name,
    )(q_t, qa, k, ka, v_t)


def _attention_bounded(q_t, qa, k, ka, v_t, *, ka_const, name):
    bsz, nt, _, t = q_t.shape
    s = k.shape[1]
    feat_spec = pl.BlockSpec((1, 1, GROUP_WIDTH, t), lambda b, i: (b, i, 0, 0))
    tok_spec = pl.BlockSpec((1, s, GROUP_WIDTH), lambda b, i: (b, 0, 0))
    ka_spec = pl.BlockSpec((1, s, PAIR_WIDTH), lambda b, i: (0, 0, 0)) if ka_const else tok_spec
    return pl.pallas_call(
        functools.partial(_attn_bounded_kernel, t=t, ka_const=ka_const),
        grid=(bsz, nt),
        in_specs=[feat_spec, feat_spec, tok_spec, ka_spec,
                  pl.BlockSpec((1, s // KEY_TILE, GROUP_WIDTH, KEY_TILE),
                               lambda b, i: (b, 0, 0, 0))],
        out_specs=pl.BlockSpec((1, t, GROUP_WIDTH), lambda b, i: (b, i, 0)),
        out_shape=jax.ShapeDtypeStruct((bsz, s, GROUP_WIDTH), F32),
        scratch_shapes=[pltpu.VMEM((N_PAIRS, 2, t, t), BF16),
                        pltpu.VMEM((N_PAIRS, 2, HEAD_DIM, t), F32),
                        pltpu.VMEM((N_PAIRS, 2, 8, t), F32)],
        compiler_params=pltpu.CompilerParams(
            dimension_semantics=("arbitrary", "arbitrary"),
            vmem_limit_bytes=VMEM_LIMIT),
        name=name,
    )(q_t, qa, k, ka, v_t)


def _attention_dispatch(g_q, g_k, q_t, qa, k, ka, v_t, *, ka_const, name):
    bound = HEAD_DIM * jnp.max(jnp.abs(g_q)) * jnp.max(jnp.abs(g_k))
    return lax.cond(
        bound <= MAX_SAFE_EXPONENT,
        functools.partial(_attention_bounded, ka_const=ka_const, name=name + "_bounded"),
        functools.partial(_attention_online, ka_const=ka_const, name=name + "_online"),
        q_t, qa, k, ka, v_t)


def _post_kernel(x_ref, om_ref, of_ref, ga_ref, scm_ref, shm_ref, gm_ref,
                 gom_ref, gof_ref, wout_ref, w1_ref, w2_ref, o_ref, *, ff_chunk):
    def rms(v):
        return v * lax.rsqrt(jnp.mean(v * v, axis=-1, keepdims=True) + EPS)

    mixed = jnp.concatenate([rms(om_ref[0]) * gom_ref[...],
                             rms(of_ref[0]) * gof_ref[...]], axis=1)
    x1 = x_ref[0] + ga_ref[...] * _dot(mixed, wout_ref[...])
    h = rms(x1) * (1.0 + scm_ref[...]) + shm_ref[...]
    d_ff = w1_ref.shape[1]
    y = None
    for c in range(d_ff // ff_chunk):
        hid = _dot(h, w1_ref[:, c * ff_chunk:(c + 1) * ff_chunk])
        hid = jnp.square(jnp.maximum(hid, 0.0))
        part = _dot(hid, w2_ref[c * ff_chunk:(c + 1) * ff_chunk, :])
        y = part if y is None else y + part
    o_ref[0] = x1 + gm_ref[...] * y


def _post(x, o_m, o_f, mod4, g_out_m, g_out_f, w_out, w_ff1, w_ff2, tm):
    bsz, s, d = x.shape
    d_ff = w_ff1.shape[1]

    def const(a):
        nd = a.ndim
        return pl.BlockSpec(a.shape, lambda b, i, _n=nd: (0,) * _n,
                            pipeline_mode=pl.Buffered(1))

    def mod_spec(k):
        return pl.BlockSpec((None, None, 1, d), lambda b, i, _k=k: (b, _k, 0, 0))

    tok = lambda w: pl.BlockSpec((1, tm, w), lambda b, i: (b, i, 0))
    return pl.pallas_call(
        functools.partial(_post_kernel, ff_chunk=1024),
        grid=(bsz, s // tm),
        in_specs=[tok(d), tok(GROUP_WIDTH), tok(GROUP_WIDTH),
                  mod_spec(2), mod_spec(4), mod_spec(3), mod_spec(5),
                  const(g_out_m), const(g_out_f), const(w_out), const(w_ff1), const(w_ff2)],
        out_specs=tok(d),
        out_shape=jax.ShapeDtypeStruct((bsz, s, d), F32),
        compiler_params=pltpu.CompilerParams(
            dimension_semantics=("arbitrary", "arbitrary"),
            vmem_limit_bytes=POST_VMEM_LIMIT),
        name="post",
    )(x, o_m, o_f, mod4, mod4, mod4, mod4, g_out_m, g_out_f, w_out, w_ff1, w_ff2)


def _rope_tables(s):
    inv_freq = ROPE_THETA ** (-np.arange(0, ROPE_DIM, 2, dtype=np.float64) / ROPE_DIM)
    ang = np.arange(s, dtype=np.float64)[:, None] * inv_freq[None, :]
    cos, sin = np.cos(ang), np.sin(ang)
    rc = np.ones((s, LANES)); rs1 = np.zeros((s, LANES)); rs2 = np.zeros((s, LANES))
    for base in (0, HEAD_DIM):
        rc[:, base:base + ROPE_HALF] = cos
        rc[:, base + ROPE_HALF:base + ROPE_DIM] = cos
        rs2[:, base:base + ROPE_HALF] = -sin
        rs1[:, base + ROPE_HALF:base + ROPE_DIM] = sin
    f = lambda a: jnp.asarray(a, dtype=F32)
    return f(cos.T), f(sin.T), f(rc), f(rs1), f(rs2)


def _static_mats(s):
    idx = np.arange(256)
    jmat = (idx[:, None] // HEAD_DIM == idx[None, :] // HEAD_DIM) / HEAD_DIM
    place = np.zeros((LANES, GROUP_WIDTH))
    for part in range(3):
        for hd in range(GROUP_HEADS):
            place[part * GROUP_HEADS + hd,
                  (hd // 2) * PAIR_WIDTH + (hd % 2) * HEAD_DIM + part] = 1.0
    kones = np.zeros((1, GROUP_WIDTH))
    for hd in range(GROUP_HEADS):
        kones[0, hd * HEAD_DIM + 3:hd * HEAD_DIM + 6] = 1.0
    ka_moba = np.zeros((1, s, PAIR_WIDTH))
    blk = np.arange(s) // MOBA_BLOCK
    for e in range(2):
        ka_moba[0, np.arange(s), e * HEAD_DIM + blk] = 1.0
    b = lambda a: jnp.asarray(a, dtype=BF16)
    return b(jmat), b(place), jnp.asarray(kones, dtype=F32), b(ka_moba)


def kernel(x, c, w_ada, b_ada, w_in, b_forget, g_qn_moba, g_kn_moba, g_qn_fox, g_kn_fox,
           g_out_moba, g_out_fox, w_out, w_ff1, w_ff2):
    bsz, s, d = x.shape
    depth = w_ada.shape[0]
    w = GROUP_WIDTH
    assert s % MOBA_BLOCK == 0 and s // MOBA_BLOCK <= 16
    tm_in = 1024
    tm_post = 512
    cos_t, sin_t, rc, rs1, rs2 = _rope_tables(s)
    jmat, place, kones, ka_moba = _static_mats(s)
    q_scale = HEAD_DIM ** -0.5 * LOG2E

    for l in range(depth):
        mod = _adaln(c, w_ada[l], b_ada[l])
        mod4 = mod.reshape(bsz, 6, 1, d)

        wl = w_in[l]
        wf =jnp.zeros((d, LANES), F32).at[:, :GROUP_HEADS].set(wl[:, 6 * w:]).astype(BF16)
        bfor = jnp.zeros((1, LANES), F32).at[0, :GROUP_HEADS].set(b_forget[l])
        q_gain = lambda g: jnp.broadcast_to((g * q_scale)[:, None], (HEAD_DIM, tm_in))
        k_gain = lambda g: jnp.tile(g, GROUP_HEADS).reshape(1, w)
        wts = _project_weights(w_in, l, w) + (
               wf, bfor,
               q_gain(g_qn_moba[l]), k_gain(g_kn_moba[l]),
               q_gain(g_qn_fox[l]), k_gain(g_kn_fox[l]))
        consts = (cos_t, sin_t, rc, rs1, rs2, jmat, place, kones)

        qm_t, km, vm_t, bias_t, qf_t, kf, vf_t, ka_f, qa_f = _inproj(
            x, mod4[:, 1], mod4[:, 0], wts, consts, tm_in)

        o_m = _attention_dispatch(g_qn_moba[l] * q_scale, g_kn_moba[l],
                                  qm_t, bias_t, km, ka_moba, vm_t, ka_const=True,
                                  name="attn_moba")
        o_f = _attention_dispatch(g_qn_fox[l] * q_scale, g_kn_fox[l],
                                  qf_t, qa_f, kf, ka_f, vf_t, ka_const=False,
                                  name="attn_fox")

        x = _post(x, o_m, o_f, mod4,
                  g_out_moba[l].reshape(1, w), g_out_fox[l].reshape(1, w),
                  w_out[l], w_ff1[l], w_ff2[l],
                  tm_post)
    return x
```

```python
import functools
import math

import numpy as np
import jax
import jax.numpy as jnp
from jax import lax
from jax.experimental import pallas as pl
from jax.experimental.pallas import tpu as pltpu

F32 = jnp.float32
BF16 = jnp.bfloat16

HEAD_DIM = 64
GROUP_HEADS = 8
GROUP_WIDTH = GROUP_HEADS * HEAD_DIM
PAIR_WIDTH = 2 * HEAD_DIM
N_PAIRS = GROUP_HEADS // 2
MOBA_BLOCK = 256
MOBA_TOPK = 3
ROPE_THETA = 500000.0
ROPE_DIM = HEAD_DIM // 4
ROPE_HALF = ROPE_DIM // 2
EPS = 1e-6
LOG2E = math.log2(math.e)
NEG_BIG = -1e30
MAX_SAFE_EXPONENT = 60.0
ATT_TILE = 512
KEY_TILE = 256
CUM_BLOCK = 128
NORM_CHUNKS = 2
PAIRS_PER_TRIP = 4
LANES = 128
VMEM_LIMIT = 48 * 1024 * 1024
POST_VMEM_LIMIT = 58 * 1024 * 1024


def _dot(a, b):
    return jnp.dot(a, b, preferred_element_type=F32)


def _split3(x):
    hi = x.astype(BF16)
    rem = x - hi.astype(F32)
    mid = rem.astype(BF16)
    lo = (rem - mid.astype(F32)).astype(BF16)
    return hi, mid, lo


def _dot_nt(a, b):
    return lax.dot_general(a, b, (((1,), (1,)), ((), ())), preferred_element_type=F32)


def _adaln_kernel(ct_ref, w_ref, b_ref, o_ref):
    c_t = ct_ref[...]
    s_t = c_t / (1.0 + jnp.exp(-c_t))
    w = w_ref[...]
    rows = [jnp.sum(s_t[:, b:b + 1] * w, axis=0, keepdims=True) for b in range(c_t.shape[1])]
    o_ref[...] = jnp.concatenate(rows, axis=0) + b_ref[...]


def _adaln(c, w_ada, b_ada):
    bsz, d = c.shape
    n = w_ada.shape[1]
    return pl.pallas_call(
        _adaln_kernel,
        grid=(n // d,),
        in_specs=[pl.BlockSpec((d, bsz), lambda j: (0, 0)),
                  pl.BlockSpec((d, d), lambda j: (0, j)),
                  pl.BlockSpec((1, d), lambda j: (0, j))],
        out_specs=pl.BlockSpec((bsz, d), lambda j: (0, j)),
        out_shape=jax.ShapeDtypeStruct((bsz, n), F32),
        compiler_params=pltpu.CompilerParams(dimension_semantics=("arbitrary",)),
        name="adaln",
    )(c.T, w_ada, b_ada.reshape(1, n))


TRANSPOSED_GROUPS = (0, 2, 3, 5)


def _wprep_kernel(w_ref, *o_refs):
    g = pl.program_id(0)
    for k, o_ref in enumerate(o_refs):
        @pl.when(g == k)
        def _(k=k, o_ref=o_ref):
            w = w_ref[...]
            o_ref[...] = (w.T if k in TRANSPOSED_GROUPS else w).astype(BF16)


def _project_weights(w_in, layer, width):
    d = w_in.shape[1]
    shapes = [(width, d) if k in TRANSPOSED_GROUPS else (d, width) for k in range(6)]
    return tuple(pl.pallas_call(
        _wprep_kernel,
        grid=(6,),
        in_specs=[pl.BlockSpec((None, d, width), lambda g: (layer, 0, g))],
        out_specs=[pl.BlockSpec(s, lambda g: (0, 0)) for s in shapes],
        out_shape=[jax.ShapeDtypeStruct(s, BF16) for s in shapes],
        compiler_params=pltpu.CompilerParams(dimension_semantics=("arbitrary",)),
        name="wprep",
    )(w_in))


def _inproj_kernel(x_ref, sc_ref, sh_ref,
                   wqm_ref, wkm_ref, wvm_ref, wqf_ref, wkf_ref, wvf_ref, wf_ref, bf_ref,
                   gqm_ref, gkm_ref, gqf_ref, gkf_ref,
                   cos_t_ref, sin_t_ref, rc_ref, rs1_ref, rs2_ref,
                   j_ref, place_ref, kones_ref,
                   qm_o, km_o, vm_o, bias_o, qf_o, kf_o, vf_o, ka_o, qa_o,
                   kmean_sc, carry_sc, *, tm):
    i = pl.program_id(1)
    nb = tm // MOBA_BLOCK

    @pl.when(i == 0)
    def _():
        kmean_sc[...] = jnp.zeros_like(kmean_sc)
        carry_sc[...] = jnp.zeros_like(carry_sc)

    rows_per_chunk = tm // NORM_CHUNKS
    hb_chunks = []
    for ch in range(NORM_CHUNKS):
        x = x_ref[0, ch * rows_per_chunk:(ch + 1) * rows_per_chunk, :]
        ms = jnp.mean(x * x, axis=-1, keepdims=True)
        h = x * lax.rsqrt(ms + EPS) * (1.0 + sc_ref[...]) + sh_ref[...]
        hb_chunks.append(h.astype(BF16))

    def proj(w):
        return jnp.concatenate([_dot(hb, w) for hb in hb_chunks], axis=0)

    def proj_t(w_t):
        return jnp.concatenate([_dot_nt(w_t, hb) for hb in hb_chunks], axis=1)

    def q_feature_major(w_ref, g_ref, rope):
        q_t = proj_t(w_ref[...])
        gain = g_ref[...]
        heads = []
        for hd in range(GROUP_HEADS):
            xh = q_t[hd * HEAD_DIM:(hd + 1) * HEAD_DIM, :]
            ss = jnp.sum(xh * xh, axis=0, keepdims=True)
            xh = xh * lax.rsqrt(ss * (1.0 / HEAD_DIM) + EPS) * gain
            if rope:
                x1 = xh[0:ROPE_HALF, :]
                x2 = xh[ROPE_HALF:ROPE_DIM, :]
                cs = cos_t_ref[...]
                sn = sin_t_ref[...]
                xh = jnp.concatenate(
                    [x1 * cs - x2 * sn, x2 * cs + x1 * sn, xh[ROPE_DIM:, :]], axis=0)
            heads.append(xh)
        return jnp.concatenate(heads, axis=0)

    def k_token_major(w_ref, g_ref, rope):
        k = proj(w_ref[...])
        chunks = []
        for c in range(GROUP_WIDTH // 256):
            seg = k[:, c * 256:(c + 1) * 256]
            msq = _dot((seg * seg).astype(BF16), j_ref[...])
            chunks.append(seg * lax.rsqrt(msq + EPS) * g_ref[:, c * 256:(c + 1) * 256])
        k = jnp.concatenate(chunks, axis=1)
        if rope:
            outs = []
            for c in range(GROUP_WIDTH // LANES):
                seg = k[:, c * LANES:(c + 1) * LANES]
                outs.append(seg * rc_ref[...]
                            + pltpu.roll(seg, ROPE_HALF, 1) * rs1_ref[...]
                            + pltpu.roll(seg, LANES - ROPE_HALF, 1) * rs2_ref[...])
            k = jnp.concatenate(outs, axis=1)
        return k

    def store_tiles(o_ref, val_t, width=ATT_TILE):
        for t in range(tm // width):
            o_ref[0, t] = val_t[:, t * width:(t + 1) * width].astype(o_ref.dtype)

    qm_t = q_feature_major(wqm_ref, gqm_ref, True)
    store_tiles(qm_o, qm_t)
    km = k_token_major(wkm_ref, gkm_ref, True)
    km_o[0] = km.astype(BF16)
    store_tiles(vm_o, proj_t(wvm_ref[...]), KEY_TILE)

    blk0 = i * nb
    lane_head = lax.broadcasted_iota(jnp.int32, (1, GROUP_WIDTH), 1) // HEAD_DIM
    for sb in range(nb):
        kmean = jnp.mean(km[sb * MOBA_BLOCK:(sb + 1) * MOBA_BLOCK, :], axis=0, keepdims=True)
        for hd in range(GROUP_HEADS):
            row = hd * 16 + blk0 + sb
            kmean_sc[pl.ds(row, 1), :] = jnp.where(lane_head == hd, kmean, 0.0)

    km_hi, km_lo, _ = _split3(kmean_sc[...])
    q_hi, q_lo, _ = _split3(qm_t)
    gate_t = _dot(km_hi, q_hi) + _dot(km_hi, q_lo) + _dot(km_lo, q_hi)
    tok_blk = blk0 + lax.broadcasted_iota(jnp.int32, (16, tm), 1) // MOBA_BLOCK
    n_idx = lax.broadcasted_iota(jnp.int32, (16, tm), 0)
    valid = n_idx < tok_blk

    zeros48 = jnp.zeros((HEAD_DIM - 16, tm), F32)
    pair_rows = []
    for hd in range(GROUP_HEADS):
        g = jnp.where(valid, gate_t[hd * 16:(hd + 1) * 16, :], -jnp.inf)
        picked = jnp.zeros((16, tm), jnp.int32)
        for _ in range(MOBA_TOPK):
            best = jnp.max(g, axis=0, keepdims=True)
            first = jnp.min(jnp.where(g == best, n_idx, 16), axis=0, keepdims=True)
            hit = n_idx == first
            picked = jnp.where(hit, 1, picked)
            g = jnp.where(hit, -jnp.inf, g)
        sel = (valid & (picked > 0)) | (n_idx == tok_blk)
        pair_rows += [jnp.where(sel, 0.0, NEG_BIG), zeros48]
    store_tiles(bias_o, jnp.concatenate(pair_rows, axis=0))

    store_tiles(qf_o, q_feature_major(wqf_ref, gqf_ref, False))
    kf_o[0] = k_token_major(wkf_ref, gkf_ref, False).astype(BF16)
    store_tiles(vf_o, proj_t(wvf_ref[...]), KEY_TILE)

    f = proj(wf_ref[...]) + bf_ref[...]
    logf = jnp.minimum(f, 0.0) - jnp.log1p(jnp.exp(-jnp.abs(f)))
    live = lax.broadcasted_iota(jnp.int32, (1, LANES), 1) < GROUP_HEADS
    logf = jnp.where(live, logf, 0.0)
    r_i = lax.broadcasted_iota(jnp.int32, (CUM_BLOCK, CUM_BLOCK), 0)
    c_i = lax.broadcasted_iota(jnp.int32, (CUM_BLOCK, CUM_BLOCK), 1)
    tri = jnp.where(r_i >= c_i, 1.0, 0.0).astype(BF16)
    n_blk = tm // CUM_BLOCK
    wide = jnp.concatenate([logf[b * CUM_BLOCK:(b + 1) * CUM_BLOCK, :] for b in range(n_blk)],
                           axis=1)
    parts = _split3(wide)
    prefix = _dot(tri, parts[0]) + _dot(tri, parts[1]) + _dot(tri, parts[2])
    carry = carry_sc[...]
    cums = []
    for blk in range(n_blk):
        c_blk = prefix[:, blk * LANES:(blk + 1) * LANES] + carry
        carry = c_blk[CUM_BLOCK - 1:CUM_BLOCK, :]
        cums.append(c_blk)
    carry_sc[...] = carry
    nc = jnp.concatenate(cums, axis=0) * (-LOG2E)
    hi, mid, lo = [part.astype(F32) for part in _split3(nc)]
    packed = hi + pltpu.roll(mid, GROUP_HEADS, 1) + pltpu.roll(lo, 2 * GROUP_HEADS, 1)
    ka_o[0] = (_dot(packed.astype(BF16), place_ref[...]) + kones_ref[...]).astype(BF16)

    nc_t = nc.T[0:GROUP_HEADS, :]
    hi_t = nc_t.astype(BF16).astype(F32)
    rem_t = nc_t - hi_t
    mid_t = rem_t.astype(BF16).astype(F32)
    lo_t = (rem_t - mid_t).astype(BF16).astype(F32)
    r8 = lax.broadcasted_iota(jnp.int32, (8, tm), 0)
    zeros56 = jnp.zeros((HEAD_DIM - 8, tm), F32)
    qa_rows = []
    for hd in range(GROUP_HEADS):
        blk = jnp.where(r8 < 3, 1.0,
                        jnp.where(r8 == 3, -hi_t[hd:hd + 1, :],
                                  jnp.where(r8 == 4, -mid_t[hd:hd + 1, :],
                                            jnp.where(r8 == 5, -lo_t[hd:hd + 1, :], 0.0))))
        qa_rows += [blk, zeros56]
    store_tiles(qa_o, jnp.concatenate(qa_rows, axis=0))


def _inproj(x, sc, sh, wts, consts, tm):
    bsz, s, d = x.shape
    nt = s // ATT_TILE
    grid = (bsz, s // tm)
    wqm, wkm, wvm, wqf, wkf, wvf, wf, bfor, gqm, gkm, gqf, gkf = wts
    cos_t, sin_t, rc, rs1, rs2, jmat, place, kones = consts

    def const(a):
        nd = a.ndim
        return pl.BlockSpec(a.shape, lambda b, i, _n=nd: (0,) * _n,
                            pipeline_mode=pl.Buffered(1))

    mod_spec = pl.BlockSpec((None, 1, d), lambda b, i: (b, 0, 0))
    in_specs = [pl.BlockSpec((1, tm, d), lambda b, i: (b, i, 0)), mod_spec, mod_spec]
    in_specs += [const(a) for a in (wqm, wkm, wvm, wqf, wkf, wvf, wf, bfor,
                                    gqm, gkm, gqf, gkf)]
    in_specs += [pl.BlockSpec((ROPE_HALF, tm), lambda b, i: (0, i)),
                 pl.BlockSpec((ROPE_HALF, tm), lambda b, i: (0, i)),
                 pl.BlockSpec((tm, LANES), lambda b, i: (i, 0)),
                 pl.BlockSpec((tm, LANES), lambda b, i: (i, 0)),
                 pl.BlockSpec((tm, LANES), lambda b, i: (i, 0)),
                 const(jmat), const(place), const(kones)]

    tiles = tm // ATT_TILE
    feat_spec = pl.BlockSpec((1, tiles, GROUP_WIDTH, ATT_TILE), lambda b, i: (b, i, 0, 0))
    tok_spec = pl.BlockSpec((1, tm, GROUP_WIDTH), lambda b, i: (b, i, 0))
    feat_shape = jax.ShapeDtypeStruct((bsz, nt, GROUP_WIDTH, ATT_TILE), BF16)
    tok_shape = jax.ShapeDtypeStruct((bsz, s, GROUP_WIDTH), BF16)
    val_spec = pl.BlockSpec((1, tm // KEY_TILE, GROUP_WIDTH, KEY_TILE),
                            lambda b, i: (b, i, 0, 0))
    val_shape = jax.ShapeDtypeStruct((bsz, s // KEY_TILE, GROUP_WIDTH, KEY_TILE), BF16)
    out_specs = [feat_spec, tok_spec, val_spec, feat_spec,
                 feat_spec, tok_spec, val_spec, tok_spec, feat_spec]
    out_shape = [feat_shape, tok_shape, val_shape, feat_shape,
                 feat_shape, tok_shape, val_shape, tok_shape, feat_shape]
    return pl.pallas_call(
        functools.partial(_inproj_kernel, tm=tm),
        grid=grid,
        in_specs=in_specs,
        out_specs=out_specs,
        out_shape=out_shape,
        scratch_shapes=[pltpu.VMEM((GROUP_HEADS * 16, GROUP_WIDTH), F32),
                        pltpu.VMEM((1, LANES), F32)],
        compiler_params=pltpu.CompilerParams(
            dimension_semantics=("arbitrary", "arbitrary"),
            vmem_limit_bytes=VMEM_LIMIT),
        name="inproj",
    )(x, sc, sh, wqm, wkm, wvm, wqf, wkf, wvf, wf, bfor, gqm, gkm, gqf, gkf,
      cos_t, sin_t, rc, rs1, rs2, jmat, place, kones)


def _attn_kernel(q_ref, qa_ref, k_ref, ka_ref, v_ref, o_ref, *, t):
    i = pl.program_id(2)
    q = q_ref[0, 0]
    qa = qa_ref[0, 0]
    row = lax.broadcasted_iota(jnp.int32, (PAIR_WIDTH, t), 0)
    rhs = []
    for e in range(2):
        keep = (row >= e * HEAD_DIM) & (row < (e + 1) * HEAD_DIM)
        zero = jnp.zeros_like(q)
        rhs.append(jnp.concatenate([jnp.where(keep, q, zero), jnp.where(keep, qa, zero)],
                                   axis=0))

    def tile_scores(j):
        start = pl.multiple_of(j * t, t)
        lhs = jnp.concatenate([k_ref[0, pl.ds(start, t), :], ka_ref[0, pl.ds(start, t), :]],
                              axis=1)
        return [_dot(lhs, rhs[e]) for e in range(2)]

    def values(j, e):
        rows = slice(e * HEAD_DIM, (e + 1) * HEAD_DIM)
        slabs = t // KEY_TILE
        return jnp.concatenate([v_ref[0, slabs * j + h, rows, :] for h in range(slabs)],
                               axis=1)

    key_i = lax.broadcasted_iota(jnp.int32, (t, t), 0)
    qry_i = lax.broadcasted_iota(jnp.int32, (t, t), 1)
    causal = key_i <= qry_i
    state = []
    for e, s_t in enumerate(tile_scores(i)):
        s_t = jnp.where(causal, s_t, NEG_BIG)
        m = jnp.max(s_t, axis=0, keepdims=True)
        p = jnp.exp2(s_t - m)
        l = jnp.sum(p, axis=0, keepdims=True)
        acc = _dot(values(i, e), p.astype(BF16))
        state += [m, l, acc]

    def body(j, carry):
        out = []
        for e, s_t in enumerate(tile_scores(j)):
            m, l, acc = carry[3 * e:3 * e + 3]
            m_new = jnp.maximum(m, jnp.max(s_t, axis=0, keepdims=True))
            alpha = jnp.exp2(m - m_new)
            p = jnp.exp2(s_t - m_new)
            l = alpha * l + jnp.sum(p, axis=0, keepdims=True)
            acc = alpha * acc + _dot(values(j, e), p.astype(BF16))
            out += [m_new, l, acc]
        return tuple(out)

    state = lax.fori_loop(0, i, body, tuple(state))
    o_t = jnp.concatenate([state[2] / state[1], state[5] / state[4]], axis=0)
    o_ref[0] = o_t.T


def _attn_bounded_kernel(q_ref, qa_ref, k_ref, ka_ref, v_ref, o_ref, p_sc, acc_sc, l_sc, *, t,
                         ka_const):
    i = pl.program_id(1)
    kt = KEY_TILE
    row = lax.broadcasted_iota(jnp.int32, (PAIR_WIDTH, t), 0)
    key_i = lax.broadcasted_iota(jnp.int32, (kt, t), 0)
    qry_i = lax.broadcasted_iota(jnp.int32, (kt, t), 1)
    assert t == 2 * kt


    def rhs_of(pr):
        feat = slice(pr * PAIR_WIDTH, (pr + 1) * PAIR_WIDTH)
        q = q_ref[0, 0, feat, :]
        qa = qa_ref[0, 0, feat, :]
        out = []
        for e in range(2):
            keep = (row >= e * HEAD_DIM) & (row < (e + 1) * HEAD_DIM)
            zero = jnp.zeros_like(q)
            out.append(jnp.concatenate([jnp.where(keep, q, zero), jnp.where(keep, qa, zero)],
                                       axis=0))
        return out

    for grp in range(N_PAIRS // PAIRS_PER_TRIP):
        prs = list(range(grp * PAIRS_PER_TRIP, (grp + 1) * PAIRS_PER_TRIP))
        rhs = {pr: rhs_of(pr) for pr in prs}

        def probs(pr, start, n_keys, row0, mask=None, q0=0, first=False, rhs=rhs):
            feat = slice(pr * PAIR_WIDTH, (pr + 1) * PAIR_WIDTH)
            k_aug = (ka_ref[0, pl.ds(start, n_keys), :] if ka_const
                     else ka_ref[0, pl.ds(start, n_keys), feat])
            lhs = jnp.concatenate([k_ref[0, pl.ds(start, n_keys), feat], k_aug], axis=1)
            for e in range(2):
                s_t = _dot(lhs, rhs[pr][e][:, q0:])
                if mask is not None:
                    s_t = jnp.where(mask[:, q0:], s_t, NEG_BIG)
                p = jnp.exp2(s_t)
                p_sc[pr, e, row0:row0 + n_keys, q0:] = p.astype(BF16)
                part = p[0:8, :]
                for r in range(8, n_keys, 8):
                    part = part + p[r:r + 8, :]
                if first:
                    l_sc[pr, e] = part
                else:
                    l_sc[pr, e, :, q0:] += part

        def values(pr, j):
            rows = slice(pr * PAIR_WIDTH, (pr + 1) * PAIR_WIDTH)
            v_pair = jnp.concatenate([v_ref[0, 2 * j, rows, :], v_ref[0, 2 * j + 1, rows, :]],
                                     axis=1)
            for e in range(2):
                upd = _dot(v_pair, p_sc[pr, e])
                acc_sc[pr, e] += upd[e * HEAD_DIM:(e + 1) * HEAD_DIM, :]

        diag = pl.multiple_of(i * t, t)
        for pr in prs:
            probs(pr, diag, kt, 0, key_i <= qry_i, first=True)
            probs(pr, diag + kt, kt, kt, key_i + kt <= qry_i, q0=kt)
            for e in range(2):
                p_sc[pr, e, kt:, :kt] = jnp.zeros((kt, kt), BF16)
                acc_sc[pr, e] = jnp.zeros((HEAD_DIM, t), F32)

        def body(j, j_prev, probs=probs, values=values, prs=prs):
            for pr in prs:
                values(pr, j_prev)
                probs(pr, pl.multiple_of(j * t, t), t, 0)
            return j

        def unrolled(n, body=body):
            def trip(m, j_prev):
                for u in range(n):
                    j_prev = body(n * m + u, j_prev)
                return j_prev
            return trip

        done = 0
        j_prev = i
        for n in (4, 2, 1):
            trips = (i - done) // n
            j_prev = lax.fori_loop(done // n, done // n + trips, unrolled(n), j_prev)
            done = done + trips * n
        for pr in prs:
            values(pr, j_prev)
        for pr in prs:
            o_t = jnp.concatenate(
                [acc_sc[pr, e] / jnp.sum(l_sc[pr, e], axis=0, keepdims=True)
                 for e in range(2)], axis=0)
            o_ref[0, :, pr * PAIR_WIDTH:(pr + 1) * PAIR_WIDTH] = o_t.T


def _attention_online(q_t, qa, k, ka, v_t, *, ka_const, name):
    bsz, nt, _, t = q_t.shape
    s = k.shape[1]
    grid = (bsz, N_PAIRS, nt)
    if ka_const:
        ka_spec = pl.BlockSpec((1, s, PAIR_WIDTH), lambda b, p, i: (0, 0, 0))
    else:
        ka_spec = pl.BlockSpec((1, s, PAIR_WIDTH), lambda b, p, i: (b, 0, p))
    return pl.pallas_call(
        functools.partial(_attn_kernel, t=t),
        grid=grid,
        in_specs=[pl.BlockSpec((1, 1, PAIR_WIDTH, t), lambda b, p, i: (b, i, p, 0)),
                  pl.BlockSpec((1, 1, PAIR_WIDTH, t), lambda b, p, i: (b, i, p, 0)),
                  pl.BlockSpec((1, s, PAIR_WIDTH), lambda b, p, i: (b, 0, p)),
                  ka_spec,
                  pl.BlockSpec((1, s // KEY_TILE, PAIR_WIDTH, KEY_TILE),
                               lambda b, p, i: (b, 0, p, 0))],
        out_specs=pl.BlockSpec((1, t, PAIR_WIDTH), lambda b, p, i: (b, i, p)),
        out_shape=jax.ShapeDtypeStruct((bsz, s, GROUP_WIDTH), F32),
        compiler_params=pltpu.CompilerParams(
            dimension_semantics=("arbitrary", "arbitrary", "arbitrary"),
            vmem_limit_bytes=VMEM_LIMIT),
        name=name,
    )(q_t, qa, k, ka, v_t)


def _attention_bounded(q_t, qa, k, ka, v_t, *, ka_const, name):
    bsz, nt, _, t = q_t.shape
    s = k.shape[1]
    feat_spec = pl.BlockSpec((1, 1, GROUP_WIDTH, t), lambda b, i: (b, i, 0, 0))
    tok_spec = pl.BlockSpec((1, s, GROUP_WIDTH), lambda b, i: (b, 0, 0))
    ka_spec = pl.BlockSpec((1, s, PAIR_WIDTH), lambda b, i: (0, 0, 0)) if ka_const else tok_spec
    return pl.pallas_call(
        functools.partial(_attn_bounded_kernel, t=t, ka_const=ka_const),
        grid=(bsz, nt),
        in_specs=[feat_spec, feat_spec, tok_spec, ka_spec,
                  pl.BlockSpec((1, s // KEY_TILE, GROUP_WIDTH, KEY_TILE),
                               lambda b, i: (b, 0, 0, 0))],
        out_specs=pl.BlockSpec((1, t, GROUP_WIDTH), lambda b, i: (b, i, 0)),
        out_shape=jax.ShapeDtypeStruct((bsz, s, GROUP_WIDTH), F32),
        scratch_shapes=[pltpu.VMEM((N_PAIRS, 2, t, t), BF16),
                        pltpu.VMEM((N_PAIRS, 2, HEAD_DIM, t), F32),
                        pltpu.VMEM((N_PAIRS, 2, 8, t), F32)],
        compiler_params=pltpu.CompilerParams(
            dimension_semantics=("arbitrary", "arbitrary"),
            vmem_limit_bytes=VMEM_LIMIT),
        name=name,
    )(q_t, qa, k, ka, v_t)


def _attention_dispatch(g_q, g_k, q_t, qa, k, ka, v_t, *, ka_const, name):
    bound = HEAD_DIM * jnp.max(jnp.abs(g_q)) * jnp.max(jnp.abs(g_k))
    return lax.cond(
        bound <= MAX_SAFE_EXPONENT,
        functools.partial(_attention_bounded, ka_const=ka_const, name=name + "_bounded"),
        functools.partial(_attention_online, ka_const=ka_const, name=name + "_online"),
        q_t, qa, k, ka, v_t)


def _post_kernel(x_ref, om_ref, of_ref, ga_ref, scm_ref, shm_ref, gm_ref,
                 gom_ref, gof_ref, wout_ref, w1_ref, w2_ref, o_ref, *, ff_chunk):
    def rms(v):
        return v * lax.rsqrt(jnp.mean(v * v, axis=-1, keepdims=True) + EPS)

    mixed = jnp.concatenate([rms(om_ref[0]) * gom_ref[...],
                             rms(of_ref[0]) * gof_ref[...]], axis=1)
    x1 = x_ref[0] + ga_ref[...] * _dot(mixed, wout_ref[...])
    h = rms(x1) * (1.0 + scm_ref[...]) + shm_ref[...]
    d_ff = w1_ref.shape[1]
    y = None
    for c in range(d_ff // ff_chunk):
        hid = _dot(h, w1_ref[:, c * ff_chunk:(c + 1) * ff_chunk])
        hid = jnp.square(jnp.maximum(hid, 0.0))
        part = _dot(hid, w2_ref[c * ff_chunk:(c + 1) * ff_chunk, :])
        y = part if y is None else y + part
    o_ref[0] = x1 + gm_ref[...] * y


def _post(x, o_m, o_f, mod4, g_out_m, g_out_f, w_out, w_ff1, w_ff2, tm):
    bsz, s, d = x.shape
    d_ff = w_ff1.shape[1]

    def const(a):
        nd = a.ndim
        return pl.BlockSpec(a.shape, lambda b, i, _n=nd: (0,) * _n,
                            pipeline_mode=pl.Buffered(1))

    def mod_spec(k):
        return pl.BlockSpec((None, None, 1, d), lambda b, i, _k=k: (b, _k, 0, 0))

    tok = lambda w: pl.BlockSpec((1, tm, w), lambda b, i: (b, i, 0))
    return pl.pallas_call(
        functools.partial(_post_kernel, ff_chunk=1024),
        grid=(bsz, s // tm),
        in_specs=[tok(d), tok(GROUP_WIDTH), tok(GROUP_WIDTH),
                  mod_spec(2), mod_spec(4), mod_spec(3), mod_spec(5),
                  const(g_out_m), const(g_out_f), const(w_out), const(w_ff1), const(w_ff2)],
        out_specs=tok(d),
        out_shape=jax.ShapeDtypeStruct((bsz, s, d), F32),
        compiler_params=pltpu.CompilerParams(
            dimension_semantics=("arbitrary", "arbitrary"),
            vmem_limit_bytes=POST_VMEM_LIMIT),
        name="post",
    )(x, o_m, o_f, mod4, mod4, mod4, mod4, g_out_m, g_out_f, w_out, w_ff1, w_ff2)


def _rope_tables(s):
    inv_freq = ROPE_THETA ** (-np.arange(0, ROPE_DIM, 2, dtype=np.float64) / ROPE_DIM)
    ang = np.arange(s, dtype=np.float64)[:, None] * inv_freq[None, :]
    cos, sin = np.cos(ang), np.sin(ang)
    rc = np.ones((s, LANES)); rs1 = np.zeros((s, LANES)); rs2 = np.zeros((s, LANES))
    for base in (0, HEAD_DIM):
        rc[:, base:base + ROPE_HALF] = cos
        rc[:, base + ROPE_HALF:base + ROPE_DIM] = cos
        rs2[:, base:base + ROPE_HALF] = -sin
        rs1[:, base + ROPE_HALF:base + ROPE_DIM] = sin
    f = lambda a: jnp.asarray(a, dtype=F32)
    return f(cos.T), f(sin.T), f(rc), f(rs1), f(rs2)


def _static_mats(s):
    idx = np.arange(256)
    jmat = (idx[:, None] // HEAD_DIM == idx[None, :] // HEAD_DIM) / HEAD_DIM
    place = np.zeros((LANES, GROUP_WIDTH))
    for part in range(3):
        for hd in range(GROUP_HEADS):
            place[part * GROUP_HEADS + hd,
                  (hd // 2) * PAIR_WIDTH + (hd % 2) * HEAD_DIM + part] = 1.0
    kones = np.zeros((1, GROUP_WIDTH))
    for hd in range(GROUP_HEADS):
        kones[0, hd * HEAD_DIM + 3:hd * HEAD_DIM + 6] = 1.0
    ka_moba = np.zeros((1, s, PAIR_WIDTH))
    blk = np.arange(s) // MOBA_BLOCK
    for e in range(2):
        ka_moba[0, np.arange(s), e * HEAD_DIM + blk] = 1.0
    b = lambda a: jnp.asarray(a, dtype=BF16)
    return b(jmat), b(place), jnp.asarray(kones, dtype=F32), b(ka_moba)


def kernel(x, c, w_ada, b_ada, w_in, b_forget, g_qn_moba, g_kn_moba, g_qn_fox, g_kn_fox,
           g_out_moba, g_out_fox, w_out, w_ff1, w_ff2):
    bsz, s, d = x.shape
    depth = w_ada.shape[0]
    w = GROUP_WIDTH
    assert s % MOBA_BLOCK == 0 and s // MOBA_BLOCK <= 16
    tm_in = 1024
    tm_post = 512
    cos_t, sin_t, rc, rs1, rs2 = _rope_tables(s)
    jmat, place, kones, ka_moba = _static_mats(s)
    q_scale = HEAD_DIM ** -0.5 * LOG2E

    for l in range(depth):
        mod = _adaln(c, w_ada[l], b_ada[l])
        mod4 = mod.reshape(bsz, 6, 1, d)

        wl = w_in[l]
        wf =jnp.zeros((d, LANES), F32).at[:, :GROUP_HEADS].set(wl[:, 6 * w:]).astype(BF16)
        bfor = jnp.zeros((1, LANES), F32).at[0, :GROUP_HEADS].set(b_forget[l])
        q_gain = lambda g: jnp.broadcast_to((g * q_scale)[:, None], (HEAD_DIM, tm_in))
        k_gain = lambda g: jnp.tile(g, GROUP_HEADS).reshape(1, w)
        wts = _project_weights(w_in, l, w) + (
               wf, bfor,
               q_gain(g_qn_moba[l]), k_gain(g_kn_moba[l]),
               q_gain(g_qn_fox[l]), k_gain(g_kn_fox[l]))
        consts = (cos_t, sin_t, rc, rs1, rs2, jmat, place, kones)

        qm_t, km, vm_t, bias_t, qf_t, kf, vf_t, ka_f, qa_f = _inproj(
            x, mod4[:, 1], mod4[:, 0], wts, consts, tm_in)

        o_m = _attention_dispatch(g_qn_moba[l] * q_scale, g_kn_moba[l],
                                  qm_t, bias_t, km, ka_moba, vm_t, ka_const=True,
                                  name="attn_moba")
        o_f = _attention_dispatch(g_qn_fox[l] * q_scale, g_kn_fox[l],
                                  qf_t, qa_f, kf, ka_f, vf_t, ka_const=False,
                                  name="attn_fox")

        x = _post(x, o_m, o_f, mod4,
                  g_out_moba[l].reshape(1, w), g_out_fox[l].reshape(1, w),
                  w_out[l], w_ff1[l], w_ff2[l],
                  tm_post)
    return x
```

```python
import functools
import math

import numpy as np
import jax
import jax.numpy as jnp
from jax import lax
from jax.experimental import pallas as pl
from jax.experimental.pallas import tpu as pltpu

F32 = jnp.float32
BF16 = jnp.bfloat16

HEAD_DIM = 64
GROUP_HEADS = 8
GROUP_WIDTH = GROUP_HEADS * HEAD_DIM
PAIR_WIDTH = 2 * HEAD_DIM
N_PAIRS = GROUP_HEADS // 2
MOBA_BLOCK = 256
MOBA_TOPK = 3
ROPE_THETA = 500000.0
ROPE_DIM = HEAD_DIM // 4
ROPE_HALF = ROPE_DIM // 2
EPS = 1e-6
LOG2E = math.log2(math.e)
NEG_BIG = -1e30
MAX_SAFE_EXPONENT = 60.0
ATT_TILE = 512
KEY_TILE = 256
CUM_BLOCK = 128
NORM_CHUNKS = 2
PAIRS_PER_TRIP = 4
LANES = 128
VMEM_LIMIT = 48 * 1024 * 1024
POST_VMEM_LIMIT = 58 * 1024 * 1024


def _dot(a, b):
    return jnp.dot(a, b, preferred_element_type=F32)


def _split3(x):
    hi = x.astype(BF16)
    rem = x - hi.astype(F32)
    mid = rem.astype(BF16)
    lo = (rem - mid.astype(F32)).astype(BF16)
    return hi, mid, lo


def _dot_nt(a, b):
    return lax.dot_general(a, b, (((1,), (1,)), ((), ())), preferred_element_type=F32)


def _adaln_kernel(ct_ref, w_ref, b_ref, o_ref):
    c_t = ct_ref[...]
    s_t = c_t / (1.0 + jnp.exp(-c_t))
    w = w_ref[...]
    rows = [jnp.sum(s_t[:, b:b + 1] * w, axis=0, keepdims=True) for b in range(c_t.shape[1])]
    o_ref[...] = jnp.concatenate(rows, axis=0) + b_ref[...]


def _adaln(c, w_ada, b_ada):
    bsz, d = c.shape
    n = w_ada.shape[1]
    return pl.pallas_call(
        _adaln_kernel,
        grid=(n // d,),
        in_specs=[pl.BlockSpec((d, bsz), lambda j: (0, 0)),
                  pl.BlockSpec((d, d), lambda j: (0, j)),
                  pl.BlockSpec((1, d), lambda j: (0, j))],
        out_specs=pl.BlockSpec((bsz, d), lambda j: (0, j)),
        out_shape=jax.ShapeDtypeStruct((bsz, n), F32),
        compiler_params=pltpu.CompilerParams(dimension_semantics=("arbitrary",)),
        name="adaln",
    )(c.T, w_ada, b_ada.reshape(1, n))


TRANSPOSED_GROUPS = (0, 2, 3, 5)


def _wprep_kernel(w_ref, *o_refs):
    g = pl.program_id(0)
    for k, o_ref in enumerate(o_refs):
        @pl.when(g == k)
        def _(k=k, o_ref=o_ref):
            w = w_ref[...]
            o_ref[...] = (w.T if k in TRANSPOSED_GROUPS else w).astype(BF16)


def _project_weights(w_in, layer, width):
    d = w_in.shape[1]
    shapes = [(width, d) if k in TRANSPOSED_GROUPS else (d, width) for k in range(6)]
    return tuple(pl.pallas_call(
        _wprep_kernel,
        grid=(6,),
        in_specs=[pl.BlockSpec((None, d, width), lambda g: (layer, 0, g))],
        out_specs=[pl.BlockSpec(s, lambda g: (0, 0)) for s in shapes],
        out_shape=[jax.ShapeDtypeStruct(s, BF16) for s in shapes],
        compiler_params=pltpu.CompilerParams(dimension_semantics=("arbitrary",)),
        name="wprep",
    )(w_in))


def _inproj_kernel(x_ref, sc_ref, sh_ref,
                   wqm_ref, wkm_ref, wvm_ref, wqf_ref, wkf_ref, wvf_ref, wf_ref, bf_ref,
                   gqm_ref, gkm_ref, gqf_ref, gkf_ref,
                   cos_t_ref, sin_t_ref, rc_ref, rs1_ref, rs2_ref,
                   j_ref, place_ref, kones_ref,
                   qm_o, km_o, vm_o, bias_o, qf_o, kf_o, vf_o, ka_o, qa_o,
                   kmean_sc, carry_sc, *, tm):
    i = pl.program_id(1)
    nb = tm // MOBA_BLOCK

    @pl.when(i == 0)
    def _():
        kmean_sc[...] = jnp.zeros_like(kmean_sc)
        carry_sc[...] = jnp.zeros_like(carry_sc)

    rows_per_chunk = tm // NORM_CHUNKS
    hb_chunks = []
    for ch in range(NORM_CHUNKS):
        x = x_ref[0, ch * rows_per_chunk:(ch + 1) * rows_per_chunk, :]
        ms = jnp.mean(x * x, axis=-1, keepdims=True)
        h = x * lax.rsqrt(ms + EPS) * (1.0 + sc_ref[...]) + sh_ref[...]
        hb_chunks.append(h.astype(BF16))

    def proj(w):
        return jnp.concatenate([_dot(hb, w) for hb in hb_chunks], axis=0)

    def proj_nt(w_t):
        return jnp.concatenate([_dot_nt(hb, w_t) for hb in hb_chunks], axis=0)

    def proj_t(w_t):
        return jnp.concatenate([_dot_nt(w_t, hb) for hb in hb_chunks], axis=1)

    def q_feature_major(w_ref, g_ref, rope):
        q_t = proj_t(w_ref[...])
        gain = g_ref[...]
        heads = []
        for hd in range(GROUP_HEADS):
            xh = q_t[hd * HEAD_DIM:(hd + 1) * HEAD_DIM, :]
            ss = jnp.sum(xh * xh, axis=0, keepdims=True)
            xh = xh * lax.rsqrt(ss * (1.0 / HEAD_DIM) + EPS) * gain
            if rope:
                x1 = xh[0:ROPE_HALF, :]
                x2 = xh[ROPE_HALF:ROPE_DIM, :]
                cs = cos_t_ref[...]
                sn = sin_t_ref[...]
                xh = jnp.concatenate(
                    [x1 * cs - x2 * sn, x2 * cs + x1 * sn, xh[ROPE_DIM:, :]], axis=0)
            heads.append(xh)
        return jnp.concatenate(heads, axis=0)

    def k_token_major(w_ref, g_ref, rope):
        k = proj_nt(w_ref[...])
        chunks = []
        for c in range(GROUP_WIDTH // 256):
            seg = k[:, c * 256:(c + 1) * 256]
            msq = _dot((seg * seg).astype(BF16), j_ref[...])
            chunks.append(seg * lax.rsqrt(msq + EPS) * g_ref[:, c * 256:(c + 1) * 256])
        k = jnp.concatenate(chunks, axis=1)
        if rope:
            outs = []
            for c in range(GROUP_WIDTH // LANES):
                seg = k[:, c * LANES:(c + 1) * LANES]
                outs.append(seg * rc_ref[...]
                            + pltpu.roll(seg, ROPE_HALF, 1) * rs1_ref[...]
                            + pltpu.roll(seg, LANES - ROPE_HALF, 1) * rs2_ref[...])
            k = jnp.concatenate(outs, axis=1)
        return k

    def store_tiles(o_ref, val_t, width=ATT_TILE):
        for t in range(tm // width):
            o_ref[0, t] = val_t[:, t * width:(t + 1) * width].astype(o_ref.dtype)

    qm_t = q_feature_major(wqm_ref, gqm_ref, True)
    store_tiles(qm_o, qm_t)
    km = k_token_major(wkm_ref, gkm_ref, True)
    km_o[0] = km.astype(BF16)
    store_tiles(vm_o, proj_t(wvm_ref[...]), KEY_TILE)

    blk0 = i * nb
    lane_head = lax.broadcasted_iota(jnp.int32, (1, GROUP_WIDTH), 1) // HEAD_DIM
    for sb in range(nb):
        kmean = jnp.mean(km[sb * MOBA_BLOCK:(sb + 1) * MOBA_BLOCK, :], axis=0, keepdims=True)
        for hd in range(GROUP_HEADS):
            row = hd * 16 + blk0 + sb
            kmean_sc[pl.ds(row, 1), :] = jnp.where(lane_head == hd, kmean, 0.0)

    km_hi, km_lo, _ = _split3(kmean_sc[...])
    q_hi, q_lo, _ = _split3(qm_t)
    gate_t = _dot(km_hi, q_hi) + _dot(km_hi, q_lo) + _dot(km_lo, q_hi)
    tok_blk = blk0 + lax.broadcasted_iota(jnp.int32, (16, tm), 1) // MOBA_BLOCK
    n_idx = lax.broadcasted_iota(jnp.int32, (16, tm), 0)
    valid = n_idx < tok_blk

    zeros48 = jnp.zeros((HEAD_DIM - 16, tm), F32)
    pair_rows = []
    for hd in range(GROUP_HEADS):
        g = jnp.where(valid, gate_t[hd * 16:(hd + 1) * 16, :], -jnp.inf)
        picked = jnp.zeros((16, tm), jnp.int32)
        for _ in range(MOBA_TOPK):
            best = jnp.max(g, axis=0, keepdims=True)
            first = jnp.min(jnp.where(g == best, n_idx, 16), axis=0, keepdims=True)
            hit = n_idx == first
            picked = jnp.where(hit, 1, picked)
            g = jnp.where(hit, -jnp.inf, g)
        sel = (valid & (picked > 0)) | (n_idx == tok_blk)
        pair_rows += [jnp.where(sel, 0.0, NEG_BIG), zeros48]
    store_tiles(bias_o, jnp.concatenate(pair_rows, axis=0))

    store_tiles(qf_o, q_feature_major(wqf_ref, gqf_ref, False))
    kf_o[0] = k_token_major(wkf_ref, gkf_ref, False).astype(BF16)
    store_tiles(vf_o, proj_t(wvf_ref[...]), KEY_TILE)

    f = proj(wf_ref[...]) + bf_ref[...]
    logf = jnp.minimum(f, 0.0) - jnp.log1p(jnp.exp(-jnp.abs(f)))
    live = lax.broadcasted_iota(jnp.int32, (1, LANES), 1) < GROUP_HEADS
    logf = jnp.where(live, logf, 0.0)
    r_i = lax.broadcasted_iota(jnp.int32, (CUM_BLOCK, CUM_BLOCK), 0)
    c_i = lax.broadcasted_iota(jnp.int32, (CUM_BLOCK, CUM_BLOCK), 1)
    tri = jnp.where(r_i >= c_i, 1.0, 0.0).astype(BF16)
    n_blk = tm // CUM_BLOCK
    wide = jnp.concatenate([logf[b * CUM_BLOCK:(b + 1) * CUM_BLOCK, :] for b in range(n_blk)],
                           axis=1)
    parts = _split3(wide)
    prefix = _dot(tri, parts[0]) + _dot(tri, parts[1]) + _dot(tri, parts[2])
    carry = carry_sc[...]
    cums = []
    for blk in range(n_blk):
        c_blk = prefix[:, blk * LANES:(blk + 1) * LANES] + carry
        carry = c_blk[CUM_BLOCK - 1:CUM_BLOCK, :]
        cums.append(c_blk)
    carry_sc[...] = carry
    nc = jnp.concatenate(cums, axis=0) * (-LOG2E)
    hi, mid, lo = [part.astype(F32) for part in _split3(nc)]
    packed = hi + pltpu.roll(mid, GROUP_HEADS, 1) + pltpu.roll(lo, 2 * GROUP_HEADS, 1)
    ka_o[0] = (_dot(packed.astype(BF16), place_ref[...]) + kones_ref[...]).astype(BF16)

    nc_t = nc.T[0:GROUP_HEADS, :]
    hi_t = nc_t.astype(BF16).astype(F32)
    rem_t = nc_t - hi_t
    mid_t = rem_t.astype(BF16).astype(F32)
    lo_t = (rem_t - mid_t).astype(BF16).astype(F32)
    r8 = lax.broadcasted_iota(jnp.int32, (8, tm), 0)
    zeros56 = jnp.zeros((HEAD_DIM - 8, tm), F32)
    qa_rows = []
    for hd in range(GROUP_HEADS):
        blk = jnp.where(r8 < 3, 1.0,
                        jnp.where(r8 == 3, -hi_t[hd:hd + 1, :],
                                  jnp.where(r8 == 4, -mid_t[hd:hd + 1, :],
                                            jnp.where(r8 == 5, -lo_t[hd:hd + 1, :], 0.0))))
        qa_rows += [blk, zeros56]
    store_tiles(qa_o, jnp.concatenate(qa_rows, axis=0))


def _inproj(x, sc, sh, wts, consts, tm):
    bsz, s, d = x.shape
    nt = s // ATT_TILE
    grid = (bsz, s // tm)
    w_t, wf, bfor, gqm, gkm, gqf, gkf = wts
    cos_t, sin_t, rc, rs1, rs2, jmat, place, kones = consts

    def const(a):
        nd = a.ndim
        return pl.BlockSpec(a.shape, lambda b, i, _n=nd: (0,) * _n,
                            pipeline_mode=pl.Buffered(1))

    mod_spec = pl.BlockSpec((None, 1, d), lambda b, i: (b, 0, 0))
    in_specs = [pl.BlockSpec((1, tm, d), lambda b, i: (b, i, 0)), mod_spec, mod_spec]
    in_specs += [pl.BlockSpec((GROUP_WIDTH, d), lambda b, i, _g=g: (_g, 0),
                              pipeline_mode=pl.Buffered(1)) for g in range(6)]
    in_specs += [const(a) for a in (wf, bfor, gqm, gkm, gqf, gkf)]
    in_specs += [pl.BlockSpec((ROPE_HALF, tm), lambda b, i: (0, i)),
                 pl.BlockSpec((ROPE_HALF, tm), lambda b, i: (0, i)),
                 pl.BlockSpec((tm, LANES), lambda b, i: (i, 0)),
                 pl.BlockSpec((tm, LANES), lambda b, i: (i, 0)),
                 pl.BlockSpec((tm, LANES), lambda b, i: (i, 0)),
                 const(jmat), const(place), const(kones)]

    tiles = tm // ATT_TILE
    feat_spec = pl.BlockSpec((1, tiles, GROUP_WIDTH, ATT_TILE), lambda b, i: (b, i, 0, 0))
    tok_spec = pl.BlockSpec((1, tm, GROUP_WIDTH), lambda b, i: (b, i, 0))
    feat_shape = jax.ShapeDtypeStruct((bsz, nt, GROUP_WIDTH, ATT_TILE), BF16)
    tok_shape = jax.ShapeDtypeStruct((bsz, s, GROUP_WIDTH), BF16)
    val_spec = pl.BlockSpec((1, tm // KEY_TILE, GROUP_WIDTH, KEY_TILE),
                            lambda b, i: (b, i, 0, 0))
    val_shape = jax.ShapeDtypeStruct((bsz, s // KEY_TILE, GROUP_WIDTH, KEY_TILE), BF16)
    out_specs = [feat_spec, tok_spec, val_spec, feat_spec,
                 feat_spec, tok_spec, val_spec, tok_spec, feat_spec]
    out_shape = [feat_shape, tok_shape, val_shape, feat_shape,
                 feat_shape, tok_shape, val_shape, tok_shape, feat_shape]
    return pl.pallas_call(
        functools.partial(_inproj_kernel, tm=tm),
        grid=grid,
        in_specs=in_specs,
        out_specs=out_specs,
        out_shape=out_shape,
        scratch_shapes=[pltpu.VMEM((GROUP_HEADS * 16, GROUP_WIDTH), F32),
                        pltpu.VMEM((1, LANES), F32)],
        compiler_params=pltpu.CompilerParams(
            dimension_semantics=("arbitrary", "arbitrary"),
            vmem_limit_bytes=VMEM_LIMIT),
        name="inproj",
    )(x, sc, sh, w_t, w_t, w_t, w_t, w_t, w_t, wf, bfor, gqm, gkm, gqf, gkf,
      cos_t, sin_t, rc, rs1, rs2, jmat, place, kones)


def _attn_kernel(q_ref, qa_ref, k_ref, ka_ref, v_ref, o_ref, *, t):
    i = pl.program_id(2)
    q = q_ref[0, 0]
    qa = qa_ref[0, 0]
    row = lax.broadcasted_iota(jnp.int32, (PAIR_WIDTH, t), 0)
    rhs = []
    for e in range(2):
        keep = (row >= e * HEAD_DIM) & (row < (e + 1) * HEAD_DIM)
        zero = jnp.zeros_like(q)
        rhs.append(jnp.concatenate([jnp.where(keep, q, zero), jnp.where(keep, qa, zero)],
                                   axis=0))

    def tile_scores(j):
        start = pl.multiple_of(j * t, t)
        lhs = jnp.concatenate([k_ref[0, pl.ds(start, t), :], ka_ref[0, pl.ds(start, t), :]],
                              axis=1)
        return [_dot(lhs, rhs[e]) for e in range(2)]

    def values(j, e):
        rows = slice(e * HEAD_DIM, (e + 1) * HEAD_DIM)
        slabs = t // KEY_TILE
        return jnp.concatenate([v_ref[0, slabs * j + h, rows, :] for h in range(slabs)],
                               axis=1)

    key_i = lax.broadcasted_iota(jnp.int32, (t, t), 0)
    qry_i = lax.broadcasted_iota(jnp.int32, (t, t), 1)
    causal = key_i <= qry_i
    state = []
    for e, s_t in enumerate(tile_scores(i)):
        s_t = jnp.where(causal, s_t, NEG_BIG)
        m = jnp.max(s_t, axis=0, keepdims=True)
        p = jnp.exp2(s_t - m)
        l = jnp.sum(p, axis=0, keepdims=True)
        acc = _dot(values(i, e), p.astype(BF16))
        state += [m, l, acc]

    def body(j, carry):
        out = []
        for e, s_t in enumerate(tile_scores(j)):
            m, l, acc = carry[3 * e:3 * e + 3]
            m_new = jnp.maximum(m, jnp.max(s_t, axis=0, keepdims=True))
            alpha = jnp.exp2(m - m_new)
            p = jnp.exp2(s_t - m_new)
            l = alpha * l + jnp.sum(p, axis=0, keepdims=True)
            acc = alpha * acc + _dot(values(j, e), p.astype(BF16))
            out += [m_new, l, acc]
        return tuple(out)

    state = lax.fori_loop(0, i, body, tuple(state))
    o_t = jnp.concatenate([state[2] / state[1], state[5] / state[4]], axis=0)
    o_ref[0] = o_t.T


def _attn_bounded_kernel(q_ref, qa_ref, k_ref, ka_ref, v_ref, o_ref, p_sc, acc_sc, l_sc, *, t,
                         ka_const):
    i = pl.program_id(1)
    kt = KEY_TILE
    row = lax.broadcasted_iota(jnp.int32, (PAIR_WIDTH, t), 0)
    key_i = lax.broadcasted_iota(jnp.int32, (kt, t), 0)
    qry_i = lax.broadcasted_iota(jnp.int32, (kt, t), 1)
    assert t == 2 * kt


    def rhs_of(pr):
        feat = slice(pr * PAIR_WIDTH, (pr + 1) * PAIR_WIDTH)
        q = q_ref[0, 0, feat, :]
        qa = qa_ref[0, 0, feat, :]
        out = []
        for e in range(2):
            keep = (row >= e * HEAD_DIM) & (row < (e + 1) * HEAD_DIM)
            zero = jnp.zeros_like(q)
            out.append(jnp.concatenate([jnp.where(keep, q, zero), jnp.where(keep, qa, zero)],
                                       axis=0))
        return out

    for grp in range(N_PAIRS // PAIRS_PER_TRIP):
        prs = list(range(grp * PAIRS_PER_TRIP, (grp + 1) * PAIRS_PER_TRIP))
        rhs = {pr: rhs_of(pr) for pr in prs}

        def probs(pr, start, n_keys, row0, mask=None, q0=0, first=False, rhs=rhs):
            feat = slice(pr * PAIR_WIDTH, (pr + 1) * PAIR_WIDTH)
            k_aug = (ka_ref[0, pl.ds(start, n_keys), :] if ka_const
                     else ka_ref[0, pl.ds(start, n_keys), feat])
            lhs = jnp.concatenate([k_ref[0, pl.ds(start, n_keys), feat], k_aug], axis=1)
            for e in range(2):
                s_t = _dot(lhs, rhs[pr][e][:, q0:])
                if mask is not None:
                    s_t = jnp.where(mask[:, q0:], s_t, NEG_BIG)
                p = jnp.exp2(s_t)
                p_sc[pr, e, row0:row0 + n_keys, q0:] = p.astype(BF16)
                part = p[0:8, :]
                for r in range(8, n_keys, 8):
                    part = part + p[r:r + 8, :]
                if first:
                    l_sc[pr, e] = part
                else:
                    l_sc[pr, e, :, q0:] += part

        def values(pr, j):
            rows = slice(pr * PAIR_WIDTH, (pr + 1) * PAIR_WIDTH)
            v_pair = jnp.concatenate([v_ref[0, 2 * j, rows, :], v_ref[0, 2 * j + 1, rows, :]],
                                     axis=1)
            for e in range(2):
                upd = _dot(v_pair, p_sc[pr, e])
                acc_sc[pr, e] += upd[e * HEAD_DIM:(e + 1) * HEAD_DIM, :]

        diag = pl.multiple_of(i * t, t)
        for pr in prs:
            probs(pr, diag, kt, 0, key_i <= qry_i, first=True)
            probs(pr, diag + kt, kt, kt, key_i + kt <= qry_i, q0=kt)
            for e in range(2):
                p_sc[pr, e, kt:, :kt] = jnp.zeros((kt, kt), BF16)
                acc_sc[pr, e] = jnp.zeros((HEAD_DIM, t), F32)

        def body(j, j_prev, probs=probs, values=values, prs=prs):
            for pr in prs:
                values(pr, j_prev)
                probs(pr, pl.multiple_of(j * t, t), t, 0)
            return j

        def unrolled(n, body=body):
            def trip(m, j_prev):
                for u in range(n):
                    j_prev = body(n * m + u, j_prev)
                return j_prev
            return trip

        done = 0
        j_prev = i
        for n in (4, 2, 1):
            trips = (i - done) // n
            j_prev = lax.fori_loop(done // n, done // n + trips, unrolled(n), j_prev)
            done = done + trips * n
        for pr in prs:
            values(pr, j_prev)
        for pr in prs:
            o_t = jnp.concatenate(
                [acc_sc[pr, e] / jnp.sum(l_sc[pr, e], axis=0, keepdims=True)
                 for e in range(2)], axis=0)
            o_ref[0, :, pr * PAIR_WIDTH:(pr + 1) * PAIR_WIDTH] = o_t.T


def _attention_online(q_t, qa, k, ka, v_t, *, ka_const, name):
    bsz, nt, _, t = q_t.shape
    s = k.shape[1]
    grid = (bsz, N_PAIRS, nt)
    if ka_const:
        ka_spec = pl.BlockSpec((1, s, PAIR_WIDTH), lambda b, p, i: (0, 0, 0))
    else:
        ka_spec = pl.BlockSpec((1, s, PAIR_WIDTH), lambda b, p, i: (b, 0, p))
    return pl.pallas_call(
        functools.partial(_attn_kernel, t=t),
        grid=grid,
        in_specs=[pl.BlockSpec((1, 1, PAIR_WIDTH, t), lambda b, p, i: (b, i, p, 0)),
                  pl.BlockSpec((1, 1, PAIR_WIDTH, t), lambda b, p, i: (b, i, p, 0)),
                  pl.BlockSpec((1, s, PAIR_WIDTH), lambda b, p, i: (b, 0, p)),
                  ka_spec,
                  pl.BlockSpec((1, s // KEY_TILE, PAIR_WIDTH, KEY_TILE),
                               lambda b, p, i: (b, 0, p, 0))],
        out_specs=pl.BlockSpec((1, t, PAIR_WIDTH), lambda b, p, i: (b, i, p)),
        out_shape=jax.ShapeDtypeStruct((bsz, s, GROUP_WIDTH), F32),
        compiler_params=pltpu.CompilerParams(
            dimension_semantics=("arbitrary", "arbitrary", "arbitrary"),
            vmem_limit_bytes=VMEM_LIMIT),
        name=name,
    )(q_t, qa, k, ka, v_t)


def _attention_bounded(q_t, qa, k, ka, v_t, *, ka_const, name):
    bsz, nt, _, t = q_t.shape
    s = k.shape[1]
    feat_spec = pl.BlockSpec((1, 1, GROUP_WIDTH, t), lambda b, i: (b, i, 0, 0))
    tok_spec = pl.BlockSpec((1, s, GROUP_WIDTH), lambda b, i: (b, 0, 0))
    ka_spec = pl.BlockSpec((1, s, PAIR_WIDTH), lambda b, i: (0, 0, 0)) if ka_const else tok_spec
    return pl.pallas_call(
        functools.partial(_attn_bounded_kernel, t=t, ka_const=ka_const),
        grid=(bsz, nt),
        in_specs=[feat_spec, feat_spec, tok_spec, ka_spec,
                  pl.BlockSpec((1, s // KEY_TILE, GROUP_WIDTH, KEY_TILE),
                               lambda b, i: (b, 0, 0, 0))],
        out_specs=pl.BlockSpec((1, t, GROUP_WIDTH), lambda b, i: (b, i, 0)),
        out_shape=jax.ShapeDtypeStruct((bsz, s, GROUP_WIDTH), F32),
        scratch_shapes=[pltpu.VMEM((N_PAIRS, 2, t, t), BF16),
                        pltpu.VMEM((N_PAIRS, 2, HEAD_DIM, t), F32),
                        pltpu.VMEM((N_PAIRS, 2, 8, t), F32)],
        compiler_params=pltpu.CompilerParams(
            dimension_semantics=("arbitrary", "arbitrary"),
            vmem_limit_bytes=VMEM_LIMIT),
        name=name,
    )(q_t, qa, k, ka, v_t)


def _attention_dispatch(g_q, g_k, q_t, qa, k, ka, v_t, *, ka_const, name):
    bound = HEAD_DIM * jnp.max(jnp.abs(g_q)) * jnp.max(jnp.abs(g_k))
    return lax.cond(
        bound <= MAX_SAFE_EXPONENT,
        functools.partial(_attention_bounded, ka_const=ka_const, name=name + "_bounded"),
        functools.partial(_attention_online, ka_const=ka_const, name=name + "_online"),
        q_t, qa, k, ka, v_t)


def _post_kernel(x_ref, om_ref, of_ref, ga_ref, scm_ref, shm_ref, gm_ref,
                 gom_ref, gof_ref, wout_ref, w1_ref, w2_ref, o_ref, *, ff_chunk):
    def rms(v):
        return v * lax.rsqrt(jnp.mean(v * v, axis=-1, keepdims=True) + EPS)

    mixed = jnp.concatenate([rms(om_ref[0]) * gom_ref[...],
                             rms(of_ref[0]) * gof_ref[...]], axis=1)
    x1 = x_ref[0] + ga_ref[...] * _dot(mixed, wout_ref[...])
    h = rms(x1) * (1.0 + scm_ref[...]) + shm_ref[...]
    d_ff = w1_ref.shape[1]
    y = None
    for c in range(d_ff // ff_chunk):
        hid = _dot(h, w1_ref[:, c * ff_chunk:(c + 1) * ff_chunk])
        hid = jnp.square(jnp.maximum(hid, 0.0))
        part = _dot(hid, w2_ref[c * ff_chunk:(c + 1) * ff_chunk, :])
        y = part if y is None else y + part
    o_ref[0] = x1 + gm_ref[...] * y


def _post(x, o_m, o_f, mod4, g_out_m, g_out_f, w_out, w_ff1, w_ff2, tm):
    bsz, s, d = x.shape
    d_ff = w_ff1.shape[1]

    def const(a):
        nd = a.ndim
        return pl.BlockSpec(a.shape, lambda b, i, _n=nd: (0,) * _n,
                            pipeline_mode=pl.Buffered(1))

    def mod_spec(k):
        return pl.BlockSpec((None, None, 1, d), lambda b, i, _k=k: (b, _k, 0, 0))

    tok = lambda w: pl.BlockSpec((1, tm, w), lambda b, i: (b, i, 0))
    return pl.pallas_call(
        functools.partial(_post_kernel, ff_chunk=1024),
        grid=(bsz, s // tm),
        in_specs=[tok(d), tok(GROUP_WIDTH), tok(GROUP_WIDTH),
                  mod_spec(2), mod_spec(4), mod_spec(3), mod_spec(5),
                  const(g_out_m), const(g_out_f), const(w_out), const(w_ff1), const(w_ff2)],
        out_specs=tok(d),
        out_shape=jax.ShapeDtypeStruct((bsz, s, d), F32),
        compiler_params=pltpu.CompilerParams(
            dimension_semantics=("arbitrary", "arbitrary"),
            vmem_limit_bytes=POST_VMEM_LIMIT),
        name="post",
    )(x, o_m, o_f, mod4, mod4, mod4, mod4, g_out_m, g_out_f, w_out, w_ff1, w_ff2)


def _rope_tables(s):
    inv_freq = ROPE_THETA ** (-np.arange(0, ROPE_DIM, 2, dtype=np.float64) / ROPE_DIM)
    ang = np.arange(s, dtype=np.float64)[:, None] * inv_freq[None, :]
    cos, sin = np.cos(ang), np.sin(ang)
    rc = np.ones((s, LANES)); rs1 = np.zeros((s, LANES)); rs2 = np.zeros((s, LANES))
    for base in (0, HEAD_DIM):
        rc[:, base:base + ROPE_HALF] = cos
        rc[:, base + ROPE_HALF:base + ROPE_DIM] = cos
        rs2[:, base:base + ROPE_HALF] = -sin
        rs1[:, base + ROPE_HALF:base + ROPE_DIM] = sin
    f = lambda a: jnp.asarray(a, dtype=F32)
    return f(cos.T), f(sin.T), f(rc), f(rs1), f(rs2)


def _static_mats(s):
    idx = np.arange(256)
    jmat = (idx[:, None] // HEAD_DIM == idx[None, :] // HEAD_DIM) / HEAD_DIM
    place = np.zeros((LANES, GROUP_WIDTH))
    for part in range(3):
        for hd in range(GROUP_HEADS):
            place[part * GROUP_HEADS + hd,
                  (hd // 2) * PAIR_WIDTH + (hd % 2) * HEAD_DIM + part] = 1.0
    kones = np.zeros((1, GROUP_WIDTH))
    for hd in range(GROUP_HEADS):
        kones[0, hd * HEAD_DIM + 3:hd * HEAD_DIM + 6] = 1.0
    ka_moba = np.zeros((1, s, PAIR_WIDTH))
    blk = np.arange(s) // MOBA_BLOCK
    for e in range(2):
        ka_moba[0, np.arange(s), e * HEAD_DIM + blk] = 1.0
    b = lambda a: jnp.asarray(a, dtype=BF16)
    return b(jmat), b(place), jnp.asarray(kones, dtype=F32), b(ka_moba)


def kernel(x, c, w_ada, b_ada, w_in, b_forget, g_qn_moba, g_kn_moba, g_qn_fox, g_kn_fox,
           g_out_moba, g_out_fox, w_out, w_ff1, w_ff2):
    bsz, s, d = x.shape
    depth = w_ada.shape[0]
    w = GROUP_WIDTH
    assert s % MOBA_BLOCK == 0 and s // MOBA_BLOCK <= 16
    tm_in = 1024
    tm_post = 512
    cos_t, sin_t, rc, rs1, rs2 = _rope_tables(s)
    jmat, place, kones, ka_moba = _static_mats(s)
    q_scale = HEAD_DIM ** -0.5 * LOG2E

    for l in range(depth):
        mod = _adaln(c, w_ada[l], b_ada[l])
        mod4 = mod.reshape(bsz, 6, 1, d)

        wl = w_in[l]
        wf =jnp.zeros((d, LANES), F32).at[:, :GROUP_HEADS].set(wl[:, 6 * w:]).astype(BF16)
        bfor = jnp.zeros((1, LANES), F32).at[0, :GROUP_HEADS].set(b_forget[l])
        q_gain = lambda g: jnp.broadcast_to((g * q_scale)[:, None], (HEAD_DIM, tm_in))
        k_gain = lambda g: jnp.tile(g, GROUP_HEADS).reshape(1, w)
        w_t = jnp.swapaxes(wl[:, :6 * w], 0, 1).astype(BF16)
        wts = (w_t, wf, bfor,
               q_gain(g_qn_moba[l]), k_gain(g_kn_moba[l]),
               q_gain(g_qn_fox[l]), k_gain(g_kn_fox[l]))
        consts = (cos_t, sin_t, rc, rs1, rs2, jmat, place, kones)

        qm_t, km, vm_t, bias_t, qf_t, kf, vf_t, ka_f, qa_f = _inproj(
            x, mod4[:, 1], mod4[:, 0], wts, consts, tm_in)

        o_m = _attention_dispatch(g_qn_moba[l] * q_scale, g_kn_moba[l],
                                  qm_t, bias_t, km, ka_moba, vm_t, ka_const=True,
                                  name="attn_moba")
        o_f = _attention_dispatch(g_qn_fox[l] * q_scale, g_kn_fox[l],
                                  qf_t, qa_f, kf, ka_f, vf_t, ka_const=False,
                                  name="attn_fox")

        x = _post(x, o_m, o_f, mod4,
                  g_out_moba[l].reshape(1, w), g_out_fox[l].reshape(1, w),
                  w_out[l], w_ff1[l], w_ff2[l],
                  tm_post)
    return x
```

```python
import functools
import math

import numpy as np
import jax
import jax.numpy as jnp
from jax import lax
from jax.experimental import pallas as pl
from jax.experimental.pallas import tpu as pltpu

F32 = jnp.float32
BF16 = jnp.bfloat16

HEAD_DIM = 64
GROUP_HEADS = 8
GROUP_WIDTH = GROUP_HEADS * HEAD_DIM
PAIR_WIDTH = 2 * HEAD_DIM
N_PAIRS = GROUP_HEADS // 2
MOBA_BLOCK = 256
MOBA_TOPK = 3
ROPE_THETA = 500000.0
ROPE_DIM = HEAD_DIM // 4
ROPE_HALF = ROPE_DIM // 2
EPS = 1e-6
LOG2E = math.log2(math.e)
NEG_BIG = -1e30
MAX_SAFE_EXPONENT = 60.0
ATT_TILE = 512
KEY_TILE = 256
CUM_BLOCK = 128
NORM_CHUNKS = 2
PAIRS_PER_TRIP = 4
LANES = 128
VMEM_LIMIT = 48 * 1024 * 1024
POST_VMEM_LIMIT = 58 * 1024 * 1024


def _dot(a, b):
    return jnp.dot(a, b, preferred_element_type=F32)


def _split3(x):
    hi = x.astype(BF16)
    rem = x - hi.astype(F32)
    mid = rem.astype(BF16)
    lo = (rem - mid.astype(F32)).astype(BF16)
    return hi, mid, lo


def _dot_nt(a, b):
    return lax.dot_general(a, b, (((1,), (1,)), ((), ())), preferred_element_type=F32)


def _adaln_kernel(ct_ref, w_ref, b_ref, o_ref):
    c_t = ct_ref[...]
    s_t = c_t / (1.0 + jnp.exp(-c_t))
    w = w_ref[...]
    rows = [jnp.sum(s_t[:, b:b + 1] * w, axis=0, keepdims=True) for b in range(c_t.shape[1])]
    o_ref[...] = jnp.concatenate(rows, axis=0) + b_ref[...]


def _adaln(c, w_ada, b_ada):
    bsz, d = c.shape
    n = w_ada.shape[1]
    return pl.pallas_call(
        _adaln_kernel,
        grid=(n // d,),
        in_specs=[pl.BlockSpec((d, bsz), lambda j: (0, 0)),
                  pl.BlockSpec((d, d), lambda j: (0, j)),
                  pl.BlockSpec((1, d), lambda j: (0, j))],
        out_specs=pl.BlockSpec((bsz, d), lambda j: (0, j)),
        out_shape=jax.ShapeDtypeStruct((bsz, n), F32),
        compiler_params=pltpu.CompilerParams(dimension_semantics=("arbitrary",)),
        name="adaln",
    )(c.T, w_ada, b_ada.reshape(1, n))


TRANSPOSED_GROUPS = (0, 2, 3, 5)


def _wprep_kernel(w_ref, *o_refs):
    g = pl.program_id(0)
    for k, o_ref in enumerate(o_refs):
        @pl.when(g == k)
        def _(k=k, o_ref=o_ref):
            w = w_ref[...]
            o_ref[...] = (w.T if k in TRANSPOSED_GROUPS else w).astype(BF16)


def _project_weights(w_in, layer, width):
    d = w_in.shape[1]
    shapes = [(width, d) if k in TRANSPOSED_GROUPS else (d, width) for k in range(6)]
    return tuple(pl.pallas_call(
        _wprep_kernel,
        grid=(6,),
        in_specs=[pl.BlockSpec((None, d, width), lambda g: (layer, 0, g))],
        out_specs=[pl.BlockSpec(s, lambda g: (0, 0)) for s in shapes],
        out_shape=[jax.ShapeDtypeStruct(s, BF16) for s in shapes],
        compiler_params=pltpu.CompilerParams(dimension_semantics=("arbitrary",)),
        name="wprep",
    )(w_in))


def _inproj_kernel(x_ref, sc_ref, sh_ref,
                   wqm_ref, wkm_ref, wvm_ref, wqf_ref, wkf_ref, wvf_ref, wf_ref, bf_ref,
                   gqm_ref, gkm_ref, gqf_ref, gkf_ref,
                   cos_t_ref, sin_t_ref, rc_ref, rs1_ref, rs2_ref,
                   j_ref, place_ref, kones_ref,
                   qm_o, km_o, vm_o, bias_o, qf_o, kf_o, vf_o, ka_o, qa_o,
                   kmean_sc, carry_sc, *, tm):
    i = pl.program_id(1)
    nb = tm // MOBA_BLOCK

    @pl.when(i == 0)
    def _():
        kmean_sc[...] = jnp.zeros_like(kmean_sc)
        carry_sc[...] = jnp.zeros_like(carry_sc)

    rows_per_chunk = tm // NORM_CHUNKS
    hb_chunks = []
    for ch in range(NORM_CHUNKS):
        x = x_ref[0, ch * rows_per_chunk:(ch + 1) * rows_per_chunk, :]
        ms = jnp.mean(x * x, axis=-1, keepdims=True)
        h = x * lax.rsqrt(ms + EPS) * (1.0 + sc_ref[...]) + sh_ref[...]
        hb_chunks.append(h.astype(BF16))

    def proj(w):
        return jnp.concatenate([_dot(hb, w) for hb in hb_chunks], axis=0)

    def proj_t(w_t):
        return jnp.concatenate([_dot_nt(w_t, hb) for hb in hb_chunks], axis=1)

    def q_feature_major(w_ref, g_ref, rope):
        q_t = proj_t(w_ref[...])
        gain = g_ref[...]
        heads = []
        for hd in range(GROUP_HEADS):
            xh = q_t[hd * HEAD_DIM:(hd + 1) * HEAD_DIM, :]
            ss = jnp.sum(xh * xh, axis=0, keepdims=True)
            xh = xh * lax.rsqrt(ss * (1.0 / HEAD_DIM) + EPS) * gain
            if rope:
                x1 = xh[0:ROPE_HALF, :]
                x2 = xh[ROPE_HALF:ROPE_DIM, :]
                cs = cos_t_ref[...]
                sn = sin_t_ref[...]
                xh = jnp.concatenate(
                    [x1 * cs - x2 * sn, x2 * cs + x1 * sn, xh[ROPE_DIM:, :]], axis=0)
            heads.append(xh)
        return jnp.concatenate(heads, axis=0)

    def k_token_major(w_ref, g_ref, rope):
        k = proj(w_ref[...])
        chunks = []
        for c in range(GROUP_WIDTH // 256):
            seg = k[:, c * 256:(c + 1) * 256]
            msq = _dot((seg * seg).astype(BF16), j_ref[...])
            chunks.append(seg * lax.rsqrt(msq + EPS) * g_ref[:, c * 256:(c + 1) * 256])
        k = jnp.concatenate(chunks, axis=1)
        if rope:
            outs = []
            for c in range(GROUP_WIDTH // LANES):
                seg = k[:, c * LANES:(c + 1) * LANES]
                outs.append(seg * rc_ref[...]
                            + pltpu.roll(seg, ROPE_HALF, 1) * rs1_ref[...]
                            + pltpu.roll(seg, LANES - ROPE_HALF, 1) * rs2_ref[...])
            k = jnp.concatenate(outs, axis=1)
        return k

    def store_tiles(o_ref, val_t, width=ATT_TILE):
        for t in range(tm // width):
            o_ref[0, t] = val_t[:, t * width:(t + 1) * width].astype(o_ref.dtype)

    qm_t = q_feature_major(wqm_ref, gqm_ref, True)
    store_tiles(qm_o, qm_t)
    km = k_token_major(wkm_ref, gkm_ref, True)
    km_o[0] = km.astype(BF16)
    store_tiles(vm_o, proj_t(wvm_ref[...]), KEY_TILE)

    blk0 = i * nb
    lane_head = lax.broadcasted_iota(jnp.int32, (1, GROUP_WIDTH), 1) // HEAD_DIM
    for sb in range(nb):
        kmean = jnp.mean(km[sb * MOBA_BLOCK:(sb + 1) * MOBA_BLOCK, :], axis=0, keepdims=True)
        for hd in range(GROUP_HEADS):
            row = hd * 16 + blk0 + sb
            kmean_sc[pl.ds(row, 1), :] = jnp.where(lane_head == hd, kmean, 0.0)

    km_hi, km_lo, _ = _split3(kmean_sc[...])
    q_hi, q_lo, _ = _split3(qm_t)
    gate_t = _dot(km_hi, q_hi) + _dot(km_hi, q_lo) + _dot(km_lo, q_hi)
    tok_blk = blk0 + lax.broadcasted_iota(jnp.int32, (16, tm), 1) // MOBA_BLOCK
    n_idx = lax.broadcasted_iota(jnp.int32, (16, tm), 0)
    valid = n_idx < tok_blk

    zeros48 = jnp.zeros((HEAD_DIM - 16, tm), F32)
    pair_rows = []
    for hd in range(GROUP_HEADS):
        g = jnp.where(valid, gate_t[hd * 16:(hd + 1) * 16, :], -jnp.inf)
        picked = jnp.zeros((16, tm), jnp.int32)
        for _ in range(MOBA_TOPK):
            best = jnp.max(g, axis=0, keepdims=True)
            first = jnp.min(jnp.where(g == best, n_idx, 16), axis=0, keepdims=True)
            hit = n_idx == first
            picked = jnp.where(hit, 1, picked)
            g = jnp.where(hit, -jnp.inf, g)
        sel = (valid & (picked > 0)) | (n_idx == tok_blk)
        pair_rows += [jnp.where(sel, 0.0, NEG_BIG), zeros48]
    store_tiles(bias_o, jnp.concatenate(pair_rows, axis=0))

    store_tiles(qf_o, q_feature_major(wqf_ref, gqf_ref, False))
    kf_o[0] = k_token_major(wkf_ref, gkf_ref, False).astype(BF16)
    store_tiles(vf_o, proj_t(wvf_ref[...]), KEY_TILE)

    f = proj(wf_ref[...]) + bf_ref[...]
    logf = jnp.minimum(f, 0.0) - jnp.log1p(jnp.exp(-jnp.abs(f)))
    live = lax.broadcasted_iota(jnp.int32, (1, LANES), 1) < GROUP_HEADS
    logf = jnp.where(live, logf, 0.0)
    r_i = lax.broadcasted_iota(jnp.int32, (CUM_BLOCK, CUM_BLOCK), 0)
    c_i = lax.broadcasted_iota(jnp.int32, (CUM_BLOCK, CUM_BLOCK), 1)
    tri = jnp.where(r_i >= c_i, 1.0, 0.0).astype(BF16)
    n_blk = tm // CUM_BLOCK
    wide = jnp.concatenate([logf[b * CUM_BLOCK:(b + 1) * CUM_BLOCK, :] for b in range(n_blk)],
                           axis=1)
    parts = _split3(wide)
    prefix = _dot(tri, parts[0]) + _dot(tri, parts[1]) + _dot(tri, parts[2])
    carry = carry_sc[...]
    cums = []
    for blk in range(n_blk):
        c_blk = prefix[:, blk * LANES:(blk + 1) * LANES] + carry
        carry = c_blk[CUM_BLOCK - 1:CUM_BLOCK, :]
        cums.append(c_blk)
    carry_sc[...] = carry
    nc = jnp.concatenate(cums, axis=0) * (-LOG2E)
    hi, mid, lo = [part.astype(F32) for part in _split3(nc)]
    packed = hi + pltpu.roll(mid, GROUP_HEADS, 1) + pltpu.roll(lo, 2 * GROUP_HEADS, 1)
    ka_o[0] = (_dot(packed.astype(BF16), place_ref[...]) + kones_ref[...]).astype(BF16)

    nc_t = nc.T[0:GROUP_HEADS, :]
    hi_t = nc_t.astype(BF16).astype(F32)
    rem_t = nc_t - hi_t
    mid_t = rem_t.astype(BF16).astype(F32)
    lo_t = (rem_t - mid_t).astype(BF16).astype(F32)
    r8 = lax.broadcasted_iota(jnp.int32, (8, tm), 0)
    zeros56 = jnp.zeros((HEAD_DIM - 8, tm), F32)
    qa_rows = []
    for hd in range(GROUP_HEADS):
        blk = jnp.where(r8 < 3, 1.0,
                        jnp.where(r8 == 3, -hi_t[hd:hd + 1, :],
                                  jnp.where(r8 == 4, -mid_t[hd:hd + 1, :],
                                            jnp.where(r8 == 5, -lo_t[hd:hd + 1, :], 0.0))))
        qa_rows += [blk, zeros56]
    store_tiles(qa_o, jnp.concatenate(qa_rows, axis=0))


def _inproj(x, sc, sh, wts, consts, tm):
    bsz, s, d = x.shape
    nt = s // ATT_TILE
    grid = (bsz, s // tm)
    wqm, wkm, wvm, wqf, wkf, wvf, wf, bfor, gqm, gkm, gqf, gkf = wts
    cos_t, sin_t, rc, rs1, rs2, jmat, place, kones = consts

    def const(a):
        nd = a.ndim
        return pl.BlockSpec(a.shape, lambda b, i, _n=nd: (0,) * _n,
                            pipeline_mode=pl.Buffered(1))

    mod_spec = pl.BlockSpec((None, 1, d), lambda b, i: (b, 0, 0))
    in_specs = [pl.BlockSpec((1, tm, d), lambda b, i: (b, i, 0)), mod_spec, mod_spec]
    in_specs += [const(a) for a in (wqm, wkm, wvm, wqf, wkf, wvf, wf, bfor,
                                    gqm, gkm, gqf, gkf)]
    in_specs += [pl.BlockSpec((ROPE_HALF, tm), lambda b, i: (0, i)),
                 pl.BlockSpec((ROPE_HALF, tm), lambda b, i: (0, i)),
                 pl.BlockSpec((tm, LANES), lambda b, i: (i, 0)),
                 pl.BlockSpec((tm, LANES), lambda b, i: (i, 0)),
                 pl.BlockSpec((tm, LANES), lambda b, i: (i, 0)),
                 const(jmat), const(place), const(kones)]

    tiles = tm // ATT_TILE
    feat_spec = pl.BlockSpec((1, tiles, GROUP_WIDTH, ATT_TILE), lambda b, i: (b, i, 0, 0))
    tok_spec = pl.BlockSpec((1, tm, GROUP_WIDTH), lambda b, i: (b, i, 0))
    feat_shape = jax.ShapeDtypeStruct((bsz, nt, GROUP_WIDTH, ATT_TILE), BF16)
    tok_shape = jax.ShapeDtypeStruct((bsz, s, GROUP_WIDTH), BF16)
    val_spec = pl.BlockSpec((1, tm // KEY_TILE, GROUP_WIDTH, KEY_TILE),
                            lambda b, i: (b, i, 0, 0))
    val_shape = jax.ShapeDtypeStruct((bsz, s // KEY_TILE, GROUP_WIDTH, KEY_TILE), BF16)
    out_specs = [feat_spec, tok_spec, val_spec, feat_spec,
                 feat_spec, tok_spec, val_spec, tok_spec, feat_spec]
    out_shape = [feat_shape, tok_shape, val_shape, feat_shape,
                 feat_shape, tok_shape, val_shape, tok_shape, feat_shape]
    return pl.pallas_call(
        functools.partial(_inproj_kernel, tm=tm),
        grid=grid,
        in_specs=in_specs,
        out_specs=out_specs,
        out_shape=out_shape,
        scratch_shapes=[pltpu.VMEM((GROUP_HEADS * 16, GROUP_WIDTH), F32),
                        pltpu.VMEM((1, LANES), F32)],
        compiler_params=pltpu.CompilerParams(
            dimension_semantics=("arbitrary", "arbitrary"),
            vmem_limit_bytes=VMEM_LIMIT),
        name="inproj",
    )(x, sc, sh, wqm, wkm, wvm, wqf, wkf, wvf, wf, bfor, gqm, gkm, gqf, gkf,
      cos_t, sin_t, rc, rs1, rs2, jmat, place, kones)


def _attn_kernel(q_ref, qa_ref, k_ref, ka_ref, v_ref, o_ref, *, t):
    i = pl.program_id(2)
    q = q_ref[0, 0]
    qa = qa_ref[0, 0]
    row = lax.broadcasted_iota(jnp.int32, (PAIR_WIDTH, t), 0)
    rhs = []
    for e in range(2):
        keep = (row >= e * HEAD_DIM) & (row < (e + 1) * HEAD_DIM)
        zero = jnp.zeros_like(q)
        rhs.append(jnp.concatenate([jnp.where(keep, q, zero), jnp.where(keep, qa, zero)],
                                   axis=0))

    def tile_scores(j):
        start = pl.multiple_of(j * t, t)
        lhs = jnp.concatenate([k_ref[0, pl.ds(start, t), :], ka_ref[0, pl.ds(start, t), :]],
                              axis=1)
        return [_dot(lhs, rhs[e]) for e in range(2)]

    def values(j, e):
        rows = slice(e * HEAD_DIM, (e + 1) * HEAD_DIM)
        slabs = t // KEY_TILE
        return jnp.concatenate([v_ref[0, slabs * j + h, rows, :] for h in range(slabs)],
                               axis=1)

    key_i = lax.broadcasted_iota(jnp.int32, (t, t), 0)
    qry_i = lax.broadcasted_iota(jnp.int32, (t, t), 1)
    causal = key_i <= qry_i
    state = []
    for e, s_t in enumerate(tile_scores(i)):
        s_t = jnp.where(causal, s_t, NEG_BIG)
        m = jnp.max(s_t, axis=0, keepdims=True)
        p = jnp.exp2(s_t - m)
        l = jnp.sum(p, axis=0, keepdims=True)
        acc = _dot(values(i, e), p.astype(BF16))
        state += [m, l, acc]

    def body(j, carry):
        out = []
        for e, s_t in enumerate(tile_scores(j)):
            m, l, acc = carry[3 * e:3 * e + 3]
            m_new = jnp.maximum(m, jnp.max(s_t, axis=0, keepdims=True))
            alpha = jnp.exp2(m - m_new)
            p = jnp.exp2(s_t - m_new)
            l = alpha * l + jnp.sum(p, axis=0, keepdims=True)
            acc = alpha * acc + _dot(values(j, e), p.astype(BF16))
            out += [m_new, l, acc]
        return tuple(out)

    state = lax.fori_loop(0, i, body, tuple(state))
    o_t = jnp.concatenate([state[2] / state[1], state[5] / state[4]], axis=0)
    o_ref[0] = o_t.T


def _attn_bounded_kernel(q_ref, qa_ref, k_ref, ka_ref, v_ref, o_ref, p_sc, acc_sc, l_sc, *, t,
                         ka_const):
    i = pl.program_id(1)
    kt = KEY_TILE
    row = lax.broadcasted_iota(jnp.int32, (PAIR_WIDTH, t), 0)
    key_i = lax.broadcasted_iota(jnp.int32, (kt, t), 0)
    qry_i = lax.broadcasted_iota(jnp.int32, (kt, t), 1)
    assert t == 2 * kt


    def rhs_of(pr):
        feat = slice(pr * PAIR_WIDTH, (pr + 1) * PAIR_WIDTH)
        q = q_ref[0, 0, feat, :]
        qa = qa_ref[0, 0, feat, :]
        out = []
        for e in range(2):
            keep = (row >= e * HEAD_DIM) & (row < (e + 1) * HEAD_DIM)
            zero = jnp.zeros_like(q)
            out.append(jnp.concatenate([jnp.where(keep, q, zero), jnp.where(keep, qa, zero)],
                                       axis=0))
        return out

    for grp in range(N_PAIRS // PAIRS_PER_TRIP):
        prs = list(range(grp * PAIRS_PER_TRIP, (grp + 1) * PAIRS_PER_TRIP))
        rhs = {pr: rhs_of(pr) for pr in prs}

        def probs(pr, start, n_keys, row0, mask=None, q0=0, first=False, rhs=rhs):
            feat = slice(pr * PAIR_WIDTH, (pr + 1) * PAIR_WIDTH)
            k_aug = (ka_ref[0, pl.ds(start, n_keys), :] if ka_const
                     else ka_ref[0, pl.ds(start, n_keys), feat])
            lhs = jnp.concatenate([k_ref[0, pl.ds(start, n_keys), feat], k_aug], axis=1)
            for e in range(2):
                s_t = _dot(lhs, rhs[pr][e][:, q0:])
                if mask is not None:
                    s_t = jnp.where(mask[:, q0:], s_t, NEG_BIG)
                p = jnp.exp2(s_t)
                p_sc[pr, e, row0:row0 + n_keys, q0:] = p.astype(BF16)
                part = p[0:8, :]
                for r in range(8, n_keys, 8):
                    part = part + p[r:r + 8, :]
                if first:
                    l_sc[pr, e] = part
                else:
                    l_sc[pr, e, :, q0:] += part

        def values(pr, j):
            rows = slice(pr * PAIR_WIDTH, (pr + 1) * PAIR_WIDTH)
            v_pair = jnp.concatenate([v_ref[0, 2 * j, rows, :], v_ref[0, 2 * j + 1, rows, :]],
                                     axis=1)
            for e in range(2):
                upd = _dot(v_pair, p_sc[pr, e])
                acc_sc[pr, e] += upd[e * HEAD_DIM:(e + 1) * HEAD_DIM, :]

        diag = pl.multiple_of(i * t, t)
        for pr in prs:
            probs(pr, diag, kt, 0, key_i <= qry_i, first=True)
            probs(pr, diag + kt, kt, kt, key_i + kt <= qry_i, q0=kt)
            for e in range(2):
                p_sc[pr, e, kt:, :kt] = jnp.zeros((kt, kt), BF16)
                acc_sc[pr, e] = jnp.zeros((HEAD_DIM, t), F32)
        for pr in prs:
            values(pr, i)

        def body(j, carry, probs=probs, values=values, prs=prs):
            for pr in prs:
                probs(pr, pl.multiple_of(j * t, t), t, 0)
            for pr in prs:
                values(pr, j)
            return carry

        def unrolled(n, body=body):
            def trip(m, j_prev):
                for u in range(n):
                    j_prev = body(n * m + u, j_prev)
                return j_prev
            return trip

        done = 0
        j_prev = i
        for n in (4, 2, 1):
            trips = (i - done) // n
            j_prev = lax.fori_loop(done // n, done // n + trips, unrolled(n), j_prev)
            done = done + trips * n
        for pr in prs:
            o_t = jnp.concatenate(
                [acc_sc[pr, e] / jnp.sum(l_sc[pr, e], axis=0, keepdims=True)
                 for e in range(2)], axis=0)
            o_ref[0, :, pr * PAIR_WIDTH:(pr + 1) * PAIR_WIDTH] = o_t.T


def _attention_online(q_t, qa, k, ka, v_t, *, ka_const, name):
    bsz, nt, _, t = q_t.shape
    s = k.shape[1]
    grid = (bsz, N_PAIRS, nt)
    if ka_const:
        ka_spec = pl.BlockSpec((1, s, PAIR_WIDTH), lambda b, p, i: (0, 0, 0))
    else:
        ka_spec = pl.BlockSpec((1, s, PAIR_WIDTH), lambda b, p, i: (b, 0, p))
    return pl.pallas_call(
        functools.partial(_attn_kernel, t=t),
        grid=grid,
        in_specs=[pl.BlockSpec((1, 1, PAIR_WIDTH, t), lambda b, p, i: (b, i, p, 0)),
                  pl.BlockSpec((1, 1, PAIR_WIDTH, t), lambda b, p, i: (b, i, p, 0)),
                  pl.BlockSpec((1, s, PAIR_WIDTH), lambda b, p, i: (b, 0, p)),
                  ka_spec,
                  pl.BlockSpec((1, s // KEY_TILE, PAIR_WIDTH, KEY_TILE),
                               lambda b, p, i: (b, 0, p, 0))],
        out_specs=pl.BlockSpec((1, t, PAIR_WIDTH), lambda b, p, i: (b, i, p)),
        out_shape=jax.ShapeDtypeStruct((bsz, s, GROUP_WIDTH), F32),
        compiler_params=pltpu.CompilerParams(
            dimension_semantics=("arbitrary", "arbitrary", "arbitrary"),
            vmem_limit_bytes=VMEM_LIMIT),
        name=name,
    )(q_t, qa, k, ka, v_t)


def _attention_bounded(q_t, qa, k, ka, v_t, *, ka_const, name):
    bsz, nt, _, t = q_t.shape
    s = k.shape[1]
    feat_spec = pl.BlockSpec((1, 1, GROUP_WIDTH, t), lambda b, i: (b, i, 0, 0))
    tok_spec = pl.BlockSpec((1, s, GROUP_WIDTH), lambda b, i: (b, 0, 0))
    ka_spec = pl.BlockSpec((1, s, PAIR_WIDTH), lambda b, i: (0, 0, 0)) if ka_const else tok_spec
    return pl.pallas_call(
        functools.partial(_attn_bounded_kernel, t=t, ka_const=ka_const),
        grid=(bsz, nt),
        in_specs=[feat_spec, feat_spec, tok_spec, ka_spec,
                  pl.BlockSpec((1, s // KEY_TILE, GROUP_WIDTH, KEY_TILE),
                               lambda b, i: (b, 0, 0, 0))],
        out_specs=pl.BlockSpec((1, t, GROUP_WIDTH), lambda b, i: (b, i, 0)),
        out_shape=jax.ShapeDtypeStruct((bsz, s, GROUP_WIDTH), F32),
        scratch_shapes=[pltpu.VMEM((N_PAIRS, 2, t, t), BF16),
                        pltpu.VMEM((N_PAIRS, 2, HEAD_DIM, t), F32),
                        pltpu.VMEM((N_PAIRS, 2, 8, t), F32)],
        compiler_params=pltpu.CompilerParams(
            dimension_semantics=("arbitrary", "arbitrary"),
            vmem_limit_bytes=VMEM_LIMIT),
        name=name,
    )(q_t, qa, k, ka, v_t)


def _attention_dispatch(g_q, g_k, q_t, qa, k, ka, v_t, *, ka_const, name):
    bound = HEAD_DIM * jnp.max(jnp.abs(g_q)) * jnp.max(jnp.abs(g_k))
    return lax.cond(
        bound <= MAX_SAFE_EXPONENT,
        functools.partial(_attention_bounded, ka_const=ka_const, name=name + "_bounded"),
        functools.partial(_attention_online, ka_const=ka_const, name=name + "_online"),
        q_t, qa, k, ka, v_t)


def _post_kernel(x_ref, om_ref, of_ref, ga_ref, scm_ref, shm_ref, gm_ref,
                 gom_ref, gof_ref, wout_ref, w1_ref, w2_ref, o_ref, *, ff_chunk):
    def rms(v):
        return v * lax.rsqrt(jnp.mean(v * v, axis=-1, keepdims=True) + EPS)

    mixed = jnp.concatenate([rms(om_ref[0]) * gom_ref[...],
                             rms(of_ref[0]) * gof_ref[...]], axis=1)
    x1 = x_ref[0] + ga_ref[...] * _dot(mixed, wout_ref[...])
    h = rms(x1) * (1.0 + scm_ref[...]) + shm_ref[...]
    d_ff = w1_ref.shape[1]
    y = None
    for c in range(d_ff // ff_chunk):
        hid = _dot(h, w1_ref[:, c * ff_chunk:(c + 1) * ff_chunk])
        hid = jnp.square(jnp.maximum(hid, 0.0))
        part = _dot(hid, w2_ref[c * ff_chunk:(c + 1) * ff_chunk, :])
        y = part if y is None else y + part
    o_ref[0] = x1 + gm_ref[...] * y


def _post(x, o_m, o_f, mod4, g_out_m, g_out_f, w_out, w_ff1, w_ff2, tm):
    bsz, s, d = x.shape
    d_ff = w_ff1.shape[1]

    def const(a):
        nd = a.ndim
        return pl.BlockSpec(a.shape, lambda b, i, _n=nd: (0,) * _n,
                            pipeline_mode=pl.Buffered(1))

    def mod_spec(k):
        return pl.BlockSpec((None, None, 1, d), lambda b, i, _k=k: (b, _k, 0, 0))

    tok = lambda w: pl.BlockSpec((1, tm, w), lambda b, i: (b, i, 0))
    return pl.pallas_call(
        functools.partial(_post_kernel, ff_chunk=1024),
        grid=(bsz, s // tm),
        in_specs=[tok(d), tok(GROUP_WIDTH), tok(GROUP_WIDTH),
                  mod_spec(2), mod_spec(4), mod_spec(3), mod_spec(5),
                  const(g_out_m), const(g_out_f), const(w_out), const(w_ff1), const(w_ff2)],
        out_specs=tok(d),
        out_shape=jax.ShapeDtypeStruct((bsz, s, d), F32),
        compiler_params=pltpu.CompilerParams(
            dimension_semantics=("arbitrary", "arbitrary"),
            vmem_limit_bytes=POST_VMEM_LIMIT),
        name="post",
    )(x, o_m, o_f, mod4, mod4, mod4, mod4, g_out_m, g_out_f, w_out, w_ff1, w_ff2)


def _rope_tables(s):
    inv_freq = ROPE_THETA ** (-np.arange(0, ROPE_DIM, 2, dtype=np.float64) / ROPE_DIM)
    ang = np.arange(s, dtype=np.float64)[:, None] * inv_freq[None, :]
    cos, sin = np.cos(ang), np.sin(ang)
    rc = np.ones((s, LANES)); rs1 = np.zeros((s, LANES)); rs2 = np.zeros((s, LANES))
    for base in (0, HEAD_DIM):
        rc[:, base:base + ROPE_HALF] = cos
        rc[:, base + ROPE_HALF:base + ROPE_DIM] = cos
        rs2[:, base:base + ROPE_HALF] = -sin
        rs1[:, base + ROPE_HALF:base + ROPE_DIM] = sin
    f = lambda a: jnp.asarray(a, dtype=F32)
    return f(cos.T), f(sin.T), f(rc), f(rs1), f(rs2)


def _static_mats(s):
    idx = np.arange(256)
    jmat = (idx[:, None] // HEAD_DIM == idx[None, :] // HEAD_DIM) / HEAD_DIM
    place = np.zeros((LANES, GROUP_WIDTH))
    for part in range(3):
        for hd in range(GROUP_HEADS):
            place[part * GROUP_HEADS + hd,
                  (hd // 2) * PAIR_WIDTH + (hd % 2) * HEAD_DIM + part] = 1.0
    kones = np.zeros((1, GROUP_WIDTH))
    for hd in range(GROUP_HEADS):
        kones[0, hd * HEAD_DIM + 3:hd * HEAD_DIM + 6] = 1.0
    ka_moba = np.zeros((1, s, PAIR_WIDTH))
    blk = np.arange(s) // MOBA_BLOCK
    for e in range(2):
        ka_moba[0, np.arange(s), e * HEAD_DIM + blk] = 1.0
    b = lambda a: jnp.asarray(a, dtype=BF16)
    return b(jmat), b(place), jnp.asarray(kones, dtype=F32), b(ka_moba)


def kernel(x, c, w_ada, b_ada, w_in, b_forget, g_qn_moba, g_kn_moba, g_qn_fox, g_kn_fox,
           g_out_moba, g_out_fox, w_out, w_ff1, w_ff2):
    bsz, s, d = x.shape
    depth = w_ada.shape[0]
    w = GROUP_WIDTH
    assert s % MOBA_BLOCK == 0 and s // MOBA_BLOCK <= 16
    tm_in = 1024
    tm_post = 512
    cos_t, sin_t, rc, rs1, rs2 = _rope_tables(s)
    jmat, place, kones, ka_moba = _static_mats(s)
    q_scale = HEAD_DIM ** -0.5 * LOG2E

    for l in range(depth):
        mod = _adaln(c, w_ada[l], b_ada[l])
        mod4 = mod.reshape(bsz, 6, 1, d)

        wl = w_in[l]
        wf =jnp.zeros((d, LANES), F32).at[:, :GROUP_HEADS].set(wl[:, 6 * w:]).astype(BF16)
        bfor = jnp.zeros((1, LANES), F32).at[0, :GROUP_HEADS].set(b_forget[l])
        q_gain = lambda g: jnp.broadcast_to((g * q_scale)[:, None], (HEAD_DIM, tm_in))
        k_gain = lambda g: jnp.tile(g, GROUP_HEADS).reshape(1, w)
        wts = _project_weights(w_in, l, w) + (
               wf, bfor,
               q_gain(g_qn_moba[l]), k_gain(g_kn_moba[l]),
               q_gain(g_qn_fox[l]), k_gain(g_kn_fox[l]))
        consts = (cos_t, sin_t, rc, rs1, rs2, jmat, place, kones)

        qm_t, km, vm_t, bias_t, qf_t, kf, vf_t, ka_f, qa_f = _inproj(
            x, mod4[:, 1], mod4[:, 0], wts, consts, tm_in)

        o_m = _attention_dispatch(g_qn_moba[l] * q_scale, g_kn_moba[l],
                                  qm_t, bias_t, km, ka_moba, vm_t, ka_const=True,
                                  name="attn_moba")
        o_f = _attention_dispatch(g_qn_fox[l] * q_scale, g_kn_fox[l],
                                  qf_t, qa_f, kf, ka_f, vf_t, ka_const=False,
                                  name="attn_fox")

        x = _post(x, o_m, o_f, mod4,
                  g_out_moba[l].reshape(1, w), g_out_fox[l].reshape(1, w),
                  w_out[l], w_ff1[l], w_ff2[l],
                  tm_post)
    return x
```

```python
import functools
import math

import numpy as np
import jax
import jax.numpy as jnp
from jax import lax
from jax.experimental import pallas as pl
from jax.experimental.pallas import tpu as pltpu

F32 = jnp.float32
BF16 = jnp.bfloat16

HEAD_DIM = 64
GROUP_HEADS = 8
GROUP_WIDTH = GROUP_HEADS * HEAD_DIM
PAIR_WIDTH = 2 * HEAD_DIM
N_PAIRS = GROUP_HEADS // 2
MOBA_BLOCK = 256
MOBA_TOPK = 3
ROPE_THETA = 500000.0
ROPE_DIM = HEAD_DIM // 4
ROPE_HALF = ROPE_DIM // 2
EPS = 1e-6
LOG2E = math.log2(math.e)
NEG_BIG = -1e30
MAX_SAFE_EXPONENT = 60.0
ATT_TILE = 512
KEY_TILE = 256
CUM_BLOCK = 128
NORM_CHUNKS = 2
PAIRS_PER_TRIP = 4
LANES = 128
VMEM_LIMIT = 48 * 1024 * 1024
POST_VMEM_LIMIT = 58 * 1024 * 1024


def _dot(a, b):
    return jnp.dot(a, b, preferred_element_type=F32)


def _split3(x):
    hi = x.astype(BF16)
    rem = x - hi.astype(F32)
    mid = rem.astype(BF16)
    lo = (rem - mid.astype(F32)).astype(BF16)
    return hi, mid, lo


def _dot_nt(a, b):
    return lax.dot_general(a, b, (((1,), (1,)), ((), ())), preferred_element_type=F32)


def _adaln_kernel(ct_ref, w_ref, b_ref, o_ref):
    c_t = ct_ref[...]
    s_t = c_t / (1.0 + jnp.exp(-c_t))
    w = w_ref[...]
    rows = [jnp.sum(s_t[:, b:b + 1] * w, axis=0, keepdims=True) for b in range(c_t.shape[1])]
    o_ref[...] = jnp.concatenate(rows, axis=0) + b_ref[...]


def _adaln(c, w_ada, b_ada):
    bsz, d = c.shape
    n = w_ada.shape[1]
    return pl.pallas_call(
        _adaln_kernel,
        grid=(n // d,),
        in_specs=[pl.BlockSpec((d, bsz), lambda j: (0, 0)),
                  pl.BlockSpec((d, d), lambda j: (0, j)),
                  pl.BlockSpec((1, d), lambda j: (0, j))],
        out_specs=pl.BlockSpec((bsz, d), lambda j: (0, j)),
        out_shape=jax.ShapeDtypeStruct((bsz, n), F32),
        compiler_params=pltpu.CompilerParams(dimension_semantics=("arbitrary",)),
        name="adaln",
    )(c.T, w_ada, b_ada.reshape(1, n))


TRANSPOSED_GROUPS = (0, 2, 3, 5)


def _wprep_kernel(w_ref, *o_refs):
    g = pl.program_id(0)
    for k, o_ref in enumerate(o_refs):
        @pl.when(g == k)
        def _(k=k, o_ref=o_ref):
            w = w_ref[...]
            o_ref[...] = (w.T if k in TRANSPOSED_GROUPS else w).astype(BF16)


def _project_weights(w_in, layer, width):
    d = w_in.shape[1]
    shapes = [(width, d) if k in TRANSPOSED_GROUPS else (d, width) for k in range(6)]
    return tuple(pl.pallas_call(
        _wprep_kernel,
        grid=(6,),
        in_specs=[pl.BlockSpec((None, d, width), lambda g: (layer, 0, g))],
        out_specs=[pl.BlockSpec(s, lambda g: (0, 0)) for s in shapes],
        out_shape=[jax.ShapeDtypeStruct(s, BF16) for s in shapes],
        compiler_params=pltpu.CompilerParams(dimension_semantics=("arbitrary",)),
        name="wprep",
    )(w_in))


def _inproj_kernel(x_ref, sc_ref, sh_ref,
                   wqm_ref, wkm_ref, wvm_ref, wqf_ref, wkf_ref, wvf_ref, wf_ref, bf_ref,
                   gqm_ref, gkm_ref, gqf_ref, gkf_ref,
                   cos_t_ref, sin_t_ref, rc_ref, rs1_ref, rs2_ref,
                   j_ref, place_ref, kones_ref,
                   qm_o, km_o, vm_o, bias_o, qf_o, kf_o, vf_o, ka_o, qa_o,
                   kmean_sc, carry_sc, *, tm):
    i = pl.program_id(1)
    nb = tm // MOBA_BLOCK

    @pl.when(i == 0)
    def _():
        kmean_sc[...] = jnp.zeros_like(kmean_sc)
        carry_sc[...] = jnp.zeros_like(carry_sc)

    rows_per_chunk = tm // NORM_CHUNKS
    hb_chunks = []
    for ch in range(NORM_CHUNKS):
        x = x_ref[0, ch * rows_per_chunk:(ch + 1) * rows_per_chunk, :]
        ms = jnp.mean(x * x, axis=-1, keepdims=True)
        h = x * lax.rsqrt(ms + EPS) * (1.0 + sc_ref[...]) + sh_ref[...]
        hb_chunks.append(h.astype(BF16))

    def proj(w):
        return jnp.concatenate([_dot(hb, w) for hb in hb_chunks], axis=0)

    def proj_t(w_t):
        return jnp.concatenate([_dot_nt(w_t, hb) for hb in hb_chunks], axis=1)

    def q_feature_major(w_ref, g_ref, rope):
        q_t = proj_t(w_ref[...])
        gain = g_ref[...]
        heads = []
        for hd in range(GROUP_HEADS):
            xh = q_t[hd * HEAD_DIM:(hd + 1) * HEAD_DIM, :]
            ss = jnp.sum(xh * xh, axis=0, keepdims=True)
            xh = xh * lax.rsqrt(ss * (1.0 / HEAD_DIM) + EPS) * gain
            if rope:
                x1 = xh[0:ROPE_HALF, :]
                x2 = xh[ROPE_HALF:ROPE_DIM, :]
                cs = cos_t_ref[...]
                sn = sin_t_ref[...]
                xh = jnp.concatenate(
                    [x1 * cs - x2 * sn, x2 * cs + x1 * sn, xh[ROPE_DIM:, :]], axis=0)
            heads.append(xh)
        return jnp.concatenate(heads, axis=0)

    def k_token_major(w_ref, g_ref, rope):
        k = proj(w_ref[...])
        chunks = []
        for c in range(GROUP_WIDTH // 256):
            seg = k[:, c * 256:(c + 1) * 256]
            msq = _dot((seg * seg).astype(BF16), j_ref[...])
            chunks.append(seg * lax.rsqrt(msq + EPS) * g_ref[:, c * 256:(c + 1) * 256])
        k = jnp.concatenate(chunks, axis=1)
        if rope:
            outs = []
            for c in range(GROUP_WIDTH // LANES):
                seg = k[:, c * LANES:(c + 1) * LANES]
                outs.append(seg * rc_ref[...]
                            + pltpu.roll(seg, ROPE_HALF, 1) * rs1_ref[...]
                            + pltpu.roll(seg, LANES - ROPE_HALF, 1) * rs2_ref[...])
            k = jnp.concatenate(outs, axis=1)
        return k

    def store_tiles(o_ref, val_t, width=ATT_TILE):
        for t in range(tm // width):
            o_ref[0, t] = val_t[:, t * width:(t + 1) * width].astype(o_ref.dtype)

    qm_t = q_feature_major(wqm_ref, gqm_ref, True)
    store_tiles(qm_o, qm_t)
    km = k_token_major(wkm_ref, gkm_ref, True)
    km_o[0] = km.astype(BF16)
    store_tiles(vm_o, proj_t(wvm_ref[...]), KEY_TILE)

    blk0 = i * nb
    lane_head = lax.broadcasted_iota(jnp.int32, (1, GROUP_WIDTH), 1) // HEAD_DIM
    for sb in range(nb):
        kmean = jnp.mean(km[sb * MOBA_BLOCK:(sb + 1) * MOBA_BLOCK, :], axis=0, keepdims=True)
        for hd in range(GROUP_HEADS):
            row = hd * 16 + blk0 + sb
            kmean_sc[pl.ds(row, 1), :] = jnp.where(lane_head == hd, kmean, 0.0)

    km_hi, km_lo, _ = _split3(kmean_sc[...])
    q_hi, q_lo, _ = _split3(qm_t)
    gate_t = _dot(km_hi, q_hi) + _dot(km_hi, q_lo) + _dot(km_lo, q_hi)
    tok_blk = blk0 + lax.broadcasted_iota(jnp.int32, (16, tm), 1) // MOBA_BLOCK
    n_idx = lax.broadcasted_iota(jnp.int32, (16, tm), 0)
    valid = n_idx < tok_blk

    zeros48 = jnp.zeros((HEAD_DIM - 16, tm), F32)
    pair_rows = []
    for hd in range(GROUP_HEADS):
        g = jnp.where(valid, gate_t[hd * 16:(hd + 1) * 16, :], -jnp.inf)
        picked = jnp.zeros((16, tm), jnp.int32)
        for _ in range(MOBA_TOPK):
            best = jnp.max(g, axis=0, keepdims=True)
            first = jnp.min(jnp.where(g == best, n_idx, 16), axis=0, keepdims=True)
            hit = n_idx == first
            picked = jnp.where(hit, 1, picked)
            g = jnp.where(hit, -jnp.inf, g)
        sel = (valid & (picked > 0)) | (n_idx == tok_blk)
        pair_rows += [jnp.where(sel, 0.0, NEG_BIG), zeros48]
    store_tiles(bias_o, jnp.concatenate(pair_rows, axis=0))

    store_tiles(qf_o, q_feature_major(wqf_ref, gqf_ref, False))
    kf_o[0] = k_token_major(wkf_ref, gkf_ref, False).astype(BF16)
    store_tiles(vf_o, proj_t(wvf_ref[...]), KEY_TILE)

    f = proj(wf_ref[...]) + bf_ref[...]
    logf = jnp.minimum(f, 0.0) - jnp.log1p(jnp.exp(-jnp.abs(f)))
    live = lax.broadcasted_iota(jnp.int32, (1, LANES), 1) < GROUP_HEADS
    logf = jnp.where(live, logf, 0.0)
    r_i = lax.broadcasted_iota(jnp.int32, (CUM_BLOCK, CUM_BLOCK), 0)
    c_i = lax.broadcasted_iota(jnp.int32, (CUM_BLOCK, CUM_BLOCK), 1)
    tri = jnp.where(r_i >= c_i, 1.0, 0.0).astype(BF16)
    n_blk = tm // CUM_BLOCK
    wide = jnp.concatenate([logf[b * CUM_BLOCK:(b + 1) * CUM_BLOCK, :] for b in range(n_blk)],
                           axis=1)
    parts = _split3(wide)
    prefix = _dot(tri, parts[0]) + _dot(tri, parts[1]) + _dot(tri, parts[2])
    carry = carry_sc[...]
    cums = []
    for blk in range(n_blk):
        c_blk = prefix[:, blk * LANES:(blk + 1) * LANES] + carry
        carry = c_blk[CUM_BLOCK - 1:CUM_BLOCK, :]
        cums.append(c_blk)
    carry_sc[...] = carry
    nc = jnp.concatenate(cums, axis=0) * (-LOG2E)
    hi, mid, lo = [part.astype(F32) for part in _split3(nc)]
    packed = hi + pltpu.roll(mid, GROUP_HEADS, 1) + pltpu.roll(lo, 2 * GROUP_HEADS, 1)
    ka_o[0] = (_dot(packed.astype(BF16), place_ref[...]) + kones_ref[...]).astype(BF16)

    nc_t = nc.T[0:GROUP_HEADS, :]
    hi_t = nc_t.astype(BF16).astype(F32)
    rem_t = nc_t - hi_t
    mid_t = rem_t.astype(BF16).astype(F32)
    lo_t = (rem_t - mid_t).astype(BF16).astype(F32)
    r8 = lax.broadcasted_iota(jnp.int32, (8, tm), 0)
    zeros56 = jnp.zeros((HEAD_DIM - 8, tm), F32)
    qa_rows = []
    for hd in range(GROUP_HEADS):
        blk = jnp.where(r8 < 3, 1.0,
                        jnp.where(r8 == 3, -hi_t[hd:hd + 1, :],
                                  jnp.where(r8 == 4, -mid_t[hd:hd + 1, :],
                                            jnp.where(r8 == 5, -lo_t[hd:hd + 1, :], 0.0))))
        qa_rows += [blk, zeros56]
    store_tiles(qa_o, jnp.concatenate(qa_rows, axis=0))


def _inproj(x, sc, sh, wts, consts, tm):
    bsz, s, d = x.shape
    nt = s // ATT_TILE
    grid = (bsz, s // tm)
    wqm, wkm, wvm, wqf, wkf, wvf, wf, bfor, gqm, gkm, gqf, gkf = wts
    cos_t, sin_t, rc, rs1, rs2, jmat, place, kones = consts

    def const(a):
        nd = a.ndim
        return pl.BlockSpec(a.shape, lambda b, i, _n=nd: (0,) * _n,
                            pipeline_mode=pl.Buffered(1))

    mod_spec = pl.BlockSpec((None, 1, d), lambda b, i: (b, 0, 0))
    in_specs = [pl.BlockSpec((1, tm, d), lambda b, i: (b, i, 0)), mod_spec, mod_spec]
    in_specs += [const(a) for a in (wqm, wkm, wvm, wqf, wkf, wvf, wf, bfor,
                                    gqm, gkm, gqf, gkf)]
    in_specs += [pl.BlockSpec((ROPE_HALF, tm), lambda b, i: (0, i)),
                 pl.BlockSpec((ROPE_HALF, tm), lambda b, i: (0, i)),
                 pl.BlockSpec((tm, LANES), lambda b, i: (i, 0)),
                 pl.BlockSpec((tm, LANES), lambda b, i: (i, 0)),
                 pl.BlockSpec((tm, LANES), lambda b, i: (i, 0)),
                 const(jmat), const(place), const(kones)]

    tiles = tm // ATT_TILE
    feat_spec = pl.BlockSpec((1, tiles, GROUP_WIDTH, ATT_TILE), lambda b, i: (b, i, 0, 0))
    tok_spec = pl.BlockSpec((1, tm, GROUP_WIDTH), lambda b, i: (b, i, 0))
    feat_shape = jax.ShapeDtypeStruct((bsz, nt, GROUP_WIDTH, ATT_TILE), BF16)
    tok_shape = jax.ShapeDtypeStruct((bsz, s, GROUP_WIDTH), BF16)
    val_spec = pl.BlockSpec((1, tm // KEY_TILE, GROUP_WIDTH, KEY_TILE),
                            lambda b, i: (b, i, 0, 0))
    val_shape = jax.ShapeDtypeStruct((bsz, s // KEY_TILE, GROUP_WIDTH, KEY_TILE), BF16)
    out_specs = [feat_spec, tok_spec, val_spec, feat_spec,
                 feat_spec, tok_spec, val_spec, tok_spec, feat_spec]
    out_shape = [feat_shape, tok_shape, val_shape, feat_shape,
                 feat_shape, tok_shape, val_shape, tok_shape, feat_shape]
    return pl.pallas_call(
        functools.partial(_inproj_kernel, tm=tm),
        grid=grid,
        in_specs=in_specs,
        out_specs=out_specs,
        out_shape=out_shape,
        scratch_shapes=[pltpu.VMEM((GROUP_HEADS * 16, GROUP_WIDTH), F32),
                        pltpu.VMEM((1, LANES), F32)],
        compiler_params=pltpu.CompilerParams(
            dimension_semantics=("arbitrary", "arbitrary"),
            vmem_limit_bytes=VMEM_LIMIT),
        name="inproj",
    )(x, sc, sh, wqm, wkm, wvm, wqf, wkf, wvf, wf, bfor, gqm, gkm, gqf, gkf,
      cos_t, sin_t, rc, rs1, rs2, jmat, place, kones)


def _attn_kernel(q_ref, qa_ref, k_ref, ka_ref, v_ref, o_ref, *, t):
    i = pl.program_id(2)
    q = q_ref[0, 0]
    qa = qa_ref[0, 0]
    row = lax.broadcasted_iota(jnp.int32, (PAIR_WIDTH, t), 0)
    rhs = []
    for e in range(2):
        keep = (row >= e * HEAD_DIM) & (row < (e + 1) * HEAD_DIM)
        zero = jnp.zeros_like(q)
        rhs.append(jnp.concatenate([jnp.where(keep, q, zero), jnp.where(keep, qa, zero)],
                                   axis=0))

    def tile_scores(j):
        start = pl.multiple_of(j * t, t)
        lhs = jnp.concatenate([k_ref[0, pl.ds(start, t), :], ka_ref[0, pl.ds(start, t), :]],
                              axis=1)
        return [_dot(lhs, rhs[e]) for e in range(2)]

    def values(j, e):
        rows = slice(e * HEAD_DIM, (e + 1) * HEAD_DIM)
        slabs = t // KEY_TILE
        return jnp.concatenate([v_ref[0, slabs * j + h, rows, :] for h in range(slabs)],
                               axis=1)

    key_i = lax.broadcasted_iota(jnp.int32, (t, t), 0)
    qry_i = lax.broadcasted_iota(jnp.int32, (t, t), 1)
    causal = key_i <= qry_i
    state = []
    for e, s_t in enumerate(tile_scores(i)):
        s_t = jnp.where(causal, s_t, NEG_BIG)
        m = jnp.max(s_t, axis=0, keepdims=True)
        p = jnp.exp2(s_t - m)
        l = jnp.sum(p, axis=0, keepdims=True)
        acc = _dot(values(i, e), p.astype(BF16))
        state += [m, l, acc]

    def body(j, carry):
        out = []
        for e, s_t in enumerate(tile_scores(j)):
            m, l, acc = carry[3 * e:3 * e + 3]
            m_new = jnp.maximum(m, jnp.max(s_t, axis=0, keepdims=True))
            alpha = jnp.exp2(m - m_new)
            p = jnp.exp2(s_t - m_new)
            l = alpha * l + jnp.sum(p, axis=0, keepdims=True)
            acc = alpha * acc + _dot(values(j, e), p.astype(BF16))
            out += [m_new, l, acc]
        return tuple(out)

    state = lax.fori_loop(0, i, body, tuple(state))
    o_t = jnp.concatenate([state[2] / state[1], state[5] / state[4]], axis=0)
    o_ref[0] = o_t.T


def _attn_bounded_kernel(q_ref, qa_ref, k_ref, ka_ref, v_ref, o_ref, p_sc, acc_sc, l_sc, *, t,
                         ka_const):
    i = pl.program_id(1)
    kt = KEY_TILE
    row = lax.broadcasted_iota(jnp.int32, (PAIR_WIDTH, t), 0)
    key_i = lax.broadcasted_iota(jnp.int32, (kt, t), 0)
    qry_i = lax.broadcasted_iota(jnp.int32, (kt, t), 1)
    assert t == 2 * kt


    def rhs_of(pr):
        feat = slice(pr * PAIR_WIDTH, (pr + 1) * PAIR_WIDTH)
        q = q_ref[0, 0, feat, :]
        qa = qa_ref[0, 0, feat, :]
        out = []
        for e in range(2):
            keep = (row >= e * HEAD_DIM) & (row < (e + 1) * HEAD_DIM)
            zero = jnp.zeros_like(q)
            out.append(jnp.concatenate([jnp.where(keep, q, zero), jnp.where(keep, qa, zero)],
                                       axis=0))
        return out

    for grp in range(N_PAIRS // PAIRS_PER_TRIP):
        prs = list(range(grp * PAIRS_PER_TRIP, (grp + 1) * PAIRS_PER_TRIP))
        rhs = {pr: rhs_of(pr) for pr in prs}

        def probs(pr, start, n_keys, row0, mask=None, q0=0, first=False, rhs=rhs):
            feat = slice(pr * PAIR_WIDTH, (pr + 1) * PAIR_WIDTH)
            k_aug = (ka_ref[0, pl.ds(start, n_keys), :] if ka_const
                     else ka_ref[0, pl.ds(start, n_keys), feat])
            lhs = jnp.concatenate([k_ref[0, pl.ds(start, n_keys), feat], k_aug], axis=1)
            for e in range(2):
                s_t = _dot(lhs, rhs[pr][e][:, q0:])
                if mask is not None:
                    s_t = jnp.where(mask[:, q0:], s_t, NEG_BIG)
                p = jnp.exp2(s_t)
                p_sc[pr, e, row0:row0 + n_keys, q0:] = p.astype(BF16)
                part = p[0:8, :]
                for r in range(8, n_keys, 8):
                    part = part + p[r:r + 8, :]
                if first:
                    l_sc[pr, e] = part
                else:
                    l_sc[pr, e, :, q0:] += part

        def values(pr, j):
            rows = slice(pr * PAIR_WIDTH, (pr + 1) * PAIR_WIDTH)
            v_pair = jnp.concatenate([v_ref[0, 2 * j, rows, :], v_ref[0, 2 * j + 1, rows, :]],
                                     axis=1)
            for e in range(2):
                upd = _dot(v_pair, p_sc[pr, e])
                acc_sc[pr, e] += upd[e * HEAD_DIM:(e + 1) * HEAD_DIM, :]

        diag = pl.multiple_of(i * t, t)
        for pr in prs:
            probs(pr, diag, kt, 0, key_i <= qry_i, first=True)
            probs(pr, diag + kt, kt, kt, key_i + kt <= qry_i, q0=kt)
        for pr in prs:
            rows = slice(pr * PAIR_WIDTH, (pr + 1) * PAIR_WIDTH)
            for e in range(2):
                head = slice(e * HEAD_DIM, (e + 1) * HEAD_DIM)
                lower = _dot(v_ref[0, 2 * i, rows, :], p_sc[pr, e, :kt, :])
                upper = _dot(v_ref[0, 2 * i + 1, rows, :], p_sc[pr, e, kt:, kt:])
                acc_sc[pr, e, :, :kt] = lower[head, :kt]
                acc_sc[pr, e, :, kt:] = lower[head, kt:] + upper[head, :]

        def body(j, carry, probs=probs, values=values, prs=prs):
            for pr in prs:
                probs(pr, pl.multiple_of(j * t, t), t, 0)
            for pr in prs:
                values(pr, j)
            return carry

        def unrolled(n, body=body):
            def trip(m, j_prev):
                for u in range(n):
                    j_prev = body(n * m + u, j_prev)
                return j_prev
            return trip

        done = 0
        j_prev = i
        for n in (4, 2, 1):
            trips = (i - done) // n
            j_prev = lax.fori_loop(done // n, done // n + trips, unrolled(n), j_prev)
            done = done + trips * n
        for pr in prs:
            o_t = jnp.concatenate(
                [acc_sc[pr, e] / jnp.sum(l_sc[pr, e], axis=0, keepdims=True)
                 for e in range(2)], axis=0)
            o_ref[0, :, pr * PAIR_WIDTH:(pr + 1) * PAIR_WIDTH] = o_t.T


def _attention_online(q_t, qa, k, ka, v_t, *, ka_const, name):
    bsz, nt, _, t = q_t.shape
    s = k.shape[1]
    grid = (bsz, N_PAIRS, nt)
    if ka_const:
        ka_spec = pl.BlockSpec((1, s, PAIR_WIDTH), lambda b, p, i: (0, 0, 0))
    else:
        ka_spec = pl.BlockSpec((1, s, PAIR_WIDTH), lambda b, p, i: (b, 0, p))
    return pl.pallas_call(
        functools.partial(_attn_kernel, t=t),
        grid=grid,
        in_specs=[pl.BlockSpec((1, 1, PAIR_WIDTH, t), lambda b, p, i: (b, i, p, 0)),
                  pl.BlockSpec((1, 1, PAIR_WIDTH, t), lambda b, p, i: (b, i, p, 0)),
                  pl.BlockSpec((1, s, PAIR_WIDTH), lambda b, p, i: (b, 0, p)),
                  ka_spec,
                  pl.BlockSpec((1, s // KEY_TILE, PAIR_WIDTH, KEY_TILE),
                               lambda b, p, i: (b, 0, p, 0))],
        out_specs=pl.BlockSpec((1, t, PAIR_WIDTH), lambda b, p, i: (b, i, p)),
        out_shape=jax.ShapeDtypeStruct((bsz, s, GROUP_WIDTH), F32),
        compiler_params=pltpu.CompilerParams(
            dimension_semantics=("arbitrary", "arbitrary", "arbitrary"),
            vmem_limit_bytes=VMEM_LIMIT),
        name=name,
    )(q_t, qa, k, ka, v_t)


def _attention_bounded(q_t, qa, k, ka, v_t, *, ka_const, name):
    bsz, nt, _, t = q_t.shape
    s = k.shape[1]
    feat_spec = pl.BlockSpec((1, 1, GROUP_WIDTH, t), lambda b, i: (b, i, 0, 0))
    tok_spec = pl.BlockSpec((1, s, GROUP_WIDTH), lambda b, i: (b, 0, 0))
    ka_spec = pl.BlockSpec((1, s, PAIR_WIDTH), lambda b, i: (0, 0, 0)) if ka_const else tok_spec
    return pl.pallas_call(
        functools.partial(_attn_bounded_kernel, t=t, ka_const=ka_const),
        grid=(bsz, nt),
        in_specs=[feat_spec, feat_spec, tok_spec, ka_spec,
                  pl.BlockSpec((1, s // KEY_TILE, GROUP_WIDTH, KEY_TILE),
                               lambda b, i: (b, 0, 0, 0))],
        out_specs=pl.BlockSpec((1, t, GROUP_WIDTH), lambda b, i: (b, i, 0)),
        out_shape=jax.ShapeDtypeStruct((bsz, s, GROUP_WIDTH), F32),
        scratch_shapes=[pltpu.VMEM((N_PAIRS, 2, t, t), BF16),
                        pltpu.VMEM((N_PAIRS, 2, HEAD_DIM, t), F32),
                        pltpu.VMEM((N_PAIRS, 2, 8, t), F32)],
        compiler_params=pltpu.CompilerParams(
            dimension_semantics=("arbitrary", "arbitrary"),
            vmem_limit_bytes=VMEM_LIMIT),
        name=name,
    )(q_t, qa, k, ka, v_t)


def _attention_dispatch(g_q, g_k, q_t, qa, k, ka, v_t, *, ka_const, name):
    bound = HEAD_DIM * jnp.max(jnp.abs(g_q)) * jnp.max(jnp.abs(g_k))
    return lax.cond(
        bound <= MAX_SAFE_EXPONENT,
        functools.partial(_attention_bounded, ka_const=ka_const, name=name + "_bounded"),
        functools.partial(_attention_online, ka_const=ka_const, name=name + "_online"),
        q_t, qa, k, ka, v_t)


def _post_kernel(x_ref, om_ref, of_ref, ga_ref, scm_ref, shm_ref, gm_ref,
                 gom_ref, gof_ref, wout_ref, w1_ref, w2_ref, o_ref, *, ff_chunk):
    def rms(v):
        return v * lax.rsqrt(jnp.mean(v * v, axis=-1, keepdims=True) + EPS)

    mixed = jnp.concatenate([rms(om_ref[0]) * gom_ref[...],
                             rms(of_ref[0]) * gof_ref[...]], axis=1)
    x1 = x_ref[0] + ga_ref[...] * _dot(mixed, wout_ref[...])
    h = rms(x1) * (1.0 + scm_ref[...]) + shm_ref[...]
    d_ff = w1_ref.shape[1]
    y = None
    for c in range(d_ff // ff_chunk):
        hid = _dot(h, w1_ref[:, c * ff_chunk:(c + 1) * ff_chunk])
        hid = jnp.square(jnp.maximum(hid, 0.0))
        part = _dot(hid, w2_ref[c * ff_chunk:(c + 1) * ff_chunk, :])
        y = part if y is None else y + part
    o_ref[0] = x1 + gm_ref[...] * y


def _post(x, o_m, o_f, mod4, g_out_m, g_out_f, w_out, w_ff1, w_ff2, tm):
    bsz, s, d = x.shape
    d_ff = w_ff1.shape[1]

    def const(a):
        nd = a.ndim
        return pl.BlockSpec(a.shape, lambda b, i, _n=nd: (0,) * _n,
                            pipeline_mode=pl.Buffered(1))

    def mod_spec(k):
        return pl.BlockSpec((None, None, 1, d), lambda b, i, _k=k: (b, _k, 0, 0))

    tok = lambda w: pl.BlockSpec((1, tm, w), lambda b, i: (b, i, 0))
    return pl.pallas_call(
        functools.partial(_post_kernel, ff_chunk=1024),
        grid=(bsz, s // tm),
        in_specs=[tok(d), tok(GROUP_WIDTH), tok(GROUP_WIDTH),
                  mod_spec(2), mod_spec(4), mod_spec(3), mod_spec(5),
                  const(g_out_m), const(g_out_f), const(w_out), const(w_ff1), const(w_ff2)],
        out_specs=tok(d),
        out_shape=jax.ShapeDtypeStruct((bsz, s, d), F32),
        compiler_params=pltpu.CompilerParams(
            dimension_semantics=("arbitrary", "arbitrary"),
            vmem_limit_bytes=POST_VMEM_LIMIT),
        name="post",
    )(x, o_m, o_f, mod4, mod4, mod4, mod4, g_out_m, g_out_f, w_out, w_ff1, w_ff2)


def _rope_tables(s):
    inv_freq = ROPE_THETA ** (-np.arange(0, ROPE_DIM, 2, dtype=np.float64) / ROPE_DIM)
    ang = np.arange(s, dtype=np.float64)[:, None] * inv_freq[None, :]
    cos, sin = np.cos(ang), np.sin(ang)
    rc = np.ones((s, LANES)); rs1 = np.zeros((s, LANES)); rs2 = np.zeros((s, LANES))
    for base in (0, HEAD_DIM):
        rc[:, base:base + ROPE_HALF] = cos
        rc[:, base + ROPE_HALF:base + ROPE_DIM] = cos
        rs2[:, base:base + ROPE_HALF] = -sin
        rs1[:, base + ROPE_HALF:base + ROPE_DIM] = sin
    f = lambda a: jnp.asarray(a, dtype=F32)
    return f(cos.T), f(sin.T), f(rc), f(rs1), f(rs2)


def _static_mats(s):
    idx = np.arange(256)
    jmat = (idx[:, None] // HEAD_DIM == idx[None, :] // HEAD_DIM) / HEAD_DIM
    place = np.zeros((LANES, GROUP_WIDTH))
    for part in range(3):
        for hd in range(GROUP_HEADS):
            place[part * GROUP_HEADS + hd,
                  (hd // 2) * PAIR_WIDTH + (hd % 2) * HEAD_DIM + part] = 1.0
    kones = np.zeros((1, GROUP_WIDTH))
    for hd in range(GROUP_HEADS):
        kones[0, hd * HEAD_DIM + 3:hd * HEAD_DIM + 6] = 1.0
    ka_moba = np.zeros((1, s, PAIR_WIDTH))
    blk = np.arange(s) // MOBA_BLOCK
    for e in range(2):
        ka_moba[0, np.arange(s), e * HEAD_DIM + blk] = 1.0
    b = lambda a: jnp.asarray(a, dtype=BF16)
    return b(jmat), b(place), jnp.asarray(kones, dtype=F32), b(ka_moba)


def kernel(x, c, w_ada, b_ada, w_in, b_forget, g_qn_moba, g_kn_moba, g_qn_fox, g_kn_fox,
           g_out_moba, g_out_fox, w_out, w_ff1, w_ff2):
    bsz, s, d = x.shape
    depth = w_ada.shape[0]
    w = GROUP_WIDTH
    assert s % MOBA_BLOCK == 0 and s // MOBA_BLOCK <= 16
    tm_in = 1024
    tm_post = 512
    cos_t, sin_t, rc, rs1, rs2 = _rope_tables(s)
    jmat, place, kones, ka_moba = _static_mats(s)
    q_scale = HEAD_DIM ** -0.5 * LOG2E

    for l in range(depth):
        mod = _adaln(c, w_ada[l], b_ada[l])
        mod4 = mod.reshape(bsz, 6, 1, d)

        wl = w_in[l]
        wf =jnp.zeros((d, LANES), F32).at[:, :GROUP_HEADS].set(wl[:, 6 * w:]).astype(BF16)
        bfor = jnp.zeros((1, LANES), F32).at[0, :GROUP_HEADS].set(b_forget[l])
        q_gain = lambda g: jnp.broadcast_to((g * q_scale)[:, None], (HEAD_DIM, tm_in))
        k_gain = lambda g: jnp.tile(g, GROUP_HEADS).reshape(1, w)
        wts = _project_weights(w_in, l, w) + (
               wf, bfor,
               q_gain(g_qn_moba[l]), k_gain(g_kn_moba[l]),
               q_gain(g_qn_fox[l]), k_gain(g_kn_fox[l]))
        consts = (cos_t, sin_t, rc, rs1, rs2, jmat, place, kones)

        qm_t, km, vm_t, bias_t, qf_t, kf, vf_t, ka_f, qa_f = _inproj(
            x, mod4[:, 1], mod4[:, 0], wts, consts, tm_in)

        o_m = _attention_dispatch(g_qn_moba[l] * q_scale, g_kn_moba[l],
                                  qm_t, bias_t, km, ka_moba, vm_t, ka_const=True,
                                  name="attn_moba")
        o_f = _attention_dispatch(g_qn_fox[l] * q_scale, g_kn_fox[l],
                                  qf_t, qa_f, kf, ka_f, vf_t, ka_const=False,
                                  name="attn_fox")

        x = _post(x, o_m, o_f, mod4,
                  g_out_moba[l].reshape(1, w), g_out_fox[l].reshape(1, w),
                  w_out[l], w_ff1[l], w_ff2[l],
                  tm_post)
    return x
```

```python
import functools
import math

import numpy as np
import jax
import jax.numpy as jnp
from jax import lax
from jax.experimental import pallas as pl
from jax.experimental.pallas import tpu as pltpu

F32 = jnp.float32
BF16 = jnp.bfloat16

HEAD_DIM = 64
GROUP_HEADS = 8
GROUP_WIDTH = GROUP_HEADS * HEAD_DIM
PAIR_WIDTH = 2 * HEAD_DIM
N_PAIRS = GROUP_HEADS // 2
MOBA_BLOCK = 256
MOBA_TOPK = 3
ROPE_THETA = 500000.0
ROPE_DIM = HEAD_DIM // 4
ROPE_HALF = ROPE_DIM // 2
EPS = 1e-6
LOG2E = math.log2(math.e)
NEG_BIG = -1e30
MAX_SAFE_EXPONENT = 60.0
ATT_TILE = 512
KEY_TILE = 256
CUM_BLOCK = 128
NORM_CHUNKS = 2
PAIRS_PER_TRIP = 4
LANES = 128
VMEM_LIMIT = 48 * 1024 * 1024
POST_VMEM_LIMIT = 58 * 1024 * 1024


def _dot(a, b):
    return jnp.dot(a, b, preferred_element_type=F32)


def _split3(x):
    hi = x.astype(BF16)
    rem = x - hi.astype(F32)
    mid = rem.astype(BF16)
    lo = (rem - mid.astype(F32)).astype(BF16)
    return hi, mid, lo


def _dot_nt(a, b):
    return lax.dot_general(a, b, (((1,), (1,)), ((), ())), preferred_element_type=F32)


def _adaln_kernel(ct_ref, w_ref, b_ref, o_ref):
    c_t = ct_ref[...]
    s_t = c_t / (1.0 + jnp.exp(-c_t))
    w = w_ref[...]
    rows = [jnp.sum(s_t[:, b:b + 1] * w, axis=0, keepdims=True) for b in range(c_t.shape[1])]
    o_ref[...] = jnp.concatenate(rows, axis=0) + b_ref[...]


def _adaln(c, w_ada, b_ada):
    bsz, d = c.shape
    n = w_ada.shape[1]
    return pl.pallas_call(
        _adaln_kernel,
        grid=(n // d,),
        in_specs=[pl.BlockSpec((d, bsz), lambda j: (0, 0)),
                  pl.BlockSpec((d, d), lambda j: (0, j)),
                  pl.BlockSpec((1, d), lambda j: (0, j))],
        out_specs=pl.BlockSpec((bsz, d), lambda j: (0, j)),
        out_shape=jax.ShapeDtypeStruct((bsz, n), F32),
        compiler_params=pltpu.CompilerParams(dimension_semantics=("arbitrary",)),
        name="adaln",
    )(c.T, w_ada, b_ada.reshape(1, n))


TRANSPOSED_GROUPS = (0, 2, 3, 5)


def _wprep_kernel(w_ref, *o_refs):
    g = pl.program_id(0)
    for k, o_ref in enumerate(o_refs):
        @pl.when(g == k)
        def _(k=k, o_ref=o_ref):
            w = w_ref[...]
            o_ref[...] = (w.T if k in TRANSPOSED_GROUPS else w).astype(BF16)


def _project_weights(w_in, layer, width):
    d = w_in.shape[1]
    shapes = [(width, d) if k in TRANSPOSED_GROUPS else (d, width) for k in range(6)]
    return tuple(pl.pallas_call(
        _wprep_kernel,
        grid=(6,),
        in_specs=[pl.BlockSpec((None, d, width), lambda g: (layer, 0, g))],
        out_specs=[pl.BlockSpec(s, lambda g: (0, 0)) for s in shapes],
        out_shape=[jax.ShapeDtypeStruct(s, BF16) for s in shapes],
        compiler_params=pltpu.CompilerParams(dimension_semantics=("arbitrary",)),
        name="wprep",
    )(w_in))


def _inproj_kernel(x_ref, sc_ref, sh_ref,
                   wqm_ref, wkm_ref, wvm_ref, wqf_ref, wkf_ref, wvf_ref, wf_ref, bf_ref,
                   gqm_ref, gkm_ref, gqf_ref, gkf_ref,
                   cos_t_ref, sin_t_ref, rc_ref, rs1_ref, rs2_ref,
                   j_ref, place_ref, kones_ref,
                   qm_o, km_o, vm_o, bias_o, qf_o, kf_o, vf_o, ka_o, qa_o,
                   kmean_sc, carry_sc, *, tm):
    i = pl.program_id(1)
    nb = tm // MOBA_BLOCK

    @pl.when(i == 0)
    def _():
        kmean_sc[...] = jnp.zeros_like(kmean_sc)
        carry_sc[...] = jnp.zeros_like(carry_sc)

    rows_per_chunk = tm // NORM_CHUNKS
    hb_chunks = []
    for ch in range(NORM_CHUNKS):
        x = x_ref[0, ch * rows_per_chunk:(ch + 1) * rows_per_chunk, :]
        ms = jnp.mean(x * x, axis=-1, keepdims=True)
        h = x * lax.rsqrt(ms + EPS) * (1.0 + sc_ref[...]) + sh_ref[...]
        hb_chunks.append(h.astype(BF16))

    def proj(w):
        return jnp.concatenate([_dot(hb, w) for hb in hb_chunks], axis=0)

    def proj_t(w_t):
        return jnp.concatenate([_dot_nt(w_t, hb) for hb in hb_chunks], axis=1)

    def q_feature_major(w_ref, g_ref, rope):
        q_t = proj_t(w_ref[...])
        gain = g_ref[...]
        heads = []
        for hd in range(GROUP_HEADS):
            xh = q_t[hd * HEAD_DIM:(hd + 1) * HEAD_DIM, :]
            ss = jnp.sum(xh * xh, axis=0, keepdims=True)
            xh = xh * lax.rsqrt(ss * (1.0 / HEAD_DIM) + EPS) * gain
            if rope:
                x1 = xh[0:ROPE_HALF, :]
                x2 = xh[ROPE_HALF:ROPE_DIM, :]
                cs = cos_t_ref[...]
                sn = sin_t_ref[...]
                xh = jnp.concatenate(
                    [x1 * cs - x2 * sn, x2 * cs + x1 * sn, xh[ROPE_DIM:, :]], axis=0)
            heads.append(xh)
        return jnp.concatenate(heads, axis=0)

    def k_token_major(w_ref, g_ref, rope):
        k = proj(w_ref[...])
        chunks = []
        for c in range(GROUP_WIDTH // 256):
            seg = k[:, c * 256:(c + 1) * 256]
            msq = _dot((seg * seg).astype(BF16), j_ref[...])
            chunks.append(seg * lax.rsqrt(msq + EPS) * g_ref[:, c * 256:(c + 1) * 256])
        k = jnp.concatenate(chunks, axis=1)
        if rope:
            outs = []
            for c in range(GROUP_WIDTH // LANES):
                seg = k[:, c * LANES:(c + 1) * LANES]
                outs.append(seg * rc_ref[...]
                            + pltpu.roll(seg, ROPE_HALF, 1) * rs1_ref[...]
                            + pltpu.roll(seg, LANES - ROPE_HALF, 1) * rs2_ref[...])
            k = jnp.concatenate(outs, axis=1)
        return k

    def store_tiles(o_ref, val_t, width=ATT_TILE):
        for t in range(tm // width):
            o_ref[0, t] = val_t[:, t * width:(t + 1) * width].astype(o_ref.dtype)

    qm_t = q_feature_major(wqm_ref, gqm_ref, True)
    store_tiles(qm_o, qm_t)
    km = k_token_major(wkm_ref, gkm_ref, True)
    km_o[0] = km.astype(BF16)
    store_tiles(vm_o, proj_t(wvm_ref[...]), KEY_TILE)

    blk0 = i * nb
    lane_head = lax.broadcasted_iota(jnp.int32, (1, GROUP_WIDTH), 1) // HEAD_DIM
    for sb in range(nb):
        kmean = jnp.mean(km[sb * MOBA_BLOCK:(sb + 1) * MOBA_BLOCK, :], axis=0, keepdims=True)
        for hd in range(GROUP_HEADS):
            row = hd * 16 + blk0 + sb
            kmean_sc[pl.ds(row, 1), :] = jnp.where(lane_head == hd, kmean, 0.0)

    km_hi, km_lo, _ = _split3(kmean_sc[...])
    q_hi, q_lo, _ = _split3(qm_t)
    gate_t = _dot(km_hi, q_hi) + _dot(km_hi, q_lo) + _dot(km_lo, q_hi)
    tok_blk = blk0 + lax.broadcasted_iota(jnp.int32, (16, tm), 1) // MOBA_BLOCK
    n_idx = lax.broadcasted_iota(jnp.int32, (16, tm), 0)
    valid = n_idx < tok_blk

    zeros48 = jnp.zeros((HEAD_DIM - 16, tm), F32)
    pair_rows = []
    for hd in range(GROUP_HEADS):
        g = jnp.where(valid, gate_t[hd * 16:(hd + 1) * 16, :], -jnp.inf)
        picked = jnp.zeros((16, tm), jnp.int32)
        for _ in range(MOBA_TOPK):
            best = jnp.max(g, axis=0, keepdims=True)
            first = jnp.min(jnp.where(g == best, n_idx, 16), axis=0, keepdims=True)
            hit = n_idx == first
            picked = jnp.where(hit, 1, picked)
            g = jnp.where(hit, -jnp.inf, g)
        sel = (valid & (picked > 0)) | (n_idx == tok_blk)
        pair_rows += [jnp.where(sel, 0.0, NEG_BIG), zeros48]
    store_tiles(bias_o, jnp.concatenate(pair_rows, axis=0))

    store_tiles(qf_o, q_feature_major(wqf_ref, gqf_ref, False))
    kf_o[0] = k_token_major(wkf_ref, gkf_ref, False).astype(BF16)
    store_tiles(vf_o, proj_t(wvf_ref[...]), KEY_TILE)

    f = proj(wf_ref[...]) + bf_ref[...]
    logf = jnp.minimum(f, 0.0) - jnp.log1p(jnp.exp(-jnp.abs(f)))
    live = lax.broadcasted_iota(jnp.int32, (1, LANES), 1) < GROUP_HEADS
    logf = jnp.where(live, logf, 0.0)
    r_i = lax.broadcasted_iota(jnp.int32, (CUM_BLOCK, CUM_BLOCK), 0)
    c_i = lax.broadcasted_iota(jnp.int32, (CUM_BLOCK, CUM_BLOCK), 1)
    tri = jnp.where(r_i >= c_i, 1.0, 0.0).astype(BF16)
    n_blk = tm // CUM_BLOCK
    wide = jnp.concatenate([logf[b * CUM_BLOCK:(b + 1) * CUM_BLOCK, :] for b in range(n_blk)],
                           axis=1)
    parts = _split3(wide)
    prefix = _dot(tri, parts[0]) + _dot(tri, parts[1]) + _dot(tri, parts[2])
    carry = carry_sc[...]
    cums = []
    for blk in range(n_blk):
        c_blk = prefix[:, blk * LANES:(blk + 1) * LANES] + carry
        carry = c_blk[CUM_BLOCK - 1:CUM_BLOCK, :]
        cums.append(c_blk)
    carry_sc[...] = carry
    nc = jnp.concatenate(cums, axis=0) * (-LOG2E)
    hi, mid, lo = [part.astype(F32) for part in _split3(nc)]
    packed = hi + pltpu.roll(mid, GROUP_HEADS, 1) + pltpu.roll(lo, 2 * GROUP_HEADS, 1)
    ka_o[0] = (_dot(packed.astype(BF16), place_ref[...]) + kones_ref[...]).astype(BF16)

    nc_t = nc.T[0:GROUP_HEADS, :]
    hi_t = nc_t.astype(BF16).astype(F32)
    rem_t = nc_t - hi_t
    mid_t = rem_t.astype(BF16).astype(F32)
    lo_t = (rem_t - mid_t).astype(BF16).astype(F32)
    r8 = lax.broadcasted_iota(jnp.int32, (8, tm), 0)
    zeros56 = jnp.zeros((HEAD_DIM - 8, tm), F32)
    qa_rows = []
    for hd in range(GROUP_HEADS):
        blk = jnp.where(r8 < 3, 1.0,
                        jnp.where(r8 == 3, -hi_t[hd:hd + 1, :],
                                  jnp.where(r8 == 4, -mid_t[hd:hd + 1, :],
                                            jnp.where(r8 == 5, -lo_t[hd:hd + 1, :], 0.0))))
        qa_rows += [blk, zeros56]
    store_tiles(qa_o, jnp.concatenate(qa_rows, axis=0))


def _inproj(x, sc, sh, wts, consts, tm):
    bsz, s, d = x.shape
    nt = s // ATT_TILE
    grid = (bsz, s // tm)
    wqm, wkm, wvm, wqf, wkf, wvf, wf, bfor, gqm, gkm, gqf, gkf = wts
    cos_t, sin_t, rc, rs1, rs2, jmat, place, kones = consts

    def const(a):
        nd = a.ndim
        return pl.BlockSpec(a.shape, lambda b, i, _n=nd: (0,) * _n,
                            pipeline_mode=pl.Buffered(1))

    mod_spec = pl.BlockSpec((None, 1, d), lambda b, i: (b, 0, 0))
    in_specs = [pl.BlockSpec((1, tm, d), lambda b, i: (b, i, 0)), mod_spec, mod_spec]
    in_specs += [const(a) for a in (wqm, wkm, wvm, wqf, wkf, wvf, wf, bfor,
                                    gqm, gkm, gqf, gkf)]
    in_specs += [pl.BlockSpec((ROPE_HALF, tm), lambda b, i: (0, i)),
                 pl.BlockSpec((ROPE_HALF, tm), lambda b, i: (0, i)),
                 pl.BlockSpec((tm, LANES), lambda b, i: (i, 0)),
                 pl.BlockSpec((tm, LANES), lambda b, i: (i, 0)),
                 pl.BlockSpec((tm, LANES), lambda b, i: (i, 0)),
                 const(jmat), const(place), const(kones)]

    tiles = tm // ATT_TILE
    feat_spec = pl.BlockSpec((1, tiles, GROUP_WIDTH, ATT_TILE), lambda b, i: (b, i, 0, 0))
    tok_spec = pl.BlockSpec((1, tm, GROUP_WIDTH), lambda b, i: (b, i, 0))
    feat_shape = jax.ShapeDtypeStruct((bsz, nt, GROUP_WIDTH, ATT_TILE), BF16)
    tok_shape = jax.ShapeDtypeStruct((bsz, s, GROUP_WIDTH), BF16)
    val_spec = pl.BlockSpec((1, tm // KEY_TILE, GROUP_WIDTH, KEY_TILE),
                            lambda b, i: (b, i, 0, 0))
    val_shape = jax.ShapeDtypeStruct((bsz, s // KEY_TILE, GROUP_WIDTH, KEY_TILE), BF16)
    out_specs = [feat_spec, tok_spec, val_spec, feat_spec,
                 feat_spec, tok_spec, val_spec, tok_spec, feat_spec]
    out_shape = [feat_shape, tok_shape, val_shape, feat_shape,
                 feat_shape, tok_shape, val_shape, tok_shape, feat_shape]
    return pl.pallas_call(
        functools.partial(_inproj_kernel, tm=tm),
        grid=grid,
        in_specs=in_specs,
        out_specs=out_specs,
        out_shape=out_shape,
        scratch_shapes=[pltpu.VMEM((GROUP_HEADS * 16, GROUP_WIDTH), F32),
                        pltpu.VMEM((1, LANES), F32)],
        compiler_params=pltpu.CompilerParams(
            dimension_semantics=("arbitrary", "arbitrary"),
            vmem_limit_bytes=VMEM_LIMIT),
        name="inproj",
    )(x, sc, sh, wqm, wkm, wvm, wqf, wkf, wvf, wf, bfor, gqm, gkm, gqf, gkf,
      cos_t, sin_t, rc, rs1, rs2, jmat, place, kones)


def _attn_kernel(q_ref, qa_ref, k_ref, ka_ref, v_ref, o_ref, *, t):
    i = pl.program_id(2)
    q = q_ref[0, 0]
    qa = qa_ref[0, 0]
    row = lax.broadcasted_iota(jnp.int32, (PAIR_WIDTH, t), 0)
    rhs = []
    for e in range(2):
        keep = (row >= e * HEAD_DIM) & (row < (e + 1) * HEAD_DIM)
        zero = jnp.zeros_like(q)
        rhs.append(jnp.concatenate([jnp.where(keep, q, zero), jnp.where(keep, qa, zero)],
                                   axis=0))

    def tile_scores(j):
        start = pl.multiple_of(j * t, t)
        lhs = jnp.concatenate([k_ref[0, pl.ds(start, t), :], ka_ref[0, pl.ds(start, t), :]],
                              axis=1)
        return [_dot(lhs, rhs[e]) for e in range(2)]

    def values(j, e):
        rows = slice(e * HEAD_DIM, (e + 1) * HEAD_DIM)
        slabs = t // KEY_TILE
        return jnp.concatenate([v_ref[0, slabs * j + h, rows, :] for h in range(slabs)],
                               axis=1)

    key_i = lax.broadcasted_iota(jnp.int32, (t, t), 0)
    qry_i = lax.broadcasted_iota(jnp.int32, (t, t), 1)
    causal = key_i <= qry_i
    state = []
    for e, s_t in enumerate(tile_scores(i)):
        s_t = jnp.where(causal, s_t, NEG_BIG)
        m = jnp.max(s_t, axis=0, keepdims=True)
        p = jnp.exp2(s_t - m)
        l = jnp.sum(p, axis=0, keepdims=True)
        acc = _dot(values(i, e), p.astype(BF16))
        state += [m, l, acc]

    def body(j, carry):
        out = []
        for e, s_t in enumerate(tile_scores(j)):
            m, l, acc = carry[3 * e:3 * e + 3]
            m_new = jnp.maximum(m, jnp.max(s_t, axis=0, keepdims=True))
            alpha = jnp.exp2(m - m_new)
            p = jnp.exp2(s_t - m_new)
            l = alpha * l + jnp.sum(p, axis=0, keepdims=True)
            acc = alpha * acc + _dot(values(j, e), p.astype(BF16))
            out += [m_new, l, acc]
        return tuple(out)

    state = lax.fori_loop(0, i, body, tuple(state))
    o_t = jnp.concatenate([state[2] / state[1], state[5] / state[4]], axis=0)
    o_ref[0] = o_t.T


def _attn_bounded_kernel(q_ref, qa_ref, k_ref, ka_ref, v_ref, o_ref, p_sc, acc_sc, l_sc, *, t,
                         ka_const):
    i = pl.program_id(1)
    kt = KEY_TILE
    row = lax.broadcasted_iota(jnp.int32, (PAIR_WIDTH, t), 0)
    key_i = lax.broadcasted_iota(jnp.int32, (kt, t), 0)
    qry_i = lax.broadcasted_iota(jnp.int32, (kt, t), 1)
    assert t == 2 * kt


    def rhs_of(pr):
        feat = slice(pr * PAIR_WIDTH, (pr + 1) * PAIR_WIDTH)
        q = q_ref[0, 0, feat, :]
        qa = qa_ref[0, 0, feat, :]
        out = []
        for e in range(2):
            keep = (row >= e * HEAD_DIM) & (row < (e + 1) * HEAD_DIM)
            zero = jnp.zeros_like(q)
            out.append(jnp.concatenate([jnp.where(keep, q, zero), jnp.where(keep, qa, zero)],
                                       axis=0))
        return out

    for grp in range(N_PAIRS // PAIRS_PER_TRIP):
        prs = list(range(grp * PAIRS_PER_TRIP, (grp + 1) * PAIRS_PER_TRIP))
        rhs = {pr: rhs_of(pr) for pr in prs}

        def probs(pr, start, n_keys, row0, mask=None, q0=0, first=False, rhs=rhs):
            feat = slice(pr * PAIR_WIDTH, (pr + 1) * PAIR_WIDTH)
            k_aug = (ka_ref[0, pl.ds(start, n_keys), :] if ka_const
                     else ka_ref[0, pl.ds(start, n_keys), feat])
            lhs = jnp.concatenate([k_ref[0, pl.ds(start, n_keys), feat], k_aug], axis=1)
            for e in range(2):
                s_t = _dot(lhs, rhs[pr][e][:, q0:])
                if mask is not None:
                    s_t = jnp.where(mask[:, q0:], s_t, NEG_BIG)
                p = jnp.exp2(s_t)
                p_sc[pr, e, row0:row0 + n_keys, q0:] = p.astype(BF16)
                part = p[0:8, :]
                for r in range(8, n_keys, 8):
                    part = part + p[r:r + 8, :]
                if first:
                    l_sc[pr, e] = part
                else:
                    l_sc[pr, e, :, q0:] += part

        def values(pr, j):
            rows = slice(pr * PAIR_WIDTH, (pr + 1) * PAIR_WIDTH)
            v_pair = jnp.concatenate([v_ref[0, 2 * j, rows, :], v_ref[0, 2 * j + 1, rows, :]],
                                     axis=1)
            for e in range(2):
                upd = _dot(v_pair, p_sc[pr, e])
                acc_sc[pr, e] += upd[e * HEAD_DIM:(e + 1) * HEAD_DIM, :]

        diag = pl.multiple_of(i * t, t)
        for pr in prs:
            probs(pr, diag, kt, 0, key_i <= qry_i, first=True)
            probs(pr, diag + kt, kt, kt, key_i + kt <= qry_i, q0=kt)
        for pr in prs:
            rows = slice(pr * PAIR_WIDTH, (pr + 1) * PAIR_WIDTH)
            for e in range(2):
                head = slice(e * HEAD_DIM, (e + 1) * HEAD_DIM)
                lower = _dot(v_ref[0, 2 * i, rows, :], p_sc[pr, e, :kt, :])
                upper = _dot(v_ref[0, 2 * i + 1, rows, :], p_sc[pr, e, kt:, kt:])
                acc_sc[pr, e, :, :kt] = lower[head, :kt]
                acc_sc[pr, e, :, kt:] = lower[head, kt:] + upper[head, :]

        def body(j, probs=probs, values=values, prs=prs):
            for pr in prs:
                probs(pr, pl.multiple_of(j * t, t), t, 0)
            for pr in prs:
                values(pr, j)

        def unrolled(n, body=body):
            def trip(m, carry):
                for u in range(n):
                    body(n * m + u)
                return carry
            return trip

        done = 0
        for n in (4, 2, 1):
            trips = (i - done) // n
            lax.fori_loop(done // n, done // n + trips, unrolled(n), 0)
            done = done + trips * n
        for pr in prs:
            o_t = jnp.concatenate(
                [acc_sc[pr, e] / jnp.sum(l_sc[pr, e], axis=0, keepdims=True)
                 for e in range(2)], axis=0)
            o_ref[0, :, pr * PAIR_WIDTH:(pr + 1) * PAIR_WIDTH] = o_t.T


def _attention_online(q_t, qa, k, ka, v_t, *, ka_const, name):
    bsz, nt, _, t = q_t.shape
    s = k.shape[1]
    grid = (bsz, N_PAIRS, nt)
    if ka_const:
        ka_spec = pl.BlockSpec((1, s, PAIR_WIDTH), lambda b, p, i: (0, 0, 0))
    else:
        ka_spec = pl.BlockSpec((1, s, PAIR_WIDTH), lambda b, p, i: (b, 0, p))
    return pl.pallas_call(
        functools.partial(_attn_kernel, t=t),
        grid=grid,
        in_specs=[pl.BlockSpec((1, 1, PAIR_WIDTH, t), lambda b, p, i: (b, i, p, 0)),
                  pl.BlockSpec((1, 1, PAIR_WIDTH, t), lambda b, p, i: (b, i, p, 0)),
                  pl.BlockSpec((1, s, PAIR_WIDTH), lambda b, p, i: (b, 0, p)),
                  ka_spec,
                  pl.BlockSpec((1, s // KEY_TILE, PAIR_WIDTH, KEY_TILE),
                               lambda b, p, i: (b, 0, p, 0))],
        out_specs=pl.BlockSpec((1, t, PAIR_WIDTH), lambda b, p, i: (b, i, p)),
        out_shape=jax.ShapeDtypeStruct((bsz, s, GROUP_WIDTH), F32),
        compiler_params=pltpu.CompilerParams(
            dimension_semantics=("arbitrary", "arbitrary", "arbitrary"),
            vmem_limit_bytes=VMEM_LIMIT),
        name=name,
    )(q_t, qa, k, ka, v_t)


def _attention_bounded(q_t, qa, k, ka, v_t, *, ka_const, name):
    bsz, nt, _, t = q_t.shape
    s = k.shape[1]
    feat_spec = pl.BlockSpec((1, 1, GROUP_WIDTH, t), lambda b, i: (b, i, 0, 0))
    tok_spec = pl.BlockSpec((1, s, GROUP_WIDTH), lambda b, i: (b, 0, 0))
    ka_spec = pl.BlockSpec((1, s, PAIR_WIDTH), lambda b, i: (0, 0, 0)) if ka_const else tok_spec
    return pl.pallas_call(
        functools.partial(_attn_bounded_kernel, t=t, ka_const=ka_const),
        grid=(bsz, nt),
        in_specs=[feat_spec, feat_spec, tok_spec, ka_spec,
                  pl.BlockSpec((1, s // KEY_TILE, GROUP_WIDTH, KEY_TILE),
                               lambda b, i: (b, 0, 0, 0))],
        out_specs=pl.BlockSpec((1, t, GROUP_WIDTH), lambda b, i: (b, i, 0)),
        out_shape=jax.ShapeDtypeStruct((bsz, s, GROUP_WIDTH), F32),
        scratch_shapes=[pltpu.VMEM((N_PAIRS, 2, t, t), BF16),
                        pltpu.VMEM((N_PAIRS, 2, HEAD_DIM, t), F32),
                        pltpu.VMEM((N_PAIRS, 2, 8, t), F32)],
        compiler_params=pltpu.CompilerParams(
            dimension_semantics=("arbitrary", "arbitrary"),
            vmem_limit_bytes=VMEM_LIMIT),
        name=name,
    )(q_t, qa, k, ka, v_t)


def _attention_dispatch(g_q, g_k, q_t, qa, k, ka, v_t, *, ka_const, name):
    bound = HEAD_DIM * jnp.max(jnp.abs(g_q)) * jnp.max(jnp.abs(g_k))
    return lax.cond(
        bound <= MAX_SAFE_EXPONENT,
        functools.partial(_attention_bounded, ka_const=ka_const, name=name + "_bounded"),
        functools.partial(_attention_online, ka_const=ka_const, name=name + "_online"),
        q_t, qa, k, ka, v_t)


def _post_kernel(x_ref, om_ref, of_ref, ga_ref, scm_ref, shm_ref, gm_ref,
                 gom_ref, gof_ref, wout_ref, w1_ref, w2_ref, o_ref, *, ff_chunk):
    def rms(v):
        return v * lax.rsqrt(jnp.mean(v * v, axis=-1, keepdims=True) + EPS)

    mixed = jnp.concatenate([rms(om_ref[0]) * gom_ref[...],
                             rms(of_ref[0]) * gof_ref[...]], axis=1)
    x1 = x_ref[0] + ga_ref[...] * _dot(mixed, wout_ref[...])
    h = rms(x1) * (1.0 + scm_ref[...]) + shm_ref[...]
    d_ff = w1_ref.shape[1]
    y = None
    for c in range(d_ff // ff_chunk):
        hid = _dot(h, w1_ref[:, c * ff_chunk:(c + 1) * ff_chunk])
        hid = jnp.square(jnp.maximum(hid, 0.0))
        part = _dot(hid, w2_ref[c * ff_chunk:(c + 1) * ff_chunk, :])
        y = part if y is None else y + part
    o_ref[0] = x1 + gm_ref[...] * y


def _post(x, o_m, o_f, mod4, g_out_m, g_out_f, w_out, w_ff1, w_ff2, tm):
    bsz, s, d = x.shape
    d_ff = w_ff1.shape[1]

    def const(a):
        nd = a.ndim
        return pl.BlockSpec(a.shape, lambda b, i, _n=nd: (0,) * _n,
                            pipeline_mode=pl.Buffered(1))

    def mod_spec(k):
        return pl.BlockSpec((None, None, 1, d), lambda b, i, _k=k: (b, _k, 0, 0))

    tok = lambda w: pl.BlockSpec((1, tm, w), lambda b, i: (b, i, 0))
    return pl.pallas_call(
        functools.partial(_post_kernel, ff_chunk=1024),
        grid=(bsz, s // tm),
        in_specs=[tok(d), tok(GROUP_WIDTH), tok(GROUP_WIDTH),
                  mod_spec(2), mod_spec(4), mod_spec(3), mod_spec(5),
                  const(g_out_m), const(g_out_f), const(w_out), const(w_ff1), const(w_ff2)],
        out_specs=tok(d),
        out_shape=jax.ShapeDtypeStruct((bsz, s, d), F32),
        compiler_params=pltpu.CompilerParams(
            dimension_semantics=("arbitrary", "arbitrary"),
            vmem_limit_bytes=POST_VMEM_LIMIT),
        name="post",
    )(x, o_m, o_f, mod4, mod4, mod4, mod4, g_out_m, g_out_f, w_out, w_ff1, w_ff2)


def _rope_tables(s):
    inv_freq = ROPE_THETA ** (-np.arange(0, ROPE_DIM, 2, dtype=np.float64) / ROPE_DIM)
    ang = np.arange(s, dtype=np.float64)[:, None] * inv_freq[None, :]
    cos, sin = np.cos(ang), np.sin(ang)
    rc = np.ones((s, LANES)); rs1 = np.zeros((s, LANES)); rs2 = np.zeros((s, LANES))
    for base in (0, HEAD_DIM):
        rc[:, base:base + ROPE_HALF] = cos
        rc[:, base + ROPE_HALF:base + ROPE_DIM] = cos
        rs2[:, base:base + ROPE_HALF] = -sin
        rs1[:, base + ROPE_HALF:base + ROPE_DIM] = sin
    f = lambda a: jnp.asarray(a, dtype=F32)
    return f(cos.T), f(sin.T), f(rc), f(rs1), f(rs2)


def _static_mats(s):
    idx = np.arange(256)
    jmat = (idx[:, None] // HEAD_DIM == idx[None, :] // HEAD_DIM) / HEAD_DIM
    place = np.zeros((LANES, GROUP_WIDTH))
    for part in range(3):
        for hd in range(GROUP_HEADS):
            place[part * GROUP_HEADS + hd,
                  (hd // 2) * PAIR_WIDTH + (hd % 2) * HEAD_DIM + part] = 1.0
    kones = np.zeros((1, GROUP_WIDTH))
    for hd in range(GROUP_HEADS):
        kones[0, hd * HEAD_DIM + 3:hd * HEAD_DIM + 6] = 1.0
    ka_moba = np.zeros((1, s, PAIR_WIDTH))
    blk = np.arange(s) // MOBA_BLOCK
    for e in range(2):
        ka_moba[0, np.arange(s), e * HEAD_DIM + blk] = 1.0
    b = lambda a: jnp.asarray(a, dtype=BF16)
    return b(jmat), b(place), jnp.asarray(kones, dtype=F32), b(ka_moba)


def kernel(x, c, w_ada, b_ada, w_in, b_forget, g_qn_moba, g_kn_moba, g_qn_fox, g_kn_fox,
           g_out_moba, g_out_fox, w_out, w_ff1, w_ff2):
    bsz, s, d = x.shape
    depth = w_ada.shape[0]
    w = GROUP_WIDTH
    assert s % MOBA_BLOCK == 0 and s // MOBA_BLOCK <= 16
    tm_in = 1024
    tm_post = 512
    cos_t, sin_t, rc, rs1, rs2 = _rope_tables(s)
    jmat, place, kones, ka_moba = _static_mats(s)
    q_scale = HEAD_DIM ** -0.5 * LOG2E

    for l in range(depth):
        mod = _adaln(c, w_ada[l], b_ada[l])
        mod4 = mod.reshape(bsz, 6, 1, d)

        wl = w_in[l]
        wf =jnp.zeros((d, LANES), F32).at[:, :GROUP_HEADS].set(wl[:, 6 * w:]).astype(BF16)
        bfor = jnp.zeros((1, LANES), F32).at[0, :GROUP_HEADS].set(b_forget[l])
        q_gain = lambda g: jnp.broadcast_to((g * q_scale)[:, None], (HEAD_DIM, tm_in))
        k_gain = lambda g: jnp.tile(g, GROUP_HEADS).reshape(1, w)
        wts = _project_weights(w_in, l, w) + (
               wf, bfor,
               q_gain(g_qn_moba[l]), k_gain(g_kn_moba[l]),
               q_gain(g_qn_fox[l]), k_gain(g_kn_fox[l]))
        consts = (cos_t, sin_t, rc, rs1, rs2, jmat, place, kones)

        qm_t, km, vm_t, bias_t, qf_t, kf, vf_t, ka_f, qa_f = _inproj(
            x, mod4[:, 1], mod4[:, 0], wts, consts, tm_in)

        o_m = _attention_dispatch(g_qn_moba[l] * q_scale, g_kn_moba[l],
                                  qm_t, bias_t, km, ka_moba, vm_t, ka_const=True,
                                  name="attn_moba")
        o_f = _attention_dispatch(g_qn_fox[l] * q_scale, g_kn_fox[l],
                                  qf_t, qa_f, kf, ka_f, vf_t, ka_const=False,
                                  name="attn_fox")

        x = _post(x, o_m, o_f, mod4,
                  g_out_moba[l].reshape(1, w), g_out_fox[l].reshape(1, w),
                  w_out[l], w_ff1[l], w_ff2[l],
                  tm_post)
    return x
```

```python
import functools
import math

import numpy as np
import jax
import jax.numpy as jnp
from jax import lax
from jax.experimental import pallas as pl
from jax.experimental.pallas import tpu as pltpu

F32 = jnp.float32
BF16 = jnp.bfloat16

HEAD_DIM = 64
GROUP_HEADS = 8
GROUP_WIDTH = GROUP_HEADS * HEAD_DIM
PAIR_WIDTH = 2 * HEAD_DIM
N_PAIRS = GROUP_HEADS // 2
MOBA_BLOCK = 256
MOBA_TOPK = 3
ROPE_THETA = 500000.0
ROPE_DIM = HEAD_DIM // 4
ROPE_HALF = ROPE_DIM // 2
EPS = 1e-6
LOG2E = math.log2(math.e)
NEG_BIG = -1e30
MAX_SAFE_EXPONENT = 60.0
ATT_TILE = 512
KEY_TILE = 256
CUM_BLOCK = 128
NORM_CHUNKS = 2
ADALN_COLS = 2048
PAIRS_PER_TRIP = 4
LANES = 128
VMEM_LIMIT = 48 * 1024 * 1024
POST_VMEM_LIMIT = 58 * 1024 * 1024


def _dot(a, b):
    return jnp.dot(a, b, preferred_element_type=F32)


def _split3(x):
    hi = x.astype(BF16)
    rem = x - hi.astype(F32)
    mid = rem.astype(BF16)
    lo = (rem - mid.astype(F32)).astype(BF16)
    return hi, mid, lo


def _dot_nt(a, b):
    return lax.dot_general(a, b, (((1,), (1,)), ((), ())), preferred_element_type=F32)


def _adaln_kernel(ct_ref, w_ref, b_ref, o_ref):
    c_t = ct_ref[...]
    s_t = c_t / (1.0 + jnp.exp(-c_t))
    w = w_ref[...]
    rows = [jnp.sum(s_t[:, b:b + 1] * w, axis=0, keepdims=True) for b in range(c_t.shape[1])]
    o_ref[...] = jnp.concatenate(rows, axis=0) + b_ref[...]


def _adaln(c, w_ada, b_ada):
    bsz, d = c.shape
    n = w_ada.shape[1]
    wide = ADALN_COLS
    return pl.pallas_call(
        _adaln_kernel,
        grid=(n // wide,),
        in_specs=[pl.BlockSpec((d, bsz), lambda j: (0, 0)),
                  pl.BlockSpec((d, wide), lambda j: (0, j)),
                  pl.BlockSpec((1, wide), lambda j: (0, j))],
        out_specs=pl.BlockSpec((bsz, wide), lambda j: (0, j)),
        out_shape=jax.ShapeDtypeStruct((bsz, n), F32),
        compiler_params=pltpu.CompilerParams(dimension_semantics=("arbitrary",)),
        name="adaln",
    )(c.T, w_ada, b_ada.reshape(1, n))


TRANSPOSED_GROUPS = (0, 2, 3, 5)


def _wprep_kernel(w_ref, *o_refs):
    g = pl.program_id(0)
    for k, o_ref in enumerate(o_refs):
        @pl.when(g == k)
        def _(k=k, o_ref=o_ref):
            w = w_ref[...]
            o_ref[...] = (w.T if k in TRANSPOSED_GROUPS else w).astype(BF16)


def _project_weights(w_in, layer, width):
    d = w_in.shape[1]
    shapes = [(width, d) if k in TRANSPOSED_GROUPS else (d, width) for k in range(6)]
    return tuple(pl.pallas_call(
        _wprep_kernel,
        grid=(6,),
        in_specs=[pl.BlockSpec((None, d, width), lambda g: (layer, 0, g))],
        out_specs=[pl.BlockSpec(s, lambda g: (0, 0)) for s in shapes],
        out_shape=[jax.ShapeDtypeStruct(s, BF16) for s in shapes],
        compiler_params=pltpu.CompilerParams(dimension_semantics=("arbitrary",)),
        name="wprep",
    )(w_in))


def _inproj_kernel(x_ref, sc_ref, sh_ref,
                   wqm_ref, wkm_ref, wvm_ref, wqf_ref, wkf_ref, wvf_ref, wf_ref, bf_ref,
                   gqm_ref, gkm_ref, gqf_ref, gkf_ref,
                   cos_t_ref, sin_t_ref, rc_ref, rs1_ref, rs2_ref,
                   j_ref, place_ref, kones_ref,
                   qm_o, km_o, vm_o, bias_o, qf_o, kf_o, vf_o, ka_o, qa_o,
                   kmean_sc, carry_sc, *, tm):
    i = pl.program_id(1)
    nb = tm // MOBA_BLOCK

    @pl.when(i == 0)
    def _():
        kmean_sc[...] = jnp.zeros_like(kmean_sc)
        carry_sc[...] = jnp.zeros_like(carry_sc)

    rows_per_chunk = tm // NORM_CHUNKS
    hb_chunks = []
    for ch in range(NORM_CHUNKS):
        x = x_ref[0, ch * rows_per_chunk:(ch + 1) * rows_per_chunk, :]
        ms = jnp.mean(x * x, axis=-1, keepdims=True)
        h = x * lax.rsqrt(ms + EPS) * (1.0 + sc_ref[...]) + sh_ref[...]
        hb_chunks.append(h.astype(BF16))

    def proj(w):
        return jnp.concatenate([_dot(hb, w) for hb in hb_chunks], axis=0)

    def proj_t(w_t):
        return jnp.concatenate([_dot_nt(w_t, hb) for hb in hb_chunks], axis=1)

    def q_feature_major(w_ref, g_ref, rope):
        q_t = proj_t(w_ref[...])
        gain = g_ref[...]
        heads = []
        for hd in range(GROUP_HEADS):
            xh = q_t[hd * HEAD_DIM:(hd + 1) * HEAD_DIM, :]
            ss = jnp.sum(xh * xh, axis=0, keepdims=True)
            xh = xh * lax.rsqrt(ss * (1.0 / HEAD_DIM) + EPS) * gain
            if rope:
                x1 = xh[0:ROPE_HALF, :]
                x2 = xh[ROPE_HALF:ROPE_DIM, :]
                cs = cos_t_ref[...]
                sn = sin_t_ref[...]
                xh = jnp.concatenate(
                    [x1 * cs - x2 * sn, x2 * cs + x1 * sn, xh[ROPE_DIM:, :]], axis=0)
            heads.append(xh)
        return jnp.concatenate(heads, axis=0)

    def k_token_major(w_ref, g_ref, rope):
        k = proj(w_ref[...])
        chunks = []
        for c in range(GROUP_WIDTH // 256):
            seg = k[:, c * 256:(c + 1) * 256]
            msq = _dot((seg * seg).astype(BF16), j_ref[...])
            chunks.append(seg * lax.rsqrt(msq + EPS) * g_ref[:, c * 256:(c + 1) * 256])
        k = jnp.concatenate(chunks, axis=1)
        if rope:
            outs = []
            for c in range(GROUP_WIDTH // LANES):
                seg = k[:, c * LANES:(c + 1) * LANES]
                outs.append(seg * rc_ref[...]
                            + pltpu.roll(seg, ROPE_HALF, 1) * rs1_ref[...]
                            + pltpu.roll(seg, LANES - ROPE_HALF, 1) * rs2_ref[...])
            k = jnp.concatenate(outs, axis=1)
        return k

    def store_tiles(o_ref, val_t, width=ATT_TILE):
        for t in range(tm // width):
            o_ref[0, t] = val_t[:, t * width:(t + 1) * width].astype(o_ref.dtype)

    qm_t = q_feature_major(wqm_ref, gqm_ref, True)
    store_tiles(qm_o, qm_t)
    km = k_token_major(wkm_ref, gkm_ref, True)
    km_o[0] = km.astype(BF16)
    store_tiles(vm_o, proj_t(wvm_ref[...]), KEY_TILE)

    blk0 = i * nb
    lane_head = lax.broadcasted_iota(jnp.int32, (1, GROUP_WIDTH), 1) // HEAD_DIM
    for sb in range(nb):
        kmean = jnp.mean(km[sb * MOBA_BLOCK:(sb + 1) * MOBA_BLOCK, :], axis=0, keepdims=True)
        for hd in range(GROUP_HEADS):
            row = hd * 16 + blk0 + sb
            kmean_sc[pl.ds(row, 1), :] = jnp.where(lane_head == hd, kmean, 0.0)

    km_hi, km_lo, _ = _split3(kmean_sc[...])
    q_hi, q_lo, _ = _split3(qm_t)
    gate_t = _dot(km_hi, q_hi) + _dot(km_hi, q_lo) + _dot(km_lo, q_hi)
    tok_blk = blk0 + lax.broadcasted_iota(jnp.int32, (16, tm), 1) // MOBA_BLOCK
    n_idx = lax.broadcasted_iota(jnp.int32, (16, tm), 0)
    valid = n_idx < tok_blk

    zeros48 = jnp.zeros((HEAD_DIM - 16, tm), F32)
    pair_rows = []
    for hd in range(GROUP_HEADS):
        g = jnp.where(valid, gate_t[hd * 16:(hd + 1) * 16, :], -jnp.inf)
        picked = jnp.zeros((16, tm), jnp.int32)
        for _ in range(MOBA_TOPK):
            best = jnp.max(g, axis=0, keepdims=True)
            first = jnp.min(jnp.where(g == best, n_idx, 16), axis=0, keepdims=True)
            hit = n_idx == first
            picked = jnp.where(hit, 1, picked)
            g = jnp.where(hit, -jnp.inf, g)
        sel = (valid & (picked > 0)) | (n_idx == tok_blk)
        pair_rows += [jnp.where(sel, 0.0, NEG_BIG), zeros48]
    store_tiles(bias_o, jnp.concatenate(pair_rows, axis=0))

    store_tiles(qf_o, q_feature_major(wqf_ref, gqf_ref, False))
    kf_o[0] = k_token_major(wkf_ref, gkf_ref, False).astype(BF16)
    store_tiles(vf_o, proj_t(wvf_ref[...]), KEY_TILE)

    f = proj(wf_ref[...]) + bf_ref[...]
    logf = jnp.minimum(f, 0.0) - jnp.log1p(jnp.exp(-jnp.abs(f)))
    live = lax.broadcasted_iota(jnp.int32, (1, LANES), 1) < GROUP_HEADS
    logf = jnp.where(live, logf, 0.0)
    r_i = lax.broadcasted_iota(jnp.int32, (CUM_BLOCK, CUM_BLOCK), 0)
    c_i = lax.broadcasted_iota(jnp.int32, (CUM_BLOCK, CUM_BLOCK), 1)
    tri = jnp.where(r_i >= c_i, 1.0, 0.0).astype(BF16)
    n_blk = tm // CUM_BLOCK
    wide = jnp.concatenate([logf[b * CUM_BLOCK:(b + 1) * CUM_BLOCK, :] for b in range(n_blk)],
                           axis=1)
    parts = _split3(wide)
    prefix = _dot(tri, parts[0]) + _dot(tri, parts[1]) + _dot(tri, parts[2])
    carry = carry_sc[...]
    cums = []
    for blk in range(n_blk):
        c_blk = prefix[:, blk * LANES:(blk + 1) * LANES] + carry
        carry = c_blk[CUM_BLOCK - 1:CUM_BLOCK, :]
        cums.append(c_blk)
    carry_sc[...] = carry
    nc = jnp.concatenate(cums, axis=0) * (-LOG2E)
    hi, mid, lo = [part.astype(F32) for part in _split3(nc)]
    packed = hi + pltpu.roll(mid, GROUP_HEADS, 1) + pltpu.roll(lo, 2 * GROUP_HEADS, 1)
    ka_o[0] = (_dot(packed.astype(BF16), place_ref[...]) + kones_ref[...]).astype(BF16)

    nc_t = nc.T[0:GROUP_HEADS, :]
    hi_t = nc_t.astype(BF16).astype(F32)
    rem_t = nc_t - hi_t
    mid_t = rem_t.astype(BF16).astype(F32)
    lo_t = (rem_t - mid_t).astype(BF16).astype(F32)
    r8 = lax.broadcasted_iota(jnp.int32, (8, tm), 0)
    zeros56 = jnp.zeros((HEAD_DIM - 8, tm), F32)
    qa_rows = []
    for hd in range(GROUP_HEADS):
        blk = jnp.where(r8 < 3, 1.0,
                        jnp.where(r8 == 3, -hi_t[hd:hd + 1, :],
                                  jnp.where(r8 == 4, -mid_t[hd:hd + 1, :],
                                            jnp.where(r8 == 5, -lo_t[hd:hd + 1, :], 0.0))))
        qa_rows += [blk, zeros56]
    store_tiles(qa_o, jnp.concatenate(qa_rows, axis=0))


def _inproj(x, sc, sh, wts, consts, tm):
    bsz, s, d = x.shape
    nt = s // ATT_TILE
    grid = (bsz, s // tm)
    wqm, wkm, wvm, wqf, wkf, wvf, wf, bfor, gqm, gkm, gqf, gkf = wts
    cos_t, sin_t, rc, rs1, rs2, jmat, place, kones = consts

    def const(a):
        nd = a.ndim
        return pl.BlockSpec(a.shape, lambda b, i, _n=nd: (0,) * _n,
                            pipeline_mode=pl.Buffered(1))

    mod_spec = pl.BlockSpec((None, 1, d), lambda b, i: (b, 0, 0))
    in_specs = [pl.BlockSpec((1, tm, d), lambda b, i: (b, i, 0)), mod_spec, mod_spec]
    in_specs += [const(a) for a in (wqm, wkm, wvm, wqf, wkf, wvf, wf, bfor,
                                    gqm, gkm, gqf, gkf)]
    in_specs += [pl.BlockSpec((ROPE_HALF, tm), lambda b, i: (0, i)),
                 pl.BlockSpec((ROPE_HALF, tm), lambda b, i: (0, i)),
                 pl.BlockSpec((tm, LANES), lambda b, i: (i, 0)),
                 pl.BlockSpec((tm, LANES), lambda b, i: (i, 0)),
                 pl.BlockSpec((tm, LANES), lambda b, i: (i, 0)),
                 const(jmat), const(place), const(kones)]

    tiles = tm // ATT_TILE
    feat_spec = pl.BlockSpec((1, tiles, GROUP_WIDTH, ATT_TILE), lambda b, i: (b, i, 0, 0))
    tok_spec = pl.BlockSpec((1, tm, GROUP_WIDTH), lambda b, i: (b, i, 0))
    feat_shape = jax.ShapeDtypeStruct((bsz, nt, GROUP_WIDTH, ATT_TILE), BF16)
    tok_shape = jax.ShapeDtypeStruct((bsz, s, GROUP_WIDTH), BF16)
    val_spec = pl.BlockSpec((1, tm // KEY_TILE, GROUP_WIDTH, KEY_TILE),
                            lambda b, i: (b, i, 0, 0))
    val_shape = jax.ShapeDtypeStruct((bsz, s // KEY_TILE, GROUP_WIDTH, KEY_TILE), BF16)
    out_specs = [feat_spec, tok_spec, val_spec, feat_spec,
                 feat_spec, tok_spec, val_spec, tok_spec, feat_spec]
    out_shape = [feat_shape, tok_shape, val_shape, feat_shape,
                 feat_shape, tok_shape, val_shape, tok_shape, feat_shape]
    return pl.pallas_call(
        functools.partial(_inproj_kernel, tm=tm),
        grid=grid,
        in_specs=in_specs,
        out_specs=out_specs,
        out_shape=out_shape,
        scratch_shapes=[pltpu.VMEM((GROUP_HEADS * 16, GROUP_WIDTH), F32),
                        pltpu.VMEM((1, LANES), F32)],
        compiler_params=pltpu.CompilerParams(
            dimension_semantics=("arbitrary", "arbitrary"),
            vmem_limit_bytes=VMEM_LIMIT),
        name="inproj",
    )(x, sc, sh, wqm, wkm, wvm, wqf, wkf, wvf, wf, bfor, gqm, gkm, gqf, gkf,
      cos_t, sin_t, rc, rs1, rs2, jmat, place, kones)


def _attn_kernel(q_ref, qa_ref, k_ref, ka_ref, v_ref, o_ref, *, t):
    i = pl.program_id(2)
    q = q_ref[0, 0]
    qa = qa_ref[0, 0]
    row = lax.broadcasted_iota(jnp.int32, (PAIR_WIDTH, t), 0)
    rhs = []
    for e in range(2):
        keep = (row >= e * HEAD_DIM) & (row < (e + 1) * HEAD_DIM)
        zero = jnp.zeros_like(q)
        rhs.append(jnp.concatenate([jnp.where(keep, q, zero), jnp.where(keep, qa, zero)],
                                   axis=0))

    def tile_scores(j):
        start = pl.multiple_of(j * t, t)
        lhs = jnp.concatenate([k_ref[0, pl.ds(start, t), :], ka_ref[0, pl.ds(start, t), :]],
                              axis=1)
        return [_dot(lhs, rhs[e]) for e in range(2)]

    def values(j, e):
        rows = slice(e * HEAD_DIM, (e + 1) * HEAD_DIM)
        slabs = t // KEY_TILE
        return jnp.concatenate([v_ref[0, slabs * j + h, rows, :] for h in range(slabs)],
                               axis=1)

    key_i = lax.broadcasted_iota(jnp.int32, (t, t), 0)
    qry_i = lax.broadcasted_iota(jnp.int32, (t, t), 1)
    causal = key_i <= qry_i
    state = []
    for e, s_t in enumerate(tile_scores(i)):
        s_t = jnp.where(causal, s_t, NEG_BIG)
        m = jnp.max(s_t, axis=0, keepdims=True)
        p = jnp.exp2(s_t - m)
        l = jnp.sum(p, axis=0, keepdims=True)
        acc = _dot(values(i, e), p.astype(BF16))
        state += [m, l, acc]

    def body(j, carry):
        out = []
        for e, s_t in enumerate(tile_scores(j)):
            m, l, acc = carry[3 * e:3 * e + 3]
            m_new = jnp.maximum(m, jnp.max(s_t, axis=0, keepdims=True))
            alpha = jnp.exp2(m - m_new)
            p = jnp.exp2(s_t - m_new)
            l = alpha * l + jnp.sum(p, axis=0, keepdims=True)
            acc = alpha * acc + _dot(values(j, e), p.astype(BF16))
            out += [m_new, l, acc]
        return tuple(out)

    state = lax.fori_loop(0, i, body, tuple(state))
    o_t = jnp.concatenate([state[2] / state[1], state[5] / state[4]], axis=0)
    o_ref[0] = o_t.T


def _attn_bounded_kernel(q_ref, qa_ref, k_ref, ka_ref, v_ref, o_ref, p_sc, acc_sc, l_sc, *, t,
                         ka_const):
    i = pl.program_id(1)
    kt = KEY_TILE
    row = lax.broadcasted_iota(jnp.int32, (PAIR_WIDTH, t), 0)
    key_i = lax.broadcasted_iota(jnp.int32, (kt, t), 0)
    qry_i = lax.broadcasted_iota(jnp.int32, (kt, t), 1)
    assert t == 2 * kt


    def rhs_of(pr):
        feat = slice(pr * PAIR_WIDTH, (pr + 1) * PAIR_WIDTH)
        q = q_ref[0, 0, feat, :]
        qa = qa_ref[0, 0, feat, :]
        out = []
        for e in range(2):
            keep = (row >= e * HEAD_DIM) & (row < (e + 1) * HEAD_DIM)
            zero = jnp.zeros_like(q)
            out.append(jnp.concatenate([jnp.where(keep, q, zero), jnp.where(keep, qa, zero)],
                                       axis=0))
        return out

    for grp in range(N_PAIRS // PAIRS_PER_TRIP):
        prs = list(range(grp * PAIRS_PER_TRIP, (grp + 1) * PAIRS_PER_TRIP))
        rhs = {pr: rhs_of(pr) for pr in prs}

        def probs(pr, start, n_keys, row0, mask=None, q0=0, first=False, rhs=rhs):
            feat = slice(pr * PAIR_WIDTH, (pr + 1) * PAIR_WIDTH)
            k_aug = (ka_ref[0, pl.ds(start, n_keys), :] if ka_const
                     else ka_ref[0, pl.ds(start, n_keys), feat])
            lhs = jnp.concatenate([k_ref[0, pl.ds(start, n_keys), feat], k_aug], axis=1)
            for e in range(2):
                s_t = _dot(lhs, rhs[pr][e][:, q0:])
                if mask is not None:
                    s_t = jnp.where(mask[:, q0:], s_t, NEG_BIG)
                p = jnp.exp2(s_t)
                p_sc[pr, e, row0:row0 + n_keys, q0:] = p.astype(BF16)
                part = p[0:8, :]
                for r in range(8, n_keys, 8):
                    part = part + p[r:r + 8, :]
                if first:
                    l_sc[pr, e] = part
                else:
                    l_sc[pr, e, :, q0:] += part

        def values(pr, j):
            rows = slice(pr * PAIR_WIDTH, (pr + 1) * PAIR_WIDTH)
            v_pair = jnp.concatenate([v_ref[0, 2 * j, rows, :], v_ref[0, 2 * j + 1, rows, :]],
                                     axis=1)
            for e in range(2):
                upd = _dot(v_pair, p_sc[pr, e])
                acc_sc[pr, e] += upd[e * HEAD_DIM:(e + 1) * HEAD_DIM, :]

        diag = pl.multiple_of(i * t, t)
        for pr in prs:
            probs(pr, diag, kt, 0, key_i <= qry_i, first=True)
            probs(pr, diag + kt, kt, kt, key_i + kt <= qry_i, q0=kt)
        for pr in prs:
            rows = slice(pr * PAIR_WIDTH, (pr + 1) * PAIR_WIDTH)
            for e in range(2):
                head = slice(e * HEAD_DIM, (e + 1) * HEAD_DIM)
                lower = _dot(v_ref[0, 2 * i, rows, :], p_sc[pr, e, :kt, :])
                upper = _dot(v_ref[0, 2 * i + 1, rows, :], p_sc[pr, e, kt:, kt:])
                acc_sc[pr, e, :, :kt] = lower[head, :kt]
                acc_sc[pr, e, :, kt:] = lower[head, kt:] + upper[head, :]

        def body(j, probs=probs, values=values, prs=prs):
            for pr in prs:
                probs(pr, pl.multiple_of(j * t, t), t, 0)
            for pr in prs:
                values(pr, j)

        def unrolled(n, body=body):
            def trip(m, carry):
                for u in range(n):
                    body(n * m + u)
                return carry
            return trip

        done = 0
        for n in (4, 2, 1):
            trips = (i - done) // n
            lax.fori_loop(done // n, done // n + trips, unrolled(n), 0)
            done = done + trips * n
        for pr in prs:
            o_t = jnp.concatenate(
                [acc_sc[pr, e] / jnp.sum(l_sc[pr, e], axis=0, keepdims=True)
                 for e in range(2)], axis=0)
            o_ref[0, :, pr * PAIR_WIDTH:(pr + 1) * PAIR_WIDTH] = o_t.T


def _attention_online(q_t, qa, k, ka, v_t, *, ka_const, name):
    bsz, nt, _, t = q_t.shape
    s = k.shape[1]
    grid = (bsz, N_PAIRS, nt)
    if ka_const:
        ka_spec = pl.BlockSpec((1, s, PAIR_WIDTH), lambda b, p, i: (0, 0, 0))
    else:
        ka_spec = pl.BlockSpec((1, s, PAIR_WIDTH), lambda b, p, i: (b, 0, p))
    return pl.pallas_call(
        functools.partial(_attn_kernel, t=t),
        grid=grid,
        in_specs=[pl.BlockSpec((1, 1, PAIR_WIDTH, t), lambda b, p, i: (b, i, p, 0)),
                  pl.BlockSpec((1, 1, PAIR_WIDTH, t), lambda b, p, i: (b, i, p, 0)),
                  pl.BlockSpec((1, s, PAIR_WIDTH), lambda b, p, i: (b, 0, p)),
                  ka_spec,
                  pl.BlockSpec((1, s // KEY_TILE, PAIR_WIDTH, KEY_TILE),
                               lambda b, p, i: (b, 0, p, 0))],
        out_specs=pl.BlockSpec((1, t, PAIR_WIDTH), lambda b, p, i: (b, i, p)),
        out_shape=jax.ShapeDtypeStruct((bsz, s, GROUP_WIDTH), F32),
        compiler_params=pltpu.CompilerParams(
            dimension_semantics=("arbitrary", "arbitrary", "arbitrary"),
            vmem_limit_bytes=VMEM_LIMIT),
        name=name,
    )(q_t, qa, k, ka, v_t)


def _attention_bounded(q_t, qa, k, ka, v_t, *, ka_const, name):
    bsz, nt, _, t = q_t.shape
    s = k.shape[1]
    feat_spec = pl.BlockSpec((1, 1, GROUP_WIDTH, t), lambda b, i: (b, i, 0, 0))
    tok_spec = pl.BlockSpec((1, s, GROUP_WIDTH), lambda b, i: (b, 0, 0))
    ka_spec = pl.BlockSpec((1, s, PAIR_WIDTH), lambda b, i: (0, 0, 0)) if ka_const else tok_spec
    return pl.pallas_call(
        functools.partial(_attn_bounded_kernel, t=t, ka_const=ka_const),
        grid=(bsz, nt),
        in_specs=[feat_spec, feat_spec, tok_spec, ka_spec,
                  pl.BlockSpec((1, s // KEY_TILE, GROUP_WIDTH, KEY_TILE),
                               lambda b, i: (b, 0, 0, 0))],
        out_specs=pl.BlockSpec((1, t, GROUP_WIDTH), lambda b, i: (b, i, 0)),
        out_shape=jax.ShapeDtypeStruct((bsz, s, GROUP_WIDTH), F32),
        scratch_shapes=[pltpu.VMEM((N_PAIRS, 2, t, t), BF16),
                        pltpu.VMEM((N_PAIRS, 2, HEAD_DIM, t), F32),
                        pltpu.VMEM((N_PAIRS, 2, 8, t), F32)],
        compiler_params=pltpu.CompilerParams(
            dimension_semantics=("arbitrary", "arbitrary"),
            vmem_limit_bytes=VMEM_LIMIT),
        name=name,
    )(q_t, qa, k, ka, v_t)


def _attention_dispatch(g_q, g_k, q_t, qa, k, ka, v_t, *, ka_const, name):
    bound = HEAD_DIM * jnp.max(jnp.abs(g_q)) * jnp.max(jnp.abs(g_k))
    return lax.cond(
        bound <= MAX_SAFE_EXPONENT,
        functools.partial(_attention_bounded, ka_const=ka_const, name=name + "_bounded"),
        functools.partial(_attention_online, ka_const=ka_const, name=name + "_online"),
        q_t, qa, k, ka, v_t)


def _post_kernel(x_ref, om_ref, of_ref, ga_ref, scm_ref, shm_ref, gm_ref,
                 gom_ref, gof_ref, wout_ref, w1_ref, w2_ref, o_ref, *, ff_chunk):
    def rms(v):
        return v * lax.rsqrt(jnp.mean(v * v, axis=-1, keepdims=True) + EPS)

    mixed = jnp.concatenate([rms(om_ref[0]) * gom_ref[...],
                             rms(of_ref[0]) * gof_ref[...]], axis=1)
    x1 = x_ref[0] + ga_ref[...] * _dot(mixed, wout_ref[...])
    h = rms(x1) * (1.0 + scm_ref[...]) + shm_ref[...]
    d_ff = w1_ref.shape[1]
    y = None
    for c in range(d_ff // ff_chunk):
        hid = _dot(h, w1_ref[:, c * ff_chunk:(c + 1) * ff_chunk])
        hid = jnp.square(jnp.maximum(hid, 0.0))
        part = _dot(hid, w2_ref[c * ff_chunk:(c + 1) * ff_chunk, :])
        y = part if y is None else y + part
    o_ref[0] = x1 + gm_ref[...] * y


def _post(x, o_m, o_f, mod4, g_out_m, g_out_f, w_out, w_ff1, w_ff2, tm):
    bsz, s, d = x.shape
    d_ff = w_ff1.shape[1]

    def const(a):
        nd = a.ndim
        return pl.BlockSpec(a.shape, lambda b, i, _n=nd: (0,) * _n,
                            pipeline_mode=pl.Buffered(1))

    def mod_spec(k):
        return pl.BlockSpec((None, None, 1, d), lambda b, i, _k=k: (b, _k, 0, 0))

    tok = lambda w: pl.BlockSpec((1, tm, w), lambda b, i: (b, i, 0))
    return pl.pallas_call(
        functools.partial(_post_kernel, ff_chunk=1024),
        grid=(bsz, s // tm),
        in_specs=[tok(d), tok(GROUP_WIDTH), tok(GROUP_WIDTH),
                  mod_spec(2), mod_spec(4), mod_spec(3), mod_spec(5),
                  const(g_out_m), const(g_out_f), const(w_out), const(w_ff1), const(w_ff2)],
        out_specs=tok(d),
        out_shape=jax.ShapeDtypeStruct((bsz, s, d), F32),
        compiler_params=pltpu.CompilerParams(
            dimension_semantics=("arbitrary", "arbitrary"),
            vmem_limit_bytes=POST_VMEM_LIMIT),
        name="post",
    )(x, o_m, o_f, mod4, mod4, mod4, mod4, g_out_m, g_out_f, w_out, w_ff1, w_ff2)


def _rope_tables(s):
    inv_freq = ROPE_THETA ** (-np.arange(0, ROPE_DIM, 2, dtype=np.float64) / ROPE_DIM)
    ang = np.arange(s, dtype=np.float64)[:, None] * inv_freq[None, :]
    cos, sin = np.cos(ang), np.sin(ang)
    rc = np.ones((s, LANES)); rs1 = np.zeros((s, LANES)); rs2 = np.zeros((s, LANES))
    for base in (0, HEAD_DIM):
        rc[:, base:base + ROPE_HALF] = cos
        rc[:, base + ROPE_HALF:base + ROPE_DIM] = cos
        rs2[:, base:base + ROPE_HALF] = -sin
        rs1[:, base + ROPE_HALF:base + ROPE_DIM] = sin
    f = lambda a: jnp.asarray(a, dtype=F32)
    return f(cos.T), f(sin.T), f(rc), f(rs1), f(rs2)


def _static_mats(s):
    idx = np.arange(256)
    jmat = (idx[:, None] // HEAD_DIM == idx[None, :] // HEAD_DIM) / HEAD_DIM
    place = np.zeros((LANES, GROUP_WIDTH))
    for part in range(3):
        for hd in range(GROUP_HEADS):
            place[part * GROUP_HEADS + hd,
                  (hd // 2) * PAIR_WIDTH + (hd % 2) * HEAD_DIM + part] = 1.0
    kones = np.zeros((1, GROUP_WIDTH))
    for hd in range(GROUP_HEADS):
        kones[0, hd * HEAD_DIM + 3:hd * HEAD_DIM + 6] = 1.0
    ka_moba = np.zeros((1, s, PAIR_WIDTH))
    blk = np.arange(s) // MOBA_BLOCK
    for e in range(2):
        ka_moba[0, np.arange(s), e * HEAD_DIM + blk] = 1.0
    b = lambda a: jnp.asarray(a, dtype=BF16)
    return b(jmat), b(place), jnp.asarray(kones, dtype=F32), b(ka_moba)


def kernel(x, c, w_ada, b_ada, w_in, b_forget, g_qn_moba, g_kn_moba, g_qn_fox, g_kn_fox,
           g_out_moba, g_out_fox, w_out, w_ff1, w_ff2):
    bsz, s, d = x.shape
    depth = w_ada.shape[0]
    w = GROUP_WIDTH
    assert s % MOBA_BLOCK == 0 and s // MOBA_BLOCK <= 16
    tm_in = 1024
    tm_post = 512
    cos_t, sin_t, rc, rs1, rs2 = _rope_tables(s)
    jmat, place, kones, ka_moba = _static_mats(s)
    q_scale = HEAD_DIM ** -0.5 * LOG2E

    for l in range(depth):
        mod = _adaln(c, w_ada[l], b_ada[l])
        mod4 = mod.reshape(bsz, 6, 1, d)

        wl = w_in[l]
        wf =jnp.zeros((d, LANES), F32).at[:, :GROUP_HEADS].set(wl[:, 6 * w:]).astype(BF16)
        bfor = jnp.zeros((1, LANES), F32).at[0, :GROUP_HEADS].set(b_forget[l])
        q_gain = lambda g: jnp.broadcast_to((g * q_scale)[:, None], (HEAD_DIM, tm_in))
        k_gain = lambda g: jnp.tile(g, GROUP_HEADS).reshape(1, w)
        wts = _project_weights(w_in, l, w) + (
               wf, bfor,
               q_gain(g_qn_moba[l]), k_gain(g_kn_moba[l]),
               q_gain(g_qn_fox[l]), k_gain(g_kn_fox[l]))
        consts = (cos_t, sin_t, rc, rs1, rs2, jmat, place, kones)

        qm_t, km, vm_t, bias_t, qf_t, kf, vf_t, ka_f, qa_f = _inproj(
            x, mod4[:, 1], mod4[:, 0], wts, consts, tm_in)

        o_m = _attention_dispatch(g_qn_moba[l] * q_scale, g_kn_moba[l],
                                  qm_t, bias_t, km, ka_moba, vm_t, ka_const=True,
                                  name="attn_moba")
        o_f = _attention_dispatch(g_qn_fox[l] * q_scale, g_kn_fox[l],
                                  qf_t, qa_f, kf, ka_f, vf_t, ka_const=False,
                                  name="attn_fox")

        x = _post(x, o_m, o_f, mod4,
                  g_out_moba[l].reshape(1, w), g_out_fox[l].reshape(1, w),
                  w_out[l], w_ff1[l], w_ff2[l],
                  tm_post)
    return x
```

```python
import functools
import math

import numpy as np
import jax
import jax.numpy as jnp
from jax import lax
from jax.experimental import pallas as pl
from jax.experimental.pallas import tpu as pltpu

F32 = jnp.float32
BF16 = jnp.bfloat16

HEAD_DIM = 64
GROUP_HEADS = 8
GROUP_WIDTH = GROUP_HEADS * HEAD_DIM
PAIR_WIDTH = 2 * HEAD_DIM
N_PAIRS = GROUP_HEADS // 2
MOBA_BLOCK = 256
MOBA_TOPK = 3
ROPE_THETA = 500000.0
ROPE_DIM = HEAD_DIM // 4
ROPE_HALF = ROPE_DIM // 2
EPS = 1e-6
LOG2E = math.log2(math.e)
NEG_BIG = -1e30
MAX_SAFE_EXPONENT = 60.0
ATT_TILE = 512
KEY_TILE = 256
CUM_BLOCK = 128
NORM_CHUNKS = 2
PAIRS_PER_TRIP = 4
HEAD_TILES = 3
LANES = 128
VMEM_LIMIT = 48 * 1024 * 1024
POST_VMEM_LIMIT = 58 * 1024 * 1024


def _dot(a, b):
    return jnp.dot(a, b, preferred_element_type=F32)


def _split3(x):
    hi = x.astype(BF16)
    rem = x - hi.astype(F32)
    mid = rem.astype(BF16)
    lo = (rem - mid.astype(F32)).astype(BF16)
    return hi, mid, lo


def _dot_nt(a, b):
    return lax.dot_general(a, b, (((1,), (1,)), ((), ())), preferred_element_type=F32)


def _adaln_kernel(ct_ref, w_ref, b_ref, o_ref):
    c_t = ct_ref[...]
    s_t = c_t / (1.0 + jnp.exp(-c_t))
    w = w_ref[...]
    rows = [jnp.sum(s_t[:, b:b + 1] * w, axis=0, keepdims=True) for b in range(c_t.shape[1])]
    o_ref[...] = jnp.concatenate(rows, axis=0) + b_ref[...]


def _adaln(c, w_ada, b_ada):
    bsz, d = c.shape
    n = w_ada.shape[1]
    return pl.pallas_call(
        _adaln_kernel,
        grid=(n // d,),
        in_specs=[pl.BlockSpec((d, bsz), lambda j: (0, 0)),
                  pl.BlockSpec((d, d), lambda j: (0, j)),
                  pl.BlockSpec((1, d), lambda j: (0, j))],
        out_specs=pl.BlockSpec((bsz, d), lambda j: (0, j)),
        out_shape=jax.ShapeDtypeStruct((bsz, n), F32),
        compiler_params=pltpu.CompilerParams(dimension_semantics=("arbitrary",)),
        name="adaln",
    )(c.T, w_ada, b_ada.reshape(1, n))


TRANSPOSED_GROUPS = (0, 2, 3, 5)


def _wprep_kernel(w_ref, *o_refs):
    g = pl.program_id(0)
    for k, o_ref in enumerate(o_refs):
        @pl.when(g == k)
        def _(k=k, o_ref=o_ref):
            w = w_ref[...]
            o_ref[...] = (w.T if k in TRANSPOSED_GROUPS else w).astype(BF16)


def _project_weights(w_in, layer, width):
    d = w_in.shape[1]
    shapes = [(width, d) if k in TRANSPOSED_GROUPS else (d, width) for k in range(6)]
    return tuple(pl.pallas_call(
        _wprep_kernel,
        grid=(6,),
        in_specs=[pl.BlockSpec((None, d, width), lambda g: (layer, 0, g))],
        out_specs=[pl.BlockSpec(s, lambda g: (0, 0)) for s in shapes],
        out_shape=[jax.ShapeDtypeStruct(s, BF16) for s in shapes],
        compiler_params=pltpu.CompilerParams(dimension_semantics=("arbitrary",)),
        name="wprep",
    )(w_in))


def _inproj_kernel(x_ref, sc_ref, sh_ref,
                   wqm_ref, wkm_ref, wvm_ref, wqf_ref, wkf_ref, wvf_ref, wf_ref, bf_ref,
                   gqm_ref, gkm_ref, gqf_ref, gkf_ref,
                   cos_t_ref, sin_t_ref, rc_ref, rs1_ref, rs2_ref,
                   j_ref, place_ref, kones_ref,
                   qm_o, km_o, vm_o, bias_o, qf_o, kf_o, vf_o, ka_o, qa_o,
                   kmean_sc, carry_sc, *, tm):
    i = pl.program_id(1)
    nb = tm // MOBA_BLOCK

    @pl.when(i == 0)
    def _():
        kmean_sc[...] = jnp.zeros_like(kmean_sc)
        carry_sc[...] = jnp.zeros_like(carry_sc)

    rows_per_chunk = tm // NORM_CHUNKS
    hb_chunks = []
    for ch in range(NORM_CHUNKS):
        x = x_ref[0, ch * rows_per_chunk:(ch + 1) * rows_per_chunk, :]
        ms = jnp.mean(x * x, axis=-1, keepdims=True)
        h = x * lax.rsqrt(ms + EPS) * (1.0 + sc_ref[...]) + sh_ref[...]
        hb_chunks.append(h.astype(BF16))

    def proj(w):
        return jnp.concatenate([_dot(hb, w) for hb in hb_chunks], axis=0)

    def proj_t(w_t):
        return jnp.concatenate([_dot_nt(w_t, hb) for hb in hb_chunks], axis=1)

    def q_feature_major(w_ref, g_ref, rope):
        q_t = proj_t(w_ref[...])
        gain = g_ref[...]
        heads = []
        for hd in range(GROUP_HEADS):
            xh = q_t[hd * HEAD_DIM:(hd + 1) * HEAD_DIM, :]
            ss = jnp.sum(xh * xh, axis=0, keepdims=True)
            xh = xh * lax.rsqrt(ss * (1.0 / HEAD_DIM) + EPS) * gain
            if rope:
                x1 = xh[0:ROPE_HALF, :]
                x2 = xh[ROPE_HALF:ROPE_DIM, :]
                cs = cos_t_ref[...]
                sn = sin_t_ref[...]
                xh = jnp.concatenate(
                    [x1 * cs - x2 * sn, x2 * cs + x1 * sn, xh[ROPE_DIM:, :]], axis=0)
            heads.append(xh)
        return jnp.concatenate(heads, axis=0)

    def k_token_major(w_ref, g_ref, rope):
        k = proj(w_ref[...])
        chunks = []
        for c in range(GROUP_WIDTH // 256):
            seg = k[:, c * 256:(c + 1) * 256]
            msq = _dot((seg * seg).astype(BF16), j_ref[...])
            chunks.append(seg * lax.rsqrt(msq + EPS) * g_ref[:, c * 256:(c + 1) * 256])
        k = jnp.concatenate(chunks, axis=1)
        if rope:
            outs = []
            for c in range(GROUP_WIDTH // LANES):
                seg = k[:, c * LANES:(c + 1) * LANES]
                outs.append(seg * rc_ref[...]
                            + pltpu.roll(seg, ROPE_HALF, 1) * rs1_ref[...]
                            + pltpu.roll(seg, LANES - ROPE_HALF, 1) * rs2_ref[...])
            k = jnp.concatenate(outs, axis=1)
        return k

    def store_tiles(o_ref, val_t, width=ATT_TILE):
        for t in range(tm // width):
            o_ref[0, t] = val_t[:, t * width:(t + 1) * width].astype(o_ref.dtype)

    qm_t = q_feature_major(wqm_ref, gqm_ref, True)
    store_tiles(qm_o, qm_t)
    km = k_token_major(wkm_ref, gkm_ref, True)
    km_o[0] = km.astype(BF16)
    store_tiles(vm_o, proj_t(wvm_ref[...]), KEY_TILE)

    blk0 = i * nb
    lane_head = lax.broadcasted_iota(jnp.int32, (1, GROUP_WIDTH), 1) // HEAD_DIM
    for sb in range(nb):
        kmean = jnp.mean(km[sb * MOBA_BLOCK:(sb + 1) * MOBA_BLOCK, :], axis=0, keepdims=True)
        for hd in range(GROUP_HEADS):
            row = hd * 16 + blk0 + sb
            kmean_sc[pl.ds(row, 1), :] = jnp.where(lane_head == hd, kmean, 0.0)

    km_hi, km_lo, _ = _split3(kmean_sc[...])
    q_hi, q_lo, _ = _split3(qm_t)
    gate_t = _dot(km_hi, q_hi) + _dot(km_hi, q_lo) + _dot(km_lo, q_hi)
    tok_blk = blk0 + lax.broadcasted_iota(jnp.int32, (16, tm), 1) // MOBA_BLOCK
    n_idx = lax.broadcasted_iota(jnp.int32, (16, tm), 0)
    valid = n_idx < tok_blk

    zeros48 = jnp.zeros((HEAD_DIM - 16, tm), F32)
    pair_rows = []
    for hd in range(GROUP_HEADS):
        g = jnp.where(valid, gate_t[hd * 16:(hd + 1) * 16, :], -jnp.inf)
        picked = jnp.zeros((16, tm), jnp.int32)
        for _ in range(MOBA_TOPK):
            best = jnp.max(g, axis=0, keepdims=True)
            first = jnp.min(jnp.where(g == best, n_idx, 16), axis=0, keepdims=True)
            hit = n_idx == first
            picked = jnp.where(hit, 1, picked)
            g = jnp.where(hit, -jnp.inf, g)
        sel = (valid & (picked > 0)) | (n_idx == tok_blk)
        pair_rows += [jnp.where(sel, 0.0, NEG_BIG), zeros48]
    store_tiles(bias_o, jnp.concatenate(pair_rows, axis=0))

    store_tiles(qf_o, q_feature_major(wqf_ref, gqf_ref, False))
    kf_o[0] = k_token_major(wkf_ref, gkf_ref, False).astype(BF16)
    store_tiles(vf_o, proj_t(wvf_ref[...]), KEY_TILE)

    f = proj(wf_ref[...]) + bf_ref[...]
    logf = jnp.minimum(f, 0.0) - jnp.log1p(jnp.exp(-jnp.abs(f)))
    live = lax.broadcasted_iota(jnp.int32, (1, LANES), 1) < GROUP_HEADS
    logf = jnp.where(live, logf, 0.0)
    r_i = lax.broadcasted_iota(jnp.int32, (CUM_BLOCK, CUM_BLOCK), 0)
    c_i = lax.broadcasted_iota(jnp.int32, (CUM_BLOCK, CUM_BLOCK), 1)
    tri = jnp.where(r_i >= c_i, 1.0, 0.0).astype(BF16)
    n_blk = tm // CUM_BLOCK
    wide = jnp.concatenate([logf[b * CUM_BLOCK:(b + 1) * CUM_BLOCK, :] for b in range(n_blk)],
                           axis=1)
    parts = _split3(wide)
    prefix = _dot(tri, parts[0]) + _dot(tri, parts[1]) + _dot(tri, parts[2])
    carry = carry_sc[...]
    cums = []
    for blk in range(n_blk):
        c_blk = prefix[:, blk * LANES:(blk + 1) * LANES] + carry
        carry = c_blk[CUM_BLOCK - 1:CUM_BLOCK, :]
        cums.append(c_blk)
    carry_sc[...] = carry
    nc = jnp.concatenate(cums, axis=0) * (-LOG2E)
    hi, mid, lo = [part.astype(F32) for part in _split3(nc)]
    packed = hi + pltpu.roll(mid, GROUP_HEADS, 1) + pltpu.roll(lo, 2 * GROUP_HEADS, 1)
    ka_o[0] = (_dot(packed.astype(BF16), place_ref[...]) + kones_ref[...]).astype(BF16)

    nc_t = nc.T[0:GROUP_HEADS, :]
    hi_t = nc_t.astype(BF16).astype(F32)
    rem_t = nc_t - hi_t
    mid_t = rem_t.astype(BF16).astype(F32)
    lo_t = (rem_t - mid_t).astype(BF16).astype(F32)
    r8 = lax.broadcasted_iota(jnp.int32, (8, tm), 0)
    zeros56 = jnp.zeros((HEAD_DIM - 8, tm), F32)
    qa_rows = []
    for hd in range(GROUP_HEADS):
        blk = jnp.where(r8 < 3, 1.0,
                        jnp.where(r8 == 3, -hi_t[hd:hd + 1, :],
                                  jnp.where(r8 == 4, -mid_t[hd:hd + 1, :],
                                            jnp.where(r8 == 5, -lo_t[hd:hd + 1, :], 0.0))))
        qa_rows += [blk, zeros56]
    store_tiles(qa_o, jnp.concatenate(qa_rows, axis=0))


def _inproj(x, sc, sh, wts, consts, tm):
    bsz, s, d = x.shape
    nt = s // ATT_TILE
    grid = (bsz, s // tm)
    wqm, wkm, wvm, wqf, wkf, wvf, wf, bfor, gqm, gkm, gqf, gkf = wts
    cos_t, sin_t, rc, rs1, rs2, jmat, place, kones = consts

    def const(a):
        nd = a.ndim
        return pl.BlockSpec(a.shape, lambda b, i, _n=nd: (0,) * _n,
                            pipeline_mode=pl.Buffered(1))

    mod_spec = pl.BlockSpec((None, 1, d), lambda b, i: (b, 0, 0))
    in_specs = [pl.BlockSpec((1, tm, d), lambda b, i: (b, i, 0)), mod_spec, mod_spec]
    in_specs += [const(a) for a in (wqm, wkm, wvm, wqf, wkf, wvf, wf, bfor,
                                    gqm, gkm, gqf, gkf)]
    in_specs += [pl.BlockSpec((ROPE_HALF, tm), lambda b, i: (0, i)),
                 pl.BlockSpec((ROPE_HALF, tm), lambda b, i: (0, i)),
                 pl.BlockSpec((tm, LANES), lambda b, i: (i, 0)),
                 pl.BlockSpec((tm, LANES), lambda b, i: (i, 0)),
                 pl.BlockSpec((tm, LANES), lambda b, i: (i, 0)),
                 const(jmat), const(place), const(kones)]

    tiles = tm // ATT_TILE
    feat_spec = pl.BlockSpec((1, tiles, GROUP_WIDTH, ATT_TILE), lambda b, i: (b, i, 0, 0))
    tok_spec = pl.BlockSpec((1, tm, GROUP_WIDTH), lambda b, i: (b, i, 0))
    feat_shape = jax.ShapeDtypeStruct((bsz, nt, GROUP_WIDTH, ATT_TILE), BF16)
    tok_shape = jax.ShapeDtypeStruct((bsz, s, GROUP_WIDTH), BF16)
    val_spec = pl.BlockSpec((1, tm // KEY_TILE, GROUP_WIDTH, KEY_TILE),
                            lambda b, i: (b, i, 0, 0))
    val_shape = jax.ShapeDtypeStruct((bsz, s // KEY_TILE, GROUP_WIDTH, KEY_TILE), BF16)
    out_specs = [feat_spec, tok_spec, val_spec, feat_spec,
                 feat_spec, tok_spec, val_spec, tok_spec, feat_spec]
    out_shape = [feat_shape, tok_shape, val_shape, feat_shape,
                 feat_shape, tok_shape, val_shape, tok_shape, feat_shape]
    return pl.pallas_call(
        functools.partial(_inproj_kernel, tm=tm),
        grid=grid,
        in_specs=in_specs,
        out_specs=out_specs,
        out_shape=out_shape,
        scratch_shapes=[pltpu.VMEM((GROUP_HEADS * 16, GROUP_WIDTH), F32),
                        pltpu.VMEM((1, LANES), F32)],
        compiler_params=pltpu.CompilerParams(
            dimension_semantics=("arbitrary", "arbitrary"),
            vmem_limit_bytes=VMEM_LIMIT),
        name="inproj",
    )(x, sc, sh, wqm, wkm, wvm, wqf, wkf, wvf, wf, bfor, gqm, gkm, gqf, gkf,
      cos_t, sin_t, rc, rs1, rs2, jmat, place, kones)


def _attn_kernel(q_ref, qa_ref, k_ref, ka_ref, v_ref, o_ref, *, t):
    i = pl.program_id(2)
    q = q_ref[0, 0]
    qa = qa_ref[0, 0]
    row = lax.broadcasted_iota(jnp.int32, (PAIR_WIDTH, t), 0)
    rhs = []
    for e in range(2):
        keep = (row >= e * HEAD_DIM) & (row < (e + 1) * HEAD_DIM)
        zero = jnp.zeros_like(q)
        rhs.append(jnp.concatenate([jnp.where(keep, q, zero), jnp.where(keep, qa, zero)],
                                   axis=0))

    def tile_scores(j):
        start = pl.multiple_of(j * t, t)
        lhs = jnp.concatenate([k_ref[0, pl.ds(start, t), :], ka_ref[0, pl.ds(start, t), :]],
                              axis=1)
        return [_dot(lhs, rhs[e]) for e in range(2)]

    def values(j, e):
        rows = slice(e * HEAD_DIM, (e + 1) * HEAD_DIM)
        slabs = t // KEY_TILE
        return jnp.concatenate([v_ref[0, slabs * j + h, rows, :] for h in range(slabs)],
                               axis=1)

    key_i = lax.broadcasted_iota(jnp.int32, (t, t), 0)
    qry_i = lax.broadcasted_iota(jnp.int32, (t, t), 1)
    causal = key_i <= qry_i
    state = []
    for e, s_t in enumerate(tile_scores(i)):
        s_t = jnp.where(causal, s_t, NEG_BIG)
        m = jnp.max(s_t, axis=0, keepdims=True)
        p = jnp.exp2(s_t - m)
        l = jnp.sum(p, axis=0, keepdims=True)
        acc = _dot(values(i, e), p.astype(BF16))
        state += [m, l, acc]

    def body(j, carry):
        out = []
        for e, s_t in enumerate(tile_scores(j)):
            m, l, acc = carry[3 * e:3 * e + 3]
            m_new = jnp.maximum(m, jnp.max(s_t, axis=0, keepdims=True))
            alpha = jnp.exp2(m - m_new)
            p = jnp.exp2(s_t - m_new)
            l = alpha * l + jnp.sum(p, axis=0, keepdims=True)
            acc = alpha * acc + _dot(values(j, e), p.astype(BF16))
            out += [m_new, l, acc]
        return tuple(out)

    state = lax.fori_loop(0, i, body, tuple(state))
    o_t = jnp.concatenate([state[2] / state[1], state[5] / state[4]], axis=0)
    o_ref[0] = o_t.T


def _attn_bounded_kernel(q_ref, qa_ref, k_ref, ka_ref, v_ref, o_ref, p_sc, acc_sc, l_sc, *, t,
                         ka_const):
    i = pl.program_id(1)
    kt = KEY_TILE
    row = lax.broadcasted_iota(jnp.int32, (PAIR_WIDTH, t), 0)
    key_i = lax.broadcasted_iota(jnp.int32, (kt, t), 0)
    qry_i = lax.broadcasted_iota(jnp.int32, (kt, t), 1)
    assert t == 2 * kt


    def rhs_of(pr):
        feat = slice(pr * PAIR_WIDTH, (pr + 1) * PAIR_WIDTH)
        q = q_ref[0, 0, feat, :]
        qa = qa_ref[0, 0, feat, :]
        out = []
        for e in range(2):
            keep = (row >= e * HEAD_DIM) & (row < (e + 1) * HEAD_DIM)
            zero = jnp.zeros_like(q)
            out.append(jnp.concatenate([jnp.where(keep, q, zero), jnp.where(keep, qa, zero)],
                                       axis=0))
        return out

    for grp in range(N_PAIRS // PAIRS_PER_TRIP):
        prs = list(range(grp * PAIRS_PER_TRIP, (grp + 1) * PAIRS_PER_TRIP))
        rhs = {pr: rhs_of(pr) for pr in prs}

        def probs(pr, start, n_keys, row0, mask=None, q0=0, first=False, rhs=rhs):
            feat = slice(pr * PAIR_WIDTH, (pr + 1) * PAIR_WIDTH)
            k_aug = (ka_ref[0, pl.ds(start, n_keys), :] if ka_const
                     else ka_ref[0, pl.ds(start, n_keys), feat])
            lhs = jnp.concatenate([k_ref[0, pl.ds(start, n_keys), feat], k_aug], axis=1)
            for e in range(2):
                s_t = _dot(lhs, rhs[pr][e][:, q0:])
                if mask is not None:
                    s_t = jnp.where(mask[:, q0:], s_t, NEG_BIG)
                p = jnp.exp2(s_t)
                p_sc[pr, e, row0:row0 + n_keys, q0:] = p.astype(BF16)
                part = p[0:8, :]
                for r in range(8, n_keys, 8):
                    part = part + p[r:r + 8, :]
                if first:
                    l_sc[pr, e] = part
                else:
                    l_sc[pr, e, :, q0:] += part

        def values(pr, j):
            rows = slice(pr * PAIR_WIDTH, (pr + 1) * PAIR_WIDTH)
            v_pair = jnp.concatenate([v_ref[0, 2 * j, rows, :], v_ref[0, 2 * j + 1, rows, :]],
                                     axis=1)
            for e in range(2):
                upd = _dot(v_pair, p_sc[pr, e])
                acc_sc[pr, e] += upd[e * HEAD_DIM:(e + 1) * HEAD_DIM, :]

        def diagonal(probs=probs, prs=prs):
            diag = pl.multiple_of(i * t, t)
            for pr in prs:
                probs(pr, diag, kt, 0, key_i <= qry_i, first=True)
                probs(pr, diag + kt, kt, kt, key_i + kt <= qry_i, q0=kt)
            for pr in prs:
                rows = slice(pr * PAIR_WIDTH, (pr + 1) * PAIR_WIDTH)
                for e in range(2):
                    head = slice(e * HEAD_DIM, (e + 1) * HEAD_DIM)
                    lower = _dot(v_ref[0, 2 * i, rows, :], p_sc[pr, e, :kt, :])
                    upper = _dot(v_ref[0, 2 * i + 1, rows, :], p_sc[pr, e, kt:, kt:])
                    acc_sc[pr, e, :, :kt] = lower[head, :kt]
                    acc_sc[pr, e, :, kt:] = lower[head, kt:] + upper[head, :]

        def body(j, probs=probs, values=values, prs=prs):
            for pr in prs:
                probs(pr, pl.multiple_of(j * t, t), t, 0)
            for pr in prs:
                values(pr, j)

        def opening(k, diagonal=diagonal, body=body):
            def arm():
                diagonal()
                for u in range(k):
                    body(u)
                return 0
            return arm

        lead = jnp.minimum(i, HEAD_TILES)
        lax.switch(lead, [opening(k) for k in range(HEAD_TILES + 1)])

        def unrolled(n, base, body=body):
            def trip(m, carry):
                for u in range(n):
                    body(base + n * m + u)
                return carry
            return trip

        done = lead
        for n in (4, 2, 1):
            trips = (i - done) // n
            lax.fori_loop(0, trips, unrolled(n, done), 0)
            done = done + trips * n
        for pr in prs:
            o_t = jnp.concatenate(
                [acc_sc[pr, e] / jnp.sum(l_sc[pr, e], axis=0, keepdims=True)
                 for e in range(2)], axis=0)
            o_ref[0, :, pr * PAIR_WIDTH:(pr + 1) * PAIR_WIDTH] = o_t.T


def _attention_online(q_t, qa, k, ka, v_t, *, ka_const, name):
    bsz, nt, _, t = q_t.shape
    s = k.shape[1]
    grid = (bsz, N_PAIRS, nt)
    if ka_const:
        ka_spec = pl.BlockSpec((1, s, PAIR_WIDTH), lambda b, p, i: (0, 0, 0))
    else:
        ka_spec = pl.BlockSpec((1, s, PAIR_WIDTH), lambda b, p, i: (b, 0, p))
    return pl.pallas_call(
        functools.partial(_attn_kernel, t=t),
        grid=grid,
        in_specs=[pl.BlockSpec((1, 1, PAIR_WIDTH, t), lambda b, p, i: (b, i, p, 0)),
                  pl.BlockSpec((1, 1, PAIR_WIDTH, t), lambda b, p, i: (b, i, p, 0)),
                  pl.BlockSpec((1, s, PAIR_WIDTH), lambda b, p, i: (b, 0, p)),
                  ka_spec,
                  pl.BlockSpec((1, s // KEY_TILE, PAIR_WIDTH, KEY_TILE),
                               lambda b, p, i: (b, 0, p, 0))],
        out_specs=pl.BlockSpec((1, t, PAIR_WIDTH), lambda b, p, i: (b, i, p)),
        out_shape=jax.ShapeDtypeStruct((bsz, s, GROUP_WIDTH), F32),
        compiler_params=pltpu.CompilerParams(
            dimension_semantics=("arbitrary", "arbitrary", "arbitrary"),
            vmem_limit_bytes=VMEM_LIMIT),
        name=name,
    )(q_t, qa, k, ka, v_t)


def _attention_bounded(q_t, qa, k, ka, v_t, *, ka_const, name):
    bsz, nt, _, t = q_t.shape
    s = k.shape[1]
    feat_spec = pl.BlockSpec((1, 1, GROUP_WIDTH, t), lambda b, i: (b, i, 0, 0))
    tok_spec = pl.BlockSpec((1, s, GROUP_WIDTH), lambda b, i: (b, 0, 0))
    ka_spec = pl.BlockSpec((1, s, PAIR_WIDTH), lambda b, i: (0, 0, 0)) if ka_const else tok_spec
    return pl.pallas_call(
        functools.partial(_attn_bounded_kernel, t=t, ka_const=ka_const),
        grid=(bsz, nt),
        in_specs=[feat_spec, feat_spec, tok_spec, ka_spec,
                  pl.BlockSpec((1, s // KEY_TILE, GROUP_WIDTH, KEY_TILE),
                               lambda b, i: (b, 0, 0, 0))],
        out_specs=pl.BlockSpec((1, t, GROUP_WIDTH), lambda b, i: (b, i, 0)),
        out_shape=jax.ShapeDtypeStruct((bsz, s, GROUP_WIDTH), F32),
        scratch_shapes=[pltpu.VMEM((N_PAIRS, 2, t, t), BF16),
                        pltpu.VMEM((N_PAIRS, 2, HEAD_DIM, t), F32),
                        pltpu.VMEM((N_PAIRS, 2, 8, t), F32)],
        compiler_params=pltpu.CompilerParams(
            dimension_semantics=("arbitrary", "arbitrary"),
            vmem_limit_bytes=VMEM_LIMIT),
        name=name,
    )(q_t, qa, k, ka, v_t)


def _attention_dispatch(g_q, g_k, q_t, qa, k, ka, v_t, *, ka_const, name):
    bound = HEAD_DIM * jnp.max(jnp.abs(g_q)) * jnp.max(jnp.abs(g_k))
    return lax.cond(
        bound <= MAX_SAFE_EXPONENT,
        functools.partial(_attention_bounded, ka_const=ka_const, name=name + "_bounded"),
        functools.partial(_attention_online, ka_const=ka_const, name=name + "_online"),
        q_t, qa, k, ka, v_t)


def _post_kernel(x_ref, om_ref, of_ref, ga_ref, scm_ref, shm_ref, gm_ref,
                 gom_ref, gof_ref, wout_ref, w1_ref, w2_ref, o_ref, *, ff_chunk):
    def rms(v):
        return v * lax.rsqrt(jnp.mean(v * v, axis=-1, keepdims=True) + EPS)

    mixed = jnp.concatenate([rms(om_ref[0]) * gom_ref[...],
                             rms(of_ref[0]) * gof_ref[...]], axis=1)
    x1 = x_ref[0] + ga_ref[...] * _dot(mixed, wout_ref[...])
    h = rms(x1) * (1.0 + scm_ref[...]) + shm_ref[...]
    d_ff = w1_ref.shape[1]
    y = None
    for c in range(d_ff // ff_chunk):
        hid = _dot(h, w1_ref[:, c * ff_chunk:(c + 1) * ff_chunk])
        hid = jnp.square(jnp.maximum(hid, 0.0))
        part = _dot(hid, w2_ref[c * ff_chunk:(c + 1) * ff_chunk, :])
        y = part if y is None else y + part
    o_ref[0] = x1 + gm_ref[...] * y


def _post(x, o_m, o_f, mod4, g_out_m, g_out_f, w_out, w_ff1, w_ff2, tm):
    bsz, s, d = x.shape
    d_ff = w_ff1.shape[1]

    def const(a):
        nd = a.ndim
        return pl.BlockSpec(a.shape, lambda b, i, _n=nd: (0,) * _n,
                            pipeline_mode=pl.Buffered(1))

    def mod_spec(k):
        return pl.BlockSpec((None, None, 1, d), lambda b, i, _k=k: (b, _k, 0, 0))

    tok = lambda w: pl.BlockSpec((1, tm, w), lambda b, i: (b, i, 0))
    return pl.pallas_call(
        functools.partial(_post_kernel, ff_chunk=1024),
        grid=(bsz, s // tm),
        in_specs=[tok(d), tok(GROUP_WIDTH), tok(GROUP_WIDTH),
                  mod_spec(2), mod_spec(4), mod_spec(3), mod_spec(5),
                  const(g_out_m), const(g_out_f), const(w_out), const(w_ff1), const(w_ff2)],
        out_specs=tok(d),
        out_shape=jax.ShapeDtypeStruct((bsz, s, d), F32),
        compiler_params=pltpu.CompilerParams(
            dimension_semantics=("arbitrary", "arbitrary"),
            vmem_limit_bytes=POST_VMEM_LIMIT),
        name="post",
    )(x, o_m, o_f, mod4, mod4, mod4, mod4, g_out_m, g_out_f, w_out, w_ff1, w_ff2)


def _rope_tables(s):
    inv_freq = ROPE_THETA ** (-np.arange(0, ROPE_DIM, 2, dtype=np.float64) / ROPE_DIM)
    ang = np.arange(s, dtype=np.float64)[:, None] * inv_freq[None, :]
    cos, sin = np.cos(ang), np.sin(ang)
    rc = np.ones((s, LANES)); rs1 = np.zeros((s, LANES)); rs2 = np.zeros((s, LANES))
    for base in (0, HEAD_DIM):
        rc[:, base:base + ROPE_HALF] = cos
        rc[:, base + ROPE_HALF:base + ROPE_DIM] = cos
        rs2[:, base:base + ROPE_HALF] = -sin
        rs1[:, base + ROPE_HALF:base + ROPE_DIM] = sin
    f = lambda a: jnp.asarray(a, dtype=F32)
    return f(cos.T), f(sin.T), f(rc), f(rs1), f(rs2)


def _static_mats(s):
    idx = np.arange(256)
    jmat = (idx[:, None] // HEAD_DIM == idx[None, :] // HEAD_DIM) / HEAD_DIM
    place = np.zeros((LANES, GROUP_WIDTH))
    for part in range(3):
        for hd in range(GROUP_HEADS):
            place[part * GROUP_HEADS + hd,
                  (hd // 2) * PAIR_WIDTH + (hd % 2) * HEAD_DIM + part] = 1.0
    kones = np.zeros((1, GROUP_WIDTH))
    for hd in range(GROUP_HEADS):
        kones[0, hd * HEAD_DIM + 3:hd * HEAD_DIM + 6] = 1.0
    ka_moba = np.zeros((1, s, PAIR_WIDTH))
    blk = np.arange(s) // MOBA_BLOCK
    for e in range(2):
        ka_moba[0, np.arange(s), e * HEAD_DIM + blk] = 1.0
    b = lambda a: jnp.asarray(a, dtype=BF16)
    return b(jmat), b(place), jnp.asarray(kones, dtype=F32), b(ka_moba)


def kernel(x, c, w_ada, b_ada, w_in, b_forget, g_qn_moba, g_kn_moba, g_qn_fox, g_kn_fox,
           g_out_moba, g_out_fox, w_out, w_ff1, w_ff2):
    bsz, s, d = x.shape
    depth = w_ada.shape[0]
    w = GROUP_WIDTH
    assert s % MOBA_BLOCK == 0 and s // MOBA_BLOCK <= 16
    tm_in = 1024
    tm_post = 512
    cos_t, sin_t, rc, rs1, rs2 = _rope_tables(s)
    jmat, place, kones, ka_moba = _static_mats(s)
    q_scale = HEAD_DIM ** -0.5 * LOG2E

    for l in range(depth):
        mod = _adaln(c, w_ada[l], b_ada[l])
        mod4 = mod.reshape(bsz, 6, 1, d)

        wl = w_in[l]
        wf =jnp.zeros((d, LANES), F32).at[:, :GROUP_HEADS].set(wl[:, 6 * w:]).astype(BF16)
        bfor = jnp.zeros((1, LANES), F32).at[0, :GROUP_HEADS].set(b_forget[l])
        q_gain = lambda g: jnp.broadcast_to((g * q_scale)[:, None], (HEAD_DIM, tm_in))
        k_gain = lambda g: jnp.tile(g, GROUP_HEADS).reshape(1, w)
        wts = _project_weights(w_in, l, w) + (
               wf, bfor,
               q_gain(g_qn_moba[l]), k_gain(g_kn_moba[l]),
               q_gain(g_qn_fox[l]), k_gain(g_kn_fox[l]))
        consts = (cos_t, sin_t, rc, rs1, rs2, jmat, place, kones)

        qm_t, km, vm_t, bias_t, qf_t, kf, vf_t, ka_f, qa_f = _inproj(
            x, mod4[:, 1], mod4[:, 0], wts, consts, tm_in)

        o_m = _attention_dispatch(g_qn_moba[l] * q_scale, g_kn_moba[l],
                                  qm_t, bias_t, km, ka_moba, vm_t, ka_const=True,
                                  name="attn_moba")
        o_f = _attention_dispatch(g_qn_fox[l] * q_scale, g_kn_fox[l],
                                  qf_t, qa_f, kf, ka_f, vf_t, ka_const=False,
                                  name="attn_fox")

        x = _post(x, o_m, o_f, mod4,
                  g_out_moba[l].reshape(1, w), g_out_fox[l].reshape(1, w),
                  w_out[l], w_ff1[l], w_ff2[l],
                  tm_post)
    return x
```
